```python
import jax, jax.numpy as jnp
from jax import lax
import numpy as np

D_MODEL = 2048
BATCH = 2
SEQ = 4096
DEPTH = 1

HEAD_DIM = 64
A_Q_HEADS = 12
A_KV_HEADS = 4
A_GROUP = A_Q_HEADS // A_KV_HEADS
WINDOW = 128
BLOCK = 128
B_HEADS = 12
C_HEADS = 4
C_HEAD_DIM = 128
MEM_TOKENS = 256
N_BRANCHES = 3
A_WIDTH = A_Q_HEADS * HEAD_DIM
A_KV_WIDTH = A_KV_HEADS * HEAD_DIM
B_WIDTH = B_HEADS * HEAD_DIM
C_WIDTH = C_HEADS * C_HEAD_DIM
EPS = 1e-6
NEG = -1e30

IN_SIZES = (A_WIDTH, A_KV_WIDTH, A_KV_WIDTH, A_WIDTH,
            B_WIDTH, B_WIDTH, B_WIDTH, B_WIDTH, B_HEADS,
            C_WIDTH, C_WIDTH,
            N_BRANCHES * D_MODEL)
IN_WIDTH = sum(IN_SIZES)
IN_OFFSETS = tuple(int(o) for o in np.cumsum(IN_SIZES)[:-1])

kernel_name = "hybrid_swa_fox_memory_gated_block"


def rms_norm(x, gain):
    x32 = x.astype(jnp.float32)
    y = x32 * lax.rsqrt(jnp.mean(x32 * x32, axis=-1, keepdims=True) + EPS)
    return (y * gain.astype(jnp.float32)).astype(x.dtype)


def alibi_slopes(n_heads):
    h = jnp.arange(1, n_heads + 1, dtype=jnp.float32)
    return jnp.exp2(-8.0 * h / n_heads)


def sliding_window_attention(q, k, v, q_gain, k_gain, sinks):
    b, s, _, d = q.shape
    nb = s // BLOCK
    q = rms_norm(q, q_gain).astype(jnp.float32)
    k = rms_norm(k, k_gain).astype(jnp.float32)
    v = v.astype(jnp.float32)
    qb = q.reshape(b, nb, BLOCK, A_KV_HEADS, A_GROUP, d)
    pad = jnp.zeros((b, BLOCK, A_KV_HEADS, d), jnp.float32)
    kp = jnp.concatenate([pad, k], axis=1).reshape(b, nb + 1, BLOCK, A_KV_HEADS, d)
    vp = jnp.concatenate([pad, v], axis=1).reshape(b, nb + 1, BLOCK, A_KV_HEADS, d)
    kb = jnp.concatenate([kp[:, :-1], kp[:, 1:]], axis=2)
    vb = jnp.concatenate([vp[:, :-1], vp[:, 1:]], axis=2)
    scores = jnp.einsum('bnqkgd,bnskd->bnkgqs', qb, kb) * (d ** -0.5)
    qi = jnp.arange(BLOCK)[:, None]
    kj = jnp.arange(2 * BLOCK)[None, :]
    rel = qi + BLOCK - kj
    key_pos = jnp.arange(nb)[:, None, None] * BLOCK - BLOCK + kj[None]
    valid = (rel >= 0) & (rel < WINDOW) & (key_pos >= 0)
    slopes = alibi_slopes(A_Q_HEADS).reshape(A_KV_HEADS, A_GROUP)
    bias = -slopes[:, :, None, None] * rel.astype(jnp.float32)
    scores = jnp.where(valid[None, :, None, None], scores + bias[None, None], NEG)
    sink = sinks.astype(jnp.float32).reshape(A_KV_HEADS, A_GROUP)[None, None, :, :, None, None]
    m = jnp.maximum(jnp.max(scores, axis=-1, keepdims=True), sink)
    p = jnp.exp(scores - m)
    denom = jnp.sum(p, axis=-1, keepdims=True) + jnp.exp(sink - m)
    out = jnp.einsum('bnkgqs,bnskd->bnqkgd', p / denom, vb)
    return out.reshape(b, s, A_Q_HEADS * d)


def forgetting_attention(q, k, v, f_logit, q_gain, k_gain):
    b, s, h, d = q.shape
    nb = s // BLOCK
    q = rms_norm(q, q_gain).astype(jnp.float32)
    k = rms_norm(k, k_gain).astype(jnp.float32)
    v = v.astype(jnp.float32)
    log_f = jax.nn.log_sigmoid(f_logit.astype(jnp.float32))
    c = jnp.cumsum(log_f, axis=1).transpose(0, 2, 1)
    q_blocks = q.reshape(b, nb, BLOCK, h, d).transpose(1, 0, 2, 3, 4)
    cq_blocks = c.reshape(b, h, nb, BLOCK).transpose(2, 0, 1, 3)
    key_pos = jnp.arange(s)
    scale = d ** -0.5

    def block_fn(args):
        qb, cqb, n = args
        sc = jnp.einsum('bqhd,bshd->bhqs', qb, k) * scale + cqb[..., None] - c[:, :, None, :]
        qpos = n * BLOCK + jnp.arange(BLOCK)
        mask = key_pos[None, :] <= qpos[:, None]
        sc = jnp.where(mask[None, None], sc, NEG)
        p = jax.nn.softmax(sc, axis=-1)
        return jnp.einsum('bhqs,bshd->bqhd', p, v)

    out = lax.map(block_fn, (q_blocks, cq_blocks, jnp.arange(nb)))
    return out.transpose(1, 0, 2, 3, 4).reshape(b, s, h * d)


def memory_attention(q, mk, mv, q_gain, k_gain):
    b, s, h, d = q.shape
    q = rms_norm(q, q_gain).astype(jnp.float32)
    mk = rms_norm(mk, k_gain).astype(jnp.float32)
    sc = jnp.einsum('bthd,bmhd->bhtm', q, mk) * (d ** -0.5)
    p = jax.nn.softmax(sc, axis=-1)
    out = jnp.einsum('bhtm,bmhd->bthd', p, mv.astype(jnp.float32))
    return out.reshape(b, s, h * d)


def hybrid_layer(x, mem, norm_gain, mem_norm_gain, w_in, b_forget,
                 q_gain_a, k_gain_a, sinks_a, q_gain_b, k_gain_b, q_gain_c, k_gain_c,
                 w_mem_kv, w_branch_a, w_branch_b, w_branch_c, w_out):
    b, s, _ = x.shape
    hn = rms_norm(x, norm_gain)
    proj = hn @ w_in
    (qa, ka, va, za, qb, kb, vb, zb, fb, qc, zc, gate_logits) = jnp.split(proj, IN_OFFSETS, axis=-1)

    ya = sliding_window_attention(qa.reshape(b, s, A_Q_HEADS, HEAD_DIM),
                                  ka.reshape(b, s, A_KV_HEADS, HEAD_DIM),
                                  va.reshape(b, s, A_KV_HEADS, HEAD_DIM),
                                  q_gain_a, k_gain_a, sinks_a).astype(x.dtype)
    ua = (ya * jax.nn.silu(za)) @ w_branch_a

    yb = forgetting_attention(qb.reshape(b, s, B_HEADS, HEAD_DIM),
                              kb.reshape(b, s, B_HEADS, HEAD_DIM),
                              vb.reshape(b, s, B_HEADS, HEAD_DIM),
                              fb + b_forget, q_gain_b, k_gain_b).astype(x.dtype)
    ub = (yb * jax.nn.silu(zb)) @ w_branch_b

    mkv = rms_norm(mem, mem_norm_gain) @ w_mem_kv
    mk, mv = jnp.split(mkv, 2, axis=-1)
    mlen = mem.shape[1]
    yc = memory_attention(qc.reshape(b, s, C_HEADS, C_HEAD_DIM),
                          mk.reshape(b, mlen, C_HEADS, C_HEAD_DIM),
                          mv.reshape(b, mlen, C_HEADS, C_HEAD_DIM),
                          q_gain_c, k_gain_c).astype(x.dtype)
    uc = (yc * jax.nn.silu(zc)) @ w_branch_c

    g = jax.nn.sigmoid(gate_logits.reshape(b, s, N_BRANCHES, D_MODEL))
    y = g[:, :, 0] * ua + g[:, :, 1] * ub + g[:, :, 2] * uc
    return x + y @ w_out


def setup_inputs(seed: int = 0) -> dict:
    key = jax.random.key(seed)
    ks = jax.random.split(key, 20)
    f32 = jnp.float32
    nrm = lambda k, shape: jax.random.normal(k, shape, f32)
    return {
        "x": nrm(ks[0], (BATCH, SEQ, D_MODEL)),
        "mem": nrm(ks[1], (BATCH, MEM_TOKENS, D_MODEL)),
        "norm_gain": 1.0 + 0.02 * nrm(ks[2], (DEPTH, D_MODEL)),
        "mem_norm_gain": 1.0 + 0.02 * nrm(ks[3], (DEPTH, D_MODEL)),
        "w_in": nrm(ks[4], (DEPTH, D_MODEL, IN_WIDTH)) * D_MODEL ** -0.5,
        "b_forget": 3.0 + 0.5 * nrm(ks[5], (DEPTH, B_HEADS)),
        "q_gain_a": 1.0 + 0.02 * nrm(ks[6], (DEPTH, HEAD_DIM)),
        "k_gain_a": 1.0 + 0.02 * nrm(ks[7], (DEPTH, HEAD_DIM)),
        "sinks_a": 0.5 * nrm(ks[8], (DEPTH, A_Q_HEADS)),
        "q_gain_b": 1.0 + 0.02 * nrm(ks[9], (DEPTH, HEAD_DIM)),
        "k_gain_b": 1.0 + 0.02 * nrm(ks[10], (DEPTH, HEAD_DIM)),
        "q_gain_c": 1.0 + 0.02 * nrm(ks[11], (DEPTH, C_HEAD_DIM)),
        "k_gain_c": 1.0 + 0.02 * nrm(ks[12], (DEPTH, C_HEAD_DIM)),
        "w_mem_kv": nrm(ks[13], (DEPTH, D_MODEL, 2 * C_WIDTH)) * D_MODEL ** -0.5,
        "w_branch_a": nrm(ks[14], (DEPTH, A_WIDTH, D_MODEL)) * A_WIDTH ** -0.5,
        "w_branch_b": nrm(ks[15], (DEPTH, B_WIDTH, D_MODEL)) * B_WIDTH ** -0.5,
        "w_branch_c": nrm(ks[16], (DEPTH, C_WIDTH, D_MODEL)) * C_WIDTH ** -0.5,
        "w_out": nrm(ks[17], (DEPTH, D_MODEL, D_MODEL)) * D_MODEL ** -0.5,
    }


def reference(x, mem, norm_gain, mem_norm_gain, w_in, b_forget,
              q_gain_a, k_gain_a, sinks_a, q_gain_b, k_gain_b, q_gain_c, k_gain_c,
              w_mem_kv, w_branch_a, w_branch_b, w_branch_c, w_out):
    for layer in range(DEPTH):
        x = hybrid_layer(x, mem, norm_gain[layer], mem_norm_gain[layer], w_in[layer], b_forget[layer],
                         q_gain_a[layer], k_gain_a[layer], sinks_a[layer],
                         q_gain_b[layer], k_gain_b[layer], q_gain_c[layer], k_gain_c[layer],
                         w_mem_kv[layer], w_branch_a[layer], w_branch_b[layer], w_branch_c[layer],
                         w_out[layer])
    return x
```

```python
import functools

import jax
import jax.numpy as jnp
import numpy as np
from jax import lax
from jax.experimental import pallas as pl
from jax.experimental.pallas import tpu as pltpu

F32 = jnp.float32
BF16 = jnp.bfloat16

D_MODEL = 2048
HEAD_DIM = 64
A_Q_HEADS = 12
A_KV_HEADS = 4
A_GROUP = A_Q_HEADS // A_KV_HEADS
WINDOW = 128
B_HEADS = 12
C_HEADS = 4
C_HEAD_DIM = 128
A_WIDTH = A_Q_HEADS * HEAD_DIM
A_KV_WIDTH = A_KV_HEADS * HEAD_DIM
B_WIDTH = B_HEADS * HEAD_DIM
C_WIDTH = C_HEADS * C_HEAD_DIM
EPS = 1e-6
NEG = -1e30

LANES = 128
HALF = LANES // 2
VMEM_LIMIT = 56 * 1024 * 1024

_SRC = {}
_off = 0
for _name, _w in (("qA", A_WIDTH), ("kA", A_KV_WIDTH), ("vA", A_KV_WIDTH), ("zA", A_WIDTH),
                  ("qB", B_WIDTH), ("kB", B_WIDTH), ("vB", B_WIDTH), ("zB", B_WIDTH),
                  ("fB", B_HEADS), ("qC", C_WIDTH), ("zC", C_WIDTH), ("g", 3 * D_MODEL)):
    _SRC[_name] = (_off, _w)
    _off += _w

_ORDER = ("qA", "qB", "kB", "kA", "qC", "zA", "vB", "zB", "vA", "zC", "g")
_DST = {}
_off = 0
for _name in _ORDER:
    _DST[_name] = _off
    assert _off % _SRC[_name][1] == 0 or _name == "g"
    _off += _SRC[_name][1]
PROJ_WIDTH = _off
NORM_WIDTH = _DST["zA"]
assert _DST["g"] % D_MODEL == 0


def _col_block(name, width):
    assert _DST[name] % width == 0
    return _DST[name] // width


def _rms(x, gain):
    ms = jnp.mean(x * x, axis=-1, keepdims=True)
    return x * lax.rsqrt(ms + EPS) * gain


def _sigmoid(t):
    return 0.5 * (jnp.tanh(0.5 * t) + 1.0)


def _dot_nt(a, b):
    return lax.dot_general(a, b, (((1,), (1,)), ((), ())), preferred_element_type=F32)


def _mem_kv_kernel(mem_ref, gain_ref, w_ref, kgain_ref, mk_ref, mv_ref):
    hn = _rms(mem_ref[0], gain_ref[...]).astype(BF16)
    kv = jnp.dot(hn, w_ref[...], preferred_element_type=F32)
    for h in range(C_HEADS):
        sl = slice(h * C_HEAD_DIM, (h + 1) * C_HEAD_DIM)
        mk_ref[0, :, sl] = _rms(kv[:, sl], kgain_ref[...]).astype(BF16)
    mv_ref[0] = kv[:, C_WIDTH:].astype(BF16)


def _mem_kv(mem, gain, w, kgain):
    b, m, d = mem.shape
    return pl.pallas_call(
        _mem_kv_kernel,
        grid=(b,),
        in_specs=[
            pl.BlockSpec((1, m, d), lambda i: (i, 0, 0)),
            pl.BlockSpec((1, d), lambda i: (0, 0)),
            pl.BlockSpec((d, 2 * C_WIDTH), lambda i: (0, 0)),
            pl.BlockSpec((1, C_HEAD_DIM), lambda i: (0, 0)),
        ],
        out_specs=[
            pl.BlockSpec((1, m, C_WIDTH), lambda i: (i, 0, 0)),
            pl.BlockSpec((1, m, C_WIDTH), lambda i: (i, 0, 0)),
        ],
        out_shape=[jax.ShapeDtypeStruct((b, m, C_WIDTH), BF16)] * 2,
        compiler_params=pltpu.CompilerParams(
            dimension_semantics=("arbitrary",), vmem_limit_bytes=VMEM_LIMIT),
        name="mem_kv",
    )(mem, gain, w, kgain)


IN_TM = 1024
IN_TN = 1024
N_NORM_TILES = NORM_WIDTH // IN_TN
assert NORM_WIDTH % IN_TN == 0


def _in_proj_kernel(x_ref, gain_ref, w_ref, wf_ref, bf_ref, cscale_ref, gmat_ref,
                    o_ref, lf_ref, hn_ref):
    j = pl.program_id(1)

    @pl.when(j == 0)
    def _():
        hn = _rms(x_ref[...], gain_ref[...]).astype(BF16)
        hn_ref[...] = hn
        f_logit = jnp.dot(hn, wf_ref[...], preferred_element_type=F32) + bf_ref[...]
        lf_ref[...] = jnp.minimum(f_logit, 0.0) - jnp.log1p(jnp.exp(-jnp.abs(f_logit)))

    acc = jnp.dot(hn_ref[...], w_ref[...], preferred_element_type=F32)

    @pl.when(j < N_NORM_TILES)
    def _():
        for c in range(IN_TN // LANES):
            sl = slice(c * LANES, (c + 1) * LANES)
            a = acc[:, sl]
            ms = jnp.dot((a * a).astype(BF16), gmat_ref[c], preferred_element_type=F32)
            o_ref[:, sl] = (a * lax.rsqrt(ms + EPS) * cscale_ref[:, sl]).astype(BF16)

    @pl.when(j >= N_NORM_TILES)
    def _():
        o_ref[...] = acc.astype(BF16)


def _in_proj(x2, gain, w, wf, bfg, cscale, gmat):
    t, d = x2.shape
    n = w.shape[1]
    grid = (t // IN_TM, n // IN_TN)
    chunks = IN_TN // LANES
    return pl.pallas_call(
        _in_proj_kernel,
        grid=grid,
        in_specs=[
            pl.BlockSpec((IN_TM, d), lambda i, j: (i, 0)),
            pl.BlockSpec((1, d), lambda i, j: (0, 0)),
            pl.BlockSpec((d, IN_TN), lambda i, j: (0, j)),
            pl.BlockSpec((d, LANES), lambda i, j: (0, 0)),
            pl.BlockSpec((1, LANES), lambda i, j: (0, 0)),
            pl.BlockSpec((1, IN_TN), lambda i, j: (0, jnp.minimum(j, N_NORM_TILES - 1))),
            pl.BlockSpec((chunks, LANES, LANES),
                         lambda i, j: (jnp.minimum(j, N_NORM_TILES - 1), 0, 0)),
        ],
        out_specs=[
            pl.BlockSpec((IN_TM, IN_TN), lambda i, j: (i, j)),
            pl.BlockSpec((IN_TM, LANES), lambda i, j: (i, 0)),
        ],
        out_shape=[jax.ShapeDtypeStruct((t, n), BF16),
                   jax.ShapeDtypeStruct((t, LANES), F32)],
        scratch_shapes=[pltpu.VMEM((IN_TM, d), BF16)],
        compiler_params=pltpu.CompilerParams(
            dimension_semantics=("arbitrary", "arbitrary"), vmem_limit_bytes=VMEM_LIMIT),
        name="in_proj",
    )(x2, gain, w, wf, bfg, cscale, gmat)


BIAS_TS = 512
N_PIECES = 3


def _fox_bias_kernel(lf_ref, place_ref, aug_ref, carry_ref):
    i = pl.program_id(1)

    @pl.when(i == 0)
    def _():
        carry_ref[...] = jnp.zeros_like(carry_ref)

    x = lf_ref[...]
    row = lax.broadcasted_iota(jnp.int32, x.shape, 0)
    shift = 1
    while shift < BIAS_TS:
        x = x + jnp.where(row >= shift, pltpu.roll(x, shift, 0), 0.0)
        shift *= 2
    c = x + carry_ref[...]
    carry_ref[...] = c[BIAS_TS - 1:BIAS_TS, :]
    rest = -c
    pieces = []
    for _ in range(N_PIECES):
        p = rest.astype(BF16)
        pieces.append(p)
        rest = rest - p.astype(F32)
    stacked = jnp.concatenate(pieces, axis=1)
    aug_ref[...] = jnp.dot(stacked, place_ref[...], preferred_element_type=F32).astype(BF16)


def _fox_bias(lf, place, batch, seq):
    nblk = seq // BIAS_TS
    return pl.pallas_call(
        _fox_bias_kernel,
        grid=(batch, nblk),
        in_specs=[
            pl.BlockSpec((BIAS_TS, LANES), lambda b, i: (b * nblk + i, 0)),
            pl.BlockSpec((N_PIECES * LANES, B_WIDTH), lambda b, i: (0, 0)),
        ],
        out_specs=pl.BlockSpec((BIAS_TS, B_WIDTH), lambda b, i: (b * nblk + i, 0)),
        out_shape=jax.ShapeDtypeStruct((batch * seq, B_WIDTH), BF16),
        scratch_shapes=[pltpu.VMEM((1, LANES), F32)],
        compiler_params=pltpu.CompilerParams(
            dimension_semantics=("arbitrary", "arbitrary"), vmem_limit_bytes=VMEM_LIMIT),
        name="fox_bias",
    )(lf, place)


def _aug_lane(head):
    return HALF if head % 2 == 0 else 0


def _placement_matrix():
    place = np.zeros((N_PIECES * LANES, B_WIDTH), np.float32)
    for h in range(B_HEADS):
        for p in range(N_PIECES):
            place[p * LANES + h, (h // 2) * LANES + _aug_lane(h) + p] = 1.0
    return place


def _swa_kernel(sinks_ref, q_ref, kp_ref, kc_ref, vp_ref, vc_ref, o_ref):
    n = pl.program_id(1)
    blk = WINDOW
    qi = lax.broadcasted_iota(jnp.int32, (blk, 2 * blk), 0)
    kj = lax.broadcasted_iota(jnp.int32, (blk, 2 * blk), 1)
    rel = qi + blk - kj
    key_pos = (n - 1) * blk + kj
    valid = (rel >= 0) & (rel < WINDOW) & (key_pos >= 0)
    relf = rel.astype(F32)

    kcat = jnp.concatenate([kp_ref[...].astype(F32), kc_ref[...].astype(F32)], axis=0)
    vcat = jnp.concatenate([vp_ref[...].astype(F32), vc_ref[...].astype(F32)], axis=0).astype(BF16)
    lane_k = lax.broadcasted_iota(jnp.int32, (2 * blk, LANES), 1)
    lane_o = lax.broadcasted_iota(jnp.int32, (blk, LANES), 1)

    k_same, k_swap = [], []
    for g in range(A_KV_HEADS):
        slab = kcat[:, (g // 2) * LANES:(g // 2 + 1) * LANES]
        own = (lane_k < HALF) if g % 2 == 0 else (lane_k >= HALF)
        kz = jnp.where(own, slab, 0.0)
        k_same.append(kz.astype(BF16))
        k_swap.append(pltpu.roll(kz, HALF, 1).astype(BF16))

    outs = []
    for h in range(A_Q_HEADS):
        g = h // A_GROUP
        same = (h % 2) == (g % 2)
        q_slab = q_ref[:, (h // 2) * LANES:(h // 2 + 1) * LANES]
        s = _dot_nt(q_slab, k_same[g] if same else k_swap[g])
        slope = float(2.0 ** (-8.0 * (h + 1) / A_Q_HEADS))
        s = jnp.where(valid, s - slope * relf, NEG)
        sink = sinks_ref[h]
        m = jnp.maximum(jnp.max(s, axis=-1, keepdims=True), sink)
        p = jnp.exp(s - m)
        denom = jnp.sum(p, axis=-1, keepdims=True) + jnp.exp(sink - m)
        v_slab = vcat[:, (g // 2) * LANES:(g // 2 + 1) * LANES]
        o = jnp.dot(p.astype(BF16), v_slab, preferred_element_type=F32) / denom
        outs.append(o if same else pltpu.roll(o, HALF, 1))
    for pr in range(A_Q_HEADS // 2):
        o_ref[:, pr * LANES:(pr + 1) * LANES] = jnp.where(
            lane_o < HALF, outs[2 * pr], outs[2 * pr + 1]).astype(BF16)


def _swa(sinks, proj, batch, seq):
    blk = WINDOW
    nb = seq // blk
    qcol = _col_block("qA", A_WIDTH)
    kcol = _col_block("kA", A_KV_WIDTH)
    vcol = _col_block("vA", A_KV_WIDTH)
    cur = lambda b, n: b * nb + n
    prev = lambda b, n: b * nb + jnp.maximum(n - 1, 0)
    return pl.pallas_call(
        _swa_kernel,
        grid=(batch, nb),
        in_specs=[
            pl.BlockSpec(memory_space=pltpu.SMEM),
            pl.BlockSpec((blk, A_WIDTH), lambda b, n: (cur(b, n), qcol)),
            pl.BlockSpec((blk, A_KV_WIDTH), lambda b, n: (prev(b, n), kcol)),
            pl.BlockSpec((blk, A_KV_WIDTH), lambda b, n: (cur(b, n), kcol)),
            pl.BlockSpec((blk, A_KV_WIDTH), lambda b, n: (prev(b, n), vcol)),
            pl.BlockSpec((blk, A_KV_WIDTH), lambda b, n: (cur(b, n), vcol)),
        ],
        out_specs=pl.BlockSpec((blk, A_WIDTH), lambda b, n: (cur(b, n), 0)),
        out_shape=jax.ShapeDtypeStruct((batch * seq, A_WIDTH), BF16),
        compiler_params=pltpu.CompilerParams(
            dimension_semantics=("arbitrary", "arbitrary"), vmem_limit_bytes=VMEM_LIMIT),
        name="swa",
    )(sinks, proj, proj, proj, proj, proj)


FOX_T = 512
FOX_BUILD = 512


def _fox_kernel(q_ref, k_ref, v_ref, aug_ref, o_ref, kop_ref, vop_ref, m_ref, acc_ref):
    qi = pl.program_id(2)
    seq = k_ref.shape[0]
    t = FOX_T

    @pl.when(qi == 0)
    def _():
        lane = lax.broadcasted_iota(jnp.int32, (FOX_BUILD, LANES), 1)
        low = lane < HALF

        def build(r, carry):
            sl = pl.ds(pl.multiple_of(r * FOX_BUILD, FOX_BUILD), FOX_BUILD)
            kk = k_ref[sl, :].astype(F32)
            aa = aug_ref[sl, :].astype(F32)
            vv = v_ref[sl, :].astype(F32)
            kop_ref[0, sl, :] = jnp.where(low, kk, aa).astype(BF16)
            kop_ref[1, sl, :] = jnp.where(low, aa, kk).astype(BF16)
            vop_ref[0, sl, :] = jnp.where(low, vv, jnp.where(lane == HALF, 1.0, 0.0)).astype(BF16)
            vop_ref[1, sl, :] = jnp.where(low, jnp.where(lane == 0, 1.0, 0.0), vv).astype(BF16)
            return carry

        lax.fori_loop(0, seq // FOX_BUILD, build, 0)

    lane_q = lax.broadcasted_iota(jnp.int32, (t, LANES), 1)
    low_q = lane_q < HALF
    q = q_ref[...].astype(F32)
    ones0 = jnp.where((lane_q >= HALF) & (lane_q < HALF + N_PIECES), 1.0, 0.0)
    ones1 = jnp.where(lane_q < N_PIECES, 1.0, 0.0)
    qop = (jnp.where(low_q, q, ones0).astype(BF16), jnp.where(low_q, ones1, q).astype(BF16))

    m_ref[...] = jnp.full(m_ref.shape, NEG, F32)
    acc_ref[...] = jnp.zeros(acc_ref.shape, F32)

    def tile(kj, masked):
        ks = pl.ds(pl.multiple_of(kj * t, t), t)
        for hh in range(2):
            s = _dot_nt(qop[hh], kop_ref[hh, ks, :])
            if masked:
                row = lax.broadcasted_iota(jnp.int32, (t, t), 0)
                col = lax.broadcasted_iota(jnp.int32, (t, t), 1)
                s = jnp.where(col <= row, s, NEG)
            m_old = m_ref[hh]
            m_new = jnp.maximum(m_old, jnp.max(s, axis=-1, keepdims=True))
            alpha = jnp.exp(m_old - m_new)
            p = jnp.exp(s - m_new[:, 0:1]).astype(BF16)
            acc_ref[hh] = acc_ref[hh] * alpha + jnp.dot(
                p, vop_ref[hh, ks, :], preferred_element_type=F32)
            m_ref[hh] = m_new

    def body(kj, carry):
        tile(kj, False)
        return carry

    lax.fori_loop(0, qi, body, 0)
    tile(qi, True)

    a0 = acc_ref[0]
    a1 = acc_ref[1]
    o0 = a0 / a0[:, HALF:HALF + 1]
    o1 = a1 / a1[:, 0:1]
    o_ref[...] = jnp.where(low_q, o0, o1).astype(BF16)


def _fox(proj, aug, batch, seq):
    t = FOX_T
    nq = seq // t
    pairs = B_HEADS // 2
    qcol = _col_block("qB", LANES)
    kcol = _col_block("kB", LANES)
    vcol = _col_block("vB", LANES)
    return pl.pallas_call(
        _fox_kernel,
        grid=(batch, pairs, nq),
        in_specs=[
            pl.BlockSpec((t, LANES), lambda b, p, i: (b * nq + i, qcol + p)),
            pl.BlockSpec((seq, LANES), lambda b, p, i: (b, kcol + p)),
            pl.BlockSpec((seq, LANES), lambda b, p, i: (b, vcol + p)),
            pl.BlockSpec((seq, LANES), lambda b, p, i: (b, p)),
        ],
        out_specs=pl.BlockSpec((t, LANES), lambda b, p, i: (b * nq + i, p)),
        out_shape=jax.ShapeDtypeStruct((batch * seq, B_WIDTH), BF16),
        scratch_shapes=[
            pltpu.VMEM((2, seq, LANES), BF16),
            pltpu.VMEM((2, seq, LANES), BF16),
            pltpu.VMEM((2, t, LANES), F32),
            pltpu.VMEM((2, t, LANES), F32),
        ],
        compiler_params=pltpu.CompilerParams(
            dimension_semantics=("arbitrary", "arbitrary", "arbitrary"),
            vmem_limit_bytes=VMEM_LIMIT),
        name="fox",
    )(proj, proj, proj, aug)


MEM_TM = 512


def _mem_attn_kernel(q_ref, mk_ref, mv_ref, o_ref):
    for h in range(C_HEADS):
        sl = slice(h * C_HEAD_DIM, (h + 1) * C_HEAD_DIM)
        s = _dot_nt(q_ref[:, sl], mk_ref[0, :, sl])
        m = jnp.max(s, axis=-1, keepdims=True)
        p = jnp.exp(s - m)
        denom = jnp.sum(p, axis=-1, keepdims=True)
        o = jnp.dot(p.astype(BF16), mv_ref[0, :, sl], preferred_element_type=F32)
        o_ref[:, sl] = (o / denom).astype(BF16)


def _mem_attn(proj, mk, mv, batch, seq):
    nt = seq // MEM_TM
    mlen = mk.shape[1]
    qcol = _col_block("qC", C_WIDTH)
    return pl.pallas_call(
        _mem_attn_kernel,
        grid=(batch * nt,),
        in_specs=[
            pl.BlockSpec((MEM_TM, C_WIDTH), lambda i: (i, qcol)),
            pl.BlockSpec((1, mlen, C_WIDTH), lambda i: (i // nt, 0, 0)),
            pl.BlockSpec((1, mlen, C_WIDTH), lambda i: (i // nt, 0, 0)),
        ],
        out_specs=pl.BlockSpec((MEM_TM, C_WIDTH), lambda i: (i, 0)),
        out_shape=jax.ShapeDtypeStruct((batch * seq, C_WIDTH), BF16),
        compiler_params=pltpu.CompilerParams(
            dimension_semantics=("arbitrary",), vmem_limit_bytes=VMEM_LIMIT),
        name="mem_attn",
    )(proj, mk, mv)


MERGE_TM = 256


def _merge_kernel(x_ref, ya_ref, za_ref, yb_ref, zb_ref, yc_ref, zc_ref,
                  g0_ref, g1_ref, g2_ref, wa_ref, wb_ref, wc_ref, wo_ref, o_ref):
    def branch(y_ref, z_ref, w_ref):
        z = z_ref[...].astype(F32)
        h = (y_ref[...].astype(F32) * (z * _sigmoid(z))).astype(BF16)
        return jnp.dot(h, w_ref[...], preferred_element_type=F32)

    y = _sigmoid(g0_ref[...].astype(F32)) * branch(ya_ref, za_ref, wa_ref)
    y = y + _sigmoid(g1_ref[...].astype(F32)) * branch(yb_ref, zb_ref, wb_ref)
    y = y + _sigmoid(g2_ref[...].astype(F32)) * branch(yc_ref, zc_ref, wc_ref)
    o_ref[...] = x_ref[...] + jnp.dot(y.astype(BF16), wo_ref[...], preferred_element_type=F32)


def _merge(x2, ya, yb, yc, proj, wa, wb, wc, wo):
    t, d = x2.shape
    tm = MERGE_TM
    gcol = _col_block("g", D_MODEL)
    row = lambda width, col: pl.BlockSpec((tm, width), lambda i: (i, col))
    full = lambda a: pl.BlockSpec(a.shape, lambda i: (0, 0), pipeline_mode=pl.Buffered(1))
    return pl.pallas_call(
        _merge_kernel,
        grid=(t // tm,),
        in_specs=[
            row(d, 0),
            row(A_WIDTH, 0), row(A_WIDTH, _col_block("zA", A_WIDTH)),
            row(B_WIDTH, 0), row(B_WIDTH, _col_block("zB", B_WIDTH)),
            row(C_WIDTH, 0), row(C_WIDTH, _col_block("zC", C_WIDTH)),
            row(d, gcol), row(d, gcol + 1), row(d, gcol + 2),
            full(wa), full(wb), full(wc), full(wo),
        ],
        out_specs=pl.BlockSpec((tm, d), lambda i: (i, 0)),
        out_shape=jax.ShapeDtypeStruct((t, d), F32),
        compiler_params=pltpu.CompilerParams(
            dimension_semantics=("arbitrary",), vmem_limit_bytes=VMEM_LIMIT),
        name="merge",
    )(x2, ya, proj, yb, proj, yc, proj, proj, proj, proj, wa, wb, wc, wo)


def _group_mean_matrices():
    lane = np.arange(LANES)
    m64 = (lane[:, None] // HEAD_DIM == lane[None, :] // HEAD_DIM).astype(np.float32) / HEAD_DIM
    m128 = np.full((LANES, LANES), 1.0 / C_HEAD_DIM, np.float32)
    n64 = _DST["qC"] // LANES
    n128 = C_WIDTH // LANES
    return np.stack([m64] * n64 + [m128] * n128)


def _layer(x, mem, norm_gain, mem_norm_gain, w_in, b_forget, q_gain_a, k_gain_a, sinks_a,
           q_gain_b, k_gain_b, q_gain_c, k_gain_c, w_mem_kv, w_branch_a, w_branch_b,
           w_branch_c, w_out):
    batch, seq, d = x.shape
    x2 = x.reshape(batch * seq, d)

    w_main = jnp.concatenate(
        [w_in[:, _SRC[n][0]:_SRC[n][0] + _SRC[n][1]] for n in _ORDER], axis=1).astype(BF16)
    f0 = _SRC["fB"][0]
    w_f = jnp.pad(w_in[:, f0:f0 + B_HEADS], ((0, 0), (0, LANES - B_HEADS))).astype(BF16)
    b_f = jnp.pad(b_forget.astype(F32), (0, LANES - B_HEADS)).reshape(1, LANES)
    tile_gain = lambda g, width, scale: jnp.tile(g.astype(F32) * scale, width // g.shape[0])
    cscale = jnp.concatenate([
        tile_gain(q_gain_a, A_WIDTH, HEAD_DIM ** -0.5),
        tile_gain(q_gain_b, B_WIDTH, HEAD_DIM ** -0.5),
        tile_gain(k_gain_b, B_WIDTH, 1.0),
        tile_gain(k_gain_a, A_KV_WIDTH, 1.0),
        tile_gain(q_gain_c, C_WIDTH, C_HEAD_DIM ** -0.5),
    ]).reshape(1, NORM_WIDTH)
    gmat = jnp.asarray(_group_mean_matrices(), BF16)
    place = jnp.asarray(_placement_matrix(), BF16)

    mk, mv = _mem_kv(mem, mem_norm_gain.reshape(1, d), w_mem_kv.astype(BF16),
                     k_gain_c.reshape(1, C_HEAD_DIM))
    proj, log_f = _in_proj(x2, norm_gain.reshape(1, d), w_main, w_f, b_f, cscale, gmat)
    aug = _fox_bias(log_f, place, batch, seq)
    ya = _swa(sinks_a.astype(F32), proj, batch, seq)
    yb = _fox(proj, aug, batch, seq)
    yc = _mem_attn(proj, mk, mv, batch, seq)
    out = _merge(x2, ya, yb, yc, proj, w_branch_a.astype(BF16), w_branch_b.astype(BF16),
                 w_branch_c.astype(BF16), w_out.astype(BF16))
    return out.reshape(batch, seq, d)


def kernel(x, mem, norm_gain, mem_norm_gain, w_in, b_forget, q_gain_a, k_gain_a, sinks_a,
           q_gain_b, k_gain_b, q_gain_c, k_gain_c, w_mem_kv, w_branch_a, w_branch_b,
           w_branch_c, w_out):
    for layer in range(norm_gain.shape[0]):
        x = _layer(x, mem, norm_gain[layer], mem_norm_gain[layer], w_in[layer], b_forget[layer],
                   q_gain_a[layer], k_gain_a[layer], sinks_a[layer], q_gain_b[layer],
                   k_gain_b[layer], q_gain_c[layer], k_gain_c[layer], w_mem_kv[layer],
                   w_branch_a[layer], w_branch_b[layer], w_branch_c[layer], w_out[layer])
    return x
```

```python
import functools

import jax
import jax.numpy as jnp
import numpy as np
from jax import lax
from jax.experimental import pallas as pl
from jax.experimental.pallas import tpu as pltpu

F32 = jnp.float32
BF16 = jnp.bfloat16

D_MODEL = 2048
HEAD_DIM = 64
A_Q_HEADS = 12
A_KV_HEADS = 4
A_GROUP = A_Q_HEADS // A_KV_HEADS
WINDOW = 128
B_HEADS = 12
C_HEADS = 4
C_HEAD_DIM = 128
A_WIDTH = A_Q_HEADS * HEAD_DIM
A_KV_WIDTH = A_KV_HEADS * HEAD_DIM
B_WIDTH = B_HEADS * HEAD_DIM
C_WIDTH = C_HEADS * C_HEAD_DIM
EPS = 1e-6
NEG = -1e30

LANES = 128
HALF = LANES // 2
VMEM_LIMIT = 56 * 1024 * 1024

_SRC = {}
_off = 0
for _name, _w in (("qA", A_WIDTH), ("kA", A_KV_WIDTH), ("vA", A_KV_WIDTH), ("zA", A_WIDTH),
                  ("qB", B_WIDTH), ("kB", B_WIDTH), ("vB", B_WIDTH), ("zB", B_WIDTH),
                  ("fB", B_HEADS), ("qC", C_WIDTH), ("zC", C_WIDTH), ("g", 3 * D_MODEL)):
    _SRC[_name] = (_off, _w)
    _off += _w

_ORDER = ("qA", "qB", "kB", "kA", "qC", "zA", "vB", "zB", "vA", "zC", "g")
_DST = {}
_off = 0
for _name in _ORDER:
    _DST[_name] = _off
    assert _off % _SRC[_name][1] == 0 or _name == "g"
    _off += _SRC[_name][1]
PROJ_WIDTH = _off
NORM_WIDTH = _DST["zA"]
assert _DST["g"] % D_MODEL == 0


def _col_block(name, width):
    assert _DST[name] % width == 0
    return _DST[name] // width


def _rms(x, gain):
    ms = jnp.mean(x * x, axis=-1, keepdims=True)
    return x * lax.rsqrt(ms + EPS) * gain


def _sigmoid(t):
    return 0.5 * (jnp.tanh(0.5 * t) + 1.0)


def _dot_nt(a, b):
    return lax.dot_general(a, b, (((1,), (1,)), ((), ())), preferred_element_type=F32)


def _mem_kv_kernel(mem_ref, gain_ref, w_ref, kgain_ref, mk_ref, mv_ref):
    hn = _rms(mem_ref[0], gain_ref[...]).astype(BF16)
    kv = jnp.dot(hn, w_ref[...], preferred_element_type=F32)
    for h in range(C_HEADS):
        sl = slice(h * C_HEAD_DIM, (h + 1) * C_HEAD_DIM)
        mk_ref[0, :, sl] = _rms(kv[:, sl], kgain_ref[...]).astype(BF16)
    mv_ref[0] = kv[:, C_WIDTH:].astype(BF16)


def _mem_kv(mem, gain, w, kgain):
    b, m, d = mem.shape
    return pl.pallas_call(
        _mem_kv_kernel,
        grid=(b,),
        in_specs=[
            pl.BlockSpec((1, m, d), lambda i: (i, 0, 0)),
            pl.BlockSpec((1, d), lambda i: (0, 0)),
            pl.BlockSpec((d, 2 * C_WIDTH), lambda i: (0, 0)),
            pl.BlockSpec((1, C_HEAD_DIM), lambda i: (0, 0)),
        ],
        out_specs=[
            pl.BlockSpec((1, m, C_WIDTH), lambda i: (i, 0, 0)),
            pl.BlockSpec((1, m, C_WIDTH), lambda i: (i, 0, 0)),
        ],
        out_shape=[jax.ShapeDtypeStruct((b, m, C_WIDTH), BF16)] * 2,
        compiler_params=pltpu.CompilerParams(
            dimension_semantics=("arbitrary",), vmem_limit_bytes=VMEM_LIMIT),
        name="mem_kv",
    )(mem, gain, w, kgain)


IN_TM = 1024
IN_TN = 1024
N_NORM_TILES = NORM_WIDTH // IN_TN
assert NORM_WIDTH % IN_TN == 0


def _in_proj_kernel(x_ref, gain_ref, w_ref, wf_ref, bf_ref, cscale_ref, gmat_ref,
                    o_ref, lf_ref, hn_ref):
    j = pl.program_id(1)

    @pl.when(j == 0)
    def _():
        hn = _rms(x_ref[...], gain_ref[...]).astype(BF16)
        hn_ref[...] = hn
        f_logit = jnp.dot(hn, wf_ref[...], preferred_element_type=F32) + bf_ref[...]
        lf_ref[...] = jnp.minimum(f_logit, 0.0) - jnp.log1p(jnp.exp(-jnp.abs(f_logit)))

    acc = jnp.dot(hn_ref[...], w_ref[...], preferred_element_type=F32)

    @pl.when(j < N_NORM_TILES)
    def _():
        for c in range(IN_TN // LANES):
            sl = slice(c * LANES, (c + 1) * LANES)
            a = acc[:, sl]
            ms = jnp.dot((a * a).astype(BF16), gmat_ref[c], preferred_element_type=F32)
            o_ref[:, sl] = (a * lax.rsqrt(ms + EPS) * cscale_ref[:, sl]).astype(BF16)

    @pl.when(j >= N_NORM_TILES)
    def _():
        o_ref[...] = acc.astype(BF16)


def _in_proj(x2, gain, w, wf, bfg, cscale, gmat):
    t, d = x2.shape
    n = w.shape[1]
    grid = (t // IN_TM, n // IN_TN)
    chunks = IN_TN // LANES
    return pl.pallas_call(
        _in_proj_kernel,
        grid=grid,
        in_specs=[
            pl.BlockSpec((IN_TM, d), lambda i, j: (i, 0)),
            pl.BlockSpec((1, d), lambda i, j: (0, 0)),
            pl.BlockSpec((d, IN_TN), lambda i, j: (0, j)),
            pl.BlockSpec((d, LANES), lambda i, j: (0, 0)),
            pl.BlockSpec((1, LANES), lambda i, j: (0, 0)),
            pl.BlockSpec((1, IN_TN), lambda i, j: (0, jnp.minimum(j, N_NORM_TILES - 1))),
            pl.BlockSpec((chunks, LANES, LANES),
                         lambda i, j: (jnp.minimum(j, N_NORM_TILES - 1), 0, 0)),
        ],
        out_specs=[
            pl.BlockSpec((IN_TM, IN_TN), lambda i, j: (i, j)),
            pl.BlockSpec((IN_TM, LANES), lambda i, j: (i, 0)),
        ],
        out_shape=[jax.ShapeDtypeStruct((t, n), BF16),
                   jax.ShapeDtypeStruct((t, LANES), F32)],
        scratch_shapes=[pltpu.VMEM((IN_TM, d), BF16)],
        compiler_params=pltpu.CompilerParams(
            dimension_semantics=("arbitrary", "arbitrary"), vmem_limit_bytes=VMEM_LIMIT),
        name="in_proj",
    )(x2, gain, w, wf, bfg, cscale, gmat)


BIAS_TS = 512
N_PIECES = 3


def _fox_bias_kernel(lf_ref, place_ref, aug_ref, carry_ref):
    i = pl.program_id(1)

    @pl.when(i == 0)
    def _():
        carry_ref[...] = jnp.zeros_like(carry_ref)

    x = lf_ref[...]
    row = lax.broadcasted_iota(jnp.int32, x.shape, 0)
    shift = 1
    while shift < BIAS_TS:
        x = x + jnp.where(row >= shift, pltpu.roll(x, shift, 0), 0.0)
        shift *= 2
    c = x + carry_ref[...]
    carry_ref[...] = c[BIAS_TS - 1:BIAS_TS, :]
    rest = c * (-LOG2E)
    pieces = []
    for _ in range(N_PIECES):
        p = rest.astype(BF16)
        pieces.append(p)
        rest = rest - p.astype(F32)
    stacked = jnp.concatenate(pieces, axis=1)
    aug_ref[...] = jnp.dot(stacked, place_ref[...], preferred_element_type=F32).astype(BF16)


def _fox_bias(lf, place, batch, seq):
    nblk = seq // BIAS_TS
    return pl.pallas_call(
        _fox_bias_kernel,
        grid=(batch, nblk),
        in_specs=[
            pl.BlockSpec((BIAS_TS, LANES), lambda b, i: (b * nblk + i, 0)),
            pl.BlockSpec((N_PIECES * LANES, B_WIDTH), lambda b, i: (0, 0)),
        ],
        out_specs=pl.BlockSpec((BIAS_TS, B_WIDTH), lambda b, i: (b * nblk + i, 0)),
        out_shape=jax.ShapeDtypeStruct((batch * seq, B_WIDTH), BF16),
        scratch_shapes=[pltpu.VMEM((1, LANES), F32)],
        compiler_params=pltpu.CompilerParams(
            dimension_semantics=("arbitrary", "arbitrary"), vmem_limit_bytes=VMEM_LIMIT),
        name="fox_bias",
    )(lf, place)


def _aug_lane(head):
    return HALF if head % 2 == 0 else 0


def _placement_matrix():
    place = np.zeros((N_PIECES * LANES, B_WIDTH), np.float32)
    for h in range(B_HEADS):
        for p in range(N_PIECES):
            place[p * LANES + h, (h // 2) * LANES + _aug_lane(h) + p] = 1.0
    return place


def _swa_kernel(sinks_ref, q_ref, kp_ref, kc_ref, vp_ref, vc_ref, o_ref):
    n = pl.program_id(1)
    blk = WINDOW
    qi = lax.broadcasted_iota(jnp.int32, (blk, 2 * blk), 0)
    kj = lax.broadcasted_iota(jnp.int32, (blk, 2 * blk), 1)
    rel = qi + blk - kj
    key_pos = (n - 1) * blk + kj
    valid = (rel >= 0) & (rel < WINDOW) & (key_pos >= 0)
    relf = rel.astype(F32)

    kcat = jnp.concatenate([kp_ref[...].astype(F32), kc_ref[...].astype(F32)], axis=0)
    vcat = jnp.concatenate([vp_ref[...].astype(F32), vc_ref[...].astype(F32)], axis=0).astype(BF16)
    lane_k = lax.broadcasted_iota(jnp.int32, (2 * blk, LANES), 1)
    lane_o = lax.broadcasted_iota(jnp.int32, (blk, LANES), 1)

    k_same, k_swap = [], []
    for g in range(A_KV_HEADS):
        slab = kcat[:, (g // 2) * LANES:(g // 2 + 1) * LANES]
        own = (lane_k < HALF) if g % 2 == 0 else (lane_k >= HALF)
        kz = jnp.where(own, slab, 0.0)
        k_same.append(kz.astype(BF16))
        k_swap.append(pltpu.roll(kz, HALF, 1).astype(BF16))

    outs = []
    for h in range(A_Q_HEADS):
        g = h // A_GROUP
        same = (h % 2) == (g % 2)
        q_slab = q_ref[:, (h // 2) * LANES:(h // 2 + 1) * LANES]
        s = _dot_nt(q_slab, k_same[g] if same else k_swap[g])
        slope = float(2.0 ** (-8.0 * (h + 1) / A_Q_HEADS))
        s = jnp.where(valid, s - slope * relf, NEG)
        sink = sinks_ref[h]
        m = jnp.maximum(jnp.max(s, axis=-1, keepdims=True), sink)
        p = jnp.exp(s - m)
        denom = jnp.sum(p, axis=-1, keepdims=True) + jnp.exp(sink - m)
        v_slab = vcat[:, (g // 2) * LANES:(g // 2 + 1) * LANES]
        o = jnp.dot(p.astype(BF16), v_slab, preferred_element_type=F32) / denom
        outs.append(o if same else pltpu.roll(o, HALF, 1))
    for pr in range(A_Q_HEADS // 2):
        o_ref[:, pr * LANES:(pr + 1) * LANES] = jnp.where(
            lane_o < HALF, outs[2 * pr], outs[2 * pr + 1]).astype(BF16)


def _swa(sinks, proj, batch, seq):
    blk = WINDOW
    nb = seq // blk
    qcol = _col_block("qA", A_WIDTH)
    kcol = _col_block("kA", A_KV_WIDTH)
    vcol = _col_block("vA", A_KV_WIDTH)
    cur = lambda b, n: b * nb + n
    prev = lambda b, n: b * nb + jnp.maximum(n - 1, 0)
    return pl.pallas_call(
        _swa_kernel,
        grid=(batch, nb),
        in_specs=[
            pl.BlockSpec(memory_space=pltpu.SMEM),
            pl.BlockSpec((blk, A_WIDTH), lambda b, n: (cur(b, n), qcol)),
            pl.BlockSpec((blk, A_KV_WIDTH), lambda b, n: (prev(b, n), kcol)),
            pl.BlockSpec((blk, A_KV_WIDTH), lambda b, n: (cur(b, n), kcol)),
            pl.BlockSpec((blk, A_KV_WIDTH), lambda b, n: (prev(b, n), vcol)),
            pl.BlockSpec((blk, A_KV_WIDTH), lambda b, n: (cur(b, n), vcol)),
        ],
        out_specs=pl.BlockSpec((blk, A_WIDTH), lambda b, n: (cur(b, n), 0)),
        out_shape=jax.ShapeDtypeStruct((batch * seq, A_WIDTH), BF16),
        compiler_params=pltpu.CompilerParams(
            dimension_semantics=("arbitrary", "arbitrary"), vmem_limit_bytes=VMEM_LIMIT),
        name="swa",
    )(sinks, proj, proj, proj, proj, proj)


FOX_T = 512
LOG2E = float(np.log2(np.e))


FOX_H = FOX_T // 2


def _fox_kernel(q_ref, k_ref, v_ref, aug_ref, o_ref,
                kop_ref, vt_ref, s_ref, cm_ref, p_ref, al_ref, m_ref, acc_ref):
    qi = pl.program_id(2)
    seq = k_ref.shape[0]
    t = FOX_T
    hk = FOX_H

    @pl.when(qi == 0)
    def _():
        lane = lax.broadcasted_iota(jnp.int32, (hk, LANES), 1)
        low = lane < HALF

        def build(r, carry):
            sl = pl.ds(pl.multiple_of(r * hk, hk), hk)
            kk = k_ref[sl, :].astype(F32)
            aa = aug_ref[sl, :].astype(F32)
            vv = v_ref[sl, :].astype(F32)
            kop_ref[0, sl, :] = jnp.where(low, kk, aa).astype(BF16)
            kop_ref[1, sl, :] = jnp.where(low, aa, kk).astype(BF16)
            v0 = jnp.where(low, vv, jnp.where(lane == HALF, 1.0, 0.0))
            v1 = jnp.where(low, jnp.where(lane == 0, 1.0, 0.0), vv)
            vt_ref[0, r] = v0.T.astype(BF16)
            vt_ref[1, r] = v1.T.astype(BF16)
            return carry

        lax.fori_loop(0, seq // hk, build, 0)

    lane_q = lax.broadcasted_iota(jnp.int32, (t, LANES), 1)
    low_q = lane_q < HALF
    q = q_ref[...].astype(F32)
    ones0 = jnp.where((lane_q >= HALF) & (lane_q < HALF + N_PIECES), 1.0, 0.0)
    ones1 = jnp.where(lane_q < N_PIECES, 1.0, 0.0)
    qop = (jnp.where(low_q, q, ones0).astype(BF16), jnp.where(low_q, ones1, q).astype(BF16))

    m_ref[...] = jnp.full(m_ref.shape, NEG, F32)
    acc_ref[...] = jnp.zeros(acc_ref.shape, F32)

    def scores(slot, half, masked):
        ks = pl.ds(pl.multiple_of(half * hk, hk), hk)
        for hh in range(2):
            st = _dot_nt(kop_ref[hh, ks, :], qop[hh])
            if masked:
                key = lax.broadcasted_iota(jnp.int32, (hk, t), 0) + slot * hk
                qry = lax.broadcasted_iota(jnp.int32, (hk, t), 1)
                st = jnp.where(key <= qry, st, NEG)
            s_ref[slot, hh] = st
            cm_ref[slot, hh] = jnp.max(st, axis=0, keepdims=True)

    def soft(slot):
        for hh in range(2):
            m_old = m_ref[hh]
            m_new = jnp.maximum(m_old, cm_ref[slot, hh])
            al_ref[slot, hh] = jnp.exp2(m_old - m_new)
            p_ref[slot, hh] = jnp.exp2(s_ref[slot, hh] - m_new).astype(BF16)
            m_ref[hh] = m_new

    def pv(slot, half):
        for hh in range(2):
            acc_ref[hh] = acc_ref[hh] * al_ref[slot, hh] + jnp.dot(
                vt_ref[hh, half], p_ref[slot, hh], preferred_element_type=F32)

    scores(0, 2 * qi, True)
    scores(1, 2 * qi + 1, True)
    soft(0)
    soft(1)

    @pl.when(qi > 0)
    def _():
        scores(0, 0, False)

    def body(i, carry):
        prev = jnp.where(i == 0, qi, i - 1)
        pv(0, 2 * prev)
        soft(0)
        scores(1, 2 * i + 1, False)
        scores(0, 2 * i + 2, False)
        pv(1, 2 * prev + 1)
        soft(1)
        return carry

    lax.fori_loop(0, qi, body, 0)
    last = jnp.where(qi == 0, qi, qi - 1)
    pv(0, 2 * last)
    pv(1, 2 * last + 1)

    a0 = acc_ref[0]
    a1 = acc_ref[1]
    feat = lax.broadcasted_iota(jnp.int32, (LANES, t), 0)
    ot = jnp.where(feat < HALF, a0 / a0[HALF:HALF + 1, :], a1 / a1[0:1, :])
    o_ref[...] = ot.T.astype(BF16)


def _fox(proj, aug, batch, seq):
    t = FOX_T
    nq = seq // t
    pairs = B_HEADS // 2
    qcol = _col_block("qB", LANES)
    kcol = _col_block("kB", LANES)
    vcol = _col_block("vB", LANES)
    return pl.pallas_call(
        _fox_kernel,
        grid=(batch, pairs, nq),
        in_specs=[
            pl.BlockSpec((t, LANES), lambda b, p, i: (b * nq + i, qcol + p)),
            pl.BlockSpec((seq, LANES), lambda b, p, i: (b, kcol + p)),
            pl.BlockSpec((seq, LANES), lambda b, p, i: (b, vcol + p)),
            pl.BlockSpec((seq, LANES), lambda b, p, i: (b, p)),
        ],
        out_specs=pl.BlockSpec((t, LANES), lambda b, p, i: (b * nq + i, p)),
        out_shape=jax.ShapeDtypeStruct((batch * seq, B_WIDTH), BF16),
        scratch_shapes=[
            pltpu.VMEM((2, seq, LANES), BF16),
            pltpu.VMEM((2, seq // FOX_H, LANES, FOX_H), BF16),
            pltpu.VMEM((2, 2, FOX_H, t), F32),
            pltpu.VMEM((2, 2, 1, t), F32),
            pltpu.VMEM((2, 2, FOX_H, t), BF16),
            pltpu.VMEM((2, 2, 1, t), F32),
            pltpu.VMEM((2, 1, t), F32),
            pltpu.VMEM((2, LANES, t), F32),
        ],
        compiler_params=pltpu.CompilerParams(
            dimension_semantics=("arbitrary", "arbitrary", "arbitrary"),
            vmem_limit_bytes=VMEM_LIMIT),
        name="fox",
    )(proj, proj, proj, aug)


MEM_TM = 512


def _mem_attn_kernel(q_ref, mk_ref, mv_ref, o_ref):
    for h in range(C_HEADS):
        sl = slice(h * C_HEAD_DIM, (h + 1) * C_HEAD_DIM)
        s = _dot_nt(q_ref[:, sl], mk_ref[0, :, sl])
        m = jnp.max(s, axis=-1, keepdims=True)
        p = jnp.exp(s - m)
        denom = jnp.sum(p, axis=-1, keepdims=True)
        o = jnp.dot(p.astype(BF16), mv_ref[0, :, sl], preferred_element_type=F32)
        o_ref[:, sl] = (o / denom).astype(BF16)


def _mem_attn(proj, mk, mv, batch, seq):
    nt = seq // MEM_TM
    mlen = mk.shape[1]
    qcol = _col_block("qC", C_WIDTH)
    return pl.pallas_call(
        _mem_attn_kernel,
        grid=(batch * nt,),
        in_specs=[
            pl.BlockSpec((MEM_TM, C_WIDTH), lambda i: (i, qcol)),
            pl.BlockSpec((1, mlen, C_WIDTH), lambda i: (i // nt, 0, 0)),
            pl.BlockSpec((1, mlen, C_WIDTH), lambda i: (i // nt, 0, 0)),
        ],
        out_specs=pl.BlockSpec((MEM_TM, C_WIDTH), lambda i: (i, 0)),
        out_shape=jax.ShapeDtypeStruct((batch * seq, C_WIDTH), BF16),
        compiler_params=pltpu.CompilerParams(
            dimension_semantics=("arbitrary",), vmem_limit_bytes=VMEM_LIMIT),
        name="mem_attn",
    )(proj, mk, mv)


MERGE_TM = 256


def _merge_kernel(x_ref, ya_ref, za_ref, yb_ref, zb_ref, yc_ref, zc_ref,
                  g0_ref, g1_ref, g2_ref, wa_ref, wb_ref, wc_ref, wo_ref, o_ref):
    def branch(y_ref, z_ref, w_ref):
        z = z_ref[...].astype(F32)
        h = (y_ref[...].astype(F32) * (z * _sigmoid(z))).astype(BF16)
        return jnp.dot(h, w_ref[...], preferred_element_type=F32)

    y = _sigmoid(g0_ref[...].astype(F32)) * branch(ya_ref, za_ref, wa_ref)
    y = y + _sigmoid(g1_ref[...].astype(F32)) * branch(yb_ref, zb_ref, wb_ref)
    y = y + _sigmoid(g2_ref[...].astype(F32)) * branch(yc_ref, zc_ref, wc_ref)
    o_ref[...] = x_ref[...] + jnp.dot(y.astype(BF16), wo_ref[...], preferred_element_type=F32)


def _merge(x2, ya, yb, yc, proj, wa, wb, wc, wo):
    t, d = x2.shape
    tm = MERGE_TM
    gcol = _col_block("g", D_MODEL)
    row = lambda width, col: pl.BlockSpec((tm, width), lambda i: (i, col))
    full = lambda a: pl.BlockSpec(a.shape, lambda i: (0, 0), pipeline_mode=pl.Buffered(1))
    return pl.pallas_call(
        _merge_kernel,
        grid=(t // tm,),
        in_specs=[
            row(d, 0),
            row(A_WIDTH, 0), row(A_WIDTH, _col_block("zA", A_WIDTH)),
            row(B_WIDTH, 0), row(B_WIDTH, _col_block("zB", B_WIDTH)),
            row(C_WIDTH, 0), row(C_WIDTH, _col_block("zC", C_WIDTH)),
            row(d, gcol), row(d, gcol + 1), row(d, gcol + 2),
            full(wa), full(wb), full(wc), full(wo),
        ],
        out_specs=pl.BlockSpec((tm, d), lambda i: (i, 0)),
        out_shape=jax.ShapeDtypeStruct((t, d), F32),
        compiler_params=pltpu.CompilerParams(
            dimension_semantics=("arbitrary",), vmem_limit_bytes=VMEM_LIMIT),
        name="merge",
    )(x2, ya, proj, yb, proj, yc, proj, proj, proj, proj, wa, wb, wc, wo)


def _group_mean_matrices():
    lane = np.arange(LANES)
    m64 = (lane[:, None] // HEAD_DIM == lane[None, :] // HEAD_DIM).astype(np.float32) / HEAD_DIM
    m128 = np.full((LANES, LANES), 1.0 / C_HEAD_DIM, np.float32)
    n64 = _DST["qC"] // LANES
    n128 = C_WIDTH // LANES
    return np.stack([m64] * n64 + [m128] * n128)


def _layer(x, mem, norm_gain, mem_norm_gain, w_in, b_forget, q_gain_a, k_gain_a, sinks_a,
           q_gain_b, k_gain_b, q_gain_c, k_gain_c, w_mem_kv, w_branch_a, w_branch_b,
           w_branch_c, w_out):
    batch, seq, d = x.shape
    x2 = x.reshape(batch * seq, d)

    w_main = jnp.concatenate(
        [w_in[:, _SRC[n][0]:_SRC[n][0] + _SRC[n][1]] for n in _ORDER], axis=1).astype(BF16)
    f0 = _SRC["fB"][0]
    w_f = jnp.pad(w_in[:, f0:f0 + B_HEADS], ((0, 0), (0, LANES - B_HEADS))).astype(BF16)
    b_f = jnp.pad(b_forget.astype(F32), (0, LANES - B_HEADS)).reshape(1, LANES)
    tile_gain = lambda g, width, scale: jnp.tile(g.astype(F32) * scale, width // g.shape[0])
    cscale = jnp.concatenate([
        tile_gain(q_gain_a, A_WIDTH, HEAD_DIM ** -0.5),
        tile_gain(q_gain_b, B_WIDTH, HEAD_DIM ** -0.5 * LOG2E),
        tile_gain(k_gain_b, B_WIDTH, 1.0),
        tile_gain(k_gain_a, A_KV_WIDTH, 1.0),
        tile_gain(q_gain_c, C_WIDTH, C_HEAD_DIM ** -0.5),
    ]).reshape(1, NORM_WIDTH)
    gmat = jnp.asarray(_group_mean_matrices(), BF16)
    place = jnp.asarray(_placement_matrix(), BF16)

    mk, mv = _mem_kv(mem, mem_norm_gain.reshape(1, d), w_mem_kv.astype(BF16),
                     k_gain_c.reshape(1, C_HEAD_DIM))
    proj, log_f = _in_proj(x2, norm_gain.reshape(1, d), w_main, w_f, b_f, cscale, gmat)
    aug = _fox_bias(log_f, place, batch, seq)
    ya = _swa(sinks_a.astype(F32), proj, batch, seq)
    yb = _fox(proj, aug, batch, seq)
    yc = _mem_attn(proj, mk, mv, batch, seq)
    out = _merge(x2, ya, yb, yc, proj, w_branch_a.astype(BF16), w_branch_b.astype(BF16),
                 w_branch_c.astype(BF16), w_out.astype(BF16))
    return out.reshape(batch, seq, d)


def kernel(x, mem, norm_gain, mem_norm_gain, w_in, b_forget, q_gain_a, k_gain_a, sinks_a,
           q_gain_b, k_gain_b, q_gain_c, k_gain_c, w_mem_kv, w_branch_a, w_branch_b,
           w_branch_c, w_out):
    for layer in range(norm_gain.shape[0]):
        x = _layer(x, mem, norm_gain[layer], mem_norm_gain[layer], w_in[layer], b_forget[layer],
                   q_gain_a[layer], k_gain_a[layer], sinks_a[layer], q_gain_b[layer],
                   k_gain_b[layer], q_gain_c[layer], k_gain_c[layer], w_mem_kv[layer],
                   w_branch_a[layer], w_branch_b[layer], w_branch_c[layer], w_out[layer])
    return x
```

```python
import functools

import jax
import jax.numpy as jnp
import numpy as np
from jax import lax
from jax.experimental import pallas as pl
from jax.experimental.pallas import tpu as pltpu

F32 = jnp.float32
BF16 = jnp.bfloat16

D_MODEL = 2048
HEAD_DIM = 64
A_Q_HEADS = 12
A_KV_HEADS = 4
A_GROUP = A_Q_HEADS // A_KV_HEADS
WINDOW = 128
B_HEADS = 12
C_HEADS = 4
C_HEAD_DIM = 128
A_WIDTH = A_Q_HEADS * HEAD_DIM
A_KV_WIDTH = A_KV_HEADS * HEAD_DIM
B_WIDTH = B_HEADS * HEAD_DIM
C_WIDTH = C_HEADS * C_HEAD_DIM
EPS = 1e-6
NEG = -1e30

LANES = 128
HALF = LANES // 2
VMEM_LIMIT = 56 * 1024 * 1024

_SRC = {}
_off = 0
for _name, _w in (("qA", A_WIDTH), ("kA", A_KV_WIDTH), ("vA", A_KV_WIDTH), ("zA", A_WIDTH),
                  ("qB", B_WIDTH), ("kB", B_WIDTH), ("vB", B_WIDTH), ("zB", B_WIDTH),
                  ("fB", B_HEADS), ("qC", C_WIDTH), ("zC", C_WIDTH), ("g", 3 * D_MODEL)):
    _SRC[_name] = (_off, _w)
    _off += _w

F_START, F_SHIFT = _SRC["fB"]
_DST = {n: (o if o < F_START else o - F_SHIFT) for n, (o, _) in _SRC.items() if n != "fB"}
PROJ_WIDTH = _SRC["g"][0] + _SRC["g"][1] - F_SHIFT
assert all(o % LANES == 0 for o in _DST.values()) and F_START % LANES == 0


def _col_block(name, width, part=0):
    assert _DST[name] % width == 0
    return _DST[name] // width + part


def _rms(x, gain):
    ms = jnp.mean(x * x, axis=-1, keepdims=True)
    return x * lax.rsqrt(ms + EPS) * gain


def _sigmoid(t):
    return 0.5 * (jnp.tanh(0.5 * t) + 1.0)


def _dot_nt(a, b):
    return lax.dot_general(a, b, (((1,), (1,)), ((), ())), preferred_element_type=F32)


def _mem_kv_kernel(mem_ref, gain_ref, w_ref, kgain_ref, mk_ref, mv_ref):
    hn = _rms(mem_ref[0], gain_ref[...]).astype(BF16)
    kv = jnp.dot(hn, w_ref[...], preferred_element_type=F32)
    for h in range(C_HEADS):
        sl = slice(h * C_HEAD_DIM, (h + 1) * C_HEAD_DIM)
        mk_ref[0, :, sl] = _rms(kv[:, sl], kgain_ref[...]).astype(BF16)
    mv_ref[0] = kv[:, C_WIDTH:].astype(BF16)


def _mem_kv(mem, gain, w, kgain):
    b, m, d = mem.shape
    return pl.pallas_call(
        _mem_kv_kernel,
        grid=(b,),
        in_specs=[
            pl.BlockSpec((1, m, d), lambda i: (i, 0, 0)),
            pl.BlockSpec((1, d), lambda i: (0, 0)),
            pl.BlockSpec((d, 2 * C_WIDTH), lambda i: (0, 0)),
            pl.BlockSpec((1, C_HEAD_DIM), lambda i: (0, 0)),
        ],
        out_specs=[
            pl.BlockSpec((1, m, C_WIDTH), lambda i: (i, 0, 0)),
            pl.BlockSpec((1, m, C_WIDTH), lambda i: (i, 0, 0)),
        ],
        out_shape=[jax.ShapeDtypeStruct((b, m, C_WIDTH), BF16)] * 2,
        compiler_params=pltpu.CompilerParams(
            dimension_semantics=("arbitrary",), vmem_limit_bytes=VMEM_LIMIT),
        name="mem_kv",
    )(mem, gain, w, kgain)


NX_TM = 512


def _norm_x_kernel(x_ref, gain_ref, wf_ref, bf_ref, hn_ref, lf_ref):
    hn = _rms(x_ref[...], gain_ref[...]).astype(BF16)
    hn_ref[...] = hn
    f_logit = jnp.dot(hn, wf_ref[...], preferred_element_type=F32) + bf_ref[...]
    lf_ref[...] = jnp.minimum(f_logit, 0.0) - jnp.log1p(jnp.exp(-jnp.abs(f_logit)))


def _norm_x(x2, gain, wf, bfg):
    t, d = x2.shape
    return pl.pallas_call(
        _norm_x_kernel,
        grid=(t // NX_TM,),
        in_specs=[
            pl.BlockSpec((NX_TM, d), lambda i: (i, 0)),
            pl.BlockSpec((1, d), lambda i: (0, 0)),
            pl.BlockSpec((d, LANES), lambda i: (0, 0)),
            pl.BlockSpec((1, LANES), lambda i: (0, 0)),
        ],
        out_specs=[
            pl.BlockSpec((NX_TM, d), lambda i: (i, 0)),
            pl.BlockSpec((NX_TM, LANES), lambda i: (i, 0)),
        ],
        out_shape=[jax.ShapeDtypeStruct((t, d), BF16),
                   jax.ShapeDtypeStruct((t, LANES), F32)],
        compiler_params=pltpu.CompilerParams(
            dimension_semantics=("arbitrary",), vmem_limit_bytes=VMEM_LIMIT),
        name="norm_x",
    )(x2, gain, wf, bfg)


IN_TM = 1024
IN_TN = 1024
IN_CHUNKS = IN_TN // LANES
IN_CAST_ROWS = 256
N_IN_TILES = PROJ_WIDTH // IN_TN
N_ALIGNED_TILES = F_START // IN_TN
assert F_START % IN_TN == 0 and PROJ_WIDTH % IN_TN == 0 and F_SHIFT < LANES

_NORM_KIND = {"qA": HEAD_DIM, "kA": HEAD_DIM, "qB": HEAD_DIM, "kB": HEAD_DIM, "qC": C_HEAD_DIM}


def _chunk_piece(col):
    for name, off in _DST.items():
        if off <= col < off + _SRC[name][1]:
            return name
    raise ValueError(col)


def _norm_prefix(tile):
    kinds = [_chunk_piece(tile * IN_TN + c * LANES) in _NORM_KIND for c in range(IN_CHUNKS)]
    n = sum(kinds)
    assert kinds == [True] * n + [False] * (IN_CHUNKS - n)
    return n


NORM_PREFIX = tuple(_norm_prefix(j) for j in range(N_IN_TILES))
NORM_TILES = tuple(j for j in range(N_IN_TILES) if NORM_PREFIX[j])


def _norm_slot(j):
    slot = 0
    for tile in NORM_TILES[1:]:
        slot = slot + (j >= tile).astype(jnp.int32)
    return slot


def _in_proj_kernel(hn_ref, w_ref, wnext_ref, cscale_ref, gmat_ref, o_ref, wbf_ref):
    j = pl.program_id(0)
    i = pl.program_id(1)
    d = w_ref.shape[0]

    @pl.when((i == 0) & (j < N_ALIGNED_TILES))
    def _():
        def cast(r, carry):
            rows = pl.ds(pl.multiple_of(r * IN_CAST_ROWS, IN_CAST_ROWS), IN_CAST_ROWS)
            wbf_ref[rows, :] = w_ref[rows, :].astype(BF16)
            return carry
        lax.fori_loop(0, d // IN_CAST_ROWS, cast, 0)

    @pl.when((i == 0) & (j >= N_ALIGNED_TILES))
    def _():
        def cast(r, carry):
            rows = pl.ds(pl.multiple_of(r * IN_CAST_ROWS, IN_CAST_ROWS), IN_CAST_ROWS)
            wide = jnp.concatenate([w_ref[rows, :], wnext_ref[rows, :]], axis=1)
            wbf_ref[rows, :] = wide[:, F_SHIFT:F_SHIFT + IN_TN].astype(BF16)
            return carry
        lax.fori_loop(0, d // IN_CAST_ROWS, cast, 0)

    acc = jnp.dot(hn_ref[...], wbf_ref[...], preferred_element_type=F32)

    def is_any(tiles):
        cond = j == tiles[0]
        for tile in tiles[1:]:
            cond = cond | (j == tile)
        return cond

    for prefix in sorted(set(NORM_PREFIX)):
        tiles = [t for t in range(N_IN_TILES) if NORM_PREFIX[t] == prefix]

        @pl.when(is_any(tiles))
        def _(prefix=prefix):
            for c in range(prefix):
                sl = slice(c * LANES, (c + 1) * LANES)
                a = acc[:, sl]
                ms = jnp.dot((a * a).astype(BF16), gmat_ref[c], preferred_element_type=F32)
                o_ref[:, sl] = (a * lax.rsqrt(ms + EPS) * cscale_ref[:, sl]).astype(BF16)
            if prefix < IN_CHUNKS:
                rest = slice(prefix * LANES, IN_TN)
                o_ref[:, rest] = acc[:, rest].astype(BF16)


def _in_proj(hn, w_in, cscale, gmat):
    t, d = hn.shape
    grid = (N_IN_TILES, t // IN_TM)
    return pl.pallas_call(
        _in_proj_kernel,
        grid=grid,
        in_specs=[
            pl.BlockSpec((IN_TM, d), lambda j, i: (i, 0)),
            pl.BlockSpec((d, IN_TN), lambda j, i: (0, j)),
            pl.BlockSpec((d, LANES), lambda j, i: (0, (j + 1) * IN_CHUNKS)),
            pl.BlockSpec((1, IN_TN), lambda j, i: (0, _norm_slot(j))),
            pl.BlockSpec((IN_CHUNKS, LANES, LANES), lambda j, i: (_norm_slot(j), 0, 0)),
        ],
        out_specs=pl.BlockSpec((IN_TM, IN_TN), lambda j, i: (i, j)),
        out_shape=jax.ShapeDtypeStruct((t, PROJ_WIDTH), BF16),
        scratch_shapes=[pltpu.VMEM((d, IN_TN), BF16)],
        compiler_params=pltpu.CompilerParams(
            dimension_semantics=("arbitrary", "arbitrary"), vmem_limit_bytes=VMEM_LIMIT),
        name="in_proj",
    )(hn, w_in, w_in, cscale, gmat)


BIAS_TS = 512
N_PIECES = 3


def _fox_bias_kernel(lf_ref, place_ref, aug_ref, carry_ref):
    i = pl.program_id(1)

    @pl.when(i == 0)
    def _():
        carry_ref[...] = jnp.zeros_like(carry_ref)

    x = lf_ref[...]
    row = lax.broadcasted_iota(jnp.int32, x.shape, 0)
    shift = 1
    while shift < BIAS_TS:
        x = x + jnp.where(row >= shift, pltpu.roll(x, shift, 0), 0.0)
        shift *= 2
    c = x + carry_ref[...]
    carry_ref[...] = c[BIAS_TS - 1:BIAS_TS, :]
    rest = c * (-LOG2E)
    pieces = []
    for _ in range(N_PIECES):
        p = rest.astype(BF16)
        pieces.append(p)
        rest = rest - p.astype(F32)
    stacked = jnp.concatenate(pieces, axis=1)
    aug_ref[...] = jnp.dot(stacked, place_ref[...], preferred_element_type=F32).astype(BF16)


def _fox_bias(lf, place, batch, seq):
    nblk = seq // BIAS_TS
    return pl.pallas_call(
        _fox_bias_kernel,
        grid=(batch, nblk),
        in_specs=[
            pl.BlockSpec((BIAS_TS, LANES), lambda b, i: (b * nblk + i, 0)),
            pl.BlockSpec((N_PIECES * LANES, B_WIDTH), lambda b, i: (0, 0)),
        ],
        out_specs=pl.BlockSpec((BIAS_TS, B_WIDTH), lambda b, i: (b * nblk + i, 0)),
        out_shape=jax.ShapeDtypeStruct((batch * seq, B_WIDTH), BF16),
        scratch_shapes=[pltpu.VMEM((1, LANES), F32)],
        compiler_params=pltpu.CompilerParams(
            dimension_semantics=("arbitrary", "arbitrary"), vmem_limit_bytes=VMEM_LIMIT),
        name="fox_bias",
    )(lf, place)


def _aug_lane(head):
    return HALF if head % 2 == 0 else 0


def _placement_matrix():
    place = np.zeros((N_PIECES * LANES, B_WIDTH), np.float32)
    for h in range(B_HEADS):
        for p in range(N_PIECES):
            place[p * LANES + h, (h // 2) * LANES + _aug_lane(h) + p] = 1.0
    return place


def _swa_kernel(sinks_ref, q_ref, kp_ref, kc_ref, vp_ref, vc_ref, o_ref):
    n = pl.program_id(1)
    blk = WINDOW
    qi = lax.broadcasted_iota(jnp.int32, (blk, 2 * blk), 0)
    kj = lax.broadcasted_iota(jnp.int32, (blk, 2 * blk), 1)
    rel = qi + blk - kj
    key_pos = (n - 1) * blk + kj
    valid = (rel >= 0) & (rel < WINDOW) & (key_pos >= 0)
    relf = rel.astype(F32)

    kcat = jnp.concatenate([kp_ref[...].astype(F32), kc_ref[...].astype(F32)], axis=0)
    vcat = jnp.concatenate([vp_ref[...].astype(F32), vc_ref[...].astype(F32)], axis=0).astype(BF16)
    lane_k = lax.broadcasted_iota(jnp.int32, (2 * blk, LANES), 1)
    lane_o = lax.broadcasted_iota(jnp.int32, (blk, LANES), 1)

    k_same, k_swap = [], []
    for g in range(A_KV_HEADS):
        slab = kcat[:, (g // 2) * LANES:(g // 2 + 1) * LANES]
        own = (lane_k < HALF) if g % 2 == 0 else (lane_k >= HALF)
        kz = jnp.where(own, slab, 0.0)
        k_same.append(kz.astype(BF16))
        k_swap.append(pltpu.roll(kz, HALF, 1).astype(BF16))

    outs = []
    for h in range(A_Q_HEADS):
        g = h // A_GROUP
        same = (h % 2) == (g % 2)
        q_slab = q_ref[:, (h // 2) * LANES:(h // 2 + 1) * LANES]
        s = _dot_nt(q_slab, k_same[g] if same else k_swap[g])
        slope = float(2.0 ** (-8.0 * (h + 1) / A_Q_HEADS))
        s = jnp.where(valid, s - slope * relf, NEG)
        sink = sinks_ref[h]
        m = jnp.maximum(jnp.max(s, axis=-1, keepdims=True), sink)
        p = jnp.exp(s - m)
        denom = jnp.sum(p, axis=-1, keepdims=True) + jnp.exp(sink - m)
        v_slab = vcat[:, (g // 2) * LANES:(g // 2 + 1) * LANES]
        o = jnp.dot(p.astype(BF16), v_slab, preferred_element_type=F32) / denom
        outs.append(o if same else pltpu.roll(o, HALF, 1))
    for pr in range(A_Q_HEADS // 2):
        o_ref[:, pr * LANES:(pr + 1) * LANES] = jnp.where(
            lane_o < HALF, outs[2 * pr], outs[2 * pr + 1]).astype(BF16)


def _swa(sinks, proj, batch, seq):
    blk = WINDOW
    nb = seq // blk
    qcol = _col_block("qA", A_WIDTH)
    kcol = _col_block("kA", A_KV_WIDTH)
    vcol = _col_block("vA", A_KV_WIDTH)
    cur = lambda b, n: b * nb + n
    prev = lambda b, n: b * nb + jnp.maximum(n - 1, 0)
    return pl.pallas_call(
        _swa_kernel,
        grid=(batch, nb),
        in_specs=[
            pl.BlockSpec(memory_space=pltpu.SMEM),
            pl.BlockSpec((blk, A_WIDTH), lambda b, n: (cur(b, n), qcol)),
            pl.BlockSpec((blk, A_KV_WIDTH), lambda b, n: (prev(b, n), kcol)),
            pl.BlockSpec((blk, A_KV_WIDTH), lambda b, n: (cur(b, n), kcol)),
            pl.BlockSpec((blk, A_KV_WIDTH), lambda b, n: (prev(b, n), vcol)),
            pl.BlockSpec((blk, A_KV_WIDTH), lambda b, n: (cur(b, n), vcol)),
        ],
        out_specs=pl.BlockSpec((blk, A_WIDTH), lambda b, n: (cur(b, n), 0)),
        out_shape=jax.ShapeDtypeStruct((batch * seq, A_WIDTH), BF16),
        compiler_params=pltpu.CompilerParams(
            dimension_semantics=("arbitrary", "arbitrary"), vmem_limit_bytes=VMEM_LIMIT),
        name="swa",
    )(sinks, proj, proj, proj, proj, proj)


FOX_T = 512
LOG2E = float(np.log2(np.e))


FOX_H = FOX_T // 2


def _fox_kernel(q_ref, k_ref, v_ref, aug_ref, o_ref,
                kop_ref, vt_ref, s_ref, cm_ref, p_ref, al_ref, m_ref, acc_ref):
    qi = pl.program_id(2)
    seq = k_ref.shape[0]
    t = FOX_T
    hk = FOX_H

    @pl.when(qi == 0)
    def _():
        lane = lax.broadcasted_iota(jnp.int32, (hk, LANES), 1)
        low = lane < HALF

        def build(r, carry):
            sl = pl.ds(pl.multiple_of(r * hk, hk), hk)
            kk = k_ref[sl, :].astype(F32)
            aa = aug_ref[sl, :].astype(F32)
            vv = v_ref[sl, :].astype(F32)
            kop_ref[0, sl, :] = jnp.where(low, kk, aa).astype(BF16)
            kop_ref[1, sl, :] = jnp.where(low, aa, kk).astype(BF16)
            v0 = jnp.where(low, vv, jnp.where(lane == HALF, 1.0, 0.0))
            v1 = jnp.where(low, jnp.where(lane == 0, 1.0, 0.0), vv)
            vt_ref[0, r] = v0.T.astype(BF16)
            vt_ref[1, r] = v1.T.astype(BF16)
            return carry

        lax.fori_loop(0, seq // hk, build, 0)

    lane_q = lax.broadcasted_iota(jnp.int32, (t, LANES), 1)
    low_q = lane_q < HALF
    q = q_ref[...].astype(F32)
    ones0 = jnp.where((lane_q >= HALF) & (lane_q < HALF + N_PIECES), 1.0, 0.0)
    ones1 = jnp.where(lane_q < N_PIECES, 1.0, 0.0)
    qop = (jnp.where(low_q, q, ones0).astype(BF16), jnp.where(low_q, ones1, q).astype(BF16))

    m_ref[...] = jnp.full(m_ref.shape, NEG, F32)
    acc_ref[...] = jnp.zeros(acc_ref.shape, F32)

    def scores(slot, half, masked):
        ks = pl.ds(pl.multiple_of(half * hk, hk), hk)
        for hh in range(2):
            st = _dot_nt(kop_ref[hh, ks, :], qop[hh])
            if masked:
                key = lax.broadcasted_iota(jnp.int32, (hk, t), 0) + slot * hk
                qry = lax.broadcasted_iota(jnp.int32, (hk, t), 1)
                st = jnp.where(key <= qry, st, NEG)
            s_ref[slot, hh] = st
            cm_ref[slot, hh] = jnp.max(st, axis=0, keepdims=True)

    def soft(slot):
        for hh in range(2):
            m_old = m_ref[hh]
            m_new = jnp.maximum(m_old, cm_ref[slot, hh])
            al_ref[slot, hh] = jnp.exp2(m_old - m_new)
            p_ref[slot, hh] = jnp.exp2(s_ref[slot, hh] - m_new).astype(BF16)
            m_ref[hh] = m_new

    def pv(slot, half):
        for hh in range(2):
            acc_ref[hh] = acc_ref[hh] * al_ref[slot, hh] + jnp.dot(
                vt_ref[hh, half], p_ref[slot, hh], preferred_element_type=F32)

    scores(0, 2 * qi, True)
    scores(1, 2 * qi + 1, True)
    soft(0)
    soft(1)

    @pl.when(qi > 0)
    def _():
        scores(0, 0, False)

    def body(i, carry):
        prev = jnp.where(i == 0, qi, i - 1)
        pv(0, 2 * prev)
        soft(0)
        scores(1, 2 * i + 1, False)
        scores(0, 2 * i + 2, False)
        pv(1, 2 * prev + 1)
        soft(1)
        return carry

    lax.fori_loop(0, qi, body, 0)
    last = jnp.where(qi == 0, qi, qi - 1)
    pv(0, 2 * last)
    pv(1, 2 * last + 1)

    a0 = acc_ref[0]
    a1 = acc_ref[1]
    feat = lax.broadcasted_iota(jnp.int32, (LANES, t), 0)
    ot = jnp.where(feat < HALF, a0 / a0[HALF:HALF + 1, :], a1 / a1[0:1, :])
    o_ref[...] = ot.T.astype(BF16)


def _fox(proj, aug, batch, seq):
    t = FOX_T
    nq = seq // t
    pairs = B_HEADS // 2
    qcol = _col_block("qB", LANES)
    kcol = _col_block("kB", LANES)
    vcol = _col_block("vB", LANES)
    return pl.pallas_call(
        _fox_kernel,
        grid=(batch, pairs, nq),
        in_specs=[
            pl.BlockSpec((t, LANES), lambda b, p, i: (b * nq + i, qcol + p)),
            pl.BlockSpec((seq, LANES), lambda b, p, i: (b, kcol + p)),
            pl.BlockSpec((seq, LANES), lambda b, p, i: (b, vcol + p)),
            pl.BlockSpec((seq, LANES), lambda b, p, i: (b, p)),
        ],
        out_specs=pl.BlockSpec((t, LANES), lambda b, p, i: (b * nq + i, p)),
        out_shape=jax.ShapeDtypeStruct((batch * seq, B_WIDTH), BF16),
        scratch_shapes=[
            pltpu.VMEM((2, seq, LANES), BF16),
            pltpu.VMEM((2, seq // FOX_H, LANES, FOX_H), BF16),
            pltpu.VMEM((2, 2, FOX_H, t), F32),
            pltpu.VMEM((2, 2, 1, t), F32),
            pltpu.VMEM((2, 2, FOX_H, t), BF16),
            pltpu.VMEM((2, 2, 1, t), F32),
            pltpu.VMEM((2, 1, t), F32),
            pltpu.VMEM((2, LANES, t), F32),
        ],
        compiler_params=pltpu.CompilerParams(
            dimension_semantics=("arbitrary", "arbitrary", "arbitrary"),
            vmem_limit_bytes=VMEM_LIMIT),
        name="fox",
    )(proj, proj, proj, aug)


MEM_TM = 512


def _mem_attn_kernel(q_ref, mk_ref, mv_ref, o_ref):
    for h in range(C_HEADS):
        sl = slice(h * C_HEAD_DIM, (h + 1) * C_HEAD_DIM)
        s = _dot_nt(q_ref[:, sl], mk_ref[0, :, sl])
        m = jnp.max(s, axis=-1, keepdims=True)
        p = jnp.exp(s - m)
        denom = jnp.sum(p, axis=-1, keepdims=True)
        o = jnp.dot(p.astype(BF16), mv_ref[0, :, sl], preferred_element_type=F32)
        o_ref[:, sl] = (o / denom).astype(BF16)


def _mem_attn(proj, mk, mv, batch, seq):
    nt = seq // MEM_TM
    mlen = mk.shape[1]
    qcol = _col_block("qC", C_WIDTH)
    return pl.pallas_call(
        _mem_attn_kernel,
        grid=(batch * nt,),
        in_specs=[
            pl.BlockSpec((MEM_TM, C_WIDTH), lambda i: (i, qcol)),
            pl.BlockSpec((1, mlen, C_WIDTH), lambda i: (i // nt, 0, 0)),
            pl.BlockSpec((1, mlen, C_WIDTH), lambda i: (i // nt, 0, 0)),
        ],
        out_specs=pl.BlockSpec((MEM_TM, C_WIDTH), lambda i: (i, 0)),
        out_shape=jax.ShapeDtypeStruct((batch * seq, C_WIDTH), BF16),
        compiler_params=pltpu.CompilerParams(
            dimension_semantics=("arbitrary",), vmem_limit_bytes=VMEM_LIMIT),
        name="mem_attn",
    )(proj, mk, mv)


MERGE_TM = 256


Z_BLOCK = 256
Z_PARTS = A_WIDTH // Z_BLOCK
assert A_WIDTH == B_WIDTH and A_WIDTH % Z_BLOCK == 0


def _merge_kernel(x_ref, ya_ref, yb_ref, yc_ref, *refs):
    za_refs = refs[:Z_PARTS]
    zb_refs = refs[Z_PARTS:2 * Z_PARTS]
    zc_ref, g0_ref, g1_ref, g2_ref, wa_ref, wb_ref, wc_ref, wo_ref, o_ref = refs[2 * Z_PARTS:]

    def branch(y_ref, z_refs, w_ref):
        z = jnp.concatenate([r[...] for r in z_refs], axis=1).astype(F32)
        h = (y_ref[...].astype(F32) * (z * _sigmoid(z))).astype(BF16)
        return jnp.dot(h, w_ref[...], preferred_element_type=F32)

    y = _sigmoid(g0_ref[...].astype(F32)) * branch(ya_ref, za_refs, wa_ref)
    y = y + _sigmoid(g1_ref[...].astype(F32)) * branch(yb_ref, zb_refs, wb_ref)
    y = y + _sigmoid(g2_ref[...].astype(F32)) * branch(yc_ref, (zc_ref,), wc_ref)
    o_ref[...] = x_ref[...] + jnp.dot(y.astype(BF16), wo_ref[...], preferred_element_type=F32)


def _merge(x2, ya, yb, yc, proj, wa, wb, wc, wo):
    t, d = x2.shape
    tm = MERGE_TM
    gcol = _col_block("g", D_MODEL)
    row = lambda width, col: pl.BlockSpec((tm, width), lambda i: (i, col))
    full = lambda a: pl.BlockSpec(a.shape, lambda i: (0, 0), pipeline_mode=pl.Buffered(1))
    z_specs = lambda name: [row(Z_BLOCK, _col_block(name, Z_BLOCK, part)) for part in range(Z_PARTS)]
    n_proj = 2 * Z_PARTS + 4
    return pl.pallas_call(
        _merge_kernel,
        grid=(t // tm,),
        in_specs=[
            row(d, 0), row(A_WIDTH, 0), row(B_WIDTH, 0), row(C_WIDTH, 0),
            *z_specs("zA"), *z_specs("zB"), row(C_WIDTH, _col_block("zC", C_WIDTH)),
            row(d, gcol), row(d, gcol + 1), row(d, gcol + 2),
            full(wa), full(wb), full(wc), full(wo),
        ],
        out_specs=pl.BlockSpec((tm, d), lambda i: (i, 0)),
        out_shape=jax.ShapeDtypeStruct((t, d), F32),
        compiler_params=pltpu.CompilerParams(
            dimension_semantics=("arbitrary",), vmem_limit_bytes=VMEM_LIMIT),
        name="merge",
    )(x2, ya, yb, yc, *([proj] * n_proj), wa, wb, wc, wo)


def _group_mean_matrices():
    lane = np.arange(LANES)
    mats = []
    for tile in NORM_TILES:
        for c in range(IN_CHUNKS):
            dim = _NORM_KIND.get(_chunk_piece(tile * IN_TN + c * LANES), LANES)
            mats.append((lane[:, None] // dim == lane[None, :] // dim).astype(np.float32) / dim)
    return np.stack(mats)


def _column_scales(gains):
    scale = {"qA": HEAD_DIM ** -0.5, "qB": HEAD_DIM ** -0.5 * LOG2E, "qC": C_HEAD_DIM ** -0.5}
    cols = []
    for tile in NORM_TILES:
        for c in range(IN_CHUNKS):
            piece = _chunk_piece(tile * IN_TN + c * LANES)
            if piece in _NORM_KIND:
                g = gains[piece].astype(F32) * scale.get(piece, 1.0)
                cols.append(jnp.tile(g, LANES // g.shape[0]))
            else:
                cols.append(jnp.zeros((LANES,), F32))
    return jnp.concatenate(cols).reshape(1, len(NORM_TILES) * IN_TN)


def _layer(x, mem, norm_gain, mem_norm_gain, w_in, b_forget, q_gain_a, k_gain_a, sinks_a,
           q_gain_b, k_gain_b, q_gain_c, k_gain_c, w_mem_kv, w_branch_a, w_branch_b,
           w_branch_c, w_out):
    batch, seq, d = x.shape
    x2 = x.reshape(batch * seq, d)

    w_f = jnp.pad(w_in[:, F_START:F_START + F_SHIFT], ((0, 0), (0, LANES - F_SHIFT))).astype(BF16)
    b_f = jnp.pad(b_forget.astype(F32), (0, LANES - F_SHIFT)).reshape(1, LANES)
    cscale = _column_scales({"qA": q_gain_a, "kA": k_gain_a, "qB": q_gain_b, "kB": k_gain_b,
                             "qC": q_gain_c})
    gmat = jnp.asarray(_group_mean_matrices(), BF16)
    place = jnp.asarray(_placement_matrix(), BF16)

    mk, mv = _mem_kv(mem, mem_norm_gain.reshape(1, d), w_mem_kv.astype(BF16),
                     k_gain_c.reshape(1, C_HEAD_DIM))
    hn, log_f = _norm_x(x2, norm_gain.reshape(1, d), w_f, b_f)
    proj = _in_proj(hn, w_in, cscale, gmat)
    aug = _fox_bias(log_f, place, batch, seq)
    ya = _swa(sinks_a.astype(F32), proj, batch, seq)
    yb = _fox(proj, aug, batch, seq)
    yc = _mem_attn(proj, mk, mv, batch, seq)
    out = _merge(x2, ya, yb, yc, proj, w_branch_a.astype(BF16), w_branch_b.astype(BF16),
                 w_branch_c.astype(BF16), w_out.astype(BF16))
    return out.reshape(batch, seq, d)


def kernel(x, mem, norm_gain, mem_norm_gain, w_in, b_forget, q_gain_a, k_gain_a, sinks_a,
           q_gain_b, k_gain_b, q_gain_c, k_gain_c, w_mem_kv, w_branch_a, w_branch_b,
           w_branch_c, w_out):
    for layer in range(norm_gain.shape[0]):
        x = _layer(x, mem, norm_gain[layer], mem_norm_gain[layer], w_in[layer], b_forget[layer],
                   q_gain_a[layer], k_gain_a[layer], sinks_a[layer], q_gain_b[layer],
                   k_gain_b[layer], q_gain_c[layer], k_gain_c[layer], w_mem_kv[layer],
                   w_branch_a[layer], w_branch_b[layer], w_branch_c[layer], w_out[layer])
    return x
```

```python
import functools

import jax
import jax.numpy as jnp
import numpy as np
from jax import lax
from jax.experimental import pallas as pl
from jax.experimental.pallas import tpu as pltpu

F32 = jnp.float32
BF16 = jnp.bfloat16

D_MODEL = 2048
HEAD_DIM = 64
A_Q_HEADS = 12
A_KV_HEADS = 4
A_GROUP = A_Q_HEADS // A_KV_HEADS
WINDOW = 128
B_HEADS = 12
C_HEADS = 4
C_HEAD_DIM = 128
A_WIDTH = A_Q_HEADS * HEAD_DIM
A_KV_WIDTH = A_KV_HEADS * HEAD_DIM
B_WIDTH = B_HEADS * HEAD_DIM
C_WIDTH = C_HEADS * C_HEAD_DIM
EPS = 1e-6
NEG = -1e30

LANES = 128
HALF = LANES // 2
VMEM_LIMIT = 56 * 1024 * 1024

_SRC = {}
_off = 0
for _name, _w in (("qA", A_WIDTH), ("kA", A_KV_WIDTH), ("vA", A_KV_WIDTH), ("zA", A_WIDTH),
                  ("qB", B_WIDTH), ("kB", B_WIDTH), ("vB", B_WIDTH), ("zB", B_WIDTH),
                  ("fB", B_HEADS), ("qC", C_WIDTH), ("zC", C_WIDTH), ("g", 3 * D_MODEL)):
    _SRC[_name] = (_off, _w)
    _off += _w

F_START, F_SHIFT = _SRC["fB"]
_DST = {n: (o if o < F_START else o - F_SHIFT) for n, (o, _) in _SRC.items() if n != "fB"}
PROJ_WIDTH = _SRC["g"][0] + _SRC["g"][1] - F_SHIFT
assert all(o % LANES == 0 for o in _DST.values()) and F_START % LANES == 0


def _col_block(name, width, part=0):
    assert _DST[name] % width == 0
    return _DST[name] // width + part


def _rms(x, gain):
    ms = jnp.mean(x * x, axis=-1, keepdims=True)
    return x * lax.rsqrt(ms + EPS) * gain


def _sigmoid(t):
    return 0.5 * (jnp.tanh(0.5 * t) + 1.0)


def _dot_nt(a, b):
    return lax.dot_general(a, b, (((1,), (1,)), ((), ())), preferred_element_type=F32)


def _mem_kv_kernel(mem_ref, gain_ref, w_ref, kgain_ref, mk_ref, mv_ref):
    hn = _rms(mem_ref[0], gain_ref[...]).astype(BF16)
    kv = jnp.dot(hn, w_ref[...], preferred_element_type=F32)
    for h in range(C_HEADS):
        sl = slice(h * C_HEAD_DIM, (h + 1) * C_HEAD_DIM)
        mk_ref[0, :, sl] = _rms(kv[:, sl], kgain_ref[...]).astype(BF16)
    mv_ref[0] = kv[:, C_WIDTH:].astype(BF16)


def _mem_kv(mem, gain, w, kgain):
    b, m, d = mem.shape
    return pl.pallas_call(
        _mem_kv_kernel,
        grid=(b,),
        in_specs=[
            pl.BlockSpec((1, m, d), lambda i: (i, 0, 0)),
            pl.BlockSpec((1, d), lambda i: (0, 0)),
            pl.BlockSpec((d, 2 * C_WIDTH), lambda i: (0, 0)),
            pl.BlockSpec((1, C_HEAD_DIM), lambda i: (0, 0)),
        ],
        out_specs=[
            pl.BlockSpec((1, m, C_WIDTH), lambda i: (i, 0, 0)),
            pl.BlockSpec((1, m, C_WIDTH), lambda i: (i, 0, 0)),
        ],
        out_shape=[jax.ShapeDtypeStruct((b, m, C_WIDTH), BF16)] * 2,
        compiler_params=pltpu.CompilerParams(
            dimension_semantics=("arbitrary",), vmem_limit_bytes=VMEM_LIMIT),
        name="mem_kv",
    )(mem, gain, w, kgain)


NX_TM = 512


def _norm_x_kernel(x_ref, gain_ref, wf_ref, bf_ref, hn_ref, lf_ref):
    hn = _rms(x_ref[...], gain_ref[...]).astype(BF16)
    hn_ref[...] = hn
    f_logit = _dot_nt(hn, wf_ref[...]) + bf_ref[...]
    lf_ref[...] = jnp.minimum(f_logit, 0.0) - jnp.log1p(jnp.exp(-jnp.abs(f_logit)))


def _norm_x(x2, gain, wf, bfg):
    t, d = x2.shape
    return pl.pallas_call(
        _norm_x_kernel,
        grid=(t // NX_TM,),
        in_specs=[
            pl.BlockSpec((NX_TM, d), lambda i: (i, 0)),
            pl.BlockSpec((1, d), lambda i: (0, 0)),
            pl.BlockSpec((LANES, d), lambda i: (0, 0)),
            pl.BlockSpec((1, LANES), lambda i: (0, 0)),
        ],
        out_specs=[
            pl.BlockSpec((NX_TM, d), lambda i: (i, 0)),
            pl.BlockSpec((NX_TM, LANES), lambda i: (i, 0)),
        ],
        out_shape=[jax.ShapeDtypeStruct((t, d), BF16),
                   jax.ShapeDtypeStruct((t, LANES), F32)],
        compiler_params=pltpu.CompilerParams(
            dimension_semantics=("arbitrary",), vmem_limit_bytes=VMEM_LIMIT),
        name="norm_x",
    )(x2, gain, wf, bfg)


IN_TM = 1024
IN_TN = 1024
IN_CHUNKS = IN_TN // LANES
K_CHUNKS = D_MODEL // LANES
N_IN_TILES = PROJ_WIDTH // IN_TN
N_ALIGNED_TILES = F_START // IN_TN
assert F_START % IN_TN == 0 and PROJ_WIDTH % IN_TN == 0 and F_SHIFT < LANES

_NORM_KIND = {"qA": HEAD_DIM, "kA": HEAD_DIM, "qB": HEAD_DIM, "kB": HEAD_DIM, "qC": C_HEAD_DIM}


def _chunk_piece(col):
    for name, off in _DST.items():
        if off <= col < off + _SRC[name][1]:
            return name
    raise ValueError(col)


def _norm_prefix(tile):
    kinds = [_chunk_piece(tile * IN_TN + c * LANES) in _NORM_KIND for c in range(IN_CHUNKS)]
    n = sum(kinds)
    assert kinds == [True] * n + [False] * (IN_CHUNKS - n)
    return n


NORM_PREFIX = tuple(_norm_prefix(j) for j in range(N_IN_TILES))
NORM_TILES = tuple(j for j in range(N_IN_TILES) if NORM_PREFIX[j])


def _norm_slot(j):
    slot = 0
    for tile in NORM_TILES[1:]:
        slot = slot + (j >= tile).astype(jnp.int32)
    return slot


def _in_proj_kernel(hn_ref, w_ref, cscale_ref, gmat_ref, o_ref, wbf_ref):
    j = pl.program_id(0)
    i = pl.program_id(1)

    @pl.when(i == 0)
    def _():
        for c in range(K_CHUNKS):
            wbf_ref[:, c * LANES:(c + 1) * LANES] = (
                w_ref[pl.ds(c, IN_TN, stride=K_CHUNKS), :].astype(BF16))

    acc = _dot_nt(hn_ref[...], wbf_ref[...])

    def is_any(tiles):
        cond = j == tiles[0]
        for tile in tiles[1:]:
            cond = cond | (j == tile)
        return cond

    for prefix in sorted(set(NORM_PREFIX)):
        tiles = [t for t in range(N_IN_TILES) if NORM_PREFIX[t] == prefix]

        @pl.when(is_any(tiles))
        def _(prefix=prefix):
            for c in range(prefix):
                sl = slice(c * LANES, (c + 1) * LANES)
                a = acc[:, sl]
                ms = jnp.dot((a * a).astype(BF16), gmat_ref[c], preferred_element_type=F32)
                o_ref[:, sl] = (a * lax.rsqrt(ms + EPS) * cscale_ref[:, sl]).astype(BF16)
            if prefix < IN_CHUNKS:
                rest = slice(prefix * LANES, IN_TN)
                o_ref[:, rest] = acc[:, rest].astype(BF16)


def _in_proj(hn, w_rows, cscale, gmat):
    t, d = hn.shape
    grid = (N_IN_TILES, t // IN_TM)

    def w_index(j, i):
        col = j * IN_TN + jnp.where(j >= N_ALIGNED_TILES, F_SHIFT, 0)
        return (col * K_CHUNKS, 0)

    return pl.pallas_call(
        _in_proj_kernel,
        grid=grid,
        in_specs=[
            pl.BlockSpec((IN_TM, d), lambda j, i: (i, 0)),
            pl.BlockSpec((pl.Element(IN_TN * K_CHUNKS), pl.Element(LANES)), w_index),
            pl.BlockSpec((1, IN_TN), lambda j, i: (0, _norm_slot(j))),
            pl.BlockSpec((IN_CHUNKS, LANES, LANES), lambda j, i: (_norm_slot(j), 0, 0)),
        ],
        out_specs=pl.BlockSpec((IN_TM, IN_TN), lambda j, i: (i, j)),
        out_shape=jax.ShapeDtypeStruct((t, PROJ_WIDTH), BF16),
        scratch_shapes=[pltpu.VMEM((IN_TN, d), BF16)],
        compiler_params=pltpu.CompilerParams(
            dimension_semantics=("arbitrary", "arbitrary"), vmem_limit_bytes=VMEM_LIMIT),
        name="in_proj",
    )(hn, w_rows, cscale, gmat)


BIAS_TS = 512
N_PIECES = 3


def _fox_bias_kernel(lf_ref, place_ref, aug_ref, carry_ref):
    i = pl.program_id(1)

    @pl.when(i == 0)
    def _():
        carry_ref[...] = jnp.zeros_like(carry_ref)

    x = lf_ref[...]
    row = lax.broadcasted_iota(jnp.int32, x.shape, 0)
    shift = 1
    while shift < BIAS_TS:
        x = x + jnp.where(row >= shift, pltpu.roll(x, shift, 0), 0.0)
        shift *= 2
    c = x + carry_ref[...]
    carry_ref[...] = c[BIAS_TS - 1:BIAS_TS, :]
    rest = c * (-LOG2E)
    pieces = []
    for _ in range(N_PIECES):
        p = rest.astype(BF16)
        pieces.append(p)
        rest = rest - p.astype(F32)
    stacked = jnp.concatenate(pieces, axis=1)
    aug_ref[...] = jnp.dot(stacked, place_ref[...], preferred_element_type=F32).astype(BF16)


def _fox_bias(lf, place, batch, seq):
    nblk = seq // BIAS_TS
    return pl.pallas_call(
        _fox_bias_kernel,
        grid=(batch, nblk),
        in_specs=[
            pl.BlockSpec((BIAS_TS, LANES), lambda b, i: (b * nblk + i, 0)),
            pl.BlockSpec((N_PIECES * LANES, B_WIDTH), lambda b, i: (0, 0)),
        ],
        out_specs=pl.BlockSpec((BIAS_TS, B_WIDTH), lambda b, i: (b * nblk + i, 0)),
        out_shape=jax.ShapeDtypeStruct((batch * seq, B_WIDTH), BF16),
        scratch_shapes=[pltpu.VMEM((1, LANES), F32)],
        compiler_params=pltpu.CompilerParams(
            dimension_semantics=("arbitrary", "arbitrary"), vmem_limit_bytes=VMEM_LIMIT),
        name="fox_bias",
    )(lf, place)


def _aug_lane(head):
    return HALF if head % 2 == 0 else 0


def _placement_matrix():
    place = np.zeros((N_PIECES * LANES, B_WIDTH), np.float32)
    for h in range(B_HEADS):
        for p in range(N_PIECES):
            place[p * LANES + h, (h // 2) * LANES + _aug_lane(h) + p] = 1.0
    return place


def _swa_kernel(sinks_ref, q_ref, kp_ref, kc_ref, vp_ref, vc_ref, o_ref):
    n = pl.program_id(1)
    blk = WINDOW
    qi = lax.broadcasted_iota(jnp.int32, (blk, 2 * blk), 0)
    kj = lax.broadcasted_iota(jnp.int32, (blk, 2 * blk), 1)
    rel = qi + blk - kj
    key_pos = (n - 1) * blk + kj
    valid = (rel >= 0) & (rel < WINDOW) & (key_pos >= 0)
    relf = rel.astype(F32)

    kcat = jnp.concatenate([kp_ref[...].astype(F32), kc_ref[...].astype(F32)], axis=0)
    vcat = jnp.concatenate([vp_ref[...].astype(F32), vc_ref[...].astype(F32)], axis=0).astype(BF16)
    lane_k = lax.broadcasted_iota(jnp.int32, (2 * blk, LANES), 1)
    lane_o = lax.broadcasted_iota(jnp.int32, (blk, LANES), 1)

    k_same, k_swap = [], []
    for g in range(A_KV_HEADS):
        slab = kcat[:, (g // 2) * LANES:(g // 2 + 1) * LANES]
        own = (lane_k < HALF) if g % 2 == 0 else (lane_k >= HALF)
        kz = jnp.where(own, slab, 0.0)
        k_same.append(kz.astype(BF16))
        k_swap.append(pltpu.roll(kz, HALF, 1).astype(BF16))

    outs = []
    for h in range(A_Q_HEADS):
        g = h // A_GROUP
        same = (h % 2) == (g % 2)
        q_slab = q_ref[:, (h // 2) * LANES:(h // 2 + 1) * LANES]
        s = _dot_nt(q_slab, k_same[g] if same else k_swap[g])
        slope = float(2.0 ** (-8.0 * (h + 1) / A_Q_HEADS))
        s = jnp.where(valid, s - slope * relf, NEG)
        sink = sinks_ref[h]
        m = jnp.maximum(jnp.max(s, axis=-1, keepdims=True), sink)
        p = jnp.exp(s - m)
        denom = jnp.sum(p, axis=-1, keepdims=True) + jnp.exp(sink - m)
        v_slab = vcat[:, (g // 2) * LANES:(g // 2 + 1) * LANES]
        o = jnp.dot(p.astype(BF16), v_slab, preferred_element_type=F32) / denom
        outs.append(o if same else pltpu.roll(o, HALF, 1))
    for pr in range(A_Q_HEADS // 2):
        o_ref[:, pr * LANES:(pr + 1) * LANES] = jnp.where(
            lane_o < HALF, outs[2 * pr], outs[2 * pr + 1]).astype(BF16)


def _swa(sinks, proj, batch, seq):
    blk = WINDOW
    nb = seq // blk
    qcol = _col_block("qA", A_WIDTH)
    kcol = _col_block("kA", A_KV_WIDTH)
    vcol = _col_block("vA", A_KV_WIDTH)
    cur = lambda b, n: b * nb + n
    prev = lambda b, n: b * nb + jnp.maximum(n - 1, 0)
    return pl.pallas_call(
        _swa_kernel,
        grid=(batch, nb),
        in_specs=[
            pl.BlockSpec(memory_space=pltpu.SMEM),
            pl.BlockSpec((blk, A_WIDTH), lambda b, n: (cur(b, n), qcol)),
            pl.BlockSpec((blk, A_KV_WIDTH), lambda b, n: (prev(b, n), kcol)),
            pl.BlockSpec((blk, A_KV_WIDTH), lambda b, n: (cur(b, n), kcol)),
            pl.BlockSpec((blk, A_KV_WIDTH), lambda b, n: (prev(b, n), vcol)),
            pl.BlockSpec((blk, A_KV_WIDTH), lambda b, n: (cur(b, n), vcol)),
        ],
        out_specs=pl.BlockSpec((blk, A_WIDTH), lambda b, n: (cur(b, n), 0)),
        out_shape=jax.ShapeDtypeStruct((batch * seq, A_WIDTH), BF16),
        compiler_params=pltpu.CompilerParams(
            dimension_semantics=("arbitrary", "arbitrary"), vmem_limit_bytes=VMEM_LIMIT),
        name="swa",
    )(sinks, proj, proj, proj, proj, proj)


FOX_T = 512
LOG2E = float(np.log2(np.e))


FOX_H = FOX_T // 2


def _fox_kernel(q_ref, k_ref, v_ref, aug_ref, o_ref,
                kop_ref, vt_ref, s_ref, cm_ref, p_ref, al_ref, m_ref, acc_ref):
    qi = pl.program_id(2)
    seq = k_ref.shape[0]
    t = FOX_T
    hk = FOX_H

    @pl.when(qi == 0)
    def _():
        lane = lax.broadcasted_iota(jnp.int32, (hk, LANES), 1)
        low = lane < HALF

        def build(r, carry):
            sl = pl.ds(pl.multiple_of(r * hk, hk), hk)
            kk = k_ref[sl, :].astype(F32)
            aa = aug_ref[sl, :].astype(F32)
            vv = v_ref[sl, :].astype(F32)
            kop_ref[0, sl, :] = jnp.where(low, kk, aa).astype(BF16)
            kop_ref[1, sl, :] = jnp.where(low, aa, kk).astype(BF16)
            v0 = jnp.where(low, vv, jnp.where(lane == HALF, 1.0, 0.0))
            v1 = jnp.where(low, jnp.where(lane == 0, 1.0, 0.0), vv)
            vt_ref[0, r] = v0.T.astype(BF16)
            vt_ref[1, r] = v1.T.astype(BF16)
            return carry

        lax.fori_loop(0, seq // hk, build, 0)

    lane_q = lax.broadcasted_iota(jnp.int32, (t, LANES), 1)
    low_q = lane_q < HALF
    q = q_ref[...].astype(F32)
    ones0 = jnp.where((lane_q >= HALF) & (lane_q < HALF + N_PIECES), 1.0, 0.0)
    ones1 = jnp.where(lane_q < N_PIECES, 1.0, 0.0)
    qop = (jnp.where(low_q, q, ones0).astype(BF16), jnp.where(low_q, ones1, q).astype(BF16))

    m_ref[...] = jnp.full(m_ref.shape, NEG, F32)
    acc_ref[...] = jnp.zeros(acc_ref.shape, F32)

    def scores(slot, half, masked):
        ks = pl.ds(pl.multiple_of(half * hk, hk), hk)
        for hh in range(2):
            st = _dot_nt(kop_ref[hh, ks, :], qop[hh])
            if masked:
                key = lax.broadcasted_iota(jnp.int32, (hk, t), 0) + slot * hk
                qry = lax.broadcasted_iota(jnp.int32, (hk, t), 1)
                st = jnp.where(key <= qry, st, NEG)
            s_ref[slot, hh] = st
            cm_ref[slot, hh] = jnp.max(st, axis=0, keepdims=True)

    def soft(slot):
        for hh in range(2):
            m_old = m_ref[hh]
            m_new = jnp.maximum(m_old, cm_ref[slot, hh])
            al_ref[slot, hh] = jnp.exp2(m_old - m_new)
            p_ref[slot, hh] = jnp.exp2(s_ref[slot, hh] - m_new).astype(BF16)
            m_ref[hh] = m_new

    def pv(slot, half):
        for hh in range(2):
            acc_ref[hh] = acc_ref[hh] * al_ref[slot, hh] + jnp.dot(
                vt_ref[hh, half], p_ref[slot, hh], preferred_element_type=F32)

    scores(0, 2 * qi, True)
    scores(1, 2 * qi + 1, True)
    soft(0)
    soft(1)

    @pl.when(qi > 0)
    def _():
        scores(0, 0, False)

    def body(i, carry):
        prev = jnp.where(i == 0, qi, i - 1)
        pv(0, 2 * prev)
        soft(0)
        scores(1, 2 * i + 1, False)
        scores(0, 2 * i + 2, False)
        pv(1, 2 * prev + 1)
        soft(1)
        return carry

    lax.fori_loop(0, qi, body, 0)
    last = jnp.where(qi == 0, qi, qi - 1)
    pv(0, 2 * last)
    pv(1, 2 * last + 1)

    a0 = acc_ref[0]
    a1 = acc_ref[1]
    feat = lax.broadcasted_iota(jnp.int32, (LANES, t), 0)
    ot = jnp.where(feat < HALF, a0 / a0[HALF:HALF + 1, :], a1 / a1[0:1, :])
    o_ref[...] = ot.T.astype(BF16)


def _fox(proj, aug, batch, seq):
    t = FOX_T
    nq = seq // t
    pairs = B_HEADS // 2
    qcol = _col_block("qB", LANES)
    kcol = _col_block("kB", LANES)
    vcol = _col_block("vB", LANES)
    return pl.pallas_call(
        _fox_kernel,
        grid=(batch, pairs, nq),
        in_specs=[
            pl.BlockSpec((t, LANES), lambda b, p, i: (b * nq + i, qcol + p)),
            pl.BlockSpec((seq, LANES), lambda b, p, i: (b, kcol + p)),
            pl.BlockSpec((seq, LANES), lambda b, p, i: (b, vcol + p)),
            pl.BlockSpec((seq, LANES), lambda b, p, i: (b, p)),
        ],
        out_specs=pl.BlockSpec((t, LANES), lambda b, p, i: (b * nq + i, p)),
        out_shape=jax.ShapeDtypeStruct((batch * seq, B_WIDTH), BF16),
        scratch_shapes=[
            pltpu.VMEM((2, seq, LANES), BF16),
            pltpu.VMEM((2, seq // FOX_H, LANES, FOX_H), BF16),
            pltpu.VMEM((2, 2, FOX_H, t), F32),
            pltpu.VMEM((2, 2, 1, t), F32),
            pltpu.VMEM((2, 2, FOX_H, t), BF16),
            pltpu.VMEM((2, 2, 1, t), F32),
            pltpu.VMEM((2, 1, t), F32),
            pltpu.VMEM((2, LANES, t), F32),
        ],
        compiler_params=pltpu.CompilerParams(
            dimension_semantics=("arbitrary", "arbitrary", "arbitrary"),
            vmem_limit_bytes=VMEM_LIMIT),
        name="fox",
    )(proj, proj, proj, aug)


MEM_TM = 512


def _mem_attn_kernel(q_ref, mk_ref, mv_ref, o_ref):
    for h in range(C_HEADS):
        sl = slice(h * C_HEAD_DIM, (h + 1) * C_HEAD_DIM)
        s = _dot_nt(q_ref[:, sl], mk_ref[0, :, sl])
        m = jnp.max(s, axis=-1, keepdims=True)
        p = jnp.exp(s - m)
        denom = jnp.sum(p, axis=-1, keepdims=True)
        o = jnp.dot(p.astype(BF16), mv_ref[0, :, sl], preferred_element_type=F32)
        o_ref[:, sl] = (o / denom).astype(BF16)


def _mem_attn(proj, mk, mv, batch, seq):
    nt = seq // MEM_TM
    mlen = mk.shape[1]
    qcol = _col_block("qC", C_WIDTH)
    return pl.pallas_call(
        _mem_attn_kernel,
        grid=(batch * nt,),
        in_specs=[
            pl.BlockSpec((MEM_TM, C_WIDTH), lambda i: (i, qcol)),
            pl.BlockSpec((1, mlen, C_WIDTH), lambda i: (i // nt, 0, 0)),
            pl.BlockSpec((1, mlen, C_WIDTH), lambda i: (i // nt, 0, 0)),
        ],
        out_specs=pl.BlockSpec((MEM_TM, C_WIDTH), lambda i: (i, 0)),
        out_shape=jax.ShapeDtypeStruct((batch * seq, C_WIDTH), BF16),
        compiler_params=pltpu.CompilerParams(
            dimension_semantics=("arbitrary",), vmem_limit_bytes=VMEM_LIMIT),
        name="mem_attn",
    )(proj, mk, mv)


MERGE_TM = 256


Z_BLOCK = 256
Z_PARTS = A_WIDTH // Z_BLOCK
assert A_WIDTH == B_WIDTH and A_WIDTH % Z_BLOCK == 0


def _merge_kernel(x_ref, ya_ref, yb_ref, yc_ref, *refs):
    za_refs = refs[:Z_PARTS]
    zb_refs = refs[Z_PARTS:2 * Z_PARTS]
    zc_ref, g0_ref, g1_ref, g2_ref, wa_ref, wb_ref, wc_ref, wo_ref, o_ref = refs[2 * Z_PARTS:]

    def branch(y_ref, z_refs, w_ref):
        z = jnp.concatenate([r[...] for r in z_refs], axis=1).astype(F32)
        h = (y_ref[...].astype(F32) * (z * _sigmoid(z))).astype(BF16)
        return jnp.dot(h, w_ref[...], preferred_element_type=F32)

    y = _sigmoid(g0_ref[...].astype(F32)) * branch(ya_ref, za_refs, wa_ref)
    y = y + _sigmoid(g1_ref[...].astype(F32)) * branch(yb_ref, zb_refs, wb_ref)
    y = y + _sigmoid(g2_ref[...].astype(F32)) * branch(yc_ref, (zc_ref,), wc_ref)
    o_ref[...] = x_ref[...] + jnp.dot(y.astype(BF16), wo_ref[...], preferred_element_type=F32)


def _merge(x2, ya, yb, yc, proj, wa, wb, wc, wo):
    t, d = x2.shape
    tm = MERGE_TM
    gcol = _col_block("g", D_MODEL)
    row = lambda width, col: pl.BlockSpec((tm, width), lambda i: (i, col))
    full = lambda a: pl.BlockSpec(a.shape, lambda i: (0, 0), pipeline_mode=pl.Buffered(1))
    z_specs = lambda name: [row(Z_BLOCK, _col_block(name, Z_BLOCK, part)) for part in range(Z_PARTS)]
    n_proj = 2 * Z_PARTS + 4
    return pl.pallas_call(
        _merge_kernel,
        grid=(t // tm,),
        in_specs=[
            row(d, 0), row(A_WIDTH, 0), row(B_WIDTH, 0), row(C_WIDTH, 0),
            *z_specs("zA"), *z_specs("zB"), row(C_WIDTH, _col_block("zC", C_WIDTH)),
            row(d, gcol), row(d, gcol + 1), row(d, gcol + 2),
            full(wa), full(wb), full(wc), full(wo),
        ],
        out_specs=pl.BlockSpec((tm, d), lambda i: (i, 0)),
        out_shape=jax.ShapeDtypeStruct((t, d), F32),
        compiler_params=pltpu.CompilerParams(
            dimension_semantics=("arbitrary",), vmem_limit_bytes=VMEM_LIMIT),
        name="merge",
    )(x2, ya, yb, yc, *([proj] * n_proj), wa, wb, wc, wo)


def _group_mean_matrices():
    lane = np.arange(LANES)
    mats = []
    for tile in NORM_TILES:
        for c in range(IN_CHUNKS):
            dim = _NORM_KIND.get(_chunk_piece(tile * IN_TN + c * LANES), LANES)
            mats.append((lane[:, None] // dim == lane[None, :] // dim).astype(np.float32) / dim)
    return np.stack(mats)


def _column_scales(gains):
    scale = {"qA": HEAD_DIM ** -0.5, "qB": HEAD_DIM ** -0.5 * LOG2E, "qC": C_HEAD_DIM ** -0.5}
    cols = []
    for tile in NORM_TILES:
        for c in range(IN_CHUNKS):
            piece = _chunk_piece(tile * IN_TN + c * LANES)
            if piece in _NORM_KIND:
                g = gains[piece].astype(F32) * scale.get(piece, 1.0)
                cols.append(jnp.tile(g, LANES // g.shape[0]))
            else:
                cols.append(jnp.zeros((LANES,), F32))
    return jnp.concatenate(cols).reshape(1, len(NORM_TILES) * IN_TN)


def _layer(x, mem, norm_gain, mem_norm_gain, w_rows, b_forget, q_gain_a, k_gain_a, sinks_a,
           q_gain_b, k_gain_b, q_gain_c, k_gain_c, w_mem_kv, w_branch_a, w_branch_b,
           w_branch_c, w_out):
    batch, seq, d = x.shape
    x2 = x.reshape(batch * seq, d)

    w_f = w_rows[F_START * K_CHUNKS:(F_START + F_SHIFT) * K_CHUNKS].reshape(F_SHIFT, d)
    w_f = jnp.pad(w_f, ((0, LANES - F_SHIFT), (0, 0))).astype(BF16)
    b_f = jnp.pad(b_forget.astype(F32), (0, LANES - F_SHIFT)).reshape(1, LANES)
    cscale = _column_scales({"qA": q_gain_a, "kA": k_gain_a, "qB": q_gain_b, "kB": k_gain_b,
                             "qC": q_gain_c})
    gmat = jnp.asarray(_group_mean_matrices(), BF16)
    place = jnp.asarray(_placement_matrix(), BF16)

    mk, mv = _mem_kv(mem, mem_norm_gain.reshape(1, d), w_mem_kv.astype(BF16),
                     k_gain_c.reshape(1, C_HEAD_DIM))
    hn, log_f = _norm_x(x2, norm_gain.reshape(1, d), w_f, b_f)
    proj = _in_proj(hn, w_rows, cscale, gmat)
    aug = _fox_bias(log_f, place, batch, seq)
    ya = _swa(sinks_a.astype(F32), proj, batch, seq)
    yb = _fox(proj, aug, batch, seq)
    yc = _mem_attn(proj, mk, mv, batch, seq)
    out = _merge(x2, ya, yb, yc, proj, w_branch_a.astype(BF16), w_branch_b.astype(BF16),
                 w_branch_c.astype(BF16), w_out.astype(BF16))
    return out.reshape(batch, seq, d)


def kernel(x, mem, norm_gain, mem_norm_gain, w_in, b_forget, q_gain_a, k_gain_a, sinks_a,
           q_gain_b, k_gain_b, q_gain_c, k_gain_c, w_mem_kv, w_branch_a, w_branch_b,
           w_branch_c, w_out):
    depth = norm_gain.shape[0]
    w_rows = jnp.swapaxes(w_in, 1, 2).reshape(depth, -1, LANES)
    for layer in range(depth):
        x = _layer(x, mem, norm_gain[layer], mem_norm_gain[layer], w_rows[layer], b_forget[layer],
                   q_gain_a[layer], k_gain_a[layer], sinks_a[layer], q_gain_b[layer],
                   k_gain_b[layer], q_gain_c[layer], k_gain_c[layer], w_mem_kv[layer],
                   w_branch_a[layer], w_branch_b[layer], w_branch_c[layer], w_out[layer])
    return x
```

```python
import functools

import jax
import jax.numpy as jnp
import numpy as np
from jax import lax
from jax.experimental import pallas as pl
from jax.experimental.pallas import tpu as pltpu

F32 = jnp.float32
BF16 = jnp.bfloat16

D_MODEL = 2048
HEAD_DIM = 64
A_Q_HEADS = 12
A_KV_HEADS = 4
A_GROUP = A_Q_HEADS // A_KV_HEADS
WINDOW = 128
B_HEADS = 12
C_HEADS = 4
C_HEAD_DIM = 128
A_WIDTH = A_Q_HEADS * HEAD_DIM
A_KV_WIDTH = A_KV_HEADS * HEAD_DIM
B_WIDTH = B_HEADS * HEAD_DIM
C_WIDTH = C_HEADS * C_HEAD_DIM
EPS = 1e-6
NEG = -1e30

LANES = 128
HALF = LANES // 2
VMEM_LIMIT = 56 * 1024 * 1024

_SRC = {}
_off = 0
for _name, _w in (("qA", A_WIDTH), ("kA", A_KV_WIDTH), ("vA", A_KV_WIDTH), ("zA", A_WIDTH),
                  ("qB", B_WIDTH), ("kB", B_WIDTH), ("vB", B_WIDTH), ("zB", B_WIDTH),
                  ("fB", B_HEADS), ("qC", C_WIDTH), ("zC", C_WIDTH), ("g", 3 * D_MODEL)):
    _SRC[_name] = (_off, _w)
    _off += _w

F_START, F_SHIFT = _SRC["fB"]
_DST = {n: (o if o < F_START else o - F_SHIFT) for n, (o, _) in _SRC.items() if n != "fB"}
PROJ_WIDTH = _SRC["g"][0] + _SRC["g"][1] - F_SHIFT
assert all(o % LANES == 0 for o in _DST.values()) and F_START % LANES == 0


def _col_block(name, width, part=0):
    assert _DST[name] % width == 0
    return _DST[name] // width + part


def _rms(x, gain):
    ms = jnp.mean(x * x, axis=-1, keepdims=True)
    return x * lax.rsqrt(ms + EPS) * gain


def _sigmoid(t):
    return 0.5 * (jnp.tanh(0.5 * t) + 1.0)


def _dot_nt(a, b):
    return lax.dot_general(a, b, (((1,), (1,)), ((), ())), preferred_element_type=F32)


def _mem_kv_kernel(mem_ref, gain_ref, w_ref, kgain_ref, mk_ref, mv_ref):
    hn = _rms(mem_ref[0], gain_ref[...]).astype(BF16)
    kv = jnp.dot(hn, w_ref[...], preferred_element_type=F32)
    for h in range(C_HEADS):
        sl = slice(h * C_HEAD_DIM, (h + 1) * C_HEAD_DIM)
        mk_ref[0, :, sl] = _rms(kv[:, sl], kgain_ref[...]).astype(BF16)
    mv_ref[0] = kv[:, C_WIDTH:].astype(BF16)


def _mem_kv(mem, gain, w, kgain):
    b, m, d = mem.shape
    return pl.pallas_call(
        _mem_kv_kernel,
        grid=(b,),
        in_specs=[
            pl.BlockSpec((1, m, d), lambda i: (i, 0, 0)),
            pl.BlockSpec((1, d), lambda i: (0, 0)),
            pl.BlockSpec((d, 2 * C_WIDTH), lambda i: (0, 0)),
            pl.BlockSpec((1, C_HEAD_DIM), lambda i: (0, 0)),
        ],
        out_specs=[
            pl.BlockSpec((1, m, C_WIDTH), lambda i: (i, 0, 0)),
            pl.BlockSpec((1, m, C_WIDTH), lambda i: (i, 0, 0)),
        ],
        out_shape=[jax.ShapeDtypeStruct((b, m, C_WIDTH), BF16)] * 2,
        compiler_params=pltpu.CompilerParams(
            dimension_semantics=("arbitrary",), vmem_limit_bytes=VMEM_LIMIT),
        name="mem_kv",
    )(mem, gain, w, kgain)


NX_TM = 512


def _norm_x_kernel(x_ref, gain_ref, wf_ref, bf_ref, hn_ref, lf_ref):
    hn = _rms(x_ref[...], gain_ref[...]).astype(BF16)
    hn_ref[...] = hn
    f_logit = _dot_nt(hn, wf_ref[...]) + bf_ref[...]
    lf_ref[...] = jnp.minimum(f_logit, 0.0) - jnp.log1p(jnp.exp(-jnp.abs(f_logit)))


def _norm_x(x2, gain, wf, bfg):
    t, d = x2.shape
    return pl.pallas_call(
        _norm_x_kernel,
        grid=(t // NX_TM,),
        in_specs=[
            pl.BlockSpec((NX_TM, d), lambda i: (i, 0)),
            pl.BlockSpec((1, d), lambda i: (0, 0)),
            pl.BlockSpec((LANES, d), lambda i: (0, 0)),
            pl.BlockSpec((1, LANES), lambda i: (0, 0)),
        ],
        out_specs=[
            pl.BlockSpec((NX_TM, d), lambda i: (i, 0)),
            pl.BlockSpec((NX_TM, LANES), lambda i: (i, 0)),
        ],
        out_shape=[jax.ShapeDtypeStruct((t, d), BF16),
                   jax.ShapeDtypeStruct((t, LANES), F32)],
        compiler_params=pltpu.CompilerParams(
            dimension_semantics=("arbitrary",), vmem_limit_bytes=VMEM_LIMIT),
        name="norm_x",
    )(x2, gain, wf, bfg)


IN_TM = 1024
IN_TN = 1024
IN_CHUNKS = IN_TN // LANES
K_CHUNKS = D_MODEL // LANES
N_IN_TILES = PROJ_WIDTH // IN_TN
N_ALIGNED_TILES = F_START // IN_TN
assert F_START % IN_TN == 0 and PROJ_WIDTH % IN_TN == 0 and F_SHIFT < LANES

_NORM_KIND = {"qA": HEAD_DIM, "kA": HEAD_DIM, "qB": HEAD_DIM, "kB": HEAD_DIM, "qC": C_HEAD_DIM}


def _chunk_piece(col):
    for name, off in _DST.items():
        if off <= col < off + _SRC[name][1]:
            return name
    raise ValueError(col)


def _norm_prefix(tile):
    kinds = [_chunk_piece(tile * IN_TN + c * LANES) in _NORM_KIND for c in range(IN_CHUNKS)]
    n = sum(kinds)
    assert kinds == [True] * n + [False] * (IN_CHUNKS - n)
    return n


NORM_PREFIX = tuple(_norm_prefix(j) for j in range(N_IN_TILES))
NORM_TILES = tuple(j for j in range(N_IN_TILES) if NORM_PREFIX[j])


def _norm_slot(j):
    slot = 0
    for tile in NORM_TILES[1:]:
        slot = slot + (j >= tile).astype(jnp.int32)
    return slot


def _in_proj_kernel(hn_ref, w_ref, cscale_ref, gmat_ref, o_ref, wbf_ref):
    j = pl.program_id(0)
    i = pl.program_id(1)

    @pl.when(i == 0)
    def _():
        for c in range(K_CHUNKS):
            wbf_ref[:, c * LANES:(c + 1) * LANES] = (
                w_ref[pl.ds(c, IN_TN, stride=K_CHUNKS), :].astype(BF16))

    acc = _dot_nt(hn_ref[...], wbf_ref[...])

    def is_any(tiles):
        cond = j == tiles[0]
        for tile in tiles[1:]:
            cond = cond | (j == tile)
        return cond

    for prefix in sorted(set(NORM_PREFIX)):
        tiles = [t for t in range(N_IN_TILES) if NORM_PREFIX[t] == prefix]

        @pl.when(is_any(tiles))
        def _(prefix=prefix):
            for c in range(prefix):
                sl = slice(c * LANES, (c + 1) * LANES)
                a = acc[:, sl]
                ms = jnp.dot((a * a).astype(BF16), gmat_ref[c], preferred_element_type=F32)
                o_ref[:, sl] = (a * lax.rsqrt(ms + EPS) * cscale_ref[:, sl]).astype(BF16)
            if prefix < IN_CHUNKS:
                rest = slice(prefix * LANES, IN_TN)
                o_ref[:, rest] = acc[:, rest].astype(BF16)


def _in_proj(hn, w_rows, cscale, gmat):
    t, d = hn.shape
    grid = (N_IN_TILES, t // IN_TM)

    def w_index(j, i):
        col = j * IN_TN + jnp.where(j >= N_ALIGNED_TILES, F_SHIFT, 0)
        return (col * K_CHUNKS, 0)

    return pl.pallas_call(
        _in_proj_kernel,
        grid=grid,
        in_specs=[
            pl.BlockSpec((IN_TM, d), lambda j, i: (i, 0)),
            pl.BlockSpec((pl.Element(IN_TN * K_CHUNKS), pl.Element(LANES)), w_index),
            pl.BlockSpec((1, IN_TN), lambda j, i: (0, _norm_slot(j))),
            pl.BlockSpec((IN_CHUNKS, LANES, LANES), lambda j, i: (_norm_slot(j), 0, 0)),
        ],
        out_specs=pl.BlockSpec((IN_TM, IN_TN), lambda j, i: (i, j)),
        out_shape=jax.ShapeDtypeStruct((t, PROJ_WIDTH), BF16),
        scratch_shapes=[pltpu.VMEM((IN_TN, d), BF16)],
        compiler_params=pltpu.CompilerParams(
            dimension_semantics=("arbitrary", "arbitrary"), vmem_limit_bytes=VMEM_LIMIT),
        name="in_proj",
    )(hn, w_rows, cscale, gmat)


BIAS_TS = 512
N_PIECES = 3


def _fox_bias_kernel(lf_ref, place_ref, aug_ref, carry_ref):
    i = pl.program_id(1)

    @pl.when(i == 0)
    def _():
        carry_ref[...] = jnp.zeros_like(carry_ref)

    x = lf_ref[...]
    row = lax.broadcasted_iota(jnp.int32, x.shape, 0)
    shift = 1
    while shift < BIAS_TS:
        x = x + jnp.where(row >= shift, pltpu.roll(x, shift, 0), 0.0)
        shift *= 2
    c = x + carry_ref[...]
    carry_ref[...] = c[BIAS_TS - 1:BIAS_TS, :]
    rest = c * (-LOG2E)
    pieces = []
    for _ in range(N_PIECES):
        p = rest.astype(BF16)
        pieces.append(p)
        rest = rest - p.astype(F32)
    stacked = jnp.concatenate(pieces, axis=1)
    aug_ref[...] = jnp.dot(stacked, place_ref[...], preferred_element_type=F32).astype(BF16)


def _fox_bias(lf, place, batch, seq):
    nblk = seq // BIAS_TS
    return pl.pallas_call(
        _fox_bias_kernel,
        grid=(batch, nblk),
        in_specs=[
            pl.BlockSpec((BIAS_TS, LANES), lambda b, i: (b * nblk + i, 0)),
            pl.BlockSpec((N_PIECES * LANES, B_WIDTH), lambda b, i: (0, 0)),
        ],
        out_specs=pl.BlockSpec((BIAS_TS, B_WIDTH), lambda b, i: (b * nblk + i, 0)),
        out_shape=jax.ShapeDtypeStruct((batch * seq, B_WIDTH), BF16),
        scratch_shapes=[pltpu.VMEM((1, LANES), F32)],
        compiler_params=pltpu.CompilerParams(
            dimension_semantics=("arbitrary", "arbitrary"), vmem_limit_bytes=VMEM_LIMIT),
        name="fox_bias",
    )(lf, place)


def _aug_lane(head):
    return HALF if head % 2 == 0 else 0


def _placement_matrix():
    place = np.zeros((N_PIECES * LANES, B_WIDTH), np.float32)
    for h in range(B_HEADS):
        for p in range(N_PIECES):
            place[p * LANES + h, (h // 2) * LANES + _aug_lane(h) + p] = 1.0
    return place


def _swa_kernel(sinks_ref, q_ref, kp_ref, kc_ref, vp_ref, vc_ref, o_ref):
    n = pl.program_id(1)
    blk = WINDOW
    qi = lax.broadcasted_iota(jnp.int32, (blk, 2 * blk), 0)
    kj = lax.broadcasted_iota(jnp.int32, (blk, 2 * blk), 1)
    rel = qi + blk - kj
    key_pos = (n - 1) * blk + kj
    valid = (rel >= 0) & (rel < WINDOW) & (key_pos >= 0)
    relf = rel.astype(F32)

    kcat = jnp.concatenate([kp_ref[...].astype(F32), kc_ref[...].astype(F32)], axis=0)
    vcat = jnp.concatenate([vp_ref[...].astype(F32), vc_ref[...].astype(F32)], axis=0).astype(BF16)
    lane_k = lax.broadcasted_iota(jnp.int32, (2 * blk, LANES), 1)
    lane_o = lax.broadcasted_iota(jnp.int32, (blk, LANES), 1)

    k_same, k_swap = [], []
    for g in range(A_KV_HEADS):
        slab = kcat[:, (g // 2) * LANES:(g // 2 + 1) * LANES]
        own = (lane_k < HALF) if g % 2 == 0 else (lane_k >= HALF)
        kz = jnp.where(own, slab, 0.0)
        k_same.append(kz.astype(BF16))
        k_swap.append(pltpu.roll(kz, HALF, 1).astype(BF16))

    outs = []
    for h in range(A_Q_HEADS):
        g = h // A_GROUP
        same = (h % 2) == (g % 2)
        q_slab = q_ref[:, (h // 2) * LANES:(h // 2 + 1) * LANES]
        s = _dot_nt(q_slab, k_same[g] if same else k_swap[g])
        slope = float(2.0 ** (-8.0 * (h + 1) / A_Q_HEADS))
        s = jnp.where(valid, s - slope * relf, NEG)
        sink = sinks_ref[h]
        m = jnp.maximum(jnp.max(s, axis=-1, keepdims=True), sink)
        p = jnp.exp(s - m)
        denom = jnp.sum(p, axis=-1, keepdims=True) + jnp.exp(sink - m)
        v_slab = vcat[:, (g // 2) * LANES:(g // 2 + 1) * LANES]
        o = jnp.dot(p.astype(BF16), v_slab, preferred_element_type=F32) / denom
        outs.append(o if same else pltpu.roll(o, HALF, 1))
    for pr in range(A_Q_HEADS // 2):
        o_ref[:, pr * LANES:(pr + 1) * LANES] = jnp.where(
            lane_o < HALF, outs[2 * pr], outs[2 * pr + 1]).astype(BF16)


def _swa(sinks, proj, batch, seq):
    blk = WINDOW
    nb = seq // blk
    qcol = _col_block("qA", A_WIDTH)
    kcol = _col_block("kA", A_KV_WIDTH)
    vcol = _col_block("vA", A_KV_WIDTH)
    cur = lambda b, n: b * nb + n
    prev = lambda b, n: b * nb + jnp.maximum(n - 1, 0)
    return pl.pallas_call(
        _swa_kernel,
        grid=(batch, nb),
        in_specs=[
            pl.BlockSpec(memory_space=pltpu.SMEM),
            pl.BlockSpec((blk, A_WIDTH), lambda b, n: (cur(b, n), qcol)),
            pl.BlockSpec((blk, A_KV_WIDTH), lambda b, n: (prev(b, n), kcol)),
            pl.BlockSpec((blk, A_KV_WIDTH), lambda b, n: (cur(b, n), kcol)),
            pl.BlockSpec((blk, A_KV_WIDTH), lambda b, n: (prev(b, n), vcol)),
            pl.BlockSpec((blk, A_KV_WIDTH), lambda b, n: (cur(b, n), vcol)),
        ],
        out_specs=pl.BlockSpec((blk, A_WIDTH), lambda b, n: (cur(b, n), 0)),
        out_shape=jax.ShapeDtypeStruct((batch * seq, A_WIDTH), BF16),
        compiler_params=pltpu.CompilerParams(
            dimension_semantics=("arbitrary", "arbitrary"), vmem_limit_bytes=VMEM_LIMIT),
        name="swa",
    )(sinks, proj, proj, proj, proj, proj)


FOX_T = 512
LOG2E = float(np.log2(np.e))


FOX_H = FOX_T // 2
FOX_VROWS = HALF + 16


def _fox_kernel(q_ref, k_ref, v_ref, aug_ref, o_ref,
                kop_ref, vt_ref, qop_ref, s_ref, cm_ref, p_ref, al_ref, m_ref, acc_ref):
    seq = k_ref.shape[0]
    t = FOX_T
    hk = FOX_H

    lane = lax.broadcasted_iota(jnp.int32, (hk, LANES), 1)
    low = lane < HALF

    def build(r, carry):
        sl = pl.ds(pl.multiple_of(r * hk, hk), hk)
        kk = k_ref[sl, :].astype(F32)
        aa = aug_ref[sl, :].astype(F32)
        vv = v_ref[sl, :].astype(F32)
        kop_ref[0, sl, :] = jnp.where(low, kk, aa).astype(BF16)
        kop_ref[1, sl, :] = jnp.where(low, aa, kk).astype(BF16)
        vvt = vv.T
        ones_rows = jnp.where(
            lax.broadcasted_iota(jnp.int32, (FOX_VROWS - HALF, hk), 0) == 0, 1.0, 0.0)
        for hh in range(2):
            vt_ref[hh, r] = jnp.concatenate(
                [vvt[hh * HALF:(hh + 1) * HALF], ones_rows], axis=0).astype(BF16)
        return carry

    lax.fori_loop(0, seq // hk, build, 0)

    def scores(slot, half, masked):
        ks = pl.ds(pl.multiple_of(half * hk, hk), hk)
        for hh in range(2):
            st = _dot_nt(kop_ref[hh, ks, :], qop_ref[hh])
            if masked:
                key = lax.broadcasted_iota(jnp.int32, (hk, t), 0) + slot * hk
                qry = lax.broadcasted_iota(jnp.int32, (hk, t), 1)
                st = jnp.where(key <= qry, st, NEG)
            s_ref[slot, hh] = st
            cm_ref[slot, hh] = jnp.max(st, axis=0, keepdims=True)

    def soft(slot):
        for hh in range(2):
            m_old = m_ref[hh]
            m_new = jnp.maximum(m_old, cm_ref[slot, hh])
            al_ref[slot, hh] = jnp.exp2(m_old - m_new)
            p_ref[slot, hh] = jnp.exp2(s_ref[slot, hh] - m_new).astype(BF16)
            m_ref[hh] = m_new

    def pv(slot, half):
        for hh in range(2):
            acc_ref[hh] = acc_ref[hh] * al_ref[slot, hh] + jnp.dot(
                vt_ref[hh, half], p_ref[slot, hh], preferred_element_type=F32)

    def step(i, prev, has_next):
        pv(0, 2 * prev)
        soft(0)
        scores(1, 2 * i + 1, False)
        if has_next:
            scores(0, 2 * i + 2, False)
        pv(1, 2 * prev + 1)
        soft(1)

    lane_q = lax.broadcasted_iota(jnp.int32, (t, LANES), 1)
    low_q = lane_q < HALF
    ones0 = jnp.where((lane_q >= HALF) & (lane_q < HALF + N_PIECES), 1.0, 0.0)
    ones1 = jnp.where(lane_q < N_PIECES, 1.0, 0.0)

    for qi in range(seq // t):
        rows = slice(qi * t, (qi + 1) * t)
        q = q_ref[rows, :].astype(F32)
        qop_ref[0] = jnp.where(low_q, q, ones0).astype(BF16)
        qop_ref[1] = jnp.where(low_q, ones1, q).astype(BF16)
        m_ref[...] = jnp.full(m_ref.shape, NEG, F32)
        acc_ref[...] = jnp.zeros(acc_ref.shape, F32)

        scores(0, 2 * qi, True)
        scores(1, 2 * qi + 1, True)
        soft(0)
        if qi > 0:
            scores(0, 0, False)
        soft(1)

        loop_pairs = max(qi - 1, 0) // 2

        def pair(j, carry, qi=qi):
            first = 2 * j
            step(first, jnp.where(j == 0, qi, first - 1), True)
            step(first + 1, first, True)
            return carry

        if loop_pairs:
            lax.fori_loop(0, loop_pairs, pair, 0)
        for i in range(2 * loop_pairs, qi):
            step(i, i - 1 if i > 0 else qi, i + 1 < qi)
        last = qi - 1 if qi > 0 else qi
        pv(0, 2 * last)
        pv(1, 2 * last + 1)

        ot = jnp.concatenate(
            [acc_ref[hh, 0:HALF, :] / acc_ref[hh, HALF:HALF + 1, :] for hh in range(2)], axis=0)
        o_ref[rows, :] = ot.T.astype(BF16)


def _fox(proj, aug, batch, seq):
    t = FOX_T
    nq = seq // t
    pairs = B_HEADS // 2
    qcol = _col_block("qB", LANES)
    kcol = _col_block("kB", LANES)
    vcol = _col_block("vB", LANES)
    return pl.pallas_call(
        _fox_kernel,
        grid=(batch, pairs),
        in_specs=[
            pl.BlockSpec((seq, LANES), lambda b, p: (b, qcol + p)),
            pl.BlockSpec((seq, LANES), lambda b, p: (b, kcol + p)),
            pl.BlockSpec((seq, LANES), lambda b, p: (b, vcol + p)),
            pl.BlockSpec((seq, LANES), lambda b, p: (b, p)),
        ],
        out_specs=pl.BlockSpec((seq, LANES), lambda b, p: (b, p)),
        out_shape=jax.ShapeDtypeStruct((batch * seq, B_WIDTH), BF16),
        scratch_shapes=[
            pltpu.VMEM((2, seq, LANES), BF16),
            pltpu.VMEM((2, seq // FOX_H, FOX_VROWS, FOX_H), BF16),
            pltpu.VMEM((2, t, LANES), BF16),
            pltpu.VMEM((2, 2, FOX_H, t), F32),
            pltpu.VMEM((2, 2, 1, t), F32),
            pltpu.VMEM((2, 2, FOX_H, t), BF16),
            pltpu.VMEM((2, 2, 1, t), F32),
            pltpu.VMEM((2, 1, t), F32),
            pltpu.VMEM((2, FOX_VROWS, t), F32),
        ],
        compiler_params=pltpu.CompilerParams(
            dimension_semantics=("arbitrary", "arbitrary"),
            vmem_limit_bytes=VMEM_LIMIT),
        name="fox",
    )(proj, proj, proj, aug)


MEM_TM = 512


def _mem_attn_kernel(q_ref, mk_ref, mv_ref, o_ref):
    for h in range(C_HEADS):
        sl = slice(h * C_HEAD_DIM, (h + 1) * C_HEAD_DIM)
        s = _dot_nt(q_ref[:, sl], mk_ref[0, :, sl])
        m = jnp.max(s, axis=-1, keepdims=True)
        p = jnp.exp(s - m)
        denom = jnp.sum(p, axis=-1, keepdims=True)
        o = jnp.dot(p.astype(BF16), mv_ref[0, :, sl], preferred_element_type=F32)
        o_ref[:, sl] = (o / denom).astype(BF16)


def _mem_attn(proj, mk, mv, batch, seq):
    nt = seq // MEM_TM
    mlen = mk.shape[1]
    qcol = _col_block("qC", C_WIDTH)
    return pl.pallas_call(
        _mem_attn_kernel,
        grid=(batch * nt,),
        in_specs=[
            pl.BlockSpec((MEM_TM, C_WIDTH), lambda i: (i, qcol)),
            pl.BlockSpec((1, mlen, C_WIDTH), lambda i: (i // nt, 0, 0)),
            pl.BlockSpec((1, mlen, C_WIDTH), lambda i: (i // nt, 0, 0)),
        ],
        out_specs=pl.BlockSpec((MEM_TM, C_WIDTH), lambda i: (i, 0)),
        out_shape=jax.ShapeDtypeStruct((batch * seq, C_WIDTH), BF16),
        compiler_params=pltpu.CompilerParams(
            dimension_semantics=("arbitrary",), vmem_limit_bytes=VMEM_LIMIT),
        name="mem_attn",
    )(proj, mk, mv)


MERGE_TM = 256


Z_BLOCK = 256
Z_PARTS = A_WIDTH // Z_BLOCK
assert A_WIDTH == B_WIDTH and A_WIDTH % Z_BLOCK == 0


def _merge_kernel(x_ref, ya_ref, yb_ref, yc_ref, *refs):
    za_refs = refs[:Z_PARTS]
    zb_refs = refs[Z_PARTS:2 * Z_PARTS]
    zc_ref, g0_ref, g1_ref, g2_ref, wa_ref, wb_ref, wc_ref, wo_ref, o_ref = refs[2 * Z_PARTS:]

    def branch(y_ref, z_refs, w_ref):
        z = jnp.concatenate([r[...] for r in z_refs], axis=1).astype(F32)
        h = (y_ref[...].astype(F32) * (z * _sigmoid(z))).astype(BF16)
        return jnp.dot(h, w_ref[...], preferred_element_type=F32)

    y = _sigmoid(g0_ref[...].astype(F32)) * branch(ya_ref, za_refs, wa_ref)
    y = y + _sigmoid(g1_ref[...].astype(F32)) * branch(yb_ref, zb_refs, wb_ref)
    y = y + _sigmoid(g2_ref[...].astype(F32)) * branch(yc_ref, (zc_ref,), wc_ref)
    o_ref[...] = x_ref[...] + jnp.dot(y.astype(BF16), wo_ref[...], preferred_element_type=F32)


def _merge(x2, ya, yb, yc, proj, wa, wb, wc, wo):
    t, d = x2.shape
    tm = MERGE_TM
    gcol = _col_block("g", D_MODEL)
    row = lambda width, col: pl.BlockSpec((tm, width), lambda i: (i, col))
    full = lambda a: pl.BlockSpec(a.shape, lambda i: (0, 0), pipeline_mode=pl.Buffered(1))
    z_specs = lambda name: [row(Z_BLOCK, _col_block(name, Z_BLOCK, part)) for part in range(Z_PARTS)]
    n_proj = 2 * Z_PARTS + 4
    return pl.pallas_call(
        _merge_kernel,
        grid=(t // tm,),
        in_specs=[
            row(d, 0), row(A_WIDTH, 0), row(B_WIDTH, 0), row(C_WIDTH, 0),
            *z_specs("zA"), *z_specs("zB"), row(C_WIDTH, _col_block("zC", C_WIDTH)),
            row(d, gcol), row(d, gcol + 1), row(d, gcol + 2),
            full(wa), full(wb), full(wc), full(wo),
        ],
        out_specs=pl.BlockSpec((tm, d), lambda i: (i, 0)),
        out_shape=jax.ShapeDtypeStruct((t, d), F32),
        compiler_params=pltpu.CompilerParams(
            dimension_semantics=("arbitrary",), vmem_limit_bytes=VMEM_LIMIT),
        name="merge",
    )(x2, ya, yb, yc, *([proj] * n_proj), wa, wb, wc, wo)


def _group_mean_matrices():
    lane = np.arange(LANES)
    mats = []
    for tile in NORM_TILES:
        for c in range(IN_CHUNKS):
            dim = _NORM_KIND.get(_chunk_piece(tile * IN_TN + c * LANES), LANES)
            mats.append((lane[:, None] // dim == lane[None, :] // dim).astype(np.float32) / dim)
    return np.stack(mats)


def _column_scales(gains):
    scale = {"qA": HEAD_DIM ** -0.5, "qB": HEAD_DIM ** -0.5 * LOG2E, "qC": C_HEAD_DIM ** -0.5}
    cols = []
    for tile in NORM_TILES:
        for c in range(IN_CHUNKS):
            piece = _chunk_piece(tile * IN_TN + c * LANES)
            if piece in _NORM_KIND:
                g = gains[piece].astype(F32) * scale.get(piece, 1.0)
                cols.append(jnp.tile(g, LANES // g.shape[0]))
            else:
                cols.append(jnp.zeros((LANES,), F32))
    return jnp.concatenate(cols).reshape(1, len(NORM_TILES) * IN_TN)


def _layer(x, mem, norm_gain, mem_norm_gain, w_rows, b_forget, q_gain_a, k_gain_a, sinks_a,
           q_gain_b, k_gain_b, q_gain_c, k_gain_c, w_mem_kv, w_branch_a, w_branch_b,
           w_branch_c, w_out):
    batch, seq, d = x.shape
    x2 = x.reshape(batch * seq, d)

    w_f = w_rows[F_START * K_CHUNKS:(F_START + F_SHIFT) * K_CHUNKS].reshape(F_SHIFT, d)
    w_f = jnp.pad(w_f, ((0, LANES - F_SHIFT), (0, 0))).astype(BF16)
    b_f = jnp.pad(b_forget.astype(F32), (0, LANES - F_SHIFT)).reshape(1, LANES)
    cscale = _column_scales({"qA": q_gain_a, "kA": k_gain_a, "qB": q_gain_b, "kB": k_gain_b,
                             "qC": q_gain_c})
    gmat = jnp.asarray(_group_mean_matrices(), BF16)
    place = jnp.asarray(_placement_matrix(), BF16)

    mk, mv = _mem_kv(mem, mem_norm_gain.reshape(1, d), w_mem_kv.astype(BF16),
                     k_gain_c.reshape(1, C_HEAD_DIM))
    hn, log_f = _norm_x(x2, norm_gain.reshape(1, d), w_f, b_f)
    proj = _in_proj(hn, w_rows, cscale, gmat)
    aug = _fox_bias(log_f, place, batch, seq)
    ya = _swa(sinks_a.astype(F32), proj, batch, seq)
    yb = _fox(proj, aug, batch, seq)
    yc = _mem_attn(proj, mk, mv, batch, seq)
    out = _merge(x2, ya, yb, yc, proj, w_branch_a.astype(BF16), w_branch_b.astype(BF16),
                 w_branch_c.astype(BF16), w_out.astype(BF16))
    return out.reshape(batch, seq, d)


def kernel(x, mem, norm_gain, mem_norm_gain, w_in, b_forget, q_gain_a, k_gain_a, sinks_a,
           q_gain_b, k_gain_b, q_gain_c, k_gain_c, w_mem_kv, w_branch_a, w_branch_b,
           w_branch_c, w_out):
    depth = norm_gain.shape[0]
    w_rows = jnp.swapaxes(w_in, 1, 2).reshape(depth, -1, LANES)
    for layer in range(depth):
        x = _layer(x, mem, norm_gain[layer], mem_norm_gain[layer], w_rows[layer], b_forget[layer],
                   q_gain_a[layer], k_gain_a[layer], sinks_a[layer], q_gain_b[layer],
                   k_gain_b[layer], q_gain_c[layer], k_gain_c[layer], w_mem_kv[layer],
                   w_branch_a[layer], w_branch_b[layer], w_branch_c[layer], w_out[layer])
    return x
```

```python
import functools

import jax
import jax.numpy as jnp
import numpy as np
from jax import lax
from jax.experimental import pallas as pl
from jax.experimental.pallas import tpu as pltpu

F32 = jnp.float32
BF16 = jnp.bfloat16

D_MODEL = 2048
HEAD_DIM = 64
A_Q_HEADS = 12
A_KV_HEADS = 4
A_GROUP = A_Q_HEADS // A_KV_HEADS
WINDOW = 128
B_HEADS = 12
C_HEADS = 4
C_HEAD_DIM = 128
A_WIDTH = A_Q_HEADS * HEAD_DIM
A_KV_WIDTH = A_KV_HEADS * HEAD_DIM
B_WIDTH = B_HEADS * HEAD_DIM
C_WIDTH = C_HEADS * C_HEAD_DIM
EPS = 1e-6
NEG = -1e30

LANES = 128
HALF = LANES // 2
VT_ROWS = HALF + 16
LOG2E = float(np.log2(np.e))
N_PIECES = 3
VMEM_LIMIT = 56 * 1024 * 1024

_SRC = {}
_off = 0
for _name, _w in (("qA", A_WIDTH), ("kA", A_KV_WIDTH), ("vA", A_KV_WIDTH), ("zA", A_WIDTH),
                  ("qB", B_WIDTH), ("kB", B_WIDTH), ("vB", B_WIDTH), ("zB", B_WIDTH),
                  ("fB", B_HEADS), ("qC", C_WIDTH), ("zC", C_WIDTH), ("g", 3 * D_MODEL)):
    _SRC[_name] = (_off, _w)
    _off += _w

F_START, F_SHIFT = _SRC["fB"]
_DST = {n: (o if o < F_START else o - F_SHIFT) for n, (o, _) in _SRC.items() if n != "fB"}
PROJ_WIDTH = _SRC["g"][0] + _SRC["g"][1] - F_SHIFT
assert all(o % LANES == 0 for o in _DST.values()) and F_START % LANES == 0


def _col_block(name, width, part=0):
    assert _DST[name] % width == 0
    return _DST[name] // width + part


def _rms(x, gain):
    ms = jnp.mean(x * x, axis=-1, keepdims=True)
    return x * lax.rsqrt(ms + EPS) * gain


def _sigmoid(t):
    return 0.5 * (jnp.tanh(0.5 * t) + 1.0)


def _dot_nt(a, b):
    return lax.dot_general(a, b, (((1,), (1,)), ((), ())), preferred_element_type=F32)


def _mem_kv_kernel(mem_ref, gain_ref, w_ref, kgain_ref, mk_ref, mv_ref):
    hn = _rms(mem_ref[0], gain_ref[...]).astype(BF16)
    kv = jnp.dot(hn, w_ref[...], preferred_element_type=F32)
    for h in range(C_HEADS):
        sl = slice(h * C_HEAD_DIM, (h + 1) * C_HEAD_DIM)
        mk_ref[0, :, sl] = _rms(kv[:, sl], kgain_ref[...]).astype(BF16)
    mv_ref[0] = kv[:, C_WIDTH:].astype(BF16)


def _mem_kv(mem, gain, w, kgain):
    b, m, d = mem.shape
    return pl.pallas_call(
        _mem_kv_kernel,
        grid=(b,),
        in_specs=[
            pl.BlockSpec((1, m, d), lambda i: (i, 0, 0)),
            pl.BlockSpec((1, d), lambda i: (0, 0)),
            pl.BlockSpec((d, 2 * C_WIDTH), lambda i: (0, 0)),
            pl.BlockSpec((1, C_HEAD_DIM), lambda i: (0, 0)),
        ],
        out_specs=[
            pl.BlockSpec((1, m, C_WIDTH), lambda i: (i, 0, 0)),
            pl.BlockSpec((1, m, C_WIDTH), lambda i: (i, 0, 0)),
        ],
        out_shape=[jax.ShapeDtypeStruct((b, m, C_WIDTH), BF16)] * 2,
        compiler_params=pltpu.CompilerParams(
            dimension_semantics=("arbitrary",), vmem_limit_bytes=VMEM_LIMIT),
        name="mem_kv",
    )(mem, gain, w, kgain)


NX_TM = 512


def _norm_x_kernel(x_ref, gain_ref, wf_ref, bf_ref, hn_ref, lf_ref):
    hn = _rms(x_ref[...], gain_ref[...]).astype(BF16)
    hn_ref[...] = hn
    f_logit = _dot_nt(hn, wf_ref[...]) + bf_ref[...]
    lf_ref[...] = jnp.minimum(f_logit, 0.0) - jnp.log1p(jnp.exp(-jnp.abs(f_logit)))


def _norm_x(x2, gain, wf, bfg):
    t, d = x2.shape
    return pl.pallas_call(
        _norm_x_kernel,
        grid=(t // NX_TM,),
        in_specs=[
            pl.BlockSpec((NX_TM, d), lambda i: (i, 0)),
            pl.BlockSpec((1, d), lambda i: (0, 0)),
            pl.BlockSpec((LANES, d), lambda i: (0, 0)),
            pl.BlockSpec((1, LANES), lambda i: (0, 0)),
        ],
        out_specs=[
            pl.BlockSpec((NX_TM, d), lambda i: (i, 0)),
            pl.BlockSpec((NX_TM, LANES), lambda i: (i, 0)),
        ],
        out_shape=[jax.ShapeDtypeStruct((t, d), BF16),
                   jax.ShapeDtypeStruct((t, LANES), F32)],
        compiler_params=pltpu.CompilerParams(
            dimension_semantics=("arbitrary",), vmem_limit_bytes=VMEM_LIMIT),
        name="norm_x",
    )(x2, gain, wf, bfg)


IN_TM = 1024
IN_TN = 1024
IN_CHUNKS = IN_TN // LANES
K_CHUNKS = D_MODEL // LANES
N_IN_TILES = PROJ_WIDTH // IN_TN
N_ALIGNED_TILES = F_START // IN_TN
assert F_START % IN_TN == 0 and PROJ_WIDTH % IN_TN == 0 and F_SHIFT < LANES

_NORM_KIND = {"qA": HEAD_DIM, "kA": HEAD_DIM, "qB": HEAD_DIM, "kB": HEAD_DIM, "qC": C_HEAD_DIM}


def _chunk_piece(col):
    for name, off in _DST.items():
        if off <= col < off + _SRC[name][1]:
            return name
    raise ValueError(col)


def _norm_prefix(tile):
    kinds = [_chunk_piece(tile * IN_TN + c * LANES) in _NORM_KIND for c in range(IN_CHUNKS)]
    n = sum(kinds)
    assert kinds == [True] * n + [False] * (IN_CHUNKS - n)
    return n


NORM_PREFIX = tuple(_norm_prefix(j) for j in range(N_IN_TILES))
NORM_TILES = tuple(j for j in range(N_IN_TILES) if NORM_PREFIX[j])


def _norm_slot(j):
    slot = 0
    for tile in NORM_TILES[1:]:
        slot = slot + (j >= tile).astype(jnp.int32)
    return slot


def _in_proj_kernel(hn_ref, w_ref, cscale_ref, gmat_ref, o_ref, wbf_ref):
    j = pl.program_id(0)
    i = pl.program_id(1)

    @pl.when(i == 0)
    def _():
        for c in range(K_CHUNKS):
            wbf_ref[:, c * LANES:(c + 1) * LANES] = (
                w_ref[pl.ds(c, IN_TN, stride=K_CHUNKS), :].astype(BF16))

    acc = _dot_nt(hn_ref[...], wbf_ref[...])

    def is_any(tiles):
        cond = j == tiles[0]
        for tile in tiles[1:]:
            cond = cond | (j == tile)
        return cond

    for prefix in sorted(set(NORM_PREFIX)):
        tiles = [t for t in range(N_IN_TILES) if NORM_PREFIX[t] == prefix]

        @pl.when(is_any(tiles))
        def _(prefix=prefix):
            for c in range(prefix):
                sl = slice(c * LANES, (c + 1) * LANES)
                a = acc[:, sl]
                ms = jnp.dot((a * a).astype(BF16), gmat_ref[c], preferred_element_type=F32)
                o_ref[:, sl] = (a * lax.rsqrt(ms + EPS) * cscale_ref[:, sl]).astype(BF16)
            if prefix < IN_CHUNKS:
                rest = slice(prefix * LANES, IN_TN)
                o_ref[:, rest] = acc[:, rest].astype(BF16)


def _in_proj(hn, w_rows, cscale, gmat):
    t, d = hn.shape
    grid = (N_IN_TILES, t // IN_TM)

    def w_index(j, i):
        col = j * IN_TN + jnp.where(j >= N_ALIGNED_TILES, F_SHIFT, 0)
        return (col * K_CHUNKS, 0)

    return pl.pallas_call(
        _in_proj_kernel,
        grid=grid,
        in_specs=[
            pl.BlockSpec((IN_TM, d), lambda j, i: (i, 0)),
            pl.BlockSpec((pl.Element(IN_TN * K_CHUNKS), pl.Element(LANES)), w_index),
            pl.BlockSpec((1, IN_TN), lambda j, i: (0, _norm_slot(j))),
            pl.BlockSpec((IN_CHUNKS, LANES, LANES), lambda j, i: (_norm_slot(j), 0, 0)),
        ],
        out_specs=pl.BlockSpec((IN_TM, IN_TN), lambda j, i: (i, j)),
        out_shape=jax.ShapeDtypeStruct((t, PROJ_WIDTH), BF16),
        scratch_shapes=[pltpu.VMEM((IN_TN, d), BF16)],
        compiler_params=pltpu.CompilerParams(
            dimension_semantics=("arbitrary", "arbitrary"), vmem_limit_bytes=VMEM_LIMIT),
        name="in_proj",
    )(hn, w_rows, cscale, gmat)


BIAS_TS = 512


def _fox_bias_kernel(lf_ref, place_ref, aug_ref, carry_ref):
    i = pl.program_id(1)

    @pl.when(i == 0)
    def _():
        carry_ref[...] = jnp.zeros_like(carry_ref)

    x = lf_ref[...]
    row = lax.broadcasted_iota(jnp.int32, x.shape, 0)
    shift = 1
    while shift < BIAS_TS:
        x = x + jnp.where(row >= shift, pltpu.roll(x, shift, 0), 0.0)
        shift *= 2
    c = x + carry_ref[...]
    carry_ref[...] = c[BIAS_TS - 1:BIAS_TS, :]
    rest = c * (-LOG2E)
    pieces = []
    for _ in range(N_PIECES):
        p = rest.astype(BF16)
        pieces.append(p)
        rest = rest - p.astype(F32)
    stacked = jnp.concatenate(pieces, axis=1)
    aug_ref[...] = jnp.dot(stacked, place_ref[...], preferred_element_type=F32).astype(BF16)


def _fox_bias(lf, place, batch, seq):
    nblk = seq // BIAS_TS
    return pl.pallas_call(
        _fox_bias_kernel,
        grid=(batch, nblk),
        in_specs=[
            pl.BlockSpec((BIAS_TS, LANES), lambda b, i: (b * nblk + i, 0)),
            pl.BlockSpec((N_PIECES * LANES, B_WIDTH), lambda b, i: (0, 0)),
        ],
        out_specs=pl.BlockSpec((BIAS_TS, B_WIDTH), lambda b, i: (b * nblk + i, 0)),
        out_shape=jax.ShapeDtypeStruct((batch * seq, B_WIDTH), BF16),
        scratch_shapes=[pltpu.VMEM((1, LANES), F32)],
        compiler_params=pltpu.CompilerParams(
            dimension_semantics=("arbitrary", "arbitrary"), vmem_limit_bytes=VMEM_LIMIT),
        name="fox_bias",
    )(lf, place)


def _aug_lane(head):
    return HALF if head % 2 == 0 else 0


def _placement_matrix():
    place = np.zeros((N_PIECES * LANES, B_WIDTH), np.float32)
    for h in range(B_HEADS):
        for p in range(N_PIECES):
            place[p * LANES + h, (h // 2) * LANES + _aug_lane(h) + p] = 1.0
    return place


SWA_QB = 4
SWA_WIN = (SWA_QB + 1) * WINDOW


def _alibi_slopes_log2():
    return [float(2.0 ** (-8.0 * (h + 1) / A_Q_HEADS)) * LOG2E for h in range(A_Q_HEADS)]


def _swa_key_bias():
    key = np.arange(2 * WINDOW)[:, None]
    qry = np.arange(WINDOW)[None, :]
    rel = qry + WINDOW - key
    visible = (rel >= 0) & (rel < WINDOW)
    slopes = np.asarray(_alibi_slopes_log2(), np.float32)[:, None, None]
    return np.where(visible[None], slopes * key[None].astype(np.float32), np.float32(NEG))


def _swa_kernel(sinks_ref, bias_ref, q_ref, kp_ref, kc_ref, vp_ref, vc_ref, o_ref,
                kop_ref, vt_ref, s_ref, p_ref, sh_ref, ot_ref):
    step = pl.program_id(1)
    blk = WINDOW
    lane = lax.broadcasted_iota(jnp.int32, (SWA_WIN, LANES), 1)

    kwin = jnp.concatenate([kp_ref[...], kc_ref[...]], axis=0).astype(F32)
    vwin_t = jnp.concatenate([vp_ref[...], vc_ref[...]], axis=0).astype(F32).T
    ones_rows = jnp.where(
        lax.broadcasted_iota(jnp.int32, (VT_ROWS - HALF, SWA_WIN), 0) == 0, 1.0, 0.0)
    for g in range(A_KV_HEADS):
        slab = kwin[:, (g // 2) * LANES:(g // 2 + 1) * LANES]
        own = (lane < HALF) if g % 2 == 0 else (lane >= HALF)
        kz = jnp.where(own, slab, 0.0)
        kop_ref[g, 0] = kz.astype(BF16)
        kop_ref[g, 1] = pltpu.roll(kz, HALF, 1).astype(BF16)
        vt_ref[g] = jnp.concatenate(
            [vwin_t[g * HEAD_DIM:(g + 1) * HEAD_DIM], ones_rows], axis=0).astype(BF16)

    t_win = lax.broadcasted_iota(jnp.int32, (1, blk), 1).astype(F32) + float(blk)
    slopes = _alibi_slopes_log2()
    sinks = [sinks_ref[h] * LOG2E + slopes[h] * t_win for h in range(A_Q_HEADS)]
    key_row = lax.broadcasted_iota(jnp.int32, (2 * blk, blk), 0)

    items = [(i, g) for i in range(SWA_QB) for g in range(A_KV_HEADS)]
    keys_of = lambda i: slice(i * blk, (i + 2) * blk)

    def stage_scores(n):
        i, g = items[n]
        for k in range(A_GROUP):
            h = g * A_GROUP + k
            q_slab = q_ref[i * blk:(i + 1) * blk, (h // 2) * LANES:(h // 2 + 1) * LANES]
            s_ref[n % 2, k] = _dot_nt(kop_ref[g, (h + g) % 2, keys_of(i), :], q_slab)

    def stage_softmax(n):
        i, g = items[n]
        for k in range(A_GROUP):
            h = g * A_GROUP + k
            st = s_ref[n % 2, k] + bias_ref[h]
            if i == 0:
                st = jnp.where(((step * SWA_QB - 1) * blk + key_row) >= 0, st, NEG)
            m = jnp.maximum(jnp.max(st, axis=0, keepdims=True), sinks[h])
            p_ref[n % 2, k] = jnp.exp2(st - m).astype(BF16)
            sh_ref[n % 2, k] = jnp.exp2(sinks[h] - m)

    def stage_values(n):
        i, g = items[n]
        for k in range(A_GROUP):
            h = g * A_GROUP + k
            ot = jnp.dot(vt_ref[g, :, keys_of(i)], p_ref[n % 2, k],
                         preferred_element_type=F32)
            ot_ref[h * HEAD_DIM:(h + 1) * HEAD_DIM, :] = (
                ot[0:HEAD_DIM] / (ot[HEAD_DIM:HEAD_DIM + 1] + sh_ref[n % 2, k]))
        if g == A_KV_HEADS - 1:
            o_ref[i * blk:(i + 1) * blk, :] = ot_ref[...].T.astype(BF16)

    for n in range(len(items) + 2):
        if n < len(items):
            stage_scores(n)
        if 2 <= n:
            stage_values(n - 2)
        if 1 <= n <= len(items):
            stage_softmax(n - 1)


def _swa(sinks, bias, proj, batch, seq):
    blk = WINDOW
    nb = seq // blk
    steps = nb // SWA_QB
    qcol = _col_block("qA", A_WIDTH)
    kcol = _col_block("kA", A_KV_WIDTH)
    vcol = _col_block("vA", A_KV_WIDTH)
    cur = lambda b, s: b * steps + s
    prev = lambda b, s: b * nb + jnp.maximum(s * SWA_QB - 1, 0)
    return pl.pallas_call(
        _swa_kernel,
        grid=(batch, steps),
        in_specs=[
            pl.BlockSpec(memory_space=pltpu.SMEM),
            pl.BlockSpec(bias.shape, lambda b, s: (0, 0, 0)),
            pl.BlockSpec((SWA_QB * blk, A_WIDTH), lambda b, s: (cur(b, s), qcol)),
            pl.BlockSpec((blk, A_KV_WIDTH), lambda b, s: (prev(b, s), kcol)),
            pl.BlockSpec((SWA_QB * blk, A_KV_WIDTH), lambda b, s: (cur(b, s), kcol)),
            pl.BlockSpec((blk, A_KV_WIDTH), lambda b, s: (prev(b, s), vcol)),
            pl.BlockSpec((SWA_QB * blk, A_KV_WIDTH), lambda b, s: (cur(b, s), vcol)),
        ],
        out_specs=pl.BlockSpec((SWA_QB * blk, A_WIDTH), lambda b, s: (cur(b, s), 0)),
        out_shape=jax.ShapeDtypeStruct((batch * seq, A_WIDTH), BF16),
        scratch_shapes=[
            pltpu.VMEM((A_KV_HEADS, 2, SWA_WIN, LANES), BF16),
            pltpu.VMEM((A_KV_HEADS, VT_ROWS, SWA_WIN), BF16),
            pltpu.VMEM((2, A_GROUP, 2 * blk, blk), F32),
            pltpu.VMEM((2, A_GROUP, 2 * blk, blk), BF16),
            pltpu.VMEM((2, A_GROUP, 1, blk), F32),
            pltpu.VMEM((A_WIDTH, blk), F32),
        ],
        compiler_params=pltpu.CompilerParams(
            dimension_semantics=("arbitrary", "arbitrary"), vmem_limit_bytes=VMEM_LIMIT),
        name="swa",
    )(sinks, bias, proj, proj, proj, proj, proj)


FOX_T = 512


FOX_H = FOX_T // 2


def _fox_kernel(q_ref, k_ref, v_ref, aug_ref, o_ref,
                kop_ref, vt_ref, qop_ref, s_ref, cm_ref, p_ref, al_ref, m_ref, acc_ref):
    seq = k_ref.shape[0]
    t = FOX_T
    hk = FOX_H

    lane = lax.broadcasted_iota(jnp.int32, (hk, LANES), 1)
    low = lane < HALF

    def build(r, carry):
        sl = pl.ds(pl.multiple_of(r * hk, hk), hk)
        kk = k_ref[sl, :].astype(F32)
        aa = aug_ref[sl, :].astype(F32)
        vv = v_ref[sl, :].astype(F32)
        kop_ref[0, sl, :] = jnp.where(low, kk, aa).astype(BF16)
        kop_ref[1, sl, :] = jnp.where(low, aa, kk).astype(BF16)
        vvt = vv.T
        ones_rows = jnp.where(
            lax.broadcasted_iota(jnp.int32, (VT_ROWS - HALF, hk), 0) == 0, 1.0, 0.0)
        for hh in range(2):
            vt_ref[hh, r] = jnp.concatenate(
                [vvt[hh * HALF:(hh + 1) * HALF], ones_rows], axis=0).astype(BF16)
        return carry

    lax.fori_loop(0, seq // hk, build, 0)

    def scores(slot, half, masked):
        ks = pl.ds(pl.multiple_of(half * hk, hk), hk)
        for hh in range(2):
            st = _dot_nt(kop_ref[hh, ks, :], qop_ref[hh])
            if masked:
                key = lax.broadcasted_iota(jnp.int32, (hk, t), 0) + slot * hk
                qry = lax.broadcasted_iota(jnp.int32, (hk, t), 1)
                st = jnp.where(key <= qry, st, NEG)
            s_ref[slot, hh] = st
            cm_ref[slot, hh] = jnp.max(st, axis=0, keepdims=True)

    def soft(slot):
        for hh in range(2):
            m_old = m_ref[hh]
            m_new = jnp.maximum(m_old, cm_ref[slot, hh])
            al_ref[slot, hh] = jnp.exp2(m_old - m_new)
            p_ref[slot, hh] = jnp.exp2(s_ref[slot, hh] - m_new).astype(BF16)
            m_ref[hh] = m_new

    def pv(slot, half):
        for hh in range(2):
            acc_ref[hh] = acc_ref[hh] * al_ref[slot, hh] + jnp.dot(
                vt_ref[hh, half], p_ref[slot, hh], preferred_element_type=F32)

    def step(i, prev, has_next):
        pv(0, 2 * prev)
        soft(0)
        scores(1, 2 * i + 1, False)
        if has_next:
            scores(0, 2 * i + 2, False)
        pv(1, 2 * prev + 1)
        soft(1)

    lane_q = lax.broadcasted_iota(jnp.int32, (t, LANES), 1)
    low_q = lane_q < HALF
    ones0 = jnp.where((lane_q >= HALF) & (lane_q < HALF + N_PIECES), 1.0, 0.0)
    ones1 = jnp.where(lane_q < N_PIECES, 1.0, 0.0)

    for qi in range(seq // t):
        rows = slice(qi * t, (qi + 1) * t)
        q = q_ref[rows, :].astype(F32)
        qop_ref[0] = jnp.where(low_q, q, ones0).astype(BF16)
        qop_ref[1] = jnp.where(low_q, ones1, q).astype(BF16)
        m_ref[...] = jnp.full(m_ref.shape, NEG, F32)
        acc_ref[...] = jnp.zeros(acc_ref.shape, F32)

        scores(0, 2 * qi, True)
        scores(1, 2 * qi + 1, True)
        soft(0)
        if qi > 0:
            scores(0, 0, False)
        soft(1)

        loop_pairs = max(qi - 1, 0) // 2

        def pair(j, carry, qi=qi):
            first = 2 * j
            step(first, jnp.where(j == 0, qi, first - 1), True)
            step(first + 1, first, True)
            return carry

        if loop_pairs:
            lax.fori_loop(0, loop_pairs, pair, 0)
        for i in range(2 * loop_pairs, qi):
            step(i, i - 1 if i > 0 else qi, i + 1 < qi)
        last = qi - 1 if qi > 0 else qi
        pv(0, 2 * last)
        pv(1, 2 * last + 1)

        ot = jnp.concatenate(
            [acc_ref[hh, 0:HALF, :] / acc_ref[hh, HALF:HALF + 1, :] for hh in range(2)], axis=0)
        o_ref[rows, :] = ot.T.astype(BF16)


def _fox(proj, aug, batch, seq):
    t = FOX_T
    nq = seq // t
    pairs = B_HEADS // 2
    qcol = _col_block("qB", LANES)
    kcol = _col_block("kB", LANES)
    vcol = _col_block("vB", LANES)
    return pl.pallas_call(
        _fox_kernel,
        grid=(batch, pairs),
        in_specs=[
            pl.BlockSpec((seq, LANES), lambda b, p: (b, qcol + p)),
            pl.BlockSpec((seq, LANES), lambda b, p: (b, kcol + p)),
            pl.BlockSpec((seq, LANES), lambda b, p: (b, vcol + p)),
            pl.BlockSpec((seq, LANES), lambda b, p: (b, p)),
        ],
        out_specs=pl.BlockSpec((seq, LANES), lambda b, p: (b, p)),
        out_shape=jax.ShapeDtypeStruct((batch * seq, B_WIDTH), BF16),
        scratch_shapes=[
            pltpu.VMEM((2, seq, LANES), BF16),
            pltpu.VMEM((2, seq // FOX_H, VT_ROWS, FOX_H), BF16),
            pltpu.VMEM((2, t, LANES), BF16),
            pltpu.VMEM((2, 2, FOX_H, t), F32),
            pltpu.VMEM((2, 2, 1, t), F32),
            pltpu.VMEM((2, 2, FOX_H, t), BF16),
            pltpu.VMEM((2, 2, 1, t), F32),
            pltpu.VMEM((2, 1, t), F32),
            pltpu.VMEM((2, VT_ROWS, t), F32),
        ],
        compiler_params=pltpu.CompilerParams(
            dimension_semantics=("arbitrary", "arbitrary"),
            vmem_limit_bytes=VMEM_LIMIT),
        name="fox",
    )(proj, proj, proj, aug)


MEM_TM = 512


def _mem_attn_kernel(q_ref, mk_ref, mv_ref, o_ref):
    for h in range(C_HEADS):
        sl = slice(h * C_HEAD_DIM, (h + 1) * C_HEAD_DIM)
        s = _dot_nt(q_ref[:, sl], mk_ref[0, :, sl])
        m = jnp.max(s, axis=-1, keepdims=True)
        p = jnp.exp(s - m)
        denom = jnp.sum(p, axis=-1, keepdims=True)
        o = jnp.dot(p.astype(BF16), mv_ref[0, :, sl], preferred_element_type=F32)
        o_ref[:, sl] = (o / denom).astype(BF16)


def _mem_attn(proj, mk, mv, batch, seq):
    nt = seq // MEM_TM
    mlen = mk.shape[1]
    qcol = _col_block("qC", C_WIDTH)
    return pl.pallas_call(
        _mem_attn_kernel,
        grid=(batch * nt,),
        in_specs=[
            pl.BlockSpec((MEM_TM, C_WIDTH), lambda i: (i, qcol)),
            pl.BlockSpec((1, mlen, C_WIDTH), lambda i: (i // nt, 0, 0)),
            pl.BlockSpec((1, mlen, C_WIDTH), lambda i: (i // nt, 0, 0)),
        ],
        out_specs=pl.BlockSpec((MEM_TM, C_WIDTH), lambda i: (i, 0)),
        out_shape=jax.ShapeDtypeStruct((batch * seq, C_WIDTH), BF16),
        compiler_params=pltpu.CompilerParams(
            dimension_semantics=("arbitrary",), vmem_limit_bytes=VMEM_LIMIT),
        name="mem_attn",
    )(proj, mk, mv)


MERGE_TM = 256


Z_BLOCK = 256
Z_PARTS = A_WIDTH // Z_BLOCK
assert A_WIDTH == B_WIDTH and A_WIDTH % Z_BLOCK == 0


def _merge_kernel(x_ref, ya_ref, yb_ref, yc_ref, *refs):
    za_refs = refs[:Z_PARTS]
    zb_refs = refs[Z_PARTS:2 * Z_PARTS]
    zc_ref, g0_ref, g1_ref, g2_ref, wa_ref, wb_ref, wc_ref, wo_ref, o_ref = refs[2 * Z_PARTS:]

    def branch(y_ref, z_refs, w_ref):
        z = jnp.concatenate([r[...] for r in z_refs], axis=1).astype(F32)
        h = (y_ref[...].astype(F32) * (z * _sigmoid(z))).astype(BF16)
        return jnp.dot(h, w_ref[...], preferred_element_type=F32)

    y = _sigmoid(g0_ref[...].astype(F32)) * branch(ya_ref, za_refs, wa_ref)
    y = y + _sigmoid(g1_ref[...].astype(F32)) * branch(yb_ref, zb_refs, wb_ref)
    y = y + _sigmoid(g2_ref[...].astype(F32)) * branch(yc_ref, (zc_ref,), wc_ref)
    o_ref[...] = x_ref[...] + jnp.dot(y.astype(BF16), wo_ref[...], preferred_element_type=F32)


def _merge(x2, ya, yb, yc, proj, wa, wb, wc, wo):
    t, d = x2.shape
    tm = MERGE_TM
    gcol = _col_block("g", D_MODEL)
    row = lambda width, col: pl.BlockSpec((tm, width), lambda i: (i, col))
    full = lambda a: pl.BlockSpec(a.shape, lambda i: (0, 0), pipeline_mode=pl.Buffered(1))
    z_specs = lambda name: [row(Z_BLOCK, _col_block(name, Z_BLOCK, part)) for part in range(Z_PARTS)]
    n_proj = 2 * Z_PARTS + 4
    return pl.pallas_call(
        _merge_kernel,
        grid=(t // tm,),
        in_specs=[
            row(d, 0), row(A_WIDTH, 0), row(B_WIDTH, 0), row(C_WIDTH, 0),
            *z_specs("zA"), *z_specs("zB"), row(C_WIDTH, _col_block("zC", C_WIDTH)),
            row(d, gcol), row(d, gcol + 1), row(d, gcol + 2),
            full(wa), full(wb), full(wc), full(wo),
        ],
        out_specs=pl.BlockSpec((tm, d), lambda i: (i, 0)),
        out_shape=jax.ShapeDtypeStruct((t, d), F32),
        compiler_params=pltpu.CompilerParams(
            dimension_semantics=("arbitrary",), vmem_limit_bytes=VMEM_LIMIT),
        name="merge",
    )(x2, ya, yb, yc, *([proj] * n_proj), wa, wb, wc, wo)


def _group_mean_matrices():
    lane = np.arange(LANES)
    mats = []
    for tile in NORM_TILES:
        for c in range(IN_CHUNKS):
            dim = _NORM_KIND.get(_chunk_piece(tile * IN_TN + c * LANES), LANES)
            mats.append((lane[:, None] // dim == lane[None, :] // dim).astype(np.float32) / dim)
    return np.stack(mats)


def _column_scales(gains):
    scale = {"qA": HEAD_DIM ** -0.5 * LOG2E, "qB": HEAD_DIM ** -0.5 * LOG2E,
             "qC": C_HEAD_DIM ** -0.5}
    cols = []
    for tile in NORM_TILES:
        for c in range(IN_CHUNKS):
            piece = _chunk_piece(tile * IN_TN + c * LANES)
            if piece in _NORM_KIND:
                g = gains[piece].astype(F32) * scale.get(piece, 1.0)
                cols.append(jnp.tile(g, LANES // g.shape[0]))
            else:
                cols.append(jnp.zeros((LANES,), F32))
    return jnp.concatenate(cols).reshape(1, len(NORM_TILES) * IN_TN)


def _layer(x, mem, norm_gain, mem_norm_gain, w_rows, b_forget, q_gain_a, k_gain_a, sinks_a,
           q_gain_b, k_gain_b, q_gain_c, k_gain_c, w_mem_kv, w_branch_a, w_branch_b,
           w_branch_c, w_out):
    batch, seq, d = x.shape
    x2 = x.reshape(batch * seq, d)

    w_f = w_rows[F_START * K_CHUNKS:(F_START + F_SHIFT) * K_CHUNKS].reshape(F_SHIFT, d)
    w_f = jnp.pad(w_f, ((0, LANES - F_SHIFT), (0, 0))).astype(BF16)
    b_f = jnp.pad(b_forget.astype(F32), (0, LANES - F_SHIFT)).reshape(1, LANES)
    cscale = _column_scales({"qA": q_gain_a, "kA": k_gain_a, "qB": q_gain_b, "kB": k_gain_b,
                             "qC": q_gain_c})
    gmat = jnp.asarray(_group_mean_matrices(), BF16)
    place = jnp.asarray(_placement_matrix(), BF16)

    mk, mv = _mem_kv(mem, mem_norm_gain.reshape(1, d), w_mem_kv.astype(BF16),
                     k_gain_c.reshape(1, C_HEAD_DIM))
    hn, log_f = _norm_x(x2, norm_gain.reshape(1, d), w_f, b_f)
    proj = _in_proj(hn, w_rows, cscale, gmat)
    aug = _fox_bias(log_f, place, batch, seq)
    ya = _swa(sinks_a.astype(F32), jnp.asarray(_swa_key_bias(), F32), proj, batch, seq)
    yb = _fox(proj, aug, batch, seq)
    yc = _mem_attn(proj, mk, mv, batch, seq)
    out = _merge(x2, ya, yb, yc, proj, w_branch_a.astype(BF16), w_branch_b.astype(BF16),
                 w_branch_c.astype(BF16), w_out.astype(BF16))
    return out.reshape(batch, seq, d)


def kernel(x, mem, norm_gain, mem_norm_gain, w_in, b_forget, q_gain_a, k_gain_a, sinks_a,
           q_gain_b, k_gain_b, q_gain_c, k_gain_c, w_mem_kv, w_branch_a, w_branch_b,
           w_branch_c, w_out):
    depth = norm_gain.shape[0]
    w_rows = jnp.swapaxes(w_in, 1, 2).reshape(depth, -1, LANES)
    for layer in range(depth):
        x = _layer(x, mem, norm_gain[layer], mem_norm_gain[layer], w_rows[layer], b_forget[layer],
                   q_gain_a[layer], k_gain_a[layer], sinks_a[layer], q_gain_b[layer],
                   k_gain_b[layer], q_gain_c[layer], k_gain_c[layer], w_mem_kv[layer],
                   w_branch_a[layer], w_branch_b[layer], w_branch_c[layer], w_out[layer])
    return x
```

```python
import functools

import jax
import jax.numpy as jnp
import numpy as np
from jax import lax
from jax.experimental import pallas as pl
from jax.experimental.pallas import tpu as pltpu

F32 = jnp.float32
BF16 = jnp.bfloat16

D_MODEL = 2048
HEAD_DIM = 64
A_Q_HEADS = 12
A_KV_HEADS = 4
A_GROUP = A_Q_HEADS // A_KV_HEADS
WINDOW = 128
B_HEADS = 12
C_HEADS = 4
C_HEAD_DIM = 128
A_WIDTH = A_Q_HEADS * HEAD_DIM
A_KV_WIDTH = A_KV_HEADS * HEAD_DIM
B_WIDTH = B_HEADS * HEAD_DIM
C_WIDTH = C_HEADS * C_HEAD_DIM
EPS = 1e-6
NEG = -1e30

LANES = 128
HALF = LANES // 2
VT_ROWS = HALF + 16
LOG2E = float(np.log2(np.e))
N_PIECES = 3
VMEM_LIMIT = 56 * 1024 * 1024

_SRC = {}
_off = 0
for _name, _w in (("qA", A_WIDTH), ("kA", A_KV_WIDTH), ("vA", A_KV_WIDTH), ("zA", A_WIDTH),
                  ("qB", B_WIDTH), ("kB", B_WIDTH), ("vB", B_WIDTH), ("zB", B_WIDTH),
                  ("fB", B_HEADS), ("qC", C_WIDTH), ("zC", C_WIDTH), ("g", 3 * D_MODEL)):
    _SRC[_name] = (_off, _w)
    _off += _w

F_START, F_SHIFT = _SRC["fB"]
_DST = {n: (o if o < F_START else o - F_SHIFT) for n, (o, _) in _SRC.items() if n != "fB"}
PROJ_WIDTH = _SRC["g"][0] + _SRC["g"][1] - F_SHIFT
assert all(o % LANES == 0 for o in _DST.values()) and F_START % LANES == 0


def _col_block(name, width, part=0):
    assert _DST[name] % width == 0
    return _DST[name] // width + part


def _rms(x, gain):
    ms = jnp.mean(x * x, axis=-1, keepdims=True)
    return x * lax.rsqrt(ms + EPS) * gain


def _sigmoid(t):
    return 0.5 * (jnp.tanh(0.5 * t) + 1.0)


def _dot_nt(a, b):
    return lax.dot_general(a, b, (((1,), (1,)), ((), ())), preferred_element_type=F32)


def _mem_kv_kernel(mem_ref, gain_ref, w_ref, kgain_ref, mk_ref, mv_ref):
    hn = _rms(mem_ref[0], gain_ref[...]).astype(BF16)
    kv = jnp.dot(hn, w_ref[...], preferred_element_type=F32)
    for h in range(C_HEADS):
        sl = slice(h * C_HEAD_DIM, (h + 1) * C_HEAD_DIM)
        mk_ref[0, :, sl] = _rms(kv[:, sl], kgain_ref[...]).astype(BF16)
    mv_ref[0] = kv[:, C_WIDTH:].astype(BF16)


def _mem_kv(mem, gain, w, kgain):
    b, m, d = mem.shape
    return pl.pallas_call(
        _mem_kv_kernel,
        grid=(b,),
        in_specs=[
            pl.BlockSpec((1, m, d), lambda i: (i, 0, 0)),
            pl.BlockSpec((1, d), lambda i: (0, 0)),
            pl.BlockSpec((d, 2 * C_WIDTH), lambda i: (0, 0)),
            pl.BlockSpec((1, C_HEAD_DIM), lambda i: (0, 0)),
        ],
        out_specs=[
            pl.BlockSpec((1, m, C_WIDTH), lambda i: (i, 0, 0)),
            pl.BlockSpec((1, m, C_WIDTH), lambda i: (i, 0, 0)),
        ],
        out_shape=[jax.ShapeDtypeStruct((b, m, C_WIDTH), BF16)] * 2,
        compiler_params=pltpu.CompilerParams(
            dimension_semantics=("arbitrary",), vmem_limit_bytes=VMEM_LIMIT),
        name="mem_kv",
    )(mem, gain, w, kgain)


NX_TM = 512


def _norm_x_kernel(x_ref, gain_ref, wf_ref, bf_ref, hn_ref, lf_ref):
    hn = _rms(x_ref[...], gain_ref[...]).astype(BF16)
    hn_ref[...] = hn
    f_logit = _dot_nt(hn, wf_ref[...]) + bf_ref[...]
    lf_ref[...] = jnp.minimum(f_logit, 0.0) - jnp.log1p(jnp.exp(-jnp.abs(f_logit)))


def _norm_x(x2, gain, wf, bfg):
    t, d = x2.shape
    return pl.pallas_call(
        _norm_x_kernel,
        grid=(t // NX_TM,),
        in_specs=[
            pl.BlockSpec((NX_TM, d), lambda i: (i, 0)),
            pl.BlockSpec((1, d), lambda i: (0, 0)),
            pl.BlockSpec((LANES, d), lambda i: (0, 0)),
            pl.BlockSpec((1, LANES), lambda i: (0, 0)),
        ],
        out_specs=[
            pl.BlockSpec((NX_TM, d), lambda i: (i, 0)),
            pl.BlockSpec((NX_TM, LANES), lambda i: (i, 0)),
        ],
        out_shape=[jax.ShapeDtypeStruct((t, d), BF16),
                   jax.ShapeDtypeStruct((t, LANES), F32)],
        compiler_params=pltpu.CompilerParams(
            dimension_semantics=("arbitrary",), vmem_limit_bytes=VMEM_LIMIT),
        name="norm_x",
    )(x2, gain, wf, bfg)


IN_TM = 1024
IN_TN = 1024
IN_CHUNKS = IN_TN // LANES
K_CHUNKS = D_MODEL // LANES
N_IN_TILES = PROJ_WIDTH // IN_TN
N_ALIGNED_TILES = F_START // IN_TN
assert F_START % IN_TN == 0 and PROJ_WIDTH % IN_TN == 0 and F_SHIFT < LANES

_NORM_KIND = {"qA": HEAD_DIM, "kA": HEAD_DIM, "qB": HEAD_DIM, "kB": HEAD_DIM, "qC": C_HEAD_DIM}


def _chunk_piece(col):
    for name, off in _DST.items():
        if off <= col < off + _SRC[name][1]:
            return name
    raise ValueError(col)


def _norm_prefix(tile):
    kinds = [_chunk_piece(tile * IN_TN + c * LANES) in _NORM_KIND for c in range(IN_CHUNKS)]
    n = sum(kinds)
    assert kinds == [True] * n + [False] * (IN_CHUNKS - n)
    return n


NORM_PREFIX = tuple(_norm_prefix(j) for j in range(N_IN_TILES))
NORM_TILES = tuple(j for j in range(N_IN_TILES) if NORM_PREFIX[j])


def _norm_slot(j):
    slot = 0
    for tile in NORM_TILES[1:]:
        slot = slot + (j >= tile).astype(jnp.int32)
    return slot


IN_MCH = 512


def _gather_weight_rows(w_ref, dst_ref, src_row, dst_rows, n_rows):
    for c in range(K_CHUNKS):
        dst_ref[dst_rows, c * LANES:(c + 1) * LANES] = (
            w_ref[pl.ds(src_row + c, n_rows, stride=K_CHUNKS), :].astype(BF16))


def _w_tile0_kernel(w_ref, o_ref):
    _gather_weight_rows(w_ref, o_ref, 0, slice(None), IN_TN)


def _w_index(tile):
    col = tile * IN_TN + jnp.where(tile >= N_ALIGNED_TILES, F_SHIFT, 0)
    return (col * K_CHUNKS, 0)


def _w_tile0(w_rows):
    return pl.pallas_call(
        _w_tile0_kernel,
        grid=(1,),
        in_specs=[pl.BlockSpec((pl.Element(IN_TN * K_CHUNKS), pl.Element(LANES)),
                               lambda i: _w_index(i))],
        out_specs=pl.BlockSpec((IN_TN, D_MODEL), lambda i: (0, 0)),
        out_shape=jax.ShapeDtypeStruct((IN_TN, D_MODEL), BF16),
        compiler_params=pltpu.CompilerParams(
            dimension_semantics=("arbitrary",), vmem_limit_bytes=VMEM_LIMIT),
        name="w_tile0",
    )(w_rows)


def _in_proj_kernel(n_row_tiles, hn_ref, w_ref, w0_ref, cscale_ref, gmat_ref, o_ref,
                    wbf_a, wbf_b, acc_ref):
    j = pl.program_id(0)
    i = pl.program_id(1)
    share = IN_TN // n_row_tiles

    @pl.when((j == 0) & (i == 0))
    def _():
        wbf_a[...] = w0_ref[...]

    def epilogue(chunk, prefix):
        rows = slice(chunk * IN_MCH, (chunk + 1) * IN_MCH)
        acc = acc_ref.at[chunk % 2]
        for c in range(prefix):
            sl = slice(c * LANES, (c + 1) * LANES)
            a = acc[:, sl]
            ms = jnp.dot((a * a).astype(BF16), gmat_ref[c], preferred_element_type=F32)
            o_ref[rows, sl] = (a * lax.rsqrt(ms + EPS) * cscale_ref[:, sl]).astype(BF16)
        if prefix < IN_CHUNKS:
            rest = slice(prefix * LANES, IN_TN)
            o_ref[rows, rest] = acc[:, rest].astype(BF16)

    def step(cur, nxt, prefix):
        src = pl.multiple_of(i * (share * K_CHUNKS), share * K_CHUNKS)
        dst = pl.ds(pl.multiple_of(i * share, share), share)
        _gather_weight_rows(w_ref, nxt, src, dst, share)
        n_chunks = hn_ref.shape[0] // IN_MCH
        for c in range(n_chunks + 1):
            if c < n_chunks:
                acc_ref[c % 2] = _dot_nt(hn_ref[c * IN_MCH:(c + 1) * IN_MCH, :], cur[...])
            if c >= 1:
                epilogue(c - 1, prefix)

    def is_any(tiles):
        cond = j == tiles[0]
        for tile in tiles[1:]:
            cond = cond | (j == tile)
        return cond

    for parity, prefix in sorted({(t % 2, NORM_PREFIX[t]) for t in range(N_IN_TILES)}):
        tiles = [t for t in range(N_IN_TILES) if (t % 2, NORM_PREFIX[t]) == (parity, prefix)]
        cur, nxt = (wbf_a, wbf_b) if parity == 0 else (wbf_b, wbf_a)
        pl.when(is_any(tiles))(functools.partial(step, cur, nxt, prefix))


def _in_proj(hn, w_rows, w_first, cscale, gmat):
    t, d = hn.shape
    n_row_tiles = t // IN_TM
    assert IN_TN % n_row_tiles == 0 and (IN_TN // n_row_tiles) % 16 == 0
    return pl.pallas_call(
        functools.partial(_in_proj_kernel, n_row_tiles),
        grid=(N_IN_TILES, n_row_tiles),
        in_specs=[
            pl.BlockSpec((IN_TM, d), lambda j, i: (i, 0)),
            pl.BlockSpec((pl.Element(IN_TN * K_CHUNKS), pl.Element(LANES)),
                         lambda j, i: _w_index(jnp.minimum(j + 1, N_IN_TILES - 1))),
            pl.BlockSpec((IN_TN, d), lambda j, i: (0, 0), pipeline_mode=pl.Buffered(1)),
            pl.BlockSpec((1, IN_TN), lambda j, i: (0, _norm_slot(j))),
            pl.BlockSpec((IN_CHUNKS, LANES, LANES), lambda j, i: (_norm_slot(j), 0, 0)),
        ],
        out_specs=pl.BlockSpec((IN_TM, IN_TN), lambda j, i: (i, j)),
        out_shape=jax.ShapeDtypeStruct((t, PROJ_WIDTH), BF16),
        scratch_shapes=[
            pltpu.VMEM((IN_TN, d), BF16),
            pltpu.VMEM((IN_TN, d), BF16),
            pltpu.VMEM((2, IN_MCH, IN_TN), F32),
        ],
        compiler_params=pltpu.CompilerParams(
            dimension_semantics=("arbitrary", "arbitrary"), vmem_limit_bytes=VMEM_LIMIT),
        name="in_proj",
    )(hn, w_rows, w_first, cscale, gmat)


BIAS_TS = 512


def _fox_bias_kernel(lf_ref, place_ref, aug_ref, carry_ref):
    i = pl.program_id(1)

    @pl.when(i == 0)
    def _():
        carry_ref[...] = jnp.zeros_like(carry_ref)

    x = lf_ref[...]
    row = lax.broadcasted_iota(jnp.int32, x.shape, 0)
    shift = 1
    while shift < BIAS_TS:
        x = x + jnp.where(row >= shift, pltpu.roll(x, shift, 0), 0.0)
        shift *= 2
    c = x + carry_ref[...]
    carry_ref[...] = c[BIAS_TS - 1:BIAS_TS, :]
    rest = c * (-LOG2E)
    pieces = []
    for _ in range(N_PIECES):
        p = rest.astype(BF16)
        pieces.append(p)
        rest = rest - p.astype(F32)
    stacked = jnp.concatenate(pieces, axis=1)
    aug_ref[...] = jnp.dot(stacked, place_ref[...], preferred_element_type=F32).astype(BF16)


def _fox_bias(lf, place, batch, seq):
    nblk = seq // BIAS_TS
    return pl.pallas_call(
        _fox_bias_kernel,
        grid=(batch, nblk),
        in_specs=[
            pl.BlockSpec((BIAS_TS, LANES), lambda b, i: (b * nblk + i, 0)),
            pl.BlockSpec((N_PIECES * LANES, B_WIDTH), lambda b, i: (0, 0)),
        ],
        out_specs=pl.BlockSpec((BIAS_TS, B_WIDTH), lambda b, i: (b * nblk + i, 0)),
        out_shape=jax.ShapeDtypeStruct((batch * seq, B_WIDTH), BF16),
        scratch_shapes=[pltpu.VMEM((1, LANES), F32)],
        compiler_params=pltpu.CompilerParams(
            dimension_semantics=("arbitrary", "arbitrary"), vmem_limit_bytes=VMEM_LIMIT),
        name="fox_bias",
    )(lf, place)


def _aug_lane(head):
    return HALF if head % 2 == 0 else 0


def _placement_matrix():
    place = np.zeros((N_PIECES * LANES, B_WIDTH), np.float32)
    for h in range(B_HEADS):
        for p in range(N_PIECES):
            place[p * LANES + h, (h // 2) * LANES + _aug_lane(h) + p] = 1.0
    return place


SWA_QB = 4
SWA_WIN = (SWA_QB + 1) * WINDOW


def _alibi_slopes_log2():
    return [float(2.0 ** (-8.0 * (h + 1) / A_Q_HEADS)) * LOG2E for h in range(A_Q_HEADS)]


def _swa_key_bias():
    key = np.arange(2 * WINDOW)[:, None]
    qry = np.arange(WINDOW)[None, :]
    rel = qry + WINDOW - key
    visible = (rel >= 0) & (rel < WINDOW)
    slopes = np.asarray(_alibi_slopes_log2(), np.float32)[:, None, None]
    return np.where(visible[None], slopes * key[None].astype(np.float32), np.float32(NEG))


def _swa_kernel(sinks_ref, bias_ref, q_ref, kp_ref, kc_ref, vp_ref, vc_ref, o_ref,
                kop_ref, vt_ref, s_ref, p_ref, sh_ref, ot_ref):
    step = pl.program_id(1)
    blk = WINDOW
    lane = lax.broadcasted_iota(jnp.int32, (SWA_WIN, LANES), 1)

    kwin = jnp.concatenate([kp_ref[...], kc_ref[...]], axis=0).astype(F32)
    vwin_t = jnp.concatenate([vp_ref[...], vc_ref[...]], axis=0).astype(F32).T
    ones_rows = jnp.where(
        lax.broadcasted_iota(jnp.int32, (VT_ROWS - HALF, SWA_WIN), 0) == 0, 1.0, 0.0)
    for g in range(A_KV_HEADS):
        slab = kwin[:, (g // 2) * LANES:(g // 2 + 1) * LANES]
        own = (lane < HALF) if g % 2 == 0 else (lane >= HALF)
        kz = jnp.where(own, slab, 0.0)
        kop_ref[g, 0] = kz.astype(BF16)
        kop_ref[g, 1] = pltpu.roll(kz, HALF, 1).astype(BF16)
        vt_ref[g] = jnp.concatenate(
            [vwin_t[g * HEAD_DIM:(g + 1) * HEAD_DIM], ones_rows], axis=0).astype(BF16)

    t_win = lax.broadcasted_iota(jnp.int32, (1, blk), 1).astype(F32) + float(blk)
    slopes = _alibi_slopes_log2()
    sinks = [sinks_ref[h] * LOG2E + slopes[h] * t_win for h in range(A_Q_HEADS)]
    key_row = lax.broadcasted_iota(jnp.int32, (2 * blk, blk), 0)

    items = [(i, g) for i in range(SWA_QB) for g in range(A_KV_HEADS)]
    keys_of = lambda i: slice(i * blk, (i + 2) * blk)

    def stage_scores(n):
        i, g = items[n]
        for k in range(A_GROUP):
            h = g * A_GROUP + k
            q_slab = q_ref[i * blk:(i + 1) * blk, (h // 2) * LANES:(h // 2 + 1) * LANES]
            s_ref[n % 2, k] = _dot_nt(kop_ref[g, (h + g) % 2, keys_of(i), :], q_slab)

    def stage_softmax(n):
        i, g = items[n]
        for k in range(A_GROUP):
            h = g * A_GROUP + k
            st = s_ref[n % 2, k] + bias_ref[h]
            if i == 0:
                st = jnp.where(((step * SWA_QB - 1) * blk + key_row) >= 0, st, NEG)
            m = jnp.maximum(jnp.max(st, axis=0, keepdims=True), sinks[h])
            p_ref[n % 2, k] = jnp.exp2(st - m).astype(BF16)
            sh_ref[n % 2, k] = jnp.exp2(sinks[h] - m)

    def stage_values(n):
        i, g = items[n]
        for k in range(A_GROUP):
            h = g * A_GROUP + k
            ot = jnp.dot(vt_ref[g, :, keys_of(i)], p_ref[n % 2, k],
                         preferred_element_type=F32)
            ot_ref[h * HEAD_DIM:(h + 1) * HEAD_DIM, :] = (
                ot[0:HEAD_DIM] / (ot[HEAD_DIM:HEAD_DIM + 1] + sh_ref[n % 2, k]))
        if g == A_KV_HEADS - 1:
            o_ref[i * blk:(i + 1) * blk, :] = ot_ref[...].T.astype(BF16)

    for n in range(len(items) + 2):
        if n < len(items):
            stage_scores(n)
        if 2 <= n:
            stage_values(n - 2)
        if 1 <= n <= len(items):
            stage_softmax(n - 1)


def _swa(sinks, bias, proj, batch, seq):
    blk = WINDOW
    nb = seq // blk
    steps = nb // SWA_QB
    qcol = _col_block("qA", A_WIDTH)
    kcol = _col_block("kA", A_KV_WIDTH)
    vcol = _col_block("vA", A_KV_WIDTH)
    cur = lambda b, s: b * steps + s
    prev = lambda b, s: b * nb + jnp.maximum(s * SWA_QB - 1, 0)
    return pl.pallas_call(
        _swa_kernel,
        grid=(batch, steps),
        in_specs=[
            pl.BlockSpec(memory_space=pltpu.SMEM),
            pl.BlockSpec(bias.shape, lambda b, s: (0, 0, 0)),
            pl.BlockSpec((SWA_QB * blk, A_WIDTH), lambda b, s: (cur(b, s), qcol)),
            pl.BlockSpec((blk, A_KV_WIDTH), lambda b, s: (prev(b, s), kcol)),
            pl.BlockSpec((SWA_QB * blk, A_KV_WIDTH), lambda b, s: (cur(b, s), kcol)),
            pl.BlockSpec((blk, A_KV_WIDTH), lambda b, s: (prev(b, s), vcol)),
            pl.BlockSpec((SWA_QB * blk, A_KV_WIDTH), lambda b, s: (cur(b, s), vcol)),
        ],
        out_specs=pl.BlockSpec((SWA_QB * blk, A_WIDTH), lambda b, s: (cur(b, s), 0)),
        out_shape=jax.ShapeDtypeStruct((batch * seq, A_WIDTH), BF16),
        scratch_shapes=[
            pltpu.VMEM((A_KV_HEADS, 2, SWA_WIN, LANES), BF16),
            pltpu.VMEM((A_KV_HEADS, VT_ROWS, SWA_WIN), BF16),
            pltpu.VMEM((2, A_GROUP, 2 * blk, blk), F32),
            pltpu.VMEM((2, A_GROUP, 2 * blk, blk), BF16),
            pltpu.VMEM((2, A_GROUP, 1, blk), F32),
            pltpu.VMEM((A_WIDTH, blk), F32),
        ],
        compiler_params=pltpu.CompilerParams(
            dimension_semantics=("arbitrary", "arbitrary"), vmem_limit_bytes=VMEM_LIMIT),
        name="swa",
    )(sinks, bias, proj, proj, proj, proj, proj)


FOX_T = 512


FOX_H = FOX_T // 2


def _fox_kernel(q_ref, k_ref, v_ref, aug_ref, o_ref,
                kop_ref, vt_ref, qop_ref, s_ref, cm_ref, p_ref, al_ref, m_ref, acc_ref):
    seq = k_ref.shape[0]
    t = FOX_T
    hk = FOX_H

    lane = lax.broadcasted_iota(jnp.int32, (hk, LANES), 1)
    low = lane < HALF

    def build(r, carry):
        sl = pl.ds(pl.multiple_of(r * hk, hk), hk)
        kk = k_ref[sl, :].astype(F32)
        aa = aug_ref[sl, :].astype(F32)
        vv = v_ref[sl, :].astype(F32)
        kop_ref[0, sl, :] = jnp.where(low, kk, aa).astype(BF16)
        kop_ref[1, sl, :] = jnp.where(low, aa, kk).astype(BF16)
        vvt = vv.T
        ones_rows = jnp.where(
            lax.broadcasted_iota(jnp.int32, (VT_ROWS - HALF, hk), 0) == 0, 1.0, 0.0)
        for hh in range(2):
            vt_ref[hh, r] = jnp.concatenate(
                [vvt[hh * HALF:(hh + 1) * HALF], ones_rows], axis=0).astype(BF16)
        return carry

    lax.fori_loop(0, seq // hk, build, 0)

    def scores(slot, half, masked):
        ks = pl.ds(pl.multiple_of(half * hk, hk), hk)
        for hh in range(2):
            st = _dot_nt(kop_ref[hh, ks, :], qop_ref[hh])
            if masked:
                key = lax.broadcasted_iota(jnp.int32, (hk, t), 0) + slot * hk
                qry = lax.broadcasted_iota(jnp.int32, (hk, t), 1)
                st = jnp.where(key <= qry, st, NEG)
            s_ref[slot, hh] = st
            cm_ref[slot, hh] = jnp.max(st, axis=0, keepdims=True)

    def soft(slot):
        for hh in range(2):
            m_old = m_ref[hh]
            m_new = jnp.maximum(m_old, cm_ref[slot, hh])
            al_ref[slot, hh] = jnp.exp2(m_old - m_new)
            p_ref[slot, hh] = jnp.exp2(s_ref[slot, hh] - m_new).astype(BF16)
            m_ref[hh] = m_new

    def pv(slot, half):
        for hh in range(2):
            acc_ref[hh] = acc_ref[hh] * al_ref[slot, hh] + jnp.dot(
                vt_ref[hh, half], p_ref[slot, hh], preferred_element_type=F32)

    def step(i, prev, has_next):
        pv(0, 2 * prev)
        soft(0)
        scores(1, 2 * i + 1, False)
        if has_next:
            scores(0, 2 * i + 2, False)
        pv(1, 2 * prev + 1)
        soft(1)

    lane_q = lax.broadcasted_iota(jnp.int32, (t, LANES), 1)
    low_q = lane_q < HALF
    ones0 = jnp.where((lane_q >= HALF) & (lane_q < HALF + N_PIECES), 1.0, 0.0)
    ones1 = jnp.where(lane_q < N_PIECES, 1.0, 0.0)

    for qi in range(seq // t):
        rows = slice(qi * t, (qi + 1) * t)
        q = q_ref[rows, :].astype(F32)
        qop_ref[0] = jnp.where(low_q, q, ones0).astype(BF16)
        qop_ref[1] = jnp.where(low_q, ones1, q).astype(BF16)
        m_ref[...] = jnp.full(m_ref.shape, NEG, F32)
        acc_ref[...] = jnp.zeros(acc_ref.shape, F32)

        scores(0, 2 * qi, True)
        scores(1, 2 * qi + 1, True)
        soft(0)
        if qi > 0:
            scores(0, 0, False)
        soft(1)

        loop_pairs = max(qi - 1, 0) // 2

        def pair(j, carry, qi=qi):
            first = 2 * j
            step(first, jnp.where(j == 0, qi, first - 1), True)
            step(first + 1, first, True)
            return carry

        if loop_pairs:
            lax.fori_loop(0, loop_pairs, pair, 0)
        for i in range(2 * loop_pairs, qi):
            step(i, i - 1 if i > 0 else qi, i + 1 < qi)
        last = qi - 1 if qi > 0 else qi
        pv(0, 2 * last)
        pv(1, 2 * last + 1)

        ot = jnp.concatenate(
            [acc_ref[hh, 0:HALF, :] / acc_ref[hh, HALF:HALF + 1, :] for hh in range(2)], axis=0)
        o_ref[rows, :] = ot.T.astype(BF16)


def _fox(proj, aug, batch, seq):
    t = FOX_T
    nq = seq // t
    pairs = B_HEADS // 2
    qcol = _col_block("qB", LANES)
    kcol = _col_block("kB", LANES)
    vcol = _col_block("vB", LANES)
    return pl.pallas_call(
        _fox_kernel,
        grid=(batch, pairs),
        in_specs=[
            pl.BlockSpec((seq, LANES), lambda b, p: (b, qcol + p)),
            pl.BlockSpec((seq, LANES), lambda b, p: (b, kcol + p)),
            pl.BlockSpec((seq, LANES), lambda b, p: (b, vcol + p)),
            pl.BlockSpec((seq, LANES), lambda b, p: (b, p)),
        ],
        out_specs=pl.BlockSpec((seq, LANES), lambda b, p: (b, p)),
        out_shape=jax.ShapeDtypeStruct((batch * seq, B_WIDTH), BF16),
        scratch_shapes=[
            pltpu.VMEM((2, seq, LANES), BF16),
            pltpu.VMEM((2, seq // FOX_H, VT_ROWS, FOX_H), BF16),
            pltpu.VMEM((2, t, LANES), BF16),
            pltpu.VMEM((2, 2, FOX_H, t), F32),
            pltpu.VMEM((2, 2, 1, t), F32),
            pltpu.VMEM((2, 2, FOX_H, t), BF16),
            pltpu.VMEM((2, 2, 1, t), F32),
            pltpu.VMEM((2, 1, t), F32),
            pltpu.VMEM((2, VT_ROWS, t), F32),
        ],
        compiler_params=pltpu.CompilerParams(
            dimension_semantics=("arbitrary", "arbitrary"),
            vmem_limit_bytes=VMEM_LIMIT),
        name="fox",
    )(proj, proj, proj, aug)


MEM_TM = 512


def _mem_attn_kernel(q_ref, mk_ref, mv_ref, o_ref):
    for h in range(C_HEADS):
        sl = slice(h * C_HEAD_DIM, (h + 1) * C_HEAD_DIM)
        s = _dot_nt(q_ref[:, sl], mk_ref[0, :, sl])
        m = jnp.max(s, axis=-1, keepdims=True)
        p = jnp.exp(s - m)
        denom = jnp.sum(p, axis=-1, keepdims=True)
        o = jnp.dot(p.astype(BF16), mv_ref[0, :, sl], preferred_element_type=F32)
        o_ref[:, sl] = (o / denom).astype(BF16)


def _mem_attn(proj, mk, mv, batch, seq):
    nt = seq // MEM_TM
    mlen = mk.shape[1]
    qcol = _col_block("qC", C_WIDTH)
    return pl.pallas_call(
        _mem_attn_kernel,
        grid=(batch * nt,),
        in_specs=[
            pl.BlockSpec((MEM_TM, C_WIDTH), lambda i: (i, qcol)),
            pl.BlockSpec((1, mlen, C_WIDTH), lambda i: (i // nt, 0, 0)),
            pl.BlockSpec((1, mlen, C_WIDTH), lambda i: (i // nt, 0, 0)),
        ],
        out_specs=pl.BlockSpec((MEM_TM, C_WIDTH), lambda i: (i, 0)),
        out_shape=jax.ShapeDtypeStruct((batch * seq, C_WIDTH), BF16),
        compiler_params=pltpu.CompilerParams(
            dimension_semantics=("arbitrary",), vmem_limit_bytes=VMEM_LIMIT),
        name="mem_attn",
    )(proj, mk, mv)


MERGE_TM = 256


Z_BLOCK = 256
Z_PARTS = A_WIDTH // Z_BLOCK
assert A_WIDTH == B_WIDTH and A_WIDTH % Z_BLOCK == 0


def _merge_kernel(x_ref, ya_ref, yb_ref, yc_ref, *refs):
    za_refs = refs[:Z_PARTS]
    zb_refs = refs[Z_PARTS:2 * Z_PARTS]
    zc_ref, g0_ref, g1_ref, g2_ref, wa_ref, wb_ref, wc_ref, wo_ref, o_ref = refs[2 * Z_PARTS:]

    def branch(y_ref, z_refs, w_ref):
        z = jnp.concatenate([r[...] for r in z_refs], axis=1).astype(F32)
        h = (y_ref[...].astype(F32) * (z * _sigmoid(z))).astype(BF16)
        return jnp.dot(h, w_ref[...], preferred_element_type=F32)

    y = _sigmoid(g0_ref[...].astype(F32)) * branch(ya_ref, za_refs, wa_ref)
    y = y + _sigmoid(g1_ref[...].astype(F32)) * branch(yb_ref, zb_refs, wb_ref)
    y = y + _sigmoid(g2_ref[...].astype(F32)) * branch(yc_ref, (zc_ref,), wc_ref)
    o_ref[...] = x_ref[...] + jnp.dot(y.astype(BF16), wo_ref[...], preferred_element_type=F32)


def _merge(x2, ya, yb, yc, proj, wa, wb, wc, wo):
    t, d = x2.shape
    tm = MERGE_TM
    gcol = _col_block("g", D_MODEL)
    row = lambda width, col: pl.BlockSpec((tm, width), lambda i: (i, col))
    full = lambda a: pl.BlockSpec(a.shape, lambda i: (0, 0), pipeline_mode=pl.Buffered(1))
    z_specs = lambda name: [row(Z_BLOCK, _col_block(name, Z_BLOCK, part)) for part in range(Z_PARTS)]
    n_proj = 2 * Z_PARTS + 4
    return pl.pallas_call(
        _merge_kernel,
        grid=(t // tm,),
        in_specs=[
            row(d, 0), row(A_WIDTH, 0), row(B_WIDTH, 0), row(C_WIDTH, 0),
            *z_specs("zA"), *z_specs("zB"), row(C_WIDTH, _col_block("zC", C_WIDTH)),
            row(d, gcol), row(d, gcol + 1), row(d, gcol + 2),
            full(wa), full(wb), full(wc), full(wo),
        ],
        out_specs=pl.BlockSpec((tm, d), lambda i: (i, 0)),
        out_shape=jax.ShapeDtypeStruct((t, d), F32),
        compiler_params=pltpu.CompilerParams(
            dimension_semantics=("arbitrary",), vmem_limit_bytes=VMEM_LIMIT),
        name="merge",
    )(x2, ya, yb, yc, *([proj] * n_proj), wa, wb, wc, wo)


def _group_mean_matrices():
    lane = np.arange(LANES)
    mats = []
    for tile in NORM_TILES:
        for c in range(IN_CHUNKS):
            dim = _NORM_KIND.get(_chunk_piece(tile * IN_TN + c * LANES), LANES)
            mats.append((lane[:, None] // dim == lane[None, :] // dim).astype(np.float32) / dim)
    return np.stack(mats)


def _column_scales(gains):
    scale = {"qA": HEAD_DIM ** -0.5 * LOG2E, "qB": HEAD_DIM ** -0.5 * LOG2E,
             "qC": C_HEAD_DIM ** -0.5}
    cols = []
    for tile in NORM_TILES:
        for c in range(IN_CHUNKS):
            piece = _chunk_piece(tile * IN_TN + c * LANES)
            if piece in _NORM_KIND:
                g = gains[piece].astype(F32) * scale.get(piece, 1.0)
                cols.append(jnp.tile(g, LANES // g.shape[0]))
            else:
                cols.append(jnp.zeros((LANES,), F32))
    return jnp.concatenate(cols).reshape(1, len(NORM_TILES) * IN_TN)


def _layer(x, mem, norm_gain, mem_norm_gain, w_rows, b_forget, q_gain_a, k_gain_a, sinks_a,
           q_gain_b, k_gain_b, q_gain_c, k_gain_c, w_mem_kv, w_branch_a, w_branch_b,
           w_branch_c, w_out):
    batch, seq, d = x.shape
    x2 = x.reshape(batch * seq, d)

    w_f = w_rows[F_START * K_CHUNKS:(F_START + F_SHIFT) * K_CHUNKS].reshape(F_SHIFT, d)
    w_f = jnp.pad(w_f, ((0, LANES - F_SHIFT), (0, 0))).astype(BF16)
    b_f = jnp.pad(b_forget.astype(F32), (0, LANES - F_SHIFT)).reshape(1, LANES)
    cscale = _column_scales({"qA": q_gain_a, "kA": k_gain_a, "qB": q_gain_b, "kB": k_gain_b,
                             "qC": q_gain_c})
    gmat = jnp.asarray(_group_mean_matrices(), BF16)
    place = jnp.asarray(_placement_matrix(), BF16)

    mk, mv = _mem_kv(mem, mem_norm_gain.reshape(1, d), w_mem_kv.astype(BF16),
                     k_gain_c.reshape(1, C_HEAD_DIM))
    hn, log_f = _norm_x(x2, norm_gain.reshape(1, d), w_f, b_f)
    proj = _in_proj(hn, w_rows, _w_tile0(w_rows), cscale, gmat)
    aug = _fox_bias(log_f, place, batch, seq)
    ya = _swa(sinks_a.astype(F32), jnp.asarray(_swa_key_bias(), F32), proj, batch, seq)
    yb = _fox(proj, aug, batch, seq)
    yc = _mem_attn(proj, mk, mv, batch, seq)
    out = _merge(x2, ya, yb, yc, proj, w_branch_a.astype(BF16), w_branch_b.astype(BF16),
                 w_branch_c.astype(BF16), w_out.astype(BF16))
    return out.reshape(batch, seq, d)


def kernel(x, mem, norm_gain, mem_norm_gain, w_in, b_forget, q_gain_a, k_gain_a, sinks_a,
           q_gain_b, k_gain_b, q_gain_c, k_gain_c, w_mem_kv, w_branch_a, w_branch_b,
           w_branch_c, w_out):
    depth = norm_gain.shape[0]
    w_rows = jnp.swapaxes(w_in, 1, 2).reshape(depth, -1, LANES)
    for layer in range(depth):
        x = _layer(x, mem, norm_gain[layer], mem_norm_gain[layer], w_rows[layer], b_forget[layer],
                   q_gain_a[layer], k_gain_a[layer], sinks_a[layer], q_gain_b[layer],
                   k_gain_b[layer], q_gain_c[layer], k_gain_c[layer], w_mem_kv[layer],
                   w_branch_a[layer], w_branch_b[layer], w_branch_c[layer], w_out[layer])
    return x
```

```python
import functools

import jax
import jax.numpy as jnp
import numpy as np
from jax import lax
from jax.experimental import pallas as pl
from jax.experimental.pallas import tpu as pltpu

F32 = jnp.float32
BF16 = jnp.bfloat16

D_MODEL = 2048
HEAD_DIM = 64
A_Q_HEADS = 12
A_KV_HEADS = 4
A_GROUP = A_Q_HEADS // A_KV_HEADS
WINDOW = 128
B_HEADS = 12
C_HEADS = 4
C_HEAD_DIM = 128
A_WIDTH = A_Q_HEADS * HEAD_DIM
A_KV_WIDTH = A_KV_HEADS * HEAD_DIM
B_WIDTH = B_HEADS * HEAD_DIM
C_WIDTH = C_HEADS * C_HEAD_DIM
EPS = 1e-6
NEG = -1e30

LANES = 128
HALF = LANES // 2
VT_ROWS = HALF + 16
LOG2E = float(np.log2(np.e))
N_PIECES = 3
VMEM_LIMIT = 56 * 1024 * 1024

_SRC = {}
_off = 0
for _name, _w in (("qA", A_WIDTH), ("kA", A_KV_WIDTH), ("vA", A_KV_WIDTH), ("zA", A_WIDTH),
                  ("qB", B_WIDTH), ("kB", B_WIDTH), ("vB", B_WIDTH), ("zB", B_WIDTH),
                  ("fB", B_HEADS), ("qC", C_WIDTH), ("zC", C_WIDTH), ("g", 3 * D_MODEL)):
    _SRC[_name] = (_off, _w)
    _off += _w

F_START, F_SHIFT = _SRC["fB"]
_DST = {n: (o if o < F_START else o - F_SHIFT) for n, (o, _) in _SRC.items() if n != "fB"}
PROJ_WIDTH = _SRC["g"][0] + _SRC["g"][1] - F_SHIFT
assert all(o % LANES == 0 for o in _DST.values()) and F_START % LANES == 0


def _col_block(name, width, part=0):
    assert _DST[name] % width == 0
    return _DST[name] // width + part


def _rms(x, gain):
    ms = jnp.mean(x * x, axis=-1, keepdims=True)
    return x * lax.rsqrt(ms + EPS) * gain


def _sigmoid(t):
    return 0.5 * (jnp.tanh(0.5 * t) + 1.0)


def _dot_nt(a, b):
    return lax.dot_general(a, b, (((1,), (1,)), ((), ())), preferred_element_type=F32)


def _mem_kv_kernel(mem_ref, gain_ref, w_ref, kgain_ref, mk_ref, mv_ref):
    hn = _rms(mem_ref[0], gain_ref[...]).astype(BF16)
    kv = jnp.dot(hn, w_ref[...], preferred_element_type=F32)
    for h in range(C_HEADS):
        sl = slice(h * C_HEAD_DIM, (h + 1) * C_HEAD_DIM)
        mk_ref[0, :, sl] = _rms(kv[:, sl], kgain_ref[...]).astype(BF16)
    mv_ref[0] = kv[:, C_WIDTH:].astype(BF16)


def _mem_kv(mem, gain, w, kgain):
    b, m, d = mem.shape
    return pl.pallas_call(
        _mem_kv_kernel,
        grid=(b,),
        in_specs=[
            pl.BlockSpec((1, m, d), lambda i: (i, 0, 0)),
            pl.BlockSpec((1, d), lambda i: (0, 0)),
            pl.BlockSpec((d, 2 * C_WIDTH), lambda i: (0, 0)),
            pl.BlockSpec((1, C_HEAD_DIM), lambda i: (0, 0)),
        ],
        out_specs=[
            pl.BlockSpec((1, m, C_WIDTH), lambda i: (i, 0, 0)),
            pl.BlockSpec((1, m, C_WIDTH), lambda i: (i, 0, 0)),
        ],
        out_shape=[jax.ShapeDtypeStruct((b, m, C_WIDTH), BF16)] * 2,
        compiler_params=pltpu.CompilerParams(
            dimension_semantics=("arbitrary",), vmem_limit_bytes=VMEM_LIMIT),
        name="mem_kv",
    )(mem, gain, w, kgain)


NX_TM = 512


def _norm_x_kernel(tiles_per_seq, x_ref, gain_ref, wf_ref, bf_ref, place_ref, hn_ref, aug_ref,
                   carry_ref):
    i = pl.program_id(0)

    @pl.when(i % tiles_per_seq == 0)
    def _():
        carry_ref[...] = jnp.zeros_like(carry_ref)

    hn = _rms(x_ref[...], gain_ref[...]).astype(BF16)
    hn_ref[...] = hn
    f_logit = _dot_nt(hn, wf_ref[...]) + bf_ref[...]
    x = jnp.minimum(f_logit, 0.0) - jnp.log1p(jnp.exp(-jnp.abs(f_logit)))
    row = lax.broadcasted_iota(jnp.int32, x.shape, 0)
    shift = 1
    while shift < NX_TM:
        x = x + jnp.where(row >= shift, pltpu.roll(x, shift, 0), 0.0)
        shift *= 2
    c = x + carry_ref[...]
    carry_ref[...] = c[NX_TM - 1:NX_TM, :]
    rest = c * (-LOG2E)
    pieces = []
    for _ in range(N_PIECES):
        p = rest.astype(BF16)
        pieces.append(p)
        rest = rest - p.astype(F32)
    stacked = jnp.concatenate(pieces, axis=1)
    aug_ref[...] = jnp.dot(stacked, place_ref[...], preferred_element_type=F32).astype(BF16)


def _norm_x(x2, gain, wf, bfg, place, seq):
    t, d = x2.shape
    return pl.pallas_call(
        functools.partial(_norm_x_kernel, seq // NX_TM),
        grid=(t // NX_TM,),
        in_specs=[
            pl.BlockSpec((NX_TM, d), lambda i: (i, 0)),
            pl.BlockSpec((1, d), lambda i: (0, 0)),
            pl.BlockSpec((LANES, d), lambda i: (0, 0)),
            pl.BlockSpec((1, LANES), lambda i: (0, 0)),
            pl.BlockSpec((N_PIECES * LANES, B_WIDTH), lambda i: (0, 0)),
        ],
        out_specs=[
            pl.BlockSpec((NX_TM, d), lambda i: (i, 0)),
            pl.BlockSpec((NX_TM, B_WIDTH), lambda i: (i, 0)),
        ],
        out_shape=[jax.ShapeDtypeStruct((t, d), BF16),
                   jax.ShapeDtypeStruct((t, B_WIDTH), BF16)],
        scratch_shapes=[pltpu.VMEM((1, LANES), F32)],
        compiler_params=pltpu.CompilerParams(
            dimension_semantics=("arbitrary",), vmem_limit_bytes=VMEM_LIMIT),
        name="norm_x",
    )(x2, gain, wf, bfg, place)


IN_TM = 1024
IN_TN = 1024
IN_CHUNKS = IN_TN // LANES
K_CHUNKS = D_MODEL // LANES
N_IN_TILES = PROJ_WIDTH // IN_TN
N_ALIGNED_TILES = F_START // IN_TN
assert F_START % IN_TN == 0 and PROJ_WIDTH % IN_TN == 0 and F_SHIFT < LANES

_NORM_KIND = {"qA": HEAD_DIM, "kA": HEAD_DIM, "qB": HEAD_DIM, "kB": HEAD_DIM, "qC": C_HEAD_DIM}


def _chunk_piece(col):
    for name, off in _DST.items():
        if off <= col < off + _SRC[name][1]:
            return name
    raise ValueError(col)


def _norm_prefix(tile):
    kinds = [_chunk_piece(tile * IN_TN + c * LANES) in _NORM_KIND for c in range(IN_CHUNKS)]
    n = sum(kinds)
    assert kinds == [True] * n + [False] * (IN_CHUNKS - n)
    return n


NORM_PREFIX = tuple(_norm_prefix(j) for j in range(N_IN_TILES))
NORM_TILES = tuple(j for j in range(N_IN_TILES) if NORM_PREFIX[j])


def _norm_slot(j):
    slot = 0
    for tile in NORM_TILES[1:]:
        slot = slot + (j >= tile).astype(jnp.int32)
    return slot


IN_MCH = 512


def _gather_weight_rows(w_ref, dst_ref, src_row, dst_rows, n_rows):
    for c in range(K_CHUNKS):
        dst_ref[dst_rows, c * LANES:(c + 1) * LANES] = (
            w_ref[pl.ds(src_row + c, n_rows, stride=K_CHUNKS), :].astype(BF16))


def _w_tile0_kernel(w_ref, o_ref):
    _gather_weight_rows(w_ref, o_ref, 0, slice(None), IN_TN)


def _w_index(tile):
    col = tile * IN_TN + jnp.where(tile >= N_ALIGNED_TILES, F_SHIFT, 0)
    return (col * K_CHUNKS, 0)


def _w_tile0(w_rows):
    return pl.pallas_call(
        _w_tile0_kernel,
        grid=(1,),
        in_specs=[pl.BlockSpec((pl.Element(IN_TN * K_CHUNKS), pl.Element(LANES)),
                               lambda i: _w_index(i))],
        out_specs=pl.BlockSpec((IN_TN, D_MODEL), lambda i: (0, 0)),
        out_shape=jax.ShapeDtypeStruct((IN_TN, D_MODEL), BF16),
        compiler_params=pltpu.CompilerParams(
            dimension_semantics=("arbitrary",), vmem_limit_bytes=VMEM_LIMIT),
        name="w_tile0",
    )(w_rows)


def _in_proj_kernel(n_row_tiles, hn_ref, w_ref, w0_ref, cscale_ref, gmat_ref, o_ref,
                    wbf_a, wbf_b, acc_ref):
    j = pl.program_id(0)
    i = pl.program_id(1)
    share = IN_TN // n_row_tiles

    @pl.when((j == 0) & (i == 0))
    def _():
        wbf_a[...] = w0_ref[...]

    def epilogue(chunk, prefix):
        rows = slice(chunk * IN_MCH, (chunk + 1) * IN_MCH)
        acc = acc_ref.at[chunk % 2]
        for c in range(prefix // 2):
            sl = slice(c * 2 * LANES, (c + 1) * 2 * LANES)
            a = acc[:, sl]
            ms = jnp.dot((a * a).astype(BF16), gmat_ref[c], preferred_element_type=F32)
            o_ref[rows, sl] = (a * lax.rsqrt(ms + EPS) * cscale_ref[:, sl]).astype(BF16)
        if prefix < IN_CHUNKS:
            rest = slice(prefix * LANES, IN_TN)
            o_ref[rows, rest] = acc[:, rest].astype(BF16)

    def step(cur, nxt, prefix):
        src = pl.multiple_of(i * (share * K_CHUNKS), share * K_CHUNKS)
        dst = pl.ds(pl.multiple_of(i * share, share), share)
        _gather_weight_rows(w_ref, nxt, src, dst, share)
        n_chunks = hn_ref.shape[0] // IN_MCH
        for c in range(n_chunks + 1):
            if c < n_chunks:
                acc_ref[c % 2] = _dot_nt(hn_ref[c * IN_MCH:(c + 1) * IN_MCH, :], cur[...])
            if c >= 1:
                epilogue(c - 1, prefix)

    def is_any(tiles):
        cond = j == tiles[0]
        for tile in tiles[1:]:
            cond = cond | (j == tile)
        return cond

    for parity, prefix in sorted({(t % 2, NORM_PREFIX[t]) for t in range(N_IN_TILES)}):
        tiles = [t for t in range(N_IN_TILES) if (t % 2, NORM_PREFIX[t]) == (parity, prefix)]
        cur, nxt = (wbf_a, wbf_b) if parity == 0 else (wbf_b, wbf_a)
        pl.when(is_any(tiles))(functools.partial(step, cur, nxt, prefix))


def _in_proj(hn, w_rows, w_first, cscale, gmat):
    t, d = hn.shape
    n_row_tiles = t // IN_TM
    assert IN_TN % n_row_tiles == 0 and (IN_TN // n_row_tiles) % 16 == 0
    return pl.pallas_call(
        functools.partial(_in_proj_kernel, n_row_tiles),
        grid=(N_IN_TILES, n_row_tiles),
        in_specs=[
            pl.BlockSpec((IN_TM, d), lambda j, i: (i, 0)),
            pl.BlockSpec((pl.Element(IN_TN * K_CHUNKS), pl.Element(LANES)),
                         lambda j, i: _w_index(jnp.minimum(j + 1, N_IN_TILES - 1))),
            pl.BlockSpec((IN_TN, d), lambda j, i: (0, 0), pipeline_mode=pl.Buffered(1)),
            pl.BlockSpec((1, IN_TN), lambda j, i: (0, _norm_slot(j))),
            pl.BlockSpec((IN_CHUNKS // 2, 2 * LANES, 2 * LANES),
                         lambda j, i: (_norm_slot(j), 0, 0)),
        ],
        out_specs=pl.BlockSpec((IN_TM, IN_TN), lambda j, i: (i, j)),
        out_shape=jax.ShapeDtypeStruct((t, PROJ_WIDTH), BF16),
        scratch_shapes=[
            pltpu.VMEM((IN_TN, d), BF16),
            pltpu.VMEM((IN_TN, d), BF16),
            pltpu.VMEM((2, IN_MCH, IN_TN), F32),
        ],
        compiler_params=pltpu.CompilerParams(
            dimension_semantics=("arbitrary", "arbitrary"), vmem_limit_bytes=VMEM_LIMIT),
        name="in_proj",
    )(hn, w_rows, w_first, cscale, gmat)


def _aug_lane(head):
    return HALF if head % 2 == 0 else 0


def _placement_matrix():
    place = np.zeros((N_PIECES * LANES, B_WIDTH), np.float32)
    for h in range(B_HEADS):
        for p in range(N_PIECES):
            place[p * LANES + h, (h // 2) * LANES + _aug_lane(h) + p] = 1.0
    return place


SWA_QB = 4
SWA_WIN = (SWA_QB + 1) * WINDOW


def _alibi_slopes_log2():
    return [float(2.0 ** (-8.0 * (h + 1) / A_Q_HEADS)) * LOG2E for h in range(A_Q_HEADS)]


def _swa_key_bias():
    key = np.arange(2 * WINDOW)[:, None]
    qry = np.arange(WINDOW)[None, :]
    rel = qry + WINDOW - key
    visible = (rel >= 0) & (rel < WINDOW)
    slopes = np.asarray(_alibi_slopes_log2(), np.float32)[:, None, None]
    return np.where(visible[None], slopes * key[None].astype(np.float32), np.float32(NEG))


def _swa_kernel(sinks_ref, bias_ref, q_ref, kp_ref, kc_ref, vp_ref, vc_ref, o_ref,
                kop_ref, vt_ref, s_ref, p_ref, sh_ref, ot_ref):
    step = pl.program_id(1)
    blk = WINDOW
    lane = lax.broadcasted_iota(jnp.int32, (SWA_WIN, LANES), 1)

    kwin = jnp.concatenate([kp_ref[...], kc_ref[...]], axis=0).astype(F32)
    vwin_t = jnp.concatenate([vp_ref[...], vc_ref[...]], axis=0).astype(F32).T
    ones_rows = jnp.where(
        lax.broadcasted_iota(jnp.int32, (VT_ROWS - HALF, SWA_WIN), 0) == 0, 1.0, 0.0)
    for g in range(A_KV_HEADS):
        slab = kwin[:, (g // 2) * LANES:(g // 2 + 1) * LANES]
        own = (lane < HALF) if g % 2 == 0 else (lane >= HALF)
        kz = jnp.where(own, slab, 0.0)
        kop_ref[g, 0] = kz.astype(BF16)
        kop_ref[g, 1] = pltpu.roll(kz, HALF, 1).astype(BF16)
        vt_ref[g] = jnp.concatenate(
            [vwin_t[g * HEAD_DIM:(g + 1) * HEAD_DIM], ones_rows], axis=0).astype(BF16)

    t_win = lax.broadcasted_iota(jnp.int32, (1, blk), 1).astype(F32) + float(blk)
    slopes = _alibi_slopes_log2()
    sinks = [sinks_ref[h] * LOG2E + slopes[h] * t_win for h in range(A_Q_HEADS)]
    key_row = lax.broadcasted_iota(jnp.int32, (2 * blk, blk), 0)

    items = [(i, g) for i in range(SWA_QB) for g in range(A_KV_HEADS)]
    keys_of = lambda i: slice(i * blk, (i + 2) * blk)

    def stage_scores(n):
        i, g = items[n]
        for k in range(A_GROUP):
            h = g * A_GROUP + k
            q_slab = q_ref[i * blk:(i + 1) * blk, (h // 2) * LANES:(h // 2 + 1) * LANES]
            s_ref[n % 2, k] = _dot_nt(kop_ref[g, (h + g) % 2, keys_of(i), :], q_slab)

    def stage_softmax(n):
        i, g = items[n]
        for k in range(A_GROUP):
            h = g * A_GROUP + k
            st = s_ref[n % 2, k] + bias_ref[h]
            if i == 0:
                st = jnp.where(((step * SWA_QB - 1) * blk + key_row) >= 0, st, NEG)
            m = jnp.maximum(jnp.max(st, axis=0, keepdims=True), sinks[h])
            p_ref[n % 2, k] = jnp.exp2(st - m).astype(BF16)
            sh_ref[n % 2, k] = jnp.exp2(sinks[h] - m)

    def stage_values(n):
        i, g = items[n]
        for k in range(A_GROUP):
            h = g * A_GROUP + k
            ot = jnp.dot(vt_ref[g, :, keys_of(i)], p_ref[n % 2, k],
                         preferred_element_type=F32)
            ot_ref[h * HEAD_DIM:(h + 1) * HEAD_DIM, :] = (
                ot[0:HEAD_DIM] / (ot[HEAD_DIM:HEAD_DIM + 1] + sh_ref[n % 2, k]))
        if g == A_KV_HEADS - 1:
            o_ref[i * blk:(i + 1) * blk, :] = ot_ref[...].T.astype(BF16)

    for n in range(len(items) + 2):
        if n < len(items):
            stage_scores(n)
        if 2 <= n:
            stage_values(n - 2)
        if 1 <= n <= len(items):
            stage_softmax(n - 1)


def _swa(sinks, bias, proj, batch, seq):
    blk = WINDOW
    nb = seq // blk
    steps = nb // SWA_QB
    qcol = _col_block("qA", A_WIDTH)
    kcol = _col_block("kA", A_KV_WIDTH)
    vcol = _col_block("vA", A_KV_WIDTH)
    cur = lambda b, s: b * steps + s
    prev = lambda b, s: b * nb + jnp.maximum(s * SWA_QB - 1, 0)
    return pl.pallas_call(
        _swa_kernel,
        grid=(batch, steps),
        in_specs=[
            pl.BlockSpec(memory_space=pltpu.SMEM),
            pl.BlockSpec(bias.shape, lambda b, s: (0, 0, 0)),
            pl.BlockSpec((SWA_QB * blk, A_WIDTH), lambda b, s: (cur(b, s), qcol)),
            pl.BlockSpec((blk, A_KV_WIDTH), lambda b, s: (prev(b, s), kcol)),
            pl.BlockSpec((SWA_QB * blk, A_KV_WIDTH), lambda b, s: (cur(b, s), kcol)),
            pl.BlockSpec((blk, A_KV_WIDTH), lambda b, s: (prev(b, s), vcol)),
            pl.BlockSpec((SWA_QB * blk, A_KV_WIDTH), lambda b, s: (cur(b, s), vcol)),
        ],
        out_specs=pl.BlockSpec((SWA_QB * blk, A_WIDTH), lambda b, s: (cur(b, s), 0)),
        out_shape=jax.ShapeDtypeStruct((batch * seq, A_WIDTH), BF16),
        scratch_shapes=[
            pltpu.VMEM((A_KV_HEADS, 2, SWA_WIN, LANES), BF16),
            pltpu.VMEM((A_KV_HEADS, VT_ROWS, SWA_WIN), BF16),
            pltpu.VMEM((2, A_GROUP, 2 * blk, blk), F32),
            pltpu.VMEM((2, A_GROUP, 2 * blk, blk), BF16),
            pltpu.VMEM((2, A_GROUP, 1, blk), F32),
            pltpu.VMEM((A_WIDTH, blk), F32),
        ],
        compiler_params=pltpu.CompilerParams(
            dimension_semantics=("arbitrary", "arbitrary"), vmem_limit_bytes=VMEM_LIMIT),
        name="swa",
    )(sinks, bias, proj, proj, proj, proj, proj)


FOX_T = 512


FOX_H = FOX_T // 2


def _fox_kernel(q_ref, k_ref, v_ref, aug_ref, o_ref,
                kop_ref, vt_ref, qop_ref, s_ref, cm_ref, p_ref, al_ref, m_ref, acc_ref):
    seq = k_ref.shape[0]
    t = FOX_T
    hk = FOX_H

    lane = lax.broadcasted_iota(jnp.int32, (hk, LANES), 1)
    low = lane < HALF

    def build(r, carry):
        sl = pl.ds(pl.multiple_of(r * hk, hk), hk)
        kk = k_ref[sl, :].astype(F32)
        aa = aug_ref[sl, :].astype(F32)
        vv = v_ref[sl, :].astype(F32)
        kop_ref[0, sl, :] = jnp.where(low, kk, aa).astype(BF16)
        kop_ref[1, sl, :] = jnp.where(low, aa, kk).astype(BF16)
        vvt = vv.T
        ones_rows = jnp.where(
            lax.broadcasted_iota(jnp.int32, (VT_ROWS - HALF, hk), 0) == 0, 1.0, 0.0)
        for hh in range(2):
            vt_ref[hh, r] = jnp.concatenate(
                [vvt[hh * HALF:(hh + 1) * HALF], ones_rows], axis=0).astype(BF16)
        return carry

    lax.fori_loop(0, seq // hk, build, 0)

    def scores(slot, half, masked):
        ks = pl.ds(pl.multiple_of(half * hk, hk), hk)
        for hh in range(2):
            st = _dot_nt(kop_ref[hh, ks, :], qop_ref[hh])
            if masked:
                key = lax.broadcasted_iota(jnp.int32, (hk, t), 0) + slot * hk
                qry = lax.broadcasted_iota(jnp.int32, (hk, t), 1)
                st = jnp.where(key <= qry, st, NEG)
            s_ref[slot, hh] = st
            cm_ref[slot, hh] = jnp.max(st, axis=0, keepdims=True)

    def soft(slot):
        for hh in range(2):
            m_old = m_ref[hh]
            m_new = jnp.maximum(m_old, cm_ref[slot, hh])
            al_ref[slot, hh] = jnp.exp2(m_old - m_new)
            p_ref[slot, hh] = jnp.exp2(s_ref[slot, hh] - m_new).astype(BF16)
            m_ref[hh] = m_new

    def pv(slot, half):
        for hh in range(2):
            acc_ref[hh] = acc_ref[hh] * al_ref[slot, hh] + jnp.dot(
                vt_ref[hh, half], p_ref[slot, hh], preferred_element_type=F32)

    def step(i, prev, has_next):
        pv(0, 2 * prev)
        soft(0)
        scores(1, 2 * i + 1, False)
        if has_next:
            scores(0, 2 * i + 2, False)
        pv(1, 2 * prev + 1)
        soft(1)

    lane_q = lax.broadcasted_iota(jnp.int32, (t, LANES), 1)
    low_q = lane_q < HALF
    ones0 = jnp.where((lane_q >= HALF) & (lane_q < HALF + N_PIECES), 1.0, 0.0)
    ones1 = jnp.where(lane_q < N_PIECES, 1.0, 0.0)

    for qi in range(seq // t):
        rows = slice(qi * t, (qi + 1) * t)
        q = q_ref[rows, :].astype(F32)
        qop_ref[0] = jnp.where(low_q, q, ones0).astype(BF16)
        qop_ref[1] = jnp.where(low_q, ones1, q).astype(BF16)
        m_ref[...] = jnp.full(m_ref.shape, NEG, F32)
        acc_ref[...] = jnp.zeros(acc_ref.shape, F32)

        scores(0, 2 * qi, True)
        scores(1, 2 * qi + 1, True)
        soft(0)
        if qi > 0:
            scores(0, 0, False)
        soft(1)

        loop_pairs = max(qi - 1, 0) // 2

        def pair(j, carry, qi=qi):
            first = 2 * j
            step(first, jnp.where(j == 0, qi, first - 1), True)
            step(first + 1, first, True)
            return carry

        if loop_pairs:
            lax.fori_loop(0, loop_pairs, pair, 0)
        for i in range(2 * loop_pairs, qi):
            step(i, i - 1 if i > 0 else qi, i + 1 < qi)
        last = qi - 1 if qi > 0 else qi
        pv(0, 2 * last)
        pv(1, 2 * last + 1)

        ot = jnp.concatenate(
            [acc_ref[hh, 0:HALF, :] / acc_ref[hh, HALF:HALF + 1, :] for hh in range(2)], axis=0)
        o_ref[rows, :] = ot.T.astype(BF16)


def _fox(proj, aug, batch, seq):
    t = FOX_T
    nq = seq // t
    pairs = B_HEADS // 2
    qcol = _col_block("qB", LANES)
    kcol = _col_block("kB", LANES)
    vcol = _col_block("vB", LANES)
    return pl.pallas_call(
        _fox_kernel,
        grid=(batch, pairs),
        in_specs=[
            pl.BlockSpec((seq, LANES), lambda b, p: (b, qcol + p)),
            pl.BlockSpec((seq, LANES), lambda b, p: (b, kcol + p)),
            pl.BlockSpec((seq, LANES), lambda b, p: (b, vcol + p)),
            pl.BlockSpec((seq, LANES), lambda b, p: (b, p)),
        ],
        out_specs=pl.BlockSpec((seq, LANES), lambda b, p: (b, p)),
        out_shape=jax.ShapeDtypeStruct((batch * seq, B_WIDTH), BF16),
        scratch_shapes=[
            pltpu.VMEM((2, seq, LANES), BF16),
            pltpu.VMEM((2, seq // FOX_H, VT_ROWS, FOX_H), BF16),
            pltpu.VMEM((2, t, LANES), BF16),
            pltpu.VMEM((2, 2, FOX_H, t), F32),
            pltpu.VMEM((2, 2, 1, t), F32),
            pltpu.VMEM((2, 2, FOX_H, t), BF16),
            pltpu.VMEM((2, 2, 1, t), F32),
            pltpu.VMEM((2, 1, t), F32),
            pltpu.VMEM((2, VT_ROWS, t), F32),
        ],
        compiler_params=pltpu.CompilerParams(
            dimension_semantics=("arbitrary", "arbitrary"),
            vmem_limit_bytes=VMEM_LIMIT),
        name="fox",
    )(proj, proj, proj, aug)


MEM_TM = 512


def _mem_attn_kernel(q_ref, mk_ref, mv_ref, o_ref):
    for h in range(C_HEADS):
        sl = slice(h * C_HEAD_DIM, (h + 1) * C_HEAD_DIM)
        s = _dot_nt(q_ref[:, sl], mk_ref[0, :, sl])
        m = jnp.max(s, axis=-1, keepdims=True)
        p = jnp.exp(s - m)
        denom = jnp.sum(p, axis=-1, keepdims=True)
        o = jnp.dot(p.astype(BF16), mv_ref[0, :, sl], preferred_element_type=F32)
        o_ref[:, sl] = (o / denom).astype(BF16)


def _mem_attn(proj, mk, mv, batch, seq):
    nt = seq // MEM_TM
    mlen = mk.shape[1]
    qcol = _col_block("qC", C_WIDTH)
    return pl.pallas_call(
        _mem_attn_kernel,
        grid=(batch * nt,),
        in_specs=[
            pl.BlockSpec((MEM_TM, C_WIDTH), lambda i: (i, qcol)),
            pl.BlockSpec((1, mlen, C_WIDTH), lambda i: (i // nt, 0, 0)),
            pl.BlockSpec((1, mlen, C_WIDTH), lambda i: (i // nt, 0, 0)),
        ],
        out_specs=pl.BlockSpec((MEM_TM, C_WIDTH), lambda i: (i, 0)),
        out_shape=jax.ShapeDtypeStruct((batch * seq, C_WIDTH), BF16),
        compiler_params=pltpu.CompilerParams(
            dimension_semantics=("arbitrary",), vmem_limit_bytes=VMEM_LIMIT),
        name="mem_attn",
    )(proj, mk, mv)


MERGE_TM = 256


Z_BLOCK = 256
Z_PARTS = A_WIDTH // Z_BLOCK
assert A_WIDTH == B_WIDTH and A_WIDTH % Z_BLOCK == 0


def _merge_kernel(x_ref, ya_ref, yb_ref, yc_ref, *refs):
    za_refs = refs[:Z_PARTS]
    zb_refs = refs[Z_PARTS:2 * Z_PARTS]
    zc_ref, g0_ref, g1_ref, g2_ref, wa_ref, wb_ref, wc_ref, wo_ref, o_ref = refs[2 * Z_PARTS:]

    def branch(y_ref, z_refs, w_ref):
        z = jnp.concatenate([r[...] for r in z_refs], axis=1).astype(F32)
        h = (y_ref[...].astype(F32) * (z * _sigmoid(z))).astype(BF16)
        return jnp.dot(h, w_ref[...], preferred_element_type=F32)

    y = _sigmoid(g0_ref[...].astype(F32)) * branch(ya_ref, za_refs, wa_ref)
    y = y + _sigmoid(g1_ref[...].astype(F32)) * branch(yb_ref, zb_refs, wb_ref)
    y = y + _sigmoid(g2_ref[...].astype(F32)) * branch(yc_ref, (zc_ref,), wc_ref)
    o_ref[...] = x_ref[...] + jnp.dot(y.astype(BF16), wo_ref[...], preferred_element_type=F32)


def _merge(x2, ya, yb, yc, proj, wa, wb, wc, wo):
    t, d = x2.shape
    tm = MERGE_TM
    gcol = _col_block("g", D_MODEL)
    row = lambda width, col: pl.BlockSpec((tm, width), lambda i: (i, col))
    full = lambda a: pl.BlockSpec(a.shape, lambda i: (0, 0), pipeline_mode=pl.Buffered(1))
    z_specs = lambda name: [row(Z_BLOCK, _col_block(name, Z_BLOCK, part)) for part in range(Z_PARTS)]
    n_proj = 2 * Z_PARTS + 4
    return pl.pallas_call(
        _merge_kernel,
        grid=(t // tm,),
        in_specs=[
            row(d, 0), row(A_WIDTH, 0), row(B_WIDTH, 0), row(C_WIDTH, 0),
            *z_specs("zA"), *z_specs("zB"), row(C_WIDTH, _col_block("zC", C_WIDTH)),
            row(d, gcol), row(d, gcol + 1), row(d, gcol + 2),
            full(wa), full(wb), full(wc), full(wo),
        ],
        out_specs=pl.BlockSpec((tm, d), lambda i: (i, 0)),
        out_shape=jax.ShapeDtypeStruct((t, d), F32),
        compiler_params=pltpu.CompilerParams(
            dimension_semantics=("arbitrary",), vmem_limit_bytes=VMEM_LIMIT),
        name="merge",
    )(x2, ya, yb, yc, *([proj] * n_proj), wa, wb, wc, wo)


def _group_mean_matrices():
    lane = np.arange(2 * LANES)
    chunk = lane // LANES
    mats = []
    for tile in NORM_TILES:
        for c in range(0, IN_CHUNKS, 2):
            dims = np.asarray([_NORM_KIND.get(_chunk_piece(tile * IN_TN + (c + k) * LANES), LANES)
                               for k in range(2)])[chunk]
            same = ((lane[:, None] // dims[:, None] == lane[None, :] // dims[None, :])
                    & (chunk[:, None] == chunk[None, :]))
            mats.append(same.astype(np.float32) / dims[None, :])
    return np.stack(mats)


def _column_scales(gains):
    scale = {"qA": HEAD_DIM ** -0.5 * LOG2E, "qB": HEAD_DIM ** -0.5 * LOG2E,
             "qC": C_HEAD_DIM ** -0.5}
    cols = []
    for tile in NORM_TILES:
        for c in range(IN_CHUNKS):
            piece = _chunk_piece(tile * IN_TN + c * LANES)
            if piece in _NORM_KIND:
                g = gains[piece].astype(F32) * scale.get(piece, 1.0)
                cols.append(jnp.tile(g, LANES // g.shape[0]))
            else:
                cols.append(jnp.zeros((LANES,), F32))
    return jnp.concatenate(cols).reshape(1, len(NORM_TILES) * IN_TN)


def _layer(x, mem, norm_gain, mem_norm_gain, w_rows, b_forget, q_gain_a, k_gain_a, sinks_a,
           q_gain_b, k_gain_b, q_gain_c, k_gain_c, w_mem_kv, w_branch_a, w_branch_b,
           w_branch_c, w_out):
    batch, seq, d = x.shape
    x2 = x.reshape(batch * seq, d)

    w_f = w_rows[F_START * K_CHUNKS:(F_START + F_SHIFT) * K_CHUNKS].reshape(F_SHIFT, d)
    w_f = jnp.pad(w_f, ((0, LANES - F_SHIFT), (0, 0))).astype(BF16)
    b_f = jnp.pad(b_forget.astype(F32), (0, LANES - F_SHIFT)).reshape(1, LANES)
    cscale = _column_scales({"qA": q_gain_a, "kA": k_gain_a, "qB": q_gain_b, "kB": k_gain_b,
                             "qC": q_gain_c})
    gmat = jnp.asarray(_group_mean_matrices(), BF16)
    place = jnp.asarray(_placement_matrix(), BF16)

    mk, mv = _mem_kv(mem, mem_norm_gain.reshape(1, d), w_mem_kv.astype(BF16),
                     k_gain_c.reshape(1, C_HEAD_DIM))
    hn, aug = _norm_x(x2, norm_gain.reshape(1, d), w_f, b_f, place, seq)
    proj = _in_proj(hn, w_rows, _w_tile0(w_rows), cscale, gmat)
    ya = _swa(sinks_a.astype(F32), jnp.asarray(_swa_key_bias(), F32), proj, batch, seq)
    yb = _fox(proj, aug, batch, seq)
    yc = _mem_attn(proj, mk, mv, batch, seq)
    out = _merge(x2, ya, yb, yc, proj, w_branch_a.astype(BF16), w_branch_b.astype(BF16),
                 w_branch_c.astype(BF16), w_out.astype(BF16))
    return out.reshape(batch, seq, d)


def kernel(x, mem, norm_gain, mem_norm_gain, w_in, b_forget, q_gain_a, k_gain_a, sinks_a,
           q_gain_b, k_gain_b, q_gain_c, k_gain_c, w_mem_kv, w_branch_a, w_branch_b,
           w_branch_c, w_out):
    depth = norm_gain.shape[0]
    w_rows = jnp.swapaxes(w_in, 1, 2).reshape(depth, -1, LANES)
    for layer in range(depth):
        x = _layer(x, mem, norm_gain[layer], mem_norm_gain[layer], w_rows[layer], b_forget[layer],
                   q_gain_a[layer], k_gain_a[layer], sinks_a[layer], q_gain_b[layer],
                   k_gain_b[layer], q_gain_c[layer], k_gain_c[layer], w_mem_kv[layer],
                   w_branch_a[layer], w_branch_b[layer], w_branch_c[layer], w_out[layer])
    return x
```

```python
import functools

import jax
import jax.numpy as jnp
import numpy as np
from jax import lax
from jax.experimental import pallas as pl
from jax.experimental.pallas import tpu as pltpu

F32 = jnp.float32
BF16 = jnp.bfloat16

D_MODEL = 2048
HEAD_DIM = 64
A_Q_HEADS = 12
A_KV_HEADS = 4
A_GROUP = A_Q_HEADS // A_KV_HEADS
WINDOW = 128
B_HEADS = 12
C_HEADS = 4
C_HEAD_DIM = 128
A_WIDTH = A_Q_HEADS * HEAD_DIM
A_KV_WIDTH = A_KV_HEADS * HEAD_DIM
B_WIDTH = B_HEADS * HEAD_DIM
C_WIDTH = C_HEADS * C_HEAD_DIM
EPS = 1e-6
NEG = -1e30

LANES = 128
HALF = LANES // 2
VT_ROWS = HALF + 16
LOG2E = float(np.log2(np.e))
N_PIECES = 3
VMEM_LIMIT = 56 * 1024 * 1024

_SRC = {}
_off = 0
for _name, _w in (("qA", A_WIDTH), ("kA", A_KV_WIDTH), ("vA", A_KV_WIDTH), ("zA", A_WIDTH),
                  ("qB", B_WIDTH), ("kB", B_WIDTH), ("vB", B_WIDTH), ("zB", B_WIDTH),
                  ("fB", B_HEADS), ("qC", C_WIDTH), ("zC", C_WIDTH), ("g", 3 * D_MODEL)):
    _SRC[_name] = (_off, _w)
    _off += _w

F_START, F_SHIFT = _SRC["fB"]
_DST = {n: (o if o < F_START else o - F_SHIFT) for n, (o, _) in _SRC.items() if n != "fB"}
PROJ_WIDTH = _SRC["g"][0] + _SRC["g"][1] - F_SHIFT
assert all(o % LANES == 0 for o in _DST.values()) and F_START % LANES == 0


def _col_block(name, width, part=0):
    assert _DST[name] % width == 0
    return _DST[name] // width + part


def _rms(x, gain):
    ms = jnp.mean(x * x, axis=-1, keepdims=True)
    return x * lax.rsqrt(ms + EPS) * gain


def _sigmoid(t):
    return 0.5 * (jnp.tanh(0.5 * t) + 1.0)


def _dot_nt(a, b):
    return lax.dot_general(a, b, (((1,), (1,)), ((), ())), preferred_element_type=F32)


def _mem_kv_kernel(mem_ref, gain_ref, w_ref, kgain_ref, mk_ref, mv_ref):
    hn = _rms(mem_ref[0], gain_ref[...]).astype(BF16)
    kv = jnp.dot(hn, w_ref[...].astype(BF16), preferred_element_type=F32)
    for h in range(C_HEADS):
        sl = slice(h * C_HEAD_DIM, (h + 1) * C_HEAD_DIM)
        mk_ref[0, :, sl] = _rms(kv[:, sl], kgain_ref[...]).astype(BF16)
    mv_ref[0] = kv[:, C_WIDTH:].astype(BF16)


def _mem_kv(mem, gain, w, kgain):
    b, m, d = mem.shape
    return pl.pallas_call(
        _mem_kv_kernel,
        grid=(b,),
        in_specs=[
            pl.BlockSpec((1, m, d), lambda i: (i, 0, 0)),
            pl.BlockSpec((1, d), lambda i: (0, 0)),
            pl.BlockSpec((d, 2 * C_WIDTH), lambda i: (0, 0)),
            pl.BlockSpec((1, C_HEAD_DIM), lambda i: (0, 0)),
        ],
        out_specs=[
            pl.BlockSpec((1, m, C_WIDTH), lambda i: (i, 0, 0)),
            pl.BlockSpec((1, m, C_WIDTH), lambda i: (i, 0, 0)),
        ],
        out_shape=[jax.ShapeDtypeStruct((b, m, C_WIDTH), BF16)] * 2,
        compiler_params=pltpu.CompilerParams(
            dimension_semantics=("arbitrary",), vmem_limit_bytes=VMEM_LIMIT),
        name="mem_kv",
    )(mem, gain, w, kgain)


NX_TM = 512


def _norm_x_kernel(tiles_per_seq, x_ref, gain_ref, wf_ref, bf_ref, place_ref, hn_ref, aug_ref,
                   carry_ref):
    i = pl.program_id(0)

    @pl.when(i % tiles_per_seq == 0)
    def _():
        carry_ref[...] = jnp.zeros_like(carry_ref)

    hn = _rms(x_ref[...], gain_ref[...]).astype(BF16)
    hn_ref[...] = hn
    f_logit = _dot_nt(hn, wf_ref[...]) + bf_ref[...]
    x = jnp.minimum(f_logit, 0.0) - jnp.log1p(jnp.exp(-jnp.abs(f_logit)))
    row = lax.broadcasted_iota(jnp.int32, x.shape, 0)
    shift = 1
    while shift < NX_TM:
        x = x + jnp.where(row >= shift, pltpu.roll(x, shift, 0), 0.0)
        shift *= 2
    c = x + carry_ref[...]
    carry_ref[...] = c[NX_TM - 1:NX_TM, :]
    rest = c * (-LOG2E)
    pieces = []
    for _ in range(N_PIECES):
        p = rest.astype(BF16)
        pieces.append(p)
        rest = rest - p.astype(F32)
    stacked = jnp.concatenate(pieces, axis=1)
    aug_ref[...] = jnp.dot(stacked, place_ref[...], preferred_element_type=F32).astype(BF16)


def _norm_x(x2, gain, wf, bfg, place, seq):
    t, d = x2.shape
    return pl.pallas_call(
        functools.partial(_norm_x_kernel, seq // NX_TM),
        grid=(t // NX_TM,),
        in_specs=[
            pl.BlockSpec((NX_TM, d), lambda i: (i, 0)),
            pl.BlockSpec((1, d), lambda i: (0, 0)),
            pl.BlockSpec((LANES, d), lambda i: (0, 0)),
            pl.BlockSpec((1, LANES), lambda i: (0, 0)),
            pl.BlockSpec((N_PIECES * LANES, B_WIDTH), lambda i: (0, 0)),
        ],
        out_specs=[
            pl.BlockSpec((NX_TM, d), lambda i: (i, 0)),
            pl.BlockSpec((NX_TM, B_WIDTH), lambda i: (i, 0)),
        ],
        out_shape=[jax.ShapeDtypeStruct((t, d), BF16),
                   jax.ShapeDtypeStruct((t, B_WIDTH), BF16)],
        scratch_shapes=[pltpu.VMEM((1, LANES), F32)],
        compiler_params=pltpu.CompilerParams(
            dimension_semantics=("arbitrary",), vmem_limit_bytes=VMEM_LIMIT),
        name="norm_x",
    )(x2, gain, wf, bfg, place)


IN_TM = 1024
IN_TN = 1024
IN_CHUNKS = IN_TN // LANES
K_CHUNKS = D_MODEL // LANES
N_IN_TILES = PROJ_WIDTH // IN_TN
N_ALIGNED_TILES = F_START // IN_TN
assert F_START % IN_TN == 0 and PROJ_WIDTH % IN_TN == 0 and F_SHIFT < LANES

_NORM_KIND = {"qA": HEAD_DIM, "kA": HEAD_DIM, "qB": HEAD_DIM, "kB": HEAD_DIM, "qC": C_HEAD_DIM}


def _chunk_piece(col):
    for name, off in _DST.items():
        if off <= col < off + _SRC[name][1]:
            return name
    raise ValueError(col)


def _norm_prefix(tile):
    kinds = [_chunk_piece(tile * IN_TN + c * LANES) in _NORM_KIND for c in range(IN_CHUNKS)]
    n = sum(kinds)
    assert kinds == [True] * n + [False] * (IN_CHUNKS - n)
    return n


NORM_PREFIX = tuple(_norm_prefix(j) for j in range(N_IN_TILES))
NORM_TILES = tuple(j for j in range(N_IN_TILES) if NORM_PREFIX[j])


def _norm_slot(j):
    slot = 0
    for tile in NORM_TILES[1:]:
        slot = slot + (j >= tile).astype(jnp.int32)
    return slot


IN_MCH = 512


def _gather_weight_rows(w_ref, dst_ref, src_row, dst_rows, n_rows):
    for c in range(K_CHUNKS):
        dst_ref[dst_rows, c * LANES:(c + 1) * LANES] = (
            w_ref[pl.ds(src_row + c, n_rows, stride=K_CHUNKS), :].astype(BF16))


def _w_tile0_kernel(w_ref, o_ref):
    _gather_weight_rows(w_ref, o_ref, 0, slice(None), IN_TN)


def _w_index(tile):
    col = tile * IN_TN + jnp.where(tile >= N_ALIGNED_TILES, F_SHIFT, 0)
    return (col * K_CHUNKS, 0)


def _w_tile0(w_rows):
    return pl.pallas_call(
        _w_tile0_kernel,
        grid=(1,),
        in_specs=[pl.BlockSpec((pl.Element(IN_TN * K_CHUNKS), pl.Element(LANES)),
                               lambda i: _w_index(i))],
        out_specs=pl.BlockSpec((IN_TN, D_MODEL), lambda i: (0, 0)),
        out_shape=jax.ShapeDtypeStruct((IN_TN, D_MODEL), BF16),
        compiler_params=pltpu.CompilerParams(
            dimension_semantics=("arbitrary",), vmem_limit_bytes=VMEM_LIMIT),
        name="w_tile0",
    )(w_rows)


def _in_proj_kernel(n_row_tiles, hn_ref, w_ref, w0_ref, cscale_ref, gmat_ref, o_ref,
                    wbf_a, wbf_b, acc_ref):
    j = pl.program_id(0)
    i = pl.program_id(1)
    share = IN_TN // n_row_tiles

    @pl.when((j == 0) & (i == 0))
    def _():
        wbf_a[...] = w0_ref[...]

    def epilogue(chunk, prefix):
        rows = slice(chunk * IN_MCH, (chunk + 1) * IN_MCH)
        acc = acc_ref.at[chunk % 2]
        for c in range(prefix // 2):
            sl = slice(c * 2 * LANES, (c + 1) * 2 * LANES)
            a = acc[:, sl]
            ms = jnp.dot((a * a).astype(BF16), gmat_ref[c], preferred_element_type=F32)
            o_ref[rows, sl] = (a * lax.rsqrt(ms + EPS) * cscale_ref[:, sl]).astype(BF16)
        if prefix < IN_CHUNKS:
            rest = slice(prefix * LANES, IN_TN)
            o_ref[rows, rest] = acc[:, rest].astype(BF16)

    def step(cur, nxt, prefix):
        src = pl.multiple_of(i * (share * K_CHUNKS), share * K_CHUNKS)
        dst = pl.ds(pl.multiple_of(i * share, share), share)
        _gather_weight_rows(w_ref, nxt, src, dst, share)
        n_chunks = hn_ref.shape[0] // IN_MCH
        for c in range(n_chunks + 1):
            if c < n_chunks:
                acc_ref[c % 2] = _dot_nt(hn_ref[c * IN_MCH:(c + 1) * IN_MCH, :], cur[...])
            if c >= 1:
                epilogue(c - 1, prefix)

    def is_any(tiles):
        cond = j == tiles[0]
        for tile in tiles[1:]:
            cond = cond | (j == tile)
        return cond

    for parity, prefix in sorted({(t % 2, NORM_PREFIX[t]) for t in range(N_IN_TILES)}):
        tiles = [t for t in range(N_IN_TILES) if (t % 2, NORM_PREFIX[t]) == (parity, prefix)]
        cur, nxt = (wbf_a, wbf_b) if parity == 0 else (wbf_b, wbf_a)
        pl.when(is_any(tiles))(functools.partial(step, cur, nxt, prefix))


def _in_proj(hn, w_rows, w_first, cscale, gmat):
    t, d = hn.shape
    n_row_tiles = t // IN_TM
    assert IN_TN % n_row_tiles == 0 and (IN_TN // n_row_tiles) % 16 == 0
    return pl.pallas_call(
        functools.partial(_in_proj_kernel, n_row_tiles),
        grid=(N_IN_TILES, n_row_tiles),
        in_specs=[
            pl.BlockSpec((IN_TM, d), lambda j, i: (i, 0)),
            pl.BlockSpec((pl.Element(IN_TN * K_CHUNKS), pl.Element(LANES)),
                         lambda j, i: _w_index(jnp.minimum(j + 1, N_IN_TILES - 1))),
            pl.BlockSpec((IN_TN, d), lambda j, i: (0, 0), pipeline_mode=pl.Buffered(1)),
            pl.BlockSpec((1, IN_TN), lambda j, i: (0, _norm_slot(j))),
            pl.BlockSpec((IN_CHUNKS // 2, 2 * LANES, 2 * LANES),
                         lambda j, i: (_norm_slot(j), 0, 0)),
        ],
        out_specs=pl.BlockSpec((IN_TM, IN_TN), lambda j, i: (i, j)),
        out_shape=jax.ShapeDtypeStruct((t, PROJ_WIDTH), BF16),
        scratch_shapes=[
            pltpu.VMEM((IN_TN, d), BF16),
            pltpu.VMEM((IN_TN, d), BF16),
            pltpu.VMEM((2, IN_MCH, IN_TN), F32),
        ],
        compiler_params=pltpu.CompilerParams(
            dimension_semantics=("arbitrary", "arbitrary"), vmem_limit_bytes=VMEM_LIMIT),
        name="in_proj",
    )(hn, w_rows, w_first, cscale, gmat)


def _aug_lane(head):
    return HALF if head % 2 == 0 else 0


def _placement_matrix():
    place = np.zeros((N_PIECES * LANES, B_WIDTH), np.float32)
    for h in range(B_HEADS):
        for p in range(N_PIECES):
            place[p * LANES + h, (h // 2) * LANES + _aug_lane(h) + p] = 1.0
    return place


SWA_QB = 4
SWA_WIN = (SWA_QB + 1) * WINDOW


def _alibi_slopes_log2():
    return [float(2.0 ** (-8.0 * (h + 1) / A_Q_HEADS)) * LOG2E for h in range(A_Q_HEADS)]


def _swa_key_bias():
    key = np.arange(2 * WINDOW)[:, None]
    qry = np.arange(WINDOW)[None, :]
    rel = qry + WINDOW - key
    visible = (rel >= 0) & (rel < WINDOW)
    slopes = np.asarray(_alibi_slopes_log2(), np.float32)[:, None, None]
    return np.where(visible[None], slopes * key[None].astype(np.float32), np.float32(NEG))


def _swa_kernel(n_cast, sinks_ref, bias_ref, q_ref, kp_ref, kc_ref, vp_ref, vc_ref, *refs):
    cast_in, o_ref, cast_out = refs[:n_cast], refs[n_cast], refs[n_cast + 1:2 * n_cast + 1]
    kop_ref, vt_ref, s_ref, p_ref, sh_ref, ot_ref = refs[2 * n_cast + 1:]
    for src, dst in zip(cast_in, cast_out):
        dst[...] = src[...].astype(BF16)
    step = pl.program_id(1)
    blk = WINDOW
    lane = lax.broadcasted_iota(jnp.int32, (SWA_WIN, LANES), 1)

    kwin = jnp.concatenate([kp_ref[...], kc_ref[...]], axis=0).astype(F32)
    vwin_t = jnp.concatenate([vp_ref[...], vc_ref[...]], axis=0).astype(F32).T
    ones_rows = jnp.where(
        lax.broadcasted_iota(jnp.int32, (VT_ROWS - HALF, SWA_WIN), 0) == 0, 1.0, 0.0)
    for g in range(A_KV_HEADS):
        slab = kwin[:, (g // 2) * LANES:(g // 2 + 1) * LANES]
        own = (lane < HALF) if g % 2 == 0 else (lane >= HALF)
        kz = jnp.where(own, slab, 0.0)
        kop_ref[g, 0] = kz.astype(BF16)
        kop_ref[g, 1] = pltpu.roll(kz, HALF, 1).astype(BF16)
        vt_ref[g] = jnp.concatenate(
            [vwin_t[g * HEAD_DIM:(g + 1) * HEAD_DIM], ones_rows], axis=0).astype(BF16)

    t_win = lax.broadcasted_iota(jnp.int32, (1, blk), 1).astype(F32) + float(blk)
    slopes = _alibi_slopes_log2()
    sinks = [sinks_ref[h] * LOG2E + slopes[h] * t_win for h in range(A_Q_HEADS)]
    key_row = lax.broadcasted_iota(jnp.int32, (2 * blk, blk), 0)

    items = [(i, g) for i in range(SWA_QB) for g in range(A_KV_HEADS)]
    keys_of = lambda i: slice(i * blk, (i + 2) * blk)

    def stage_scores(n):
        i, g = items[n]
        for k in range(A_GROUP):
            h = g * A_GROUP + k
            q_slab = q_ref[i * blk:(i + 1) * blk, (h // 2) * LANES:(h // 2 + 1) * LANES]
            s_ref[n % 2, k] = _dot_nt(kop_ref[g, (h + g) % 2, keys_of(i), :], q_slab)

    def stage_softmax(n):
        i, g = items[n]
        for k in range(A_GROUP):
            h = g * A_GROUP + k
            st = s_ref[n % 2, k] + bias_ref[h]
            if i == 0:
                st = jnp.where(((step * SWA_QB - 1) * blk + key_row) >= 0, st, NEG)
            m = jnp.maximum(jnp.max(st, axis=0, keepdims=True), sinks[h])
            p_ref[n % 2, k] = jnp.exp2(st - m).astype(BF16)
            sh_ref[n % 2, k] = jnp.exp2(sinks[h] - m)

    def stage_values(n):
        i, g = items[n]
        for k in range(A_GROUP):
            h = g * A_GROUP + k
            ot = jnp.dot(vt_ref[g, :, keys_of(i)], p_ref[n % 2, k],
                         preferred_element_type=F32)
            ot_ref[h * HEAD_DIM:(h + 1) * HEAD_DIM, :] = (
                ot[0:HEAD_DIM] / (ot[HEAD_DIM:HEAD_DIM + 1] + sh_ref[n % 2, k]))
        if g == A_KV_HEADS - 1:
            o_ref[i * blk:(i + 1) * blk, :] = ot_ref[...].T.astype(BF16)

    for n in range(len(items) + 2):
        if n < len(items):
            stage_scores(n)
        if 2 <= n:
            stage_values(n - 2)
        if 1 <= n <= len(items):
            stage_softmax(n - 1)


def _swa(sinks, bias, proj, cast_weights, batch, seq):
    blk = WINDOW
    nb = seq // blk
    steps = nb // SWA_QB
    n_steps = batch * steps
    qcol = _col_block("qA", A_WIDTH)
    kcol = _col_block("kA", A_KV_WIDTH)
    vcol = _col_block("vA", A_KV_WIDTH)
    cur = lambda b, s: b * steps + s
    prev = lambda b, s: b * nb + jnp.maximum(s * SWA_QB - 1, 0)
    slab_specs = []
    for w in cast_weights:
        assert w.shape[0] % (16 * n_steps) == 0
        slab_specs.append(pl.BlockSpec((w.shape[0] // n_steps, w.shape[1]),
                                       lambda b, s: (cur(b, s), 0)))
    outs = pl.pallas_call(
        functools.partial(_swa_kernel, len(cast_weights)),
        grid=(batch, steps),
        in_specs=[
            pl.BlockSpec(memory_space=pltpu.SMEM),
            pl.BlockSpec(bias.shape, lambda b, s: (0, 0, 0)),
            pl.BlockSpec((SWA_QB * blk, A_WIDTH), lambda b, s: (cur(b, s), qcol)),
            pl.BlockSpec((blk, A_KV_WIDTH), lambda b, s: (prev(b, s), kcol)),
            pl.BlockSpec((SWA_QB * blk, A_KV_WIDTH), lambda b, s: (cur(b, s), kcol)),
            pl.BlockSpec((blk, A_KV_WIDTH), lambda b, s: (prev(b, s), vcol)),
            pl.BlockSpec((SWA_QB * blk, A_KV_WIDTH), lambda b, s: (cur(b, s), vcol)),
            *slab_specs,
        ],
        out_specs=[pl.BlockSpec((SWA_QB * blk, A_WIDTH), lambda b, s: (cur(b, s), 0)),
                   *slab_specs],
        out_shape=[jax.ShapeDtypeStruct((batch * seq, A_WIDTH), BF16),
                   *[jax.ShapeDtypeStruct(w.shape, BF16) for w in cast_weights]],
        scratch_shapes=[
            pltpu.VMEM((A_KV_HEADS, 2, SWA_WIN, LANES), BF16),
            pltpu.VMEM((A_KV_HEADS, VT_ROWS, SWA_WIN), BF16),
            pltpu.VMEM((2, A_GROUP, 2 * blk, blk), F32),
            pltpu.VMEM((2, A_GROUP, 2 * blk, blk), BF16),
            pltpu.VMEM((2, A_GROUP, 1, blk), F32),
            pltpu.VMEM((A_WIDTH, blk), F32),
        ],
        compiler_params=pltpu.CompilerParams(
            dimension_semantics=("arbitrary", "arbitrary"), vmem_limit_bytes=VMEM_LIMIT),
        name="swa",
    )(sinks, bias, proj, proj, proj, proj, proj, *cast_weights)
    return outs[0], outs[1:]


FOX_T = 512


FOX_H = FOX_T // 2


def _fox_kernel(q_ref, k_ref, v_ref, aug_ref, o_ref,
                kop_ref, vt_ref, qop_ref, s_ref, cm_ref, p_ref, al_ref, m_ref, acc_ref):
    seq = k_ref.shape[0]
    t = FOX_T
    hk = FOX_H

    lane = lax.broadcasted_iota(jnp.int32, (hk, LANES), 1)
    low = lane < HALF

    def build(r, carry):
        sl = pl.ds(pl.multiple_of(r * hk, hk), hk)
        kk = k_ref[sl, :].astype(F32)
        aa = aug_ref[sl, :].astype(F32)
        vv = v_ref[sl, :].astype(F32)
        kop_ref[0, sl, :] = jnp.where(low, kk, aa).astype(BF16)
        kop_ref[1, sl, :] = jnp.where(low, aa, kk).astype(BF16)
        vvt = vv.T
        ones_rows = jnp.where(
            lax.broadcasted_iota(jnp.int32, (VT_ROWS - HALF, hk), 0) == 0, 1.0, 0.0)
        for hh in range(2):
            vt_ref[hh, r] = jnp.concatenate(
                [vvt[hh * HALF:(hh + 1) * HALF], ones_rows], axis=0).astype(BF16)
        return carry

    lax.fori_loop(0, seq // hk, build, 0)

    def scores(slot, half, masked):
        ks = pl.ds(pl.multiple_of(half * hk, hk), hk)
        for hh in range(2):
            st = _dot_nt(kop_ref[hh, ks, :], qop_ref[hh])
            if masked:
                key = lax.broadcasted_iota(jnp.int32, (hk, t), 0) + slot * hk
                qry = lax.broadcasted_iota(jnp.int32, (hk, t), 1)
                st = jnp.where(key <= qry, st, NEG)
            s_ref[slot, hh] = st
            cm_ref[slot, hh] = jnp.max(st, axis=0, keepdims=True)

    def soft(slot):
        for hh in range(2):
            m_old = m_ref[hh]
            m_new = jnp.maximum(m_old, cm_ref[slot, hh])
            al_ref[slot, hh] = jnp.exp2(m_old - m_new)
            p_ref[slot, hh] = jnp.exp2(s_ref[slot, hh] - m_new).astype(BF16)
            m_ref[hh] = m_new

    def pv(slot, half):
        for hh in range(2):
            acc_ref[hh] = acc_ref[hh] * al_ref[slot, hh] + jnp.dot(
                vt_ref[hh, half], p_ref[slot, hh], preferred_element_type=F32)

    def step(i, prev, has_next):
        pv(0, 2 * prev)
        soft(0)
        scores(1, 2 * i + 1, False)
        if has_next:
            scores(0, 2 * i + 2, False)
        pv(1, 2 * prev + 1)
        soft(1)

    lane_q = lax.broadcasted_iota(jnp.int32, (t, LANES), 1)
    low_q = lane_q < HALF
    ones0 = jnp.where((lane_q >= HALF) & (lane_q < HALF + N_PIECES), 1.0, 0.0)
    ones1 = jnp.where(lane_q < N_PIECES, 1.0, 0.0)

    for qi in range(seq // t):
        rows = slice(qi * t, (qi + 1) * t)
        q = q_ref[rows, :].astype(F32)
        qop_ref[0] = jnp.where(low_q, q, ones0).astype(BF16)
        qop_ref[1] = jnp.where(low_q, ones1, q).astype(BF16)
        m_ref[...] = jnp.full(m_ref.shape, NEG, F32)
        acc_ref[...] = jnp.zeros(acc_ref.shape, F32)

        scores(0, 2 * qi, True)
        scores(1, 2 * qi + 1, True)
        soft(0)
        if qi > 0:
            scores(0, 0, False)
        soft(1)

        loop_pairs = max(qi - 1, 0) // 2

        def pair(j, carry, qi=qi):
            first = 2 * j
            step(first, jnp.where(j == 0, qi, first - 1), True)
            step(first + 1, first, True)
            return carry

        if loop_pairs:
            lax.fori_loop(0, loop_pairs, pair, 0)
        for i in range(2 * loop_pairs, qi):
            step(i, i - 1 if i > 0 else qi, i + 1 < qi)
        last = qi - 1 if qi > 0 else qi
        pv(0, 2 * last)
        pv(1, 2 * last + 1)

        ot = jnp.concatenate(
            [acc_ref[hh, 0:HALF, :] / acc_ref[hh, HALF:HALF + 1, :] for hh in range(2)], axis=0)
        o_ref[rows, :] = ot.T.astype(BF16)


def _fox(proj, aug, batch, seq):
    t = FOX_T
    nq = seq // t
    pairs = B_HEADS // 2
    qcol = _col_block("qB", LANES)
    kcol = _col_block("kB", LANES)
    vcol = _col_block("vB", LANES)
    return pl.pallas_call(
        _fox_kernel,
        grid=(batch, pairs),
        in_specs=[
            pl.BlockSpec((seq, LANES), lambda b, p: (b, qcol + p)),
            pl.BlockSpec((seq, LANES), lambda b, p: (b, kcol + p)),
            pl.BlockSpec((seq, LANES), lambda b, p: (b, vcol + p)),
            pl.BlockSpec((seq, LANES), lambda b, p: (b, p)),
        ],
        out_specs=pl.BlockSpec((seq, LANES), lambda b, p: (b, p)),
        out_shape=jax.ShapeDtypeStruct((batch * seq, B_WIDTH), BF16),
        scratch_shapes=[
            pltpu.VMEM((2, seq, LANES), BF16),
            pltpu.VMEM((2, seq // FOX_H, VT_ROWS, FOX_H), BF16),
            pltpu.VMEM((2, t, LANES), BF16),
            pltpu.VMEM((2, 2, FOX_H, t), F32),
            pltpu.VMEM((2, 2, 1, t), F32),
            pltpu.VMEM((2, 2, FOX_H, t), BF16),
            pltpu.VMEM((2, 2, 1, t), F32),
            pltpu.VMEM((2, 1, t), F32),
            pltpu.VMEM((2, VT_ROWS, t), F32),
        ],
        compiler_params=pltpu.CompilerParams(
            dimension_semantics=("arbitrary", "arbitrary"),
            vmem_limit_bytes=VMEM_LIMIT),
        name="fox",
    )(proj, proj, proj, aug)


MERGE_TM = 256


Z_BLOCK = 256
Z_PARTS = A_WIDTH // Z_BLOCK
assert A_WIDTH == B_WIDTH and A_WIDTH % Z_BLOCK == 0


def _merge_kernel(x_ref, ya_ref, yb_ref, mk_ref, mv_ref, qc_ref, *refs):
    za_refs = refs[:Z_PARTS]
    zb_refs = refs[Z_PARTS:2 * Z_PARTS]
    zc_ref, g0_ref, g1_ref, g2_ref, wa_ref, wb_ref, wc_ref, wo_ref, o_ref = refs[2 * Z_PARTS:]

    def memory_attention():
        outs = []
        for h in range(C_HEADS):
            sl = slice(h * C_HEAD_DIM, (h + 1) * C_HEAD_DIM)
            s = _dot_nt(qc_ref[:, sl], mk_ref[0, :, sl])
            p = jnp.exp(s - jnp.max(s, axis=-1, keepdims=True))
            o = jnp.dot(p.astype(BF16), mv_ref[0, :, sl], preferred_element_type=F32)
            outs.append(o / jnp.sum(p, axis=-1, keepdims=True))
        return jnp.concatenate(outs, axis=1)

    def branch(y, z_refs, w_ref):
        z = jnp.concatenate([r[...] for r in z_refs], axis=1).astype(F32)
        h = (y * (z * _sigmoid(z))).astype(BF16)
        return jnp.dot(h, w_ref[...], preferred_element_type=F32)

    y = _sigmoid(g0_ref[...].astype(F32)) * branch(ya_ref[...].astype(F32), za_refs, wa_ref)
    y = y + _sigmoid(g1_ref[...].astype(F32)) * branch(yb_ref[...].astype(F32), zb_refs, wb_ref)
    y = y + _sigmoid(g2_ref[...].astype(F32)) * branch(memory_attention(), (zc_ref,), wc_ref)
    o_ref[...] = x_ref[...] + jnp.dot(y.astype(BF16), wo_ref[...], preferred_element_type=F32)


def _merge(x2, ya, yb, mk, mv, proj, wa, wb, wc, wo, seq):
    t, d = x2.shape
    tm = MERGE_TM
    tiles_per_seq = seq // tm
    gcol = _col_block("g", D_MODEL)
    row = lambda width, col: pl.BlockSpec((tm, width), lambda i: (i, col))
    full = lambda a: pl.BlockSpec(a.shape, lambda i: (0, 0), pipeline_mode=pl.Buffered(1))
    mem = lambda a: pl.BlockSpec((1,) + a.shape[1:], lambda i: (i // tiles_per_seq, 0, 0))
    z_specs = lambda name: [row(Z_BLOCK, _col_block(name, Z_BLOCK, part)) for part in range(Z_PARTS)]
    n_proj = 2 * Z_PARTS + 5
    return pl.pallas_call(
        _merge_kernel,
        grid=(t // tm,),
        in_specs=[
            row(d, 0), row(A_WIDTH, 0), row(B_WIDTH, 0), mem(mk), mem(mv),
            row(C_WIDTH, _col_block("qC", C_WIDTH)),
            *z_specs("zA"), *z_specs("zB"), row(C_WIDTH, _col_block("zC", C_WIDTH)),
            row(d, gcol), row(d, gcol + 1), row(d, gcol + 2),
            full(wa), full(wb), full(wc), full(wo),
        ],
        out_specs=pl.BlockSpec((tm, d), lambda i: (i, 0)),
        out_shape=jax.ShapeDtypeStruct((t, d), F32),
        compiler_params=pltpu.CompilerParams(
            dimension_semantics=("arbitrary",), vmem_limit_bytes=VMEM_LIMIT),
        name="merge",
    )(x2, ya, yb, mk, mv, *([proj] * n_proj), wa, wb, wc, wo)


def _group_mean_matrices():
    lane = np.arange(2 * LANES)
    chunk = lane // LANES
    mats = []
    for tile in NORM_TILES:
        for c in range(0, IN_CHUNKS, 2):
            dims = np.asarray([_NORM_KIND.get(_chunk_piece(tile * IN_TN + (c + k) * LANES), LANES)
                               for k in range(2)])[chunk]
            same = ((lane[:, None] // dims[:, None] == lane[None, :] // dims[None, :])
                    & (chunk[:, None] == chunk[None, :]))
            mats.append(same.astype(np.float32) / dims[None, :])
    return np.stack(mats)


def _column_scales(gains):
    scale = {"qA": HEAD_DIM ** -0.5 * LOG2E, "qB": HEAD_DIM ** -0.5 * LOG2E,
             "qC": C_HEAD_DIM ** -0.5}
    cols = []
    for tile in NORM_TILES:
        for c in range(IN_CHUNKS):
            piece = _chunk_piece(tile * IN_TN + c * LANES)
            if piece in _NORM_KIND:
                g = gains[piece].astype(F32) * scale.get(piece, 1.0)
                cols.append(jnp.tile(g, LANES // g.shape[0]))
            else:
                cols.append(jnp.zeros((LANES,), F32))
    return jnp.concatenate(cols).reshape(1, len(NORM_TILES) * IN_TN)


def _layer(x, mem, norm_gain, mem_norm_gain, w_rows, b_forget, q_gain_a, k_gain_a, sinks_a,
           q_gain_b, k_gain_b, q_gain_c, k_gain_c, w_mem_kv, w_branch_a, w_branch_b,
           w_branch_c, w_out):
    batch, seq, d = x.shape
    x2 = x.reshape(batch * seq, d)

    w_f = w_rows[F_START * K_CHUNKS:(F_START + F_SHIFT) * K_CHUNKS].reshape(F_SHIFT, d)
    w_f = jnp.pad(w_f, ((0, LANES - F_SHIFT), (0, 0))).astype(BF16)
    b_f = jnp.pad(b_forget.astype(F32), (0, LANES - F_SHIFT)).reshape(1, LANES)
    cscale = _column_scales({"qA": q_gain_a, "kA": k_gain_a, "qB": q_gain_b, "kB": k_gain_b,
                             "qC": q_gain_c})
    gmat = jnp.asarray(_group_mean_matrices(), BF16)
    place = jnp.asarray(_placement_matrix(), BF16)

    mk, mv = _mem_kv(mem, mem_norm_gain.reshape(1, d), w_mem_kv, k_gain_c.reshape(1, C_HEAD_DIM))
    hn, aug = _norm_x(x2, norm_gain.reshape(1, d), w_f, b_f, place, seq)
    proj = _in_proj(hn, w_rows, _w_tile0(w_rows), cscale, gmat)
    ya, merge_weights = _swa(sinks_a.astype(F32), jnp.asarray(_swa_key_bias(), F32), proj,
                             (w_branch_a, w_branch_b, w_branch_c, w_out), batch, seq)
    yb = _fox(proj, aug, batch, seq)
    out = _merge(x2, ya, yb, mk, mv, proj, *merge_weights, seq)
    return out.reshape(batch, seq, d)


def kernel(x, mem, norm_gain, mem_norm_gain, w_in, b_forget, q_gain_a, k_gain_a, sinks_a,
           q_gain_b, k_gain_b, q_gain_c, k_gain_c, w_mem_kv, w_branch_a, w_branch_b,
           w_branch_c, w_out):
    depth = norm_gain.shape[0]
    w_rows = jnp.swapaxes(w_in, 1, 2).reshape(depth, -1, LANES)
    for layer in range(depth):
        x = _layer(x, mem, norm_gain[layer], mem_norm_gain[layer], w_rows[layer], b_forget[layer],
                   q_gain_a[layer], k_gain_a[layer], sinks_a[layer], q_gain_b[layer],
                   k_gain_b[layer], q_gain_c[layer], k_gain_c[layer], w_mem_kv[layer],
                   w_branch_a[layer], w_branch_b[layer], w_branch_c[layer], w_out[layer])
    return x
```

```python
import functools

import jax
import jax.numpy as jnp
import numpy as np
from jax import lax
from jax.experimental import pallas as pl
from jax.experimental.pallas import tpu as pltpu

F32 = jnp.float32
BF16 = jnp.bfloat16

D_MODEL = 2048
HEAD_DIM = 64
A_Q_HEADS = 12
A_KV_HEADS = 4
A_GROUP = A_Q_HEADS // A_KV_HEADS
WINDOW = 128
B_HEADS = 12
C_HEADS = 4
C_HEAD_DIM = 128
A_WIDTH = A_Q_HEADS * HEAD_DIM
A_KV_WIDTH = A_KV_HEADS * HEAD_DIM
B_WIDTH = B_HEADS * HEAD_DIM
C_WIDTH = C_HEADS * C_HEAD_DIM
EPS = 1e-6
NEG = -1e30

LANES = 128
HALF = LANES // 2
VT_ROWS = HALF + 16
LOG2E = float(np.log2(np.e))
N_PIECES = 3
VMEM_LIMIT = 56 * 1024 * 1024

_SRC = {}
_off = 0
for _name, _w in (("qA", A_WIDTH), ("kA", A_KV_WIDTH), ("vA", A_KV_WIDTH), ("zA", A_WIDTH),
                  ("qB", B_WIDTH), ("kB", B_WIDTH), ("vB", B_WIDTH), ("zB", B_WIDTH),
                  ("fB", B_HEADS), ("qC", C_WIDTH), ("zC", C_WIDTH), ("g", 3 * D_MODEL)):
    _SRC[_name] = (_off, _w)
    _off += _w

F_START, F_SHIFT = _SRC["fB"]
_DST = {n: (o if o < F_START else o - F_SHIFT) for n, (o, _) in _SRC.items() if n != "fB"}
PROJ_WIDTH = _SRC["g"][0] + _SRC["g"][1] - F_SHIFT
assert all(o % LANES == 0 for o in _DST.values()) and F_START % LANES == 0


def _col_block(name, width, part=0):
    assert _DST[name] % width == 0
    return _DST[name] // width + part


def _rms(x, gain):
    ms = jnp.mean(x * x, axis=-1, keepdims=True)
    return x * lax.rsqrt(ms + EPS) * gain


def _sigmoid(t):
    return 0.5 * (jnp.tanh(0.5 * t) + 1.0)


def _dot_nt(a, b):
    return lax.dot_general(a, b, (((1,), (1,)), ((), ())), preferred_element_type=F32)


def _mem_kv_kernel(mem_ref, gain_ref, w_ref, kgain_ref, mk_ref, mv_ref):
    hn = _rms(mem_ref[0], gain_ref[...]).astype(BF16)
    kv = jnp.dot(hn, w_ref[...].astype(BF16), preferred_element_type=F32)
    for h in range(C_HEADS):
        sl = slice(h * C_HEAD_DIM, (h + 1) * C_HEAD_DIM)
        mk_ref[0, :, sl] = _rms(kv[:, sl], kgain_ref[...]).astype(BF16)
    mv_ref[0] = kv[:, C_WIDTH:].astype(BF16)


def _mem_kv(mem, gain, w, kgain):
    b, m, d = mem.shape
    return pl.pallas_call(
        _mem_kv_kernel,
        grid=(b,),
        in_specs=[
            pl.BlockSpec((1, m, d), lambda i: (i, 0, 0)),
            pl.BlockSpec((1, d), lambda i: (0, 0)),
            pl.BlockSpec((d, 2 * C_WIDTH), lambda i: (0, 0)),
            pl.BlockSpec((1, C_HEAD_DIM), lambda i: (0, 0)),
        ],
        out_specs=[
            pl.BlockSpec((1, m, C_WIDTH), lambda i: (i, 0, 0)),
            pl.BlockSpec((1, m, C_WIDTH), lambda i: (i, 0, 0)),
        ],
        out_shape=[jax.ShapeDtypeStruct((b, m, C_WIDTH), BF16)] * 2,
        compiler_params=pltpu.CompilerParams(
            dimension_semantics=("arbitrary",), vmem_limit_bytes=VMEM_LIMIT),
        name="mem_kv",
    )(mem, gain, w, kgain)


NX_TM = 1024


def _norm_x_kernel(tiles_per_seq, x_ref, gain_ref, wf_ref, bf_ref, place_ref, hn_ref, aug_ref,
                   carry_ref):
    i = pl.program_id(0)

    @pl.when(i % tiles_per_seq == 0)
    def _():
        carry_ref[...] = jnp.zeros_like(carry_ref)

    hn = _rms(x_ref[...], gain_ref[...]).astype(BF16)
    hn_ref[...] = hn
    f_logit = _dot_nt(hn, wf_ref[...]) + bf_ref[...]
    x = jnp.minimum(f_logit, 0.0) - jnp.log1p(jnp.exp(-jnp.abs(f_logit)))
    row = lax.broadcasted_iota(jnp.int32, x.shape, 0)
    shift = 1
    while shift < NX_TM:
        x = x + jnp.where(row >= shift, pltpu.roll(x, shift, 0), 0.0)
        shift *= 2
    c = x + carry_ref[...]
    carry_ref[...] = c[NX_TM - 1:NX_TM, :]
    rest = c * (-LOG2E)
    pieces = []
    for _ in range(N_PIECES):
        p = rest.astype(BF16)
        pieces.append(p)
        rest = rest - p.astype(F32)
    stacked = jnp.concatenate(pieces, axis=1)
    aug_ref[...] = jnp.dot(stacked, place_ref[...], preferred_element_type=F32).astype(BF16)


def _norm_x(x2, gain, wf, bfg, place, seq):
    t, d = x2.shape
    return pl.pallas_call(
        functools.partial(_norm_x_kernel, seq // NX_TM),
        grid=(t // NX_TM,),
        in_specs=[
            pl.BlockSpec((NX_TM, d), lambda i: (i, 0)),
            pl.BlockSpec((1, d), lambda i: (0, 0)),
            pl.BlockSpec((LANES, d), lambda i: (0, 0)),
            pl.BlockSpec((1, LANES), lambda i: (0, 0)),
            pl.BlockSpec((N_PIECES * LANES, B_WIDTH), lambda i: (0, 0)),
        ],
        out_specs=[
            pl.BlockSpec((NX_TM, d), lambda i: (i, 0)),
            pl.BlockSpec((NX_TM, B_WIDTH), lambda i: (i, 0)),
        ],
        out_shape=[jax.ShapeDtypeStruct((t, d), BF16),
                   jax.ShapeDtypeStruct((t, B_WIDTH), BF16)],
        scratch_shapes=[pltpu.VMEM((1, LANES), F32)],
        compiler_params=pltpu.CompilerParams(
            dimension_semantics=("arbitrary",), vmem_limit_bytes=VMEM_LIMIT),
        name="norm_x",
    )(x2, gain, wf, bfg, place)


IN_TM = 1024
IN_TN = 1024
IN_CHUNKS = IN_TN // LANES
K_CHUNKS = D_MODEL // LANES
N_IN_TILES = PROJ_WIDTH // IN_TN
N_ALIGNED_TILES = F_START // IN_TN
assert F_START % IN_TN == 0 and PROJ_WIDTH % IN_TN == 0 and F_SHIFT < LANES

_NORM_KIND = {"qA": HEAD_DIM, "kA": HEAD_DIM, "qB": HEAD_DIM, "kB": HEAD_DIM, "qC": C_HEAD_DIM}


def _chunk_piece(col):
    for name, off in _DST.items():
        if off <= col < off + _SRC[name][1]:
            return name
    raise ValueError(col)


def _norm_prefix(tile):
    kinds = [_chunk_piece(tile * IN_TN + c * LANES) in _NORM_KIND for c in range(IN_CHUNKS)]
    n = sum(kinds)
    assert kinds == [True] * n + [False] * (IN_CHUNKS - n)
    return n


NORM_PREFIX = tuple(_norm_prefix(j) for j in range(N_IN_TILES))
NORM_TILES = tuple(j for j in range(N_IN_TILES) if NORM_PREFIX[j])


def _norm_slot(j):
    slot = 0
    for tile in NORM_TILES[1:]:
        slot = slot + (j >= tile).astype(jnp.int32)
    return slot


IN_MCH = 512


def _gather_weight_rows(w_ref, dst_ref, src_row, dst_rows, n_rows):
    for c in range(K_CHUNKS):
        dst_ref[dst_rows, c * LANES:(c + 1) * LANES] = (
            w_ref[pl.ds(src_row + c, n_rows, stride=K_CHUNKS), :].astype(BF16))


def _w_tile0_kernel(w_ref, o_ref):
    _gather_weight_rows(w_ref, o_ref, 0, slice(None), IN_TN)


def _w_index(tile):
    col = tile * IN_TN + jnp.where(tile >= N_ALIGNED_TILES, F_SHIFT, 0)
    return (col * K_CHUNKS, 0)


def _w_tile0(w_rows):
    return pl.pallas_call(
        _w_tile0_kernel,
        grid=(1,),
        in_specs=[pl.BlockSpec((pl.Element(IN_TN * K_CHUNKS), pl.Element(LANES)),
                               lambda i: _w_index(i))],
        out_specs=pl.BlockSpec((IN_TN, D_MODEL), lambda i: (0, 0)),
        out_shape=jax.ShapeDtypeStruct((IN_TN, D_MODEL), BF16),
        compiler_params=pltpu.CompilerParams(
            dimension_semantics=("arbitrary",), vmem_limit_bytes=VMEM_LIMIT),
        name="w_tile0",
    )(w_rows)


def _in_proj_kernel(n_row_tiles, hn_ref, w_ref, w0_ref, cscale_ref, gmat_ref, o_ref,
                    wbf_a, wbf_b, acc_ref):
    j = pl.program_id(0)
    i = pl.program_id(1)
    share = IN_TN // n_row_tiles

    @pl.when((j == 0) & (i == 0))
    def _():
        wbf_a[...] = w0_ref[...]

    def epilogue(chunk, prefix):
        rows = slice(chunk * IN_MCH, (chunk + 1) * IN_MCH)
        acc = acc_ref.at[chunk % 2]
        for c in range(prefix // 2):
            sl = slice(c * 2 * LANES, (c + 1) * 2 * LANES)
            a = acc[:, sl]
            ms = jnp.dot((a * a).astype(BF16), gmat_ref[c], preferred_element_type=F32)
            o_ref[rows, sl] = (a * lax.rsqrt(ms + EPS) * cscale_ref[:, sl]).astype(BF16)
        if prefix < IN_CHUNKS:
            rest = slice(prefix * LANES, IN_TN)
            o_ref[rows, rest] = acc[:, rest].astype(BF16)

    def step(cur, nxt, prefix):
        src = pl.multiple_of(i * (share * K_CHUNKS), share * K_CHUNKS)
        dst = pl.ds(pl.multiple_of(i * share, share), share)
        _gather_weight_rows(w_ref, nxt, src, dst, share)
        n_chunks = hn_ref.shape[0] // IN_MCH
        for c in range(n_chunks + 1):
            if c < n_chunks:
                acc_ref[c % 2] = _dot_nt(hn_ref[c * IN_MCH:(c + 1) * IN_MCH, :], cur[...])
            if c >= 1:
                epilogue(c - 1, prefix)

    def is_any(tiles):
        cond = j == tiles[0]
        for tile in tiles[1:]:
            cond = cond | (j == tile)
        return cond

    for parity, prefix in sorted({(t % 2, NORM_PREFIX[t]) for t in range(N_IN_TILES)}):
        tiles = [t for t in range(N_IN_TILES) if (t % 2, NORM_PREFIX[t]) == (parity, prefix)]
        cur, nxt = (wbf_a, wbf_b) if parity == 0 else (wbf_b, wbf_a)
        pl.when(is_any(tiles))(functools.partial(step, cur, nxt, prefix))


def _in_proj(hn, w_rows, w_first, cscale, gmat):
    t, d = hn.shape
    n_row_tiles = t // IN_TM
    assert IN_TN % n_row_tiles == 0 and (IN_TN // n_row_tiles) % 16 == 0
    return pl.pallas_call(
        functools.partial(_in_proj_kernel, n_row_tiles),
        grid=(N_IN_TILES, n_row_tiles),
        in_specs=[
            pl.BlockSpec((IN_TM, d), lambda j, i: (i, 0)),
            pl.BlockSpec((pl.Element(IN_TN * K_CHUNKS), pl.Element(LANES)),
                         lambda j, i: _w_index(jnp.minimum(j + 1, N_IN_TILES - 1))),
            pl.BlockSpec((IN_TN, d), lambda j, i: (0, 0), pipeline_mode=pl.Buffered(1)),
            pl.BlockSpec((1, IN_TN), lambda j, i: (0, _norm_slot(j))),
            pl.BlockSpec((IN_CHUNKS // 2, 2 * LANES, 2 * LANES),
                         lambda j, i: (_norm_slot(j), 0, 0)),
        ],
        out_specs=pl.BlockSpec((IN_TM, IN_TN), lambda j, i: (i, j)),
        out_shape=jax.ShapeDtypeStruct((t, PROJ_WIDTH), BF16),
        scratch_shapes=[
            pltpu.VMEM((IN_TN, d), BF16),
            pltpu.VMEM((IN_TN, d), BF16),
            pltpu.VMEM((2, IN_MCH, IN_TN), F32),
        ],
        compiler_params=pltpu.CompilerParams(
            dimension_semantics=("arbitrary", "arbitrary"), vmem_limit_bytes=VMEM_LIMIT),
        name="in_proj",
    )(hn, w_rows, w_first, cscale, gmat)


def _aug_lane(head):
    return HALF if head % 2 == 0 else 0


def _placement_matrix():
    place = np.zeros((N_PIECES * LANES, B_WIDTH), np.float32)
    for h in range(B_HEADS):
        for p in range(N_PIECES):
            place[p * LANES + h, (h // 2) * LANES + _aug_lane(h) + p] = 1.0
    return place


SWA_QB = 4
SWA_WIN = (SWA_QB + 1) * WINDOW


def _alibi_slopes_log2():
    return [float(2.0 ** (-8.0 * (h + 1) / A_Q_HEADS)) * LOG2E for h in range(A_Q_HEADS)]


def _swa_key_bias():
    key = np.arange(2 * WINDOW)[:, None]
    qry = np.arange(WINDOW)[None, :]
    rel = qry + WINDOW - key
    visible = (rel >= 0) & (rel < WINDOW)
    slopes = np.asarray(_alibi_slopes_log2(), np.float32)[:, None, None]
    return np.where(visible[None], slopes * key[None].astype(np.float32), np.float32(NEG))


def _swa_kernel(n_cast, sinks_ref, bias_ref, q_ref, kp_ref, kc_ref, vp_ref, vc_ref, *refs):
    cast_in, o_ref, cast_out = refs[:n_cast], refs[n_cast], refs[n_cast + 1:2 * n_cast + 1]
    kop_ref, vt_ref, s_ref, p_ref, sh_ref, ot_ref = refs[2 * n_cast + 1:]
    for src, dst in zip(cast_in, cast_out):
        dst[...] = src[...].astype(BF16)
    step = pl.program_id(1)
    blk = WINDOW
    lane = lax.broadcasted_iota(jnp.int32, (SWA_WIN, LANES), 1)

    kwin = jnp.concatenate([kp_ref[...], kc_ref[...]], axis=0).astype(F32)
    vwin_t = jnp.concatenate([vp_ref[...], vc_ref[...]], axis=0).astype(F32).T
    ones_rows = jnp.where(
        lax.broadcasted_iota(jnp.int32, (VT_ROWS - HALF, SWA_WIN), 0) == 0, 1.0, 0.0)
    for g in range(A_KV_HEADS):
        slab = kwin[:, (g // 2) * LANES:(g // 2 + 1) * LANES]
        own = (lane < HALF) if g % 2 == 0 else (lane >= HALF)
        kz = jnp.where(own, slab, 0.0)
        kop_ref[g, 0] = kz.astype(BF16)
        kop_ref[g, 1] = pltpu.roll(kz, HALF, 1).astype(BF16)
        vt_ref[g] = jnp.concatenate(
            [vwin_t[g * HEAD_DIM:(g + 1) * HEAD_DIM], ones_rows], axis=0).astype(BF16)

    t_win = lax.broadcasted_iota(jnp.int32, (1, blk), 1).astype(F32) + float(blk)
    slopes = _alibi_slopes_log2()
    sinks = [sinks_ref[h] * LOG2E + slopes[h] * t_win for h in range(A_Q_HEADS)]
    key_row = lax.broadcasted_iota(jnp.int32, (2 * blk, blk), 0)

    items = [(i, g) for i in range(SWA_QB) for g in range(A_KV_HEADS)]
    keys_of = lambda i: slice(i * blk, (i + 2) * blk)

    def stage_scores(n):
        i, g = items[n]
        for k in range(A_GROUP):
            h = g * A_GROUP + k
            q_slab = q_ref[i * blk:(i + 1) * blk, (h // 2) * LANES:(h // 2 + 1) * LANES]
            s_ref[n % 2, k] = _dot_nt(kop_ref[g, (h + g) % 2, keys_of(i), :], q_slab)

    def stage_softmax(n):
        i, g = items[n]
        for k in range(A_GROUP):
            h = g * A_GROUP + k
            st = s_ref[n % 2, k] + bias_ref[h]
            if i == 0:
                st = jnp.where(((step * SWA_QB - 1) * blk + key_row) >= 0, st, NEG)
            m = jnp.maximum(jnp.max(st, axis=0, keepdims=True), sinks[h])
            p_ref[n % 2, k] = jnp.exp2(st - m).astype(BF16)
            sh_ref[n % 2, k] = jnp.exp2(sinks[h] - m)

    def stage_values(n):
        i, g = items[n]
        for k in range(A_GROUP):
            h = g * A_GROUP + k
            ot = jnp.dot(vt_ref[g, :, keys_of(i)], p_ref[n % 2, k],
                         preferred_element_type=F32)
            ot_ref[h * HEAD_DIM:(h + 1) * HEAD_DIM, :] = (
                ot[0:HEAD_DIM] / (ot[HEAD_DIM:HEAD_DIM + 1] + sh_ref[n % 2, k]))
        if g == A_KV_HEADS - 1:
            o_ref[i * blk:(i + 1) * blk, :] = ot_ref[...].T.astype(BF16)

    for n in range(len(items) + 2):
        if n < len(items):
            stage_scores(n)
        if 2 <= n:
            stage_values(n - 2)
        if 1 <= n <= len(items):
            stage_softmax(n - 1)


def _swa(sinks, bias, proj, cast_weights, batch, seq):
    blk = WINDOW
    nb = seq // blk
    steps = nb // SWA_QB
    n_steps = batch * steps
    qcol = _col_block("qA", A_WIDTH)
    kcol = _col_block("kA", A_KV_WIDTH)
    vcol = _col_block("vA", A_KV_WIDTH)
    cur = lambda b, s: b * steps + s
    prev = lambda b, s: b * nb + jnp.maximum(s * SWA_QB - 1, 0)
    slab_specs = []
    for w in cast_weights:
        assert w.shape[0] % (16 * n_steps) == 0
        slab_specs.append(pl.BlockSpec((w.shape[0] // n_steps, w.shape[1]),
                                       lambda b, s: (cur(b, s), 0)))
    outs = pl.pallas_call(
        functools.partial(_swa_kernel, len(cast_weights)),
        grid=(batch, steps),
        in_specs=[
            pl.BlockSpec(memory_space=pltpu.SMEM),
            pl.BlockSpec(bias.shape, lambda b, s: (0, 0, 0)),
            pl.BlockSpec((SWA_QB * blk, A_WIDTH), lambda b, s: (cur(b, s), qcol)),
            pl.BlockSpec((blk, A_KV_WIDTH), lambda b, s: (prev(b, s), kcol)),
            pl.BlockSpec((SWA_QB * blk, A_KV_WIDTH), lambda b, s: (cur(b, s), kcol)),
            pl.BlockSpec((blk, A_KV_WIDTH), lambda b, s: (prev(b, s), vcol)),
            pl.BlockSpec((SWA_QB * blk, A_KV_WIDTH), lambda b, s: (cur(b, s), vcol)),
            *slab_specs,
        ],
        out_specs=[pl.BlockSpec((SWA_QB * blk, A_WIDTH), lambda b, s: (cur(b, s), 0)),
                   *slab_specs],
        out_shape=[jax.ShapeDtypeStruct((batch * seq, A_WIDTH), BF16),
                   *[jax.ShapeDtypeStruct(w.shape, BF16) for w in cast_weights]],
        scratch_shapes=[
            pltpu.VMEM((A_KV_HEADS, 2, SWA_WIN, LANES), BF16),
            pltpu.VMEM((A_KV_HEADS, VT_ROWS, SWA_WIN), BF16),
            pltpu.VMEM((2, A_GROUP, 2 * blk, blk), F32),
            pltpu.VMEM((2, A_GROUP, 2 * blk, blk), BF16),
            pltpu.VMEM((2, A_GROUP, 1, blk), F32),
            pltpu.VMEM((A_WIDTH, blk), F32),
        ],
        compiler_params=pltpu.CompilerParams(
            dimension_semantics=("arbitrary", "arbitrary"), vmem_limit_bytes=VMEM_LIMIT),
        name="swa",
    )(sinks, bias, proj, proj, proj, proj, proj, *cast_weights)
    return outs[0], outs[1:]


FOX_T = 512


FOX_H = FOX_T // 2


def _fox_kernel(q_ref, k_ref, v_ref, aug_ref, o_ref,
                kop_ref, vt_ref, qop_ref, s_ref, cm_ref, p_ref, al_ref, m_ref, acc_ref):
    seq = k_ref.shape[0]
    t = FOX_T
    hk = FOX_H

    lane = lax.broadcasted_iota(jnp.int32, (hk, LANES), 1)
    low = lane < HALF

    def build(r):
        sl = slice(r * hk, (r + 1) * hk)
        kk = k_ref[sl, :].astype(F32)
        aa = aug_ref[sl, :].astype(F32)
        vv = v_ref[sl, :].astype(F32)
        kop_ref[0, sl, :] = jnp.where(low, kk, aa).astype(BF16)
        kop_ref[1, sl, :] = jnp.where(low, aa, kk).astype(BF16)
        vvt = vv.T
        ones_rows = jnp.where(
            lax.broadcasted_iota(jnp.int32, (VT_ROWS - HALF, hk), 0) == 0, 1.0, 0.0)
        for hh in range(2):
            vt_ref[hh, r] = jnp.concatenate(
                [vvt[hh * HALF:(hh + 1) * HALF], ones_rows], axis=0).astype(BF16)

    def scores(slot, half, masked):
        ks = pl.ds(pl.multiple_of(half * hk, hk), hk)
        for hh in range(2):
            st = _dot_nt(kop_ref[hh, ks, :], qop_ref[hh])
            if masked:
                key = lax.broadcasted_iota(jnp.int32, (hk, t), 0) + slot * hk
                qry = lax.broadcasted_iota(jnp.int32, (hk, t), 1)
                st = jnp.where(key <= qry, st, NEG)
            s_ref[slot, hh] = st
            cm_ref[slot, hh] = jnp.max(st, axis=0, keepdims=True)

    def soft(slot):
        for hh in range(2):
            m_old = m_ref[hh]
            m_new = jnp.maximum(m_old, cm_ref[slot, hh])
            al_ref[slot, hh] = jnp.exp2(m_old - m_new)
            p_ref[slot, hh] = jnp.exp2(s_ref[slot, hh] - m_new).astype(BF16)
            m_ref[hh] = m_new

    def pv(slot, half):
        for hh in range(2):
            acc_ref[hh] = acc_ref[hh] * al_ref[slot, hh] + jnp.dot(
                vt_ref[hh, half], p_ref[slot, hh], preferred_element_type=F32)

    def step(i, prev, has_next):
        pv(0, 2 * prev)
        soft(0)
        scores(1, 2 * i + 1, False)
        if has_next:
            scores(0, 2 * i + 2, False)
        pv(1, 2 * prev + 1)
        soft(1)

    lane_q = lax.broadcasted_iota(jnp.int32, (t, LANES), 1)
    low_q = lane_q < HALF
    ones0 = jnp.where((lane_q >= HALF) & (lane_q < HALF + N_PIECES), 1.0, 0.0)
    ones1 = jnp.where(lane_q < N_PIECES, 1.0, 0.0)

    for qi in range(seq // t):
        build(2 * qi)
        build(2 * qi + 1)
        rows = slice(qi * t, (qi + 1) * t)
        q = q_ref[rows, :].astype(F32)
        qop_ref[0] = jnp.where(low_q, q, ones0).astype(BF16)
        qop_ref[1] = jnp.where(low_q, ones1, q).astype(BF16)
        m_ref[...] = jnp.full(m_ref.shape, NEG, F32)
        acc_ref[...] = jnp.zeros(acc_ref.shape, F32)

        scores(0, 2 * qi, True)
        scores(1, 2 * qi + 1, True)
        soft(0)
        if qi > 0:
            scores(0, 0, False)
        soft(1)

        loop_pairs = max(qi - 1, 0) // 2

        def pair(j, carry, qi=qi):
            first = 2 * j
            step(first, jnp.where(j == 0, qi, first - 1), True)
            step(first + 1, first, True)
            return carry

        if loop_pairs:
            lax.fori_loop(0, loop_pairs, pair, 0)
        for i in range(2 * loop_pairs, qi):
            step(i, i - 1 if i > 0 else qi, i + 1 < qi)
        last = qi - 1 if qi > 0 else qi
        pv(0, 2 * last)
        pv(1, 2 * last + 1)

        ot = jnp.concatenate(
            [acc_ref[hh, 0:HALF, :] / acc_ref[hh, HALF:HALF + 1, :] for hh in range(2)], axis=0)
        o_ref[rows, :] = ot.T.astype(BF16)


def _fox(proj, aug, batch, seq):
    t = FOX_T
    nq = seq // t
    pairs = B_HEADS // 2
    qcol = _col_block("qB", LANES)
    kcol = _col_block("kB", LANES)
    vcol = _col_block("vB", LANES)
    return pl.pallas_call(
        _fox_kernel,
        grid=(batch, pairs),
        in_specs=[
            pl.BlockSpec((seq, LANES), lambda b, p: (b, qcol + p)),
            pl.BlockSpec((seq, LANES), lambda b, p: (b, kcol + p)),
            pl.BlockSpec((seq, LANES), lambda b, p: (b, vcol + p)),
            pl.BlockSpec((seq, LANES), lambda b, p: (b, p)),
        ],
        out_specs=pl.BlockSpec((seq, LANES), lambda b, p: (b, p)),
        out_shape=jax.ShapeDtypeStruct((batch * seq, B_WIDTH), BF16),
        scratch_shapes=[
            pltpu.VMEM((2, seq, LANES), BF16),
            pltpu.VMEM((2, seq // FOX_H, VT_ROWS, FOX_H), BF16),
            pltpu.VMEM((2, t, LANES), BF16),
            pltpu.VMEM((2, 2, FOX_H, t), F32),
            pltpu.VMEM((2, 2, 1, t), F32),
            pltpu.VMEM((2, 2, FOX_H, t), BF16),
            pltpu.VMEM((2, 2, 1, t), F32),
            pltpu.VMEM((2, 1, t), F32),
            pltpu.VMEM((2, VT_ROWS, t), F32),
        ],
        compiler_params=pltpu.CompilerParams(
            dimension_semantics=("arbitrary", "arbitrary"),
            vmem_limit_bytes=VMEM_LIMIT),
        name="fox",
    )(proj, proj, proj, aug)


MERGE_TM = 256


Z_BLOCK = 256
Z_PARTS = A_WIDTH // Z_BLOCK
assert A_WIDTH == B_WIDTH and A_WIDTH % Z_BLOCK == 0


def _merge_kernel(x_ref, ya_ref, yb_ref, mk_ref, mv_ref, qc_ref, *refs):
    za_refs = refs[:Z_PARTS]
    zb_refs = refs[Z_PARTS:2 * Z_PARTS]
    zc_ref, g0_ref, g1_ref, g2_ref, wa_ref, wb_ref, wc_ref, wo_ref, o_ref = refs[2 * Z_PARTS:]

    def memory_attention():
        outs = []
        for h in range(C_HEADS):
            sl = slice(h * C_HEAD_DIM, (h + 1) * C_HEAD_DIM)
            s = _dot_nt(qc_ref[:, sl], mk_ref[0, :, sl])
            p = jnp.exp(s - jnp.max(s, axis=-1, keepdims=True))
            o = jnp.dot(p.astype(BF16), mv_ref[0, :, sl], preferred_element_type=F32)
            outs.append(o / jnp.sum(p, axis=-1, keepdims=True))
        return jnp.concatenate(outs, axis=1)

    def branch(y, z_refs, w_ref):
        z = jnp.concatenate([r[...] for r in z_refs], axis=1).astype(F32)
        h = (y * (z * _sigmoid(z))).astype(BF16)
        return jnp.dot(h, w_ref[...], preferred_element_type=F32)

    y = _sigmoid(g0_ref[...].astype(F32)) * branch(ya_ref[...].astype(F32), za_refs, wa_ref)
    y = y + _sigmoid(g1_ref[...].astype(F32)) * branch(yb_ref[...].astype(F32), zb_refs, wb_ref)
    y = y + _sigmoid(g2_ref[...].astype(F32)) * branch(memory_attention(), (zc_ref,), wc_ref)
    o_ref[...] = x_ref[...] + jnp.dot(y.astype(BF16), wo_ref[...], preferred_element_type=F32)


def _merge(x2, ya, yb, mk, mv, proj, wa, wb, wc, wo, seq):
    t, d = x2.shape
    tm = MERGE_TM
    tiles_per_seq = seq // tm
    gcol = _col_block("g", D_MODEL)
    row = lambda width, col: pl.BlockSpec((tm, width), lambda i: (i, col))
    full = lambda a: pl.BlockSpec(a.shape, lambda i: (0, 0), pipeline_mode=pl.Buffered(1))
    mem = lambda a: pl.BlockSpec((1,) + a.shape[1:], lambda i: (i // tiles_per_seq, 0, 0))
    z_specs = lambda name: [row(Z_BLOCK, _col_block(name, Z_BLOCK, part)) for part in range(Z_PARTS)]
    n_proj = 2 * Z_PARTS + 5
    return pl.pallas_call(
        _merge_kernel,
        grid=(t // tm,),
        in_specs=[
            row(d, 0), row(A_WIDTH, 0), row(B_WIDTH, 0), mem(mk), mem(mv),
            row(C_WIDTH, _col_block("qC", C_WIDTH)),
            *z_specs("zA"), *z_specs("zB"), row(C_WIDTH, _col_block("zC", C_WIDTH)),
            row(d, gcol), row(d, gcol + 1), row(d, gcol + 2),
            full(wa), full(wb), full(wc), full(wo),
        ],
        out_specs=pl.BlockSpec((tm, d), lambda i: (i, 0)),
        out_shape=jax.ShapeDtypeStruct((t, d), F32),
        compiler_params=pltpu.CompilerParams(
            dimension_semantics=("arbitrary",), vmem_limit_bytes=VMEM_LIMIT),
        name="merge",
    )(x2, ya, yb, mk, mv, *([proj] * n_proj), wa, wb, wc, wo)


def _group_mean_matrices():
    lane = np.arange(2 * LANES)
    chunk = lane // LANES
    mats = []
    for tile in NORM_TILES:
        for c in range(0, IN_CHUNKS, 2):
            dims = np.asarray([_NORM_KIND.get(_chunk_piece(tile * IN_TN + (c + k) * LANES), LANES)
                               for k in range(2)])[chunk]
            same = ((lane[:, None] // dims[:, None] == lane[None, :] // dims[None, :])
                    & (chunk[:, None] == chunk[None, :]))
            mats.append(same.astype(np.float32) / dims[None, :])
    return np.stack(mats)


def _column_scales(gains):
    scale = {"qA": HEAD_DIM ** -0.5 * LOG2E, "qB": HEAD_DIM ** -0.5 * LOG2E,
             "qC": C_HEAD_DIM ** -0.5}
    cols = []
    for tile in NORM_TILES:
        for c in range(IN_CHUNKS):
            piece = _chunk_piece(tile * IN_TN + c * LANES)
            if piece in _NORM_KIND:
                g = gains[piece].astype(F32) * scale.get(piece, 1.0)
                cols.append(jnp.tile(g, LANES // g.shape[0]))
            else:
                cols.append(jnp.zeros((LANES,), F32))
    return jnp.concatenate(cols).reshape(1, len(NORM_TILES) * IN_TN)


def _layer(x, mem, norm_gain, mem_norm_gain, w_rows, b_forget, q_gain_a, k_gain_a, sinks_a,
           q_gain_b, k_gain_b, q_gain_c, k_gain_c, w_mem_kv, w_branch_a, w_branch_b,
           w_branch_c, w_out):
    batch, seq, d = x.shape
    x2 = x.reshape(batch * seq, d)

    w_f = w_rows[F_START * K_CHUNKS:(F_START + F_SHIFT) * K_CHUNKS].reshape(F_SHIFT, d)
    w_f = jnp.pad(w_f, ((0, LANES - F_SHIFT), (0, 0))).astype(BF16)
    b_f = jnp.pad(b_forget.astype(F32), (0, LANES - F_SHIFT)).reshape(1, LANES)
    cscale = _column_scales({"qA": q_gain_a, "kA": k_gain_a, "qB": q_gain_b, "kB": k_gain_b,
                             "qC": q_gain_c})
    gmat = jnp.asarray(_group_mean_matrices(), BF16)
    place = jnp.asarray(_placement_matrix(), BF16)

    mk, mv = _mem_kv(mem, mem_norm_gain.reshape(1, d), w_mem_kv, k_gain_c.reshape(1, C_HEAD_DIM))
    hn, aug = _norm_x(x2, norm_gain.reshape(1, d), w_f, b_f, place, seq)
    proj = _in_proj(hn, w_rows, _w_tile0(w_rows), cscale, gmat)
    ya, merge_weights = _swa(sinks_a.astype(F32), jnp.asarray(_swa_key_bias(), F32), proj,
                             (w_branch_a, w_branch_b, w_branch_c, w_out), batch, seq)
    yb = _fox(proj, aug, batch, seq)
    out = _merge(x2, ya, yb, mk, mv, proj, *merge_weights, seq)
    return out.reshape(batch, seq, d)


def kernel(x, mem, norm_gain, mem_norm_gain, w_in, b_forget, q_gain_a, k_gain_a, sinks_a,
           q_gain_b, k_gain_b, q_gain_c, k_gain_c, w_mem_kv, w_branch_a, w_branch_b,
           w_branch_c, w_out):
    depth = norm_gain.shape[0]
    w_rows = jnp.swapaxes(w_in, 1, 2).reshape(depth, -1, LANES)
    for layer in range(depth):
        x = _layer(x, mem, norm_gain[layer], mem_norm_gain[layer], w_rows[layer], b_forget[layer],
                   q_gain_a[layer], k_gain_a[layer], sinks_a[layer], q_gain_b[layer],
                   k_gain_b[layer], q_gain_c[layer], k_gain_c[layer], w_mem_kv[layer],
                   w_branch_a[layer], w_branch_b[layer], w_branch_c[layer], w_out[layer])
    return x
```

```python
import functools

import jax
import jax.numpy as jnp
import numpy as np
from jax import lax
from jax.experimental import pallas as pl
from jax.experimental.pallas import tpu as pltpu

F32 = jnp.float32
BF16 = jnp.bfloat16

D_MODEL = 2048
HEAD_DIM = 64
A_Q_HEADS = 12
A_KV_HEADS = 4
A_GROUP = A_Q_HEADS // A_KV_HEADS
WINDOW = 128
B_HEADS = 12
C_HEADS = 4
C_HEAD_DIM = 128
A_WIDTH = A_Q_HEADS * HEAD_DIM
A_KV_WIDTH = A_KV_HEADS * HEAD_DIM
B_WIDTH = B_HEADS * HEAD_DIM
C_WIDTH = C_HEADS * C_HEAD_DIM
EPS = 1e-6
NEG = -1e30

LANES = 128
HALF = LANES // 2
VT_ROWS = HALF + 16
LOG2E = float(np.log2(np.e))
N_PIECES = 3
VMEM_LIMIT = 56 * 1024 * 1024

_SRC = {}
_off = 0
for _name, _w in (("qA", A_WIDTH), ("kA", A_KV_WIDTH), ("vA", A_KV_WIDTH), ("zA", A_WIDTH),
                  ("qB", B_WIDTH), ("kB", B_WIDTH), ("vB", B_WIDTH), ("zB", B_WIDTH),
                  ("fB", B_HEADS), ("qC", C_WIDTH), ("zC", C_WIDTH), ("g", 3 * D_MODEL)):
    _SRC[_name] = (_off, _w)
    _off += _w

F_START, F_SHIFT = _SRC["fB"]
_DST = {n: (o if o < F_START else o - F_SHIFT) for n, (o, _) in _SRC.items() if n != "fB"}
PROJ_WIDTH = _SRC["g"][0] + _SRC["g"][1] - F_SHIFT
assert all(o % LANES == 0 for o in _DST.values()) and F_START % LANES == 0


def _col_block(name, width, part=0):
    assert _DST[name] % width == 0
    return _DST[name] // width + part


def _rms(x, gain):
    ms = jnp.mean(x * x, axis=-1, keepdims=True)
    return x * lax.rsqrt(ms + EPS) * gain


def _sigmoid(t):
    return 0.5 * (jnp.tanh(0.5 * t) + 1.0)


def _dot_nt(a, b):
    return lax.dot_general(a, b, (((1,), (1,)), ((), ())), preferred_element_type=F32)


def _mem_kv_kernel(mem_ref, gain_ref, w_ref, kgain_ref, mk_ref, mv_ref):
    hn = _rms(mem_ref[0], gain_ref[...]).astype(BF16)
    kv = jnp.dot(hn, w_ref[...].astype(BF16), preferred_element_type=F32)
    for h in range(C_HEADS):
        sl = slice(h * C_HEAD_DIM, (h + 1) * C_HEAD_DIM)
        mk_ref[0, :, sl] = _rms(kv[:, sl], kgain_ref[...]).astype(BF16)
    mv_ref[0] = kv[:, C_WIDTH:].astype(BF16)


def _mem_kv(mem, gain, w, kgain):
    b, m, d = mem.shape
    return pl.pallas_call(
        _mem_kv_kernel,
        grid=(b,),
        in_specs=[
            pl.BlockSpec((1, m, d), lambda i: (i, 0, 0)),
            pl.BlockSpec((1, d), lambda i: (0, 0)),
            pl.BlockSpec((d, 2 * C_WIDTH), lambda i: (0, 0)),
            pl.BlockSpec((1, C_HEAD_DIM), lambda i: (0, 0)),
        ],
        out_specs=[
            pl.BlockSpec((1, m, C_WIDTH), lambda i: (i, 0, 0)),
            pl.BlockSpec((1, m, C_WIDTH), lambda i: (i, 0, 0)),
        ],
        out_shape=[jax.ShapeDtypeStruct((b, m, C_WIDTH), BF16)] * 2,
        compiler_params=pltpu.CompilerParams(
            dimension_semantics=("arbitrary",), vmem_limit_bytes=VMEM_LIMIT),
        name="mem_kv",
    )(mem, gain, w, kgain)


NX_TM = 1024


def _norm_x_kernel(tiles_per_seq, x_ref, gain_ref, wf_ref, bf_ref, place_ref, hn_ref, aug_ref,
                   carry_ref):
    i = pl.program_id(0)

    @pl.when(i % tiles_per_seq == 0)
    def _():
        carry_ref[...] = jnp.zeros_like(carry_ref)

    hn = _rms(x_ref[...], gain_ref[...]).astype(BF16)
    hn_ref[...] = hn
    f_logit = _dot_nt(hn, wf_ref[...]) + bf_ref[...]
    x = jnp.minimum(f_logit, 0.0) - jnp.log1p(jnp.exp(-jnp.abs(f_logit)))
    row = lax.broadcasted_iota(jnp.int32, x.shape, 0)
    shift = 1
    while shift < NX_TM:
        x = x + jnp.where(row >= shift, pltpu.roll(x, shift, 0), 0.0)
        shift *= 2
    c = x + carry_ref[...]
    carry_ref[...] = c[NX_TM - 1:NX_TM, :]
    rest = c * (-LOG2E)
    pieces = []
    for _ in range(N_PIECES):
        p = rest.astype(BF16)
        pieces.append(p)
        rest = rest - p.astype(F32)
    stacked = jnp.concatenate(pieces, axis=1)
    aug_ref[...] = jnp.dot(stacked, place_ref[...], preferred_element_type=F32).astype(BF16)


def _norm_x(x2, gain, wf, bfg, place, seq):
    t, d = x2.shape
    return pl.pallas_call(
        functools.partial(_norm_x_kernel, seq // NX_TM),
        grid=(t // NX_TM,),
        in_specs=[
            pl.BlockSpec((NX_TM, d), lambda i: (i, 0)),
            pl.BlockSpec((1, d), lambda i: (0, 0)),
            pl.BlockSpec((LANES, d), lambda i: (0, 0)),
            pl.BlockSpec((1, LANES), lambda i: (0, 0)),
            pl.BlockSpec((N_PIECES * LANES, B_WIDTH), lambda i: (0, 0)),
        ],
        out_specs=[
            pl.BlockSpec((NX_TM, d), lambda i: (i, 0)),
            pl.BlockSpec((NX_TM, B_WIDTH), lambda i: (i, 0)),
        ],
        out_shape=[jax.ShapeDtypeStruct((t, d), BF16),
                   jax.ShapeDtypeStruct((t, B_WIDTH), BF16)],
        scratch_shapes=[pltpu.VMEM((1, LANES), F32)],
        compiler_params=pltpu.CompilerParams(
            dimension_semantics=("arbitrary",), vmem_limit_bytes=VMEM_LIMIT),
        name="norm_x",
    )(x2, gain, wf, bfg, place)


IN_TM = 1024
IN_TN = 1024
IN_CHUNKS = IN_TN // LANES
K_CHUNKS = D_MODEL // LANES
N_IN_TILES = PROJ_WIDTH // IN_TN
N_ALIGNED_TILES = F_START // IN_TN
assert F_START % IN_TN == 0 and PROJ_WIDTH % IN_TN == 0 and F_SHIFT < LANES

_NORM_KIND = {"qA": HEAD_DIM, "kA": HEAD_DIM, "qB": HEAD_DIM, "kB": HEAD_DIM, "qC": C_HEAD_DIM}


def _chunk_piece(col):
    for name, off in _DST.items():
        if off <= col < off + _SRC[name][1]:
            return name
    raise ValueError(col)


def _norm_prefix(tile):
    kinds = [_chunk_piece(tile * IN_TN + c * LANES) in _NORM_KIND for c in range(IN_CHUNKS)]
    n = sum(kinds)
    assert kinds == [True] * n + [False] * (IN_CHUNKS - n)
    return n


NORM_PREFIX = tuple(_norm_prefix(j) for j in range(N_IN_TILES))
NORM_TILES = tuple(j for j in range(N_IN_TILES) if NORM_PREFIX[j])


def _norm_slot(j):
    slot = 0
    for tile in NORM_TILES[1:]:
        slot = slot + (j >= tile).astype(jnp.int32)
    return slot


IN_NCH = 2 * LANES
assert all(p % 2 == 0 for p in NORM_PREFIX)


def _gather_weight_rows(w_ref, dst_ref, src_row, dst_rows, n_rows):
    for c in range(K_CHUNKS):
        dst_ref[dst_rows, c * LANES:(c + 1) * LANES] = (
            w_ref[pl.ds(src_row + c, n_rows, stride=K_CHUNKS), :].astype(BF16))


def _w_tile0_kernel(w_ref, o_ref):
    _gather_weight_rows(w_ref, o_ref, 0, slice(None), IN_TN)


def _w_index(tile):
    col = tile * IN_TN + jnp.where(tile >= N_ALIGNED_TILES, F_SHIFT, 0)
    return (col * K_CHUNKS, 0)


def _w_tile0(w_rows):
    return pl.pallas_call(
        _w_tile0_kernel,
        grid=(1,),
        in_specs=[pl.BlockSpec((pl.Element(IN_TN * K_CHUNKS), pl.Element(LANES)),
                               lambda i: _w_index(i))],
        out_specs=pl.BlockSpec((IN_TN, D_MODEL), lambda i: (0, 0)),
        out_shape=jax.ShapeDtypeStruct((IN_TN, D_MODEL), BF16),
        compiler_params=pltpu.CompilerParams(
            dimension_semantics=("arbitrary",), vmem_limit_bytes=VMEM_LIMIT),
        name="w_tile0",
    )(w_rows)


def _in_proj_kernel(n_row_tiles, hn_ref, w_ref, w0_ref, cscale_ref, gmat_ref, o_ref,
                    wbf_a, wbf_b, acc_ref):
    j = pl.program_id(0)
    i = pl.program_id(1)
    share = IN_TN // n_row_tiles

    @pl.when((j == 0) & (i == 0))
    def _():
        wbf_a[...] = w0_ref[...]

    def epilogue(chunk, prefix):
        cols = slice(chunk * IN_NCH, (chunk + 1) * IN_NCH)
        a = acc_ref[chunk % 2]
        if chunk * IN_NCH < prefix * LANES:
            ms = jnp.dot((a * a).astype(BF16), gmat_ref[chunk], preferred_element_type=F32)
            a = a * lax.rsqrt(ms + EPS) * cscale_ref[:, cols]
        o_ref[:, cols] = a.astype(BF16)

    def step(cur, nxt, prefix):
        src = pl.multiple_of(i * (share * K_CHUNKS), share * K_CHUNKS)
        dst = pl.ds(pl.multiple_of(i * share, share), share)
        n_chunks = IN_TN // IN_NCH
        for c in range(n_chunks + 1):
            if c < n_chunks:
                acc_ref[c % 2] = _dot_nt(hn_ref[...], cur[c * IN_NCH:(c + 1) * IN_NCH, :])
            if c == 1:
                _gather_weight_rows(w_ref, nxt, src, dst, share)
            if c >= 1:
                epilogue(c - 1, prefix)

    def is_any(tiles):
        cond = j == tiles[0]
        for tile in tiles[1:]:
            cond = cond | (j == tile)
        return cond

    for parity, prefix in sorted({(t % 2, NORM_PREFIX[t]) for t in range(N_IN_TILES)}):
        tiles = [t for t in range(N_IN_TILES) if (t % 2, NORM_PREFIX[t]) == (parity, prefix)]
        cur, nxt = (wbf_a, wbf_b) if parity == 0 else (wbf_b, wbf_a)
        pl.when(is_any(tiles))(functools.partial(step, cur, nxt, prefix))


def _in_proj(hn, w_rows, w_first, cscale, gmat):
    t, d = hn.shape
    n_row_tiles = t // IN_TM
    assert IN_TN % n_row_tiles == 0 and (IN_TN // n_row_tiles) % 16 == 0
    return pl.pallas_call(
        functools.partial(_in_proj_kernel, n_row_tiles),
        grid=(N_IN_TILES, n_row_tiles),
        in_specs=[
            pl.BlockSpec((IN_TM, d), lambda j, i: (i, 0)),
            pl.BlockSpec((pl.Element(IN_TN * K_CHUNKS), pl.Element(LANES)),
                         lambda j, i: _w_index(jnp.minimum(j + 1, N_IN_TILES - 1))),
            pl.BlockSpec((IN_TN, d), lambda j, i: (0, 0), pipeline_mode=pl.Buffered(1)),
            pl.BlockSpec((1, IN_TN), lambda j, i: (0, _norm_slot(j))),
            pl.BlockSpec((IN_CHUNKS // 2, 2 * LANES, 2 * LANES),
                         lambda j, i: (_norm_slot(j), 0, 0)),
        ],
        out_specs=pl.BlockSpec((IN_TM, IN_TN), lambda j, i: (i, j)),
        out_shape=jax.ShapeDtypeStruct((t, PROJ_WIDTH), BF16),
        scratch_shapes=[
            pltpu.VMEM((IN_TN, d), BF16),
            pltpu.VMEM((IN_TN, d), BF16),
            pltpu.VMEM((2, IN_TM, IN_NCH), F32),
        ],
        compiler_params=pltpu.CompilerParams(
            dimension_semantics=("arbitrary", "arbitrary"), vmem_limit_bytes=VMEM_LIMIT),
        name="in_proj",
    )(hn, w_rows, w_first, cscale, gmat)


def _aug_lane(head):
    return HALF if head % 2 == 0 else 0


def _placement_matrix():
    place = np.zeros((N_PIECES * LANES, B_WIDTH), np.float32)
    for h in range(B_HEADS):
        for p in range(N_PIECES):
            place[p * LANES + h, (h // 2) * LANES + _aug_lane(h) + p] = 1.0
    return place


SWA_QB = 4
SWA_WIN = (SWA_QB + 1) * WINDOW


def _alibi_slopes_log2():
    return [float(2.0 ** (-8.0 * (h + 1) / A_Q_HEADS)) * LOG2E for h in range(A_Q_HEADS)]


def _swa_key_bias():
    key = np.arange(2 * WINDOW)[:, None]
    qry = np.arange(WINDOW)[None, :]
    rel = qry + WINDOW - key
    visible = (rel >= 0) & (rel < WINDOW)
    slopes = np.asarray(_alibi_slopes_log2(), np.float32)[:, None, None]
    return np.where(visible[None], slopes * key[None].astype(np.float32), np.float32(NEG))


def _swa_kernel(n_cast, sinks_ref, bias_ref, q_ref, kp_ref, kc_ref, vp_ref, vc_ref, *refs):
    cast_in, o_ref, cast_out = refs[:n_cast], refs[n_cast], refs[n_cast + 1:2 * n_cast + 1]
    kop_ref, vt_ref, s_ref, p_ref, sh_ref, ot_ref = refs[2 * n_cast + 1:]
    for src, dst in zip(cast_in, cast_out):
        dst[...] = src[...].astype(BF16)
    step = pl.program_id(1)
    blk = WINDOW
    lane = lax.broadcasted_iota(jnp.int32, (SWA_WIN, LANES), 1)

    kwin = jnp.concatenate([kp_ref[...], kc_ref[...]], axis=0).astype(F32)
    vwin_t = jnp.concatenate([vp_ref[...], vc_ref[...]], axis=0).astype(F32).T
    ones_rows = jnp.where(
        lax.broadcasted_iota(jnp.int32, (VT_ROWS - HALF, SWA_WIN), 0) == 0, 1.0, 0.0)
    for g in range(A_KV_HEADS):
        slab = kwin[:, (g // 2) * LANES:(g // 2 + 1) * LANES]
        own = (lane < HALF) if g % 2 == 0 else (lane >= HALF)
        kz = jnp.where(own, slab, 0.0)
        kop_ref[g, 0] = kz.astype(BF16)
        kop_ref[g, 1] = pltpu.roll(kz, HALF, 1).astype(BF16)
        vt_ref[g] = jnp.concatenate(
            [vwin_t[g * HEAD_DIM:(g + 1) * HEAD_DIM], ones_rows], axis=0).astype(BF16)

    t_win = lax.broadcasted_iota(jnp.int32, (1, blk), 1).astype(F32) + float(blk)
    slopes = _alibi_slopes_log2()
    sinks = [sinks_ref[h] * LOG2E + slopes[h] * t_win for h in range(A_Q_HEADS)]
    key_row = lax.broadcasted_iota(jnp.int32, (2 * blk, blk), 0)

    items = [(i, g) for i in range(SWA_QB) for g in range(A_KV_HEADS)]
    keys_of = lambda i: slice(i * blk, (i + 2) * blk)

    def stage_scores(n):
        i, g = items[n]
        for k in range(A_GROUP):
            h = g * A_GROUP + k
            q_slab = q_ref[i * blk:(i + 1) * blk, (h // 2) * LANES:(h // 2 + 1) * LANES]
            s_ref[n % 2, k] = _dot_nt(kop_ref[g, (h + g) % 2, keys_of(i), :], q_slab)

    def stage_softmax(n):
        i, g = items[n]
        for k in range(A_GROUP):
            h = g * A_GROUP + k
            st = s_ref[n % 2, k] + bias_ref[h]
            if i == 0:
                st = jnp.where(((step * SWA_QB - 1) * blk + key_row) >= 0, st, NEG)
            m = jnp.maximum(jnp.max(st, axis=0, keepdims=True), sinks[h])
            p_ref[n % 2, k] = jnp.exp2(st - m).astype(BF16)
            sh_ref[n % 2, k] = jnp.exp2(sinks[h] - m)

    def stage_values(n):
        i, g = items[n]
        for k in range(A_GROUP):
            h = g * A_GROUP + k
            ot = jnp.dot(vt_ref[g, :, keys_of(i)], p_ref[n % 2, k],
                         preferred_element_type=F32)
            ot_ref[h * HEAD_DIM:(h + 1) * HEAD_DIM, :] = (
                ot[0:HEAD_DIM] / (ot[HEAD_DIM:HEAD_DIM + 1] + sh_ref[n % 2, k]))
        if g == A_KV_HEADS - 1:
            o_ref[i * blk:(i + 1) * blk, :] = ot_ref[...].T.astype(BF16)

    for n in range(len(items) + 2):
        if n < len(items):
            stage_scores(n)
        if 2 <= n:
            stage_values(n - 2)
        if 1 <= n <= len(items):
            stage_softmax(n - 1)


def _swa(sinks, bias, proj, cast_weights, batch, seq):
    blk = WINDOW
    nb = seq // blk
    steps = nb // SWA_QB
    n_steps = batch * steps
    qcol = _col_block("qA", A_WIDTH)
    kcol = _col_block("kA", A_KV_WIDTH)
    vcol = _col_block("vA", A_KV_WIDTH)
    cur = lambda b, s: b * steps + s
    prev = lambda b, s: b * nb + jnp.maximum(s * SWA_QB - 1, 0)
    slab_specs = []
    for w in cast_weights:
        assert w.shape[0] % (16 * n_steps) == 0
        slab_specs.append(pl.BlockSpec((w.shape[0] // n_steps, w.shape[1]),
                                       lambda b, s: (cur(b, s), 0)))
    outs = pl.pallas_call(
        functools.partial(_swa_kernel, len(cast_weights)),
        grid=(batch, steps),
        in_specs=[
            pl.BlockSpec(memory_space=pltpu.SMEM),
            pl.BlockSpec(bias.shape, lambda b, s: (0, 0, 0)),
            pl.BlockSpec((SWA_QB * blk, A_WIDTH), lambda b, s: (cur(b, s), qcol)),
            pl.BlockSpec((blk, A_KV_WIDTH), lambda b, s: (prev(b, s), kcol)),
            pl.BlockSpec((SWA_QB * blk, A_KV_WIDTH), lambda b, s: (cur(b, s), kcol)),
            pl.BlockSpec((blk, A_KV_WIDTH), lambda b, s: (prev(b, s), vcol)),
            pl.BlockSpec((SWA_QB * blk, A_KV_WIDTH), lambda b, s: (cur(b, s), vcol)),
            *slab_specs,
        ],
        out_specs=[pl.BlockSpec((SWA_QB * blk, A_WIDTH), lambda b, s: (cur(b, s), 0)),
                   *slab_specs],
        out_shape=[jax.ShapeDtypeStruct((batch * seq, A_WIDTH), BF16),
                   *[jax.ShapeDtypeStruct(w.shape, BF16) for w in cast_weights]],
        scratch_shapes=[
            pltpu.VMEM((A_KV_HEADS, 2, SWA_WIN, LANES), BF16),
            pltpu.VMEM((A_KV_HEADS, VT_ROWS, SWA_WIN), BF16),
            pltpu.VMEM((2, A_GROUP, 2 * blk, blk), F32),
            pltpu.VMEM((2, A_GROUP, 2 * blk, blk), BF16),
            pltpu.VMEM((2, A_GROUP, 1, blk), F32),
            pltpu.VMEM((A_WIDTH, blk), F32),
        ],
        compiler_params=pltpu.CompilerParams(
            dimension_semantics=("arbitrary", "arbitrary"), vmem_limit_bytes=VMEM_LIMIT),
        name="swa",
    )(sinks, bias, proj, proj, proj, proj, proj, *cast_weights)
    return outs[0], outs[1:]


FOX_T = 512


FOX_H = FOX_T // 2


def _fox_kernel(q_ref, k_ref, v_ref, aug_ref, o_ref,
                kop_ref, vt_ref, qop_ref, s_ref, cm_ref, p_ref, al_ref, m_ref, acc_ref):
    seq = k_ref.shape[0]
    t = FOX_T
    hk = FOX_H

    lane = lax.broadcasted_iota(jnp.int32, (hk, LANES), 1)
    low = lane < HALF

    def build(r):
        sl = slice(r * hk, (r + 1) * hk)
        kk = k_ref[sl, :].astype(F32)
        aa = aug_ref[sl, :].astype(F32)
        vv = v_ref[sl, :].astype(F32)
        kop_ref[0, sl, :] = jnp.where(low, kk, aa).astype(BF16)
        kop_ref[1, sl, :] = jnp.where(low, aa, kk).astype(BF16)
        vvt = vv.T
        ones_rows = jnp.where(
            lax.broadcasted_iota(jnp.int32, (VT_ROWS - HALF, hk), 0) == 0, 1.0, 0.0)
        for hh in range(2):
            vt_ref[hh, r] = jnp.concatenate(
                [vvt[hh * HALF:(hh + 1) * HALF], ones_rows], axis=0).astype(BF16)

    def scores(slot, half, masked):
        ks = pl.ds(pl.multiple_of(half * hk, hk), hk)
        for hh in range(2):
            st = _dot_nt(kop_ref[hh, ks, :], qop_ref[hh])
            if masked:
                key = lax.broadcasted_iota(jnp.int32, (hk, t), 0) + slot * hk
                qry = lax.broadcasted_iota(jnp.int32, (hk, t), 1)
                st = jnp.where(key <= qry, st, NEG)
            s_ref[slot, hh] = st
            cm_ref[slot, hh] = jnp.max(st, axis=0, keepdims=True)

    def soft(slot):
        for hh in range(2):
            m_old = m_ref[hh]
            m_new = jnp.maximum(m_old, cm_ref[slot, hh])
            al_ref[slot, hh] = jnp.exp2(m_old - m_new)
            p_ref[slot, hh] = jnp.exp2(s_ref[slot, hh] - m_new).astype(BF16)
            m_ref[hh] = m_new

    def pv(slot, half):
        for hh in range(2):
            acc_ref[hh] = acc_ref[hh] * al_ref[slot, hh] + jnp.dot(
                vt_ref[hh, half], p_ref[slot, hh], preferred_element_type=F32)

    def step(i, prev, has_next):
        pv(0, 2 * prev)
        soft(0)
        scores(1, 2 * i + 1, False)
        if has_next:
            scores(0, 2 * i + 2, False)
        pv(1, 2 * prev + 1)
        soft(1)

    lane_q = lax.broadcasted_iota(jnp.int32, (t, LANES), 1)
    low_q = lane_q < HALF
    ones0 = jnp.where((lane_q >= HALF) & (lane_q < HALF + N_PIECES), 1.0, 0.0)
    ones1 = jnp.where(lane_q < N_PIECES, 1.0, 0.0)

    for qi in range(seq // t):
        build(2 * qi)
        build(2 * qi + 1)
        rows = slice(qi * t, (qi + 1) * t)
        q = q_ref[rows, :].astype(F32)
        qop_ref[0] = jnp.where(low_q, q, ones0).astype(BF16)
        qop_ref[1] = jnp.where(low_q, ones1, q).astype(BF16)
        m_ref[...] = jnp.full(m_ref.shape, NEG, F32)
        acc_ref[...] = jnp.zeros(acc_ref.shape, F32)

        scores(0, 2 * qi, True)
        scores(1, 2 * qi + 1, True)
        soft(0)
        if qi > 0:
            scores(0, 0, False)
        soft(1)

        loop_pairs = max(qi - 1, 0) // 2

        def pair(j, carry, qi=qi):
            first = 2 * j
            step(first, jnp.where(j == 0, qi, first - 1), True)
            step(first + 1, first, True)
            return carry

        if loop_pairs:
            lax.fori_loop(0, loop_pairs, pair, 0)
        for i in range(2 * loop_pairs, qi):
            step(i, i - 1 if i > 0 else qi, i + 1 < qi)
        last = qi - 1 if qi > 0 else qi
        pv(0, 2 * last)
        pv(1, 2 * last + 1)

        ot = jnp.concatenate(
            [acc_ref[hh, 0:HALF, :] / acc_ref[hh, HALF:HALF + 1, :] for hh in range(2)], axis=0)
        o_ref[rows, :] = ot.T.astype(BF16)


def _fox(proj, aug, batch, seq):
    t = FOX_T
    nq = seq // t
    pairs = B_HEADS // 2
    qcol = _col_block("qB", LANES)
    kcol = _col_block("kB", LANES)
    vcol = _col_block("vB", LANES)
    return pl.pallas_call(
        _fox_kernel,
        grid=(batch, pairs),
        in_specs=[
            pl.BlockSpec((seq, LANES), lambda b, p: (b, qcol + p)),
            pl.BlockSpec((seq, LANES), lambda b, p: (b, kcol + p)),
            pl.BlockSpec((seq, LANES), lambda b, p: (b, vcol + p)),
            pl.BlockSpec((seq, LANES), lambda b, p: (b, p)),
        ],
        out_specs=pl.BlockSpec((seq, LANES), lambda b, p: (b, p)),
        out_shape=jax.ShapeDtypeStruct((batch * seq, B_WIDTH), BF16),
        scratch_shapes=[
            pltpu.VMEM((2, seq, LANES), BF16),
            pltpu.VMEM((2, seq // FOX_H, VT_ROWS, FOX_H), BF16),
            pltpu.VMEM((2, t, LANES), BF16),
            pltpu.VMEM((2, 2, FOX_H, t), F32),
            pltpu.VMEM((2, 2, 1, t), F32),
            pltpu.VMEM((2, 2, FOX_H, t), BF16),
            pltpu.VMEM((2, 2, 1, t), F32),
            pltpu.VMEM((2, 1, t), F32),
            pltpu.VMEM((2, VT_ROWS, t), F32),
        ],
        compiler_params=pltpu.CompilerParams(
            dimension_semantics=("arbitrary", "arbitrary"),
            vmem_limit_bytes=VMEM_LIMIT),
        name="fox",
    )(proj, proj, proj, aug)


MERGE_TM = 256


Z_BLOCK = 256
Z_PARTS = A_WIDTH // Z_BLOCK
assert A_WIDTH == B_WIDTH and A_WIDTH % Z_BLOCK == 0


def _merge_kernel(x_ref, ya_ref, yb_ref, mk_ref, mv_ref, qc_ref, *refs):
    za_refs = refs[:Z_PARTS]
    zb_refs = refs[Z_PARTS:2 * Z_PARTS]
    zc_ref, g0_ref, g1_ref, g2_ref, wa_ref, wb_ref, wc_ref, wo_ref, o_ref = refs[2 * Z_PARTS:]

    def memory_attention():
        outs = []
        for h in range(C_HEADS):
            sl = slice(h * C_HEAD_DIM, (h + 1) * C_HEAD_DIM)
            s = _dot_nt(qc_ref[:, sl], mk_ref[0, :, sl])
            p = jnp.exp(s - jnp.max(s, axis=-1, keepdims=True))
            o = jnp.dot(p.astype(BF16), mv_ref[0, :, sl], preferred_element_type=F32)
            outs.append(o / jnp.sum(p, axis=-1, keepdims=True))
        return jnp.concatenate(outs, axis=1)

    def branch(y, z_refs, w_ref):
        z = jnp.concatenate([r[...] for r in z_refs], axis=1).astype(F32)
        h = (y * (z * _sigmoid(z))).astype(BF16)
        return jnp.dot(h, w_ref[...], preferred_element_type=F32)

    y = _sigmoid(g0_ref[...].astype(F32)) * branch(ya_ref[...].astype(F32), za_refs, wa_ref)
    y = y + _sigmoid(g1_ref[...].astype(F32)) * branch(yb_ref[...].astype(F32), zb_refs, wb_ref)
    y = y + _sigmoid(g2_ref[...].astype(F32)) * branch(memory_attention(), (zc_ref,), wc_ref)
    o_ref[...] = x_ref[...] + jnp.dot(y.astype(BF16), wo_ref[...], preferred_element_type=F32)


def _merge(x2, ya, yb, mk, mv, proj, wa, wb, wc, wo, seq):
    t, d = x2.shape
    tm = MERGE_TM
    tiles_per_seq = seq // tm
    gcol = _col_block("g", D_MODEL)
    row = lambda width, col: pl.BlockSpec((tm, width), lambda i: (i, col))
    full = lambda a: pl.BlockSpec(a.shape, lambda i: (0, 0), pipeline_mode=pl.Buffered(1))
    mem = lambda a: pl.BlockSpec((1,) + a.shape[1:], lambda i: (i // tiles_per_seq, 0, 0))
    z_specs = lambda name: [row(Z_BLOCK, _col_block(name, Z_BLOCK, part)) for part in range(Z_PARTS)]
    n_proj = 2 * Z_PARTS + 5
    return pl.pallas_call(
        _merge_kernel,
        grid=(t // tm,),
        in_specs=[
            row(d, 0), row(A_WIDTH, 0), row(B_WIDTH, 0), mem(mk), mem(mv),
            row(C_WIDTH, _col_block("qC", C_WIDTH)),
            *z_specs("zA"), *z_specs("zB"), row(C_WIDTH, _col_block("zC", C_WIDTH)),
            row(d, gcol), row(d, gcol + 1), row(d, gcol + 2),
            full(wa), full(wb), full(wc), full(wo),
        ],
        out_specs=pl.BlockSpec((tm, d), lambda i: (i, 0)),
        out_shape=jax.ShapeDtypeStruct((t, d), F32),
        compiler_params=pltpu.CompilerParams(
            dimension_semantics=("arbitrary",), vmem_limit_bytes=VMEM_LIMIT),
        name="merge",
    )(x2, ya, yb, mk, mv, *([proj] * n_proj), wa, wb, wc, wo)


def _group_mean_matrices():
    lane = np.arange(2 * LANES)
    chunk = lane // LANES
    mats = []
    for tile in NORM_TILES:
        for c in range(0, IN_CHUNKS, 2):
            dims = np.asarray([_NORM_KIND.get(_chunk_piece(tile * IN_TN + (c + k) * LANES), LANES)
                               for k in range(2)])[chunk]
            same = ((lane[:, None] // dims[:, None] == lane[None, :] // dims[None, :])
                    & (chunk[:, None] == chunk[None, :]))
            mats.append(same.astype(np.float32) / dims[None, :])
    return np.stack(mats)


def _column_scales(gains):
    scale = {"qA": HEAD_DIM ** -0.5 * LOG2E, "qB": HEAD_DIM ** -0.5 * LOG2E,
             "qC": C_HEAD_DIM ** -0.5}
    cols = []
    for tile in NORM_TILES:
        for c in range(IN_CHUNKS):
            piece = _chunk_piece(tile * IN_TN + c * LANES)
            if piece in _NORM_KIND:
                g = gains[piece].astype(F32) * scale.get(piece, 1.0)
                cols.append(jnp.tile(g, LANES // g.shape[0]))
            else:
                cols.append(jnp.zeros((LANES,), F32))
    return jnp.concatenate(cols).reshape(1, len(NORM_TILES) * IN_TN)


def _layer(x, mem, norm_gain, mem_norm_gain, w_rows, b_forget, q_gain_a, k_gain_a, sinks_a,
           q_gain_b, k_gain_b, q_gain_c, k_gain_c, w_mem_kv, w_branch_a, w_branch_b,
           w_branch_c, w_out):
    batch, seq, d = x.shape
    x2 = x.reshape(batch * seq, d)

    w_f = w_rows[F_START * K_CHUNKS:(F_START + F_SHIFT) * K_CHUNKS].reshape(F_SHIFT, d)
    w_f = jnp.pad(w_f, ((0, LANES - F_SHIFT), (0, 0))).astype(BF16)
    b_f = jnp.pad(b_forget.astype(F32), (0, LANES - F_SHIFT)).reshape(1, LANES)
    cscale = _column_scales({"qA": q_gain_a, "kA": k_gain_a, "qB": q_gain_b, "kB": k_gain_b,
                             "qC": q_gain_c})
    gmat = jnp.asarray(_group_mean_matrices(), BF16)
    place = jnp.asarray(_placement_matrix(), BF16)

    mk, mv = _mem_kv(mem, mem_norm_gain.reshape(1, d), w_mem_kv, k_gain_c.reshape(1, C_HEAD_DIM))
    hn, aug = _norm_x(x2, norm_gain.reshape(1, d), w_f, b_f, place, seq)
    proj = _in_proj(hn, w_rows, _w_tile0(w_rows), cscale, gmat)
    ya, merge_weights = _swa(sinks_a.astype(F32), jnp.asarray(_swa_key_bias(), F32), proj,
                             (w_branch_a, w_branch_b, w_branch_c, w_out), batch, seq)
    yb = _fox(proj, aug, batch, seq)
    out = _merge(x2, ya, yb, mk, mv, proj, *merge_weights, seq)
    return out.reshape(batch, seq, d)


def kernel(x, mem, norm_gain, mem_norm_gain, w_in, b_forget, q_gain_a, k_gain_a, sinks_a,
           q_gain_b, k_gain_b, q_gain_c, k_gain_c, w_mem_kv, w_branch_a, w_branch_b,
           w_branch_c, w_out):
    depth = norm_gain.shape[0]
    w_rows = jnp.swapaxes(w_in, 1, 2).reshape(depth, -1, LANES)
    for layer in range(depth):
        x = _layer(x, mem, norm_gain[layer], mem_norm_gain[layer], w_rows[layer], b_forget[layer],
                   q_gain_a[layer], k_gain_a[layer], sinks_a[layer], q_gain_b[layer],
                   k_gain_b[layer], q_gain_c[layer], k_gain_c[layer], w_mem_kv[layer],
                   w_branch_a[layer], w_branch_b[layer], w_branch_c[layer], w_out[layer])
    return x
```

```python
import functools

import jax
import jax.numpy as jnp
import numpy as np
from jax import lax
from jax.experimental import pallas as pl
from jax.experimental.pallas import tpu as pltpu

F32 = jnp.float32
BF16 = jnp.bfloat16

D_MODEL = 2048
HEAD_DIM = 64
A_Q_HEADS = 12
A_KV_HEADS = 4
A_GROUP = A_Q_HEADS // A_KV_HEADS
WINDOW = 128
B_HEADS = 12
C_HEADS = 4
C_HEAD_DIM = 128
A_WIDTH = A_Q_HEADS * HEAD_DIM
A_KV_WIDTH = A_KV_HEADS * HEAD_DIM
B_WIDTH = B_HEADS * HEAD_DIM
C_WIDTH = C_HEADS * C_HEAD_DIM
EPS = 1e-6
NEG = -1e30

LANES = 128
HALF = LANES // 2
VT_ROWS = HALF + 16
LOG2E = float(np.log2(np.e))
N_PIECES = 3
VMEM_LIMIT = 56 * 1024 * 1024

_SRC = {}
_off = 0
for _name, _w in (("qA", A_WIDTH), ("kA", A_KV_WIDTH), ("vA", A_KV_WIDTH), ("zA", A_WIDTH),
                  ("qB", B_WIDTH), ("kB", B_WIDTH), ("vB", B_WIDTH), ("zB", B_WIDTH),
                  ("fB", B_HEADS), ("qC", C_WIDTH), ("zC", C_WIDTH), ("g", 3 * D_MODEL)):
    _SRC[_name] = (_off, _w)
    _off += _w

F_START, F_SHIFT = _SRC["fB"]
_DST = {n: (o if o < F_START else o - F_SHIFT) for n, (o, _) in _SRC.items() if n != "fB"}
PROJ_WIDTH = _SRC["g"][0] + _SRC["g"][1] - F_SHIFT
assert all(o % LANES == 0 for o in _DST.values()) and F_START % LANES == 0


def _col_block(name, width, part=0):
    assert _DST[name] % width == 0
    return _DST[name] // width + part


def _rms(x, gain):
    ms = jnp.mean(x * x, axis=-1, keepdims=True)
    return x * lax.rsqrt(ms + EPS) * gain


def _sigmoid(t):
    return 0.5 * (jnp.tanh(0.5 * t) + 1.0)


def _dot_nt(a, b):
    return lax.dot_general(a, b, (((1,), (1,)), ((), ())), preferred_element_type=F32)


def _mem_kv_kernel(mem_ref, gain_ref, w_ref, kgain_ref, mk_ref, mv_ref):
    hn = _rms(mem_ref[0], gain_ref[...]).astype(BF16)
    kv = jnp.dot(hn, w_ref[...].astype(BF16), preferred_element_type=F32)
    for h in range(C_HEADS):
        sl = slice(h * C_HEAD_DIM, (h + 1) * C_HEAD_DIM)
        mk_ref[0, :, sl] = _rms(kv[:, sl], kgain_ref[...]).astype(BF16)
    mv_ref[0] = kv[:, C_WIDTH:].astype(BF16)


def _mem_kv(mem, gain, w, kgain):
    b, m, d = mem.shape
    return pl.pallas_call(
        _mem_kv_kernel,
        grid=(b,),
        in_specs=[
            pl.BlockSpec((1, m, d), lambda i: (i, 0, 0)),
            pl.BlockSpec((1, d), lambda i: (0, 0)),
            pl.BlockSpec((d, 2 * C_WIDTH), lambda i: (0, 0)),
            pl.BlockSpec((1, C_HEAD_DIM), lambda i: (0, 0)),
        ],
        out_specs=[
            pl.BlockSpec((1, m, C_WIDTH), lambda i: (i, 0, 0)),
            pl.BlockSpec((1, m, C_WIDTH), lambda i: (i, 0, 0)),
        ],
        out_shape=[jax.ShapeDtypeStruct((b, m, C_WIDTH), BF16)] * 2,
        compiler_params=pltpu.CompilerParams(
            dimension_semantics=("arbitrary",), vmem_limit_bytes=VMEM_LIMIT),
        name="mem_kv",
    )(mem, gain, w, kgain)


NX_TM = 1024


def _norm_x_kernel(tiles_per_seq, x_ref, gain_ref, wf_ref, bf_ref, place_ref, hn_ref, aug_ref,
                   carry_ref):
    i = pl.program_id(0)

    @pl.when(i % tiles_per_seq == 0)
    def _():
        carry_ref[...] = jnp.zeros_like(carry_ref)

    hn = _rms(x_ref[...], gain_ref[...]).astype(BF16)
    hn_ref[...] = hn
    f_logit = _dot_nt(hn, wf_ref[...]) + bf_ref[...]
    x = jnp.minimum(f_logit, 0.0) - jnp.log1p(jnp.exp(-jnp.abs(f_logit)))
    row = lax.broadcasted_iota(jnp.int32, x.shape, 0)
    shift = 1
    while shift < NX_TM:
        x = x + jnp.where(row >= shift, pltpu.roll(x, shift, 0), 0.0)
        shift *= 2
    c = x + carry_ref[...]
    carry_ref[...] = c[NX_TM - 1:NX_TM, :]
    rest = c * (-LOG2E)
    pieces = []
    for _ in range(N_PIECES):
        p = rest.astype(BF16)
        pieces.append(p)
        rest = rest - p.astype(F32)
    stacked = jnp.concatenate(pieces, axis=1)
    aug_ref[...] = jnp.dot(stacked, place_ref[...], preferred_element_type=F32).astype(BF16)


def _norm_x(x2, gain, wf, bfg, place, seq):
    t, d = x2.shape
    return pl.pallas_call(
        functools.partial(_norm_x_kernel, seq // NX_TM),
        grid=(t // NX_TM,),
        in_specs=[
            pl.BlockSpec((NX_TM, d), lambda i: (i, 0)),
            pl.BlockSpec((1, d), lambda i: (0, 0)),
            pl.BlockSpec((LANES, d), lambda i: (0, 0)),
            pl.BlockSpec((1, LANES), lambda i: (0, 0)),
            pl.BlockSpec((N_PIECES * LANES, B_WIDTH), lambda i: (0, 0)),
        ],
        out_specs=[
            pl.BlockSpec((NX_TM, d), lambda i: (i, 0)),
            pl.BlockSpec((NX_TM, B_WIDTH), lambda i: (i, 0)),
        ],
        out_shape=[jax.ShapeDtypeStruct((t, d), BF16),
                   jax.ShapeDtypeStruct((t, B_WIDTH), BF16)],
        scratch_shapes=[pltpu.VMEM((1, LANES), F32)],
        compiler_params=pltpu.CompilerParams(
            dimension_semantics=("arbitrary",), vmem_limit_bytes=VMEM_LIMIT),
        name="norm_x",
    )(x2, gain, wf, bfg, place)


IN_TM = 1024
IN_TN = 1024
IN_CHUNKS = IN_TN // LANES
K_CHUNKS = D_MODEL // LANES
N_IN_TILES = PROJ_WIDTH // IN_TN
N_ALIGNED_TILES = F_START // IN_TN
assert F_START % IN_TN == 0 and PROJ_WIDTH % IN_TN == 0 and F_SHIFT < LANES

_NORM_KIND = {"qA": HEAD_DIM, "kA": HEAD_DIM, "qB": HEAD_DIM, "kB": HEAD_DIM, "qC": C_HEAD_DIM}


def _chunk_piece(col):
    for name, off in _DST.items():
        if off <= col < off + _SRC[name][1]:
            return name
    raise ValueError(col)


def _norm_prefix(tile):
    kinds = [_chunk_piece(tile * IN_TN + c * LANES) in _NORM_KIND for c in range(IN_CHUNKS)]
    n = sum(kinds)
    assert kinds == [True] * n + [False] * (IN_CHUNKS - n)
    return n


NORM_PREFIX = tuple(_norm_prefix(j) for j in range(N_IN_TILES))
NORM_TILES = tuple(j for j in range(N_IN_TILES) if NORM_PREFIX[j])


def _norm_slot(j):
    slot = 0
    for tile in NORM_TILES[1:]:
        slot = slot + (j >= tile).astype(jnp.int32)
    return slot


IN_MCH = 512


def _gather_weight_rows(w_ref, dst_ref, src_row, dst_rows, n_rows):
    for c in range(K_CHUNKS):
        dst_ref[dst_rows, c * LANES:(c + 1) * LANES] = (
            w_ref[pl.ds(src_row + c, n_rows, stride=K_CHUNKS), :].astype(BF16))


def _w_tile0_kernel(w_ref, o_ref):
    _gather_weight_rows(w_ref, o_ref, 0, slice(None), IN_TN)


def _w_index(tile):
    col = tile * IN_TN + jnp.where(tile >= N_ALIGNED_TILES, F_SHIFT, 0)
    return (col * K_CHUNKS, 0)


def _w_tile0(w_rows):
    return pl.pallas_call(
        _w_tile0_kernel,
        grid=(1,),
        in_specs=[pl.BlockSpec((pl.Element(IN_TN * K_CHUNKS), pl.Element(LANES)),
                               lambda i: _w_index(i))],
        out_specs=pl.BlockSpec((IN_TN, D_MODEL), lambda i: (0, 0)),
        out_shape=jax.ShapeDtypeStruct((IN_TN, D_MODEL), BF16),
        compiler_params=pltpu.CompilerParams(
            dimension_semantics=("arbitrary",), vmem_limit_bytes=VMEM_LIMIT),
        name="w_tile0",
    )(w_rows)


def _in_proj_kernel(n_row_tiles, hn_ref, w_ref, w0_ref, cscale_ref, gmat_ref, o_ref,
                    wbf_a, wbf_b, acc_ref):
    j = pl.program_id(0)
    i = pl.program_id(1)
    share = IN_TN // n_row_tiles

    @pl.when((j == 0) & (i == 0))
    def _():
        wbf_a[...] = w0_ref[...]

    def epilogue(chunk, prefix):
        rows = slice(chunk * IN_MCH, (chunk + 1) * IN_MCH)
        acc = acc_ref.at[chunk % 2]
        for c in range(prefix // 2):
            sl = slice(c * 2 * LANES, (c + 1) * 2 * LANES)
            a = acc[:, sl]
            ms = jnp.dot((a * a).astype(BF16), gmat_ref[c], preferred_element_type=F32)
            o_ref[rows, sl] = (a * lax.rsqrt(ms + EPS) * cscale_ref[:, sl]).astype(BF16)
        if prefix < IN_CHUNKS:
            rest = slice(prefix * LANES, IN_TN)
            o_ref[rows, rest] = acc[:, rest].astype(BF16)

    def step(cur, nxt, prefix):
        src = pl.multiple_of(i * (share * K_CHUNKS), share * K_CHUNKS)
        dst = pl.ds(pl.multiple_of(i * share, share), share)
        n_chunks = hn_ref.shape[0] // IN_MCH
        for c in range(n_chunks + 1):
            if c < n_chunks:
                acc_ref[c % 2] = _dot_nt(hn_ref[c * IN_MCH:(c + 1) * IN_MCH, :], cur[...])
            if c == 0:
                _gather_weight_rows(w_ref, nxt, src, dst, share)
            if c >= 1:
                epilogue(c - 1, prefix)

    def is_any(tiles):
        cond = j == tiles[0]
        for tile in tiles[1:]:
            cond = cond | (j == tile)
        return cond

    for parity, prefix in sorted({(t % 2, NORM_PREFIX[t]) for t in range(N_IN_TILES)}):
        tiles = [t for t in range(N_IN_TILES) if (t % 2, NORM_PREFIX[t]) == (parity, prefix)]
        cur, nxt = (wbf_a, wbf_b) if parity == 0 else (wbf_b, wbf_a)
        pl.when(is_any(tiles))(functools.partial(step, cur, nxt, prefix))


def _in_proj(hn, w_rows, w_first, cscale, gmat):
    t, d = hn.shape
    n_row_tiles = t // IN_TM
    assert IN_TN % n_row_tiles == 0 and (IN_TN // n_row_tiles) % 16 == 0
    return pl.pallas_call(
        functools.partial(_in_proj_kernel, n_row_tiles),
        grid=(N_IN_TILES, n_row_tiles),
        in_specs=[
            pl.BlockSpec((IN_TM, d), lambda j, i: (i, 0)),
            pl.BlockSpec((pl.Element(IN_TN * K_CHUNKS), pl.Element(LANES)),
                         lambda j, i: _w_index(jnp.minimum(j + 1, N_IN_TILES - 1))),
            pl.BlockSpec((IN_TN, d), lambda j, i: (0, 0), pipeline_mode=pl.Buffered(1)),
            pl.BlockSpec((1, IN_TN), lambda j, i: (0, _norm_slot(j))),
            pl.BlockSpec((IN_CHUNKS // 2, 2 * LANES, 2 * LANES),
                         lambda j, i: (_norm_slot(j), 0, 0)),
        ],
        out_specs=pl.BlockSpec((IN_TM, IN_TN), lambda j, i: (i, j)),
        out_shape=jax.ShapeDtypeStruct((t, PROJ_WIDTH), BF16),
        scratch_shapes=[
            pltpu.VMEM((IN_TN, d), BF16),
            pltpu.VMEM((IN_TN, d), BF16),
            pltpu.VMEM((2, IN_MCH, IN_TN), F32),
        ],
        compiler_params=pltpu.CompilerParams(
            dimension_semantics=("arbitrary", "arbitrary"), vmem_limit_bytes=VMEM_LIMIT),
        name="in_proj",
    )(hn, w_rows, w_first, cscale, gmat)


def _aug_lane(head):
    return HALF if head % 2 == 0 else 0


def _placement_matrix():
    place = np.zeros((N_PIECES * LANES, B_WIDTH), np.float32)
    for h in range(B_HEADS):
        for p in range(N_PIECES):
            place[p * LANES + h, (h // 2) * LANES + _aug_lane(h) + p] = 1.0
    return place


SWA_QB = 4
SWA_WIN = (SWA_QB + 1) * WINDOW


def _alibi_slopes_log2():
    return [float(2.0 ** (-8.0 * (h + 1) / A_Q_HEADS)) * LOG2E for h in range(A_Q_HEADS)]


def _swa_key_bias():
    key = np.arange(2 * WINDOW)[:, None]
    qry = np.arange(WINDOW)[None, :]
    rel = qry + WINDOW - key
    visible = (rel >= 0) & (rel < WINDOW)
    slopes = np.asarray(_alibi_slopes_log2(), np.float32)[:, None, None]
    return np.where(visible[None], slopes * key[None].astype(np.float32), np.float32(NEG))


def _swa_kernel(n_cast, sinks_ref, bias_ref, q_ref, kp_ref, kc_ref, vp_ref, vc_ref, *refs):
    cast_in, o_ref, cast_out = refs[:n_cast], refs[n_cast], refs[n_cast + 1:2 * n_cast + 1]
    kop_ref, vt_ref, s_ref, p_ref, sh_ref, ot_ref = refs[2 * n_cast + 1:]
    for src, dst in zip(cast_in, cast_out):
        dst[...] = src[...].astype(BF16)
    step = pl.program_id(1)
    blk = WINDOW
    lane = lax.broadcasted_iota(jnp.int32, (SWA_WIN, LANES), 1)

    kwin = jnp.concatenate([kp_ref[...], kc_ref[...]], axis=0).astype(F32)
    vwin_t = jnp.concatenate([vp_ref[...], vc_ref[...]], axis=0).astype(F32).T
    ones_rows = jnp.where(
        lax.broadcasted_iota(jnp.int32, (VT_ROWS - HALF, SWA_WIN), 0) == 0, 1.0, 0.0)
    for g in range(A_KV_HEADS):
        slab = kwin[:, (g // 2) * LANES:(g // 2 + 1) * LANES]
        own = (lane < HALF) if g % 2 == 0 else (lane >= HALF)
        kz = jnp.where(own, slab, 0.0)
        kop_ref[g, 0] = kz.astype(BF16)
        kop_ref[g, 1] = pltpu.roll(kz, HALF, 1).astype(BF16)
        vt_ref[g] = jnp.concatenate(
            [vwin_t[g * HEAD_DIM:(g + 1) * HEAD_DIM], ones_rows], axis=0).astype(BF16)

    t_win = lax.broadcasted_iota(jnp.int32, (1, blk), 1).astype(F32) + float(blk)
    slopes = _alibi_slopes_log2()
    sinks = [sinks_ref[h] * LOG2E + slopes[h] * t_win for h in range(A_Q_HEADS)]
    key_row = lax.broadcasted_iota(jnp.int32, (2 * blk, blk), 0)

    items = [(i, g) for i in range(SWA_QB) for g in range(A_KV_HEADS)]
    keys_of = lambda i: slice(i * blk, (i + 2) * blk)

    def stage_scores(n):
        i, g = items[n]
        for k in range(A_GROUP):
            h = g * A_GROUP + k
            q_slab = q_ref[i * blk:(i + 1) * blk, (h // 2) * LANES:(h // 2 + 1) * LANES]
            s_ref[n % 2, k] = _dot_nt(kop_ref[g, (h + g) % 2, keys_of(i), :], q_slab)

    def stage_softmax(n):
        i, g = items[n]
        for k in range(A_GROUP):
            h = g * A_GROUP + k
            st = s_ref[n % 2, k] + bias_ref[h]
            if i == 0:
                st = jnp.where(((step * SWA_QB - 1) * blk + key_row) >= 0, st, NEG)
            m = jnp.maximum(jnp.max(st, axis=0, keepdims=True), sinks[h])
            p_ref[n % 2, k] = jnp.exp2(st - m).astype(BF16)
            sh_ref[n % 2, k] = jnp.exp2(sinks[h] - m)

    def stage_values(n):
        i, g = items[n]
        for k in range(A_GROUP):
            h = g * A_GROUP + k
            ot = jnp.dot(vt_ref[g, :, keys_of(i)], p_ref[n % 2, k],
                         preferred_element_type=F32)
            ot_ref[h * HEAD_DIM:(h + 1) * HEAD_DIM, :] = (
                ot[0:HEAD_DIM] / (ot[HEAD_DIM:HEAD_DIM + 1] + sh_ref[n % 2, k]))
        if g == A_KV_HEADS - 1:
            o_ref[i * blk:(i + 1) * blk, :] = ot_ref[...].T.astype(BF16)

    for n in range(len(items) + 2):
        if n < len(items):
            stage_scores(n)
        if 2 <= n:
            stage_values(n - 2)
        if 1 <= n <= len(items):
            stage_softmax(n - 1)


def _swa(sinks, bias, proj, cast_weights, batch, seq):
    blk = WINDOW
    nb = seq // blk
    steps = nb // SWA_QB
    n_steps = batch * steps
    qcol = _col_block("qA", A_WIDTH)
    kcol = _col_block("kA", A_KV_WIDTH)
    vcol = _col_block("vA", A_KV_WIDTH)
    cur = lambda b, s: b * steps + s
    prev = lambda b, s: b * nb + jnp.maximum(s * SWA_QB - 1, 0)
    slab_specs = []
    for w in cast_weights:
        assert w.shape[0] % (16 * n_steps) == 0
        slab_specs.append(pl.BlockSpec((w.shape[0] // n_steps, w.shape[1]),
                                       lambda b, s: (cur(b, s), 0)))
    outs = pl.pallas_call(
        functools.partial(_swa_kernel, len(cast_weights)),
        grid=(batch, steps),
        in_specs=[
            pl.BlockSpec(memory_space=pltpu.SMEM),
            pl.BlockSpec(bias.shape, lambda b, s: (0, 0, 0)),
            pl.BlockSpec((SWA_QB * blk, A_WIDTH), lambda b, s: (cur(b, s), qcol)),
            pl.BlockSpec((blk, A_KV_WIDTH), lambda b, s: (prev(b, s), kcol)),
            pl.BlockSpec((SWA_QB * blk, A_KV_WIDTH), lambda b, s: (cur(b, s), kcol)),
            pl.BlockSpec((blk, A_KV_WIDTH), lambda b, s: (prev(b, s), vcol)),
            pl.BlockSpec((SWA_QB * blk, A_KV_WIDTH), lambda b, s: (cur(b, s), vcol)),
            *slab_specs,
        ],
        out_specs=[pl.BlockSpec((SWA_QB * blk, A_WIDTH), lambda b, s: (cur(b, s), 0)),
                   *slab_specs],
        out_shape=[jax.ShapeDtypeStruct((batch * seq, A_WIDTH), BF16),
                   *[jax.ShapeDtypeStruct(w.shape, BF16) for w in cast_weights]],
        scratch_shapes=[
            pltpu.VMEM((A_KV_HEADS, 2, SWA_WIN, LANES), BF16),
            pltpu.VMEM((A_KV_HEADS, VT_ROWS, SWA_WIN), BF16),
            pltpu.VMEM((2, A_GROUP, 2 * blk, blk), F32),
            pltpu.VMEM((2, A_GROUP, 2 * blk, blk), BF16),
            pltpu.VMEM((2, A_GROUP, 1, blk), F32),
            pltpu.VMEM((A_WIDTH, blk), F32),
        ],
        compiler_params=pltpu.CompilerParams(
            dimension_semantics=("arbitrary", "arbitrary"), vmem_limit_bytes=VMEM_LIMIT),
        name="swa",
    )(sinks, bias, proj, proj, proj, proj, proj, *cast_weights)
    return outs[0], outs[1:]


FOX_T = 512


FOX_H = FOX_T // 2


def _fox_kernel(q_ref, k_ref, v_ref, aug_ref, o_ref,
                kop_ref, vt_ref, qop_ref, s_ref, cm_ref, p_ref, al_ref, m_ref, acc_ref):
    seq = k_ref.shape[0]
    t = FOX_T
    hk = FOX_H

    lane = lax.broadcasted_iota(jnp.int32, (hk, LANES), 1)
    low = lane < HALF

    def build(r):
        sl = slice(r * hk, (r + 1) * hk)
        kk = k_ref[sl, :].astype(F32)
        aa = aug_ref[sl, :].astype(F32)
        vv = v_ref[sl, :].astype(F32)
        kop_ref[0, sl, :] = jnp.where(low, kk, aa).astype(BF16)
        kop_ref[1, sl, :] = jnp.where(low, aa, kk).astype(BF16)
        vvt = vv.T
        ones_rows = jnp.where(
            lax.broadcasted_iota(jnp.int32, (VT_ROWS - HALF, hk), 0) == 0, 1.0, 0.0)
        for hh in range(2):
            vt_ref[hh, r] = jnp.concatenate(
                [vvt[hh * HALF:(hh + 1) * HALF], ones_rows], axis=0).astype(BF16)

    def scores(slot, half, masked):
        ks = pl.ds(pl.multiple_of(half * hk, hk), hk)
        for hh in range(2):
            st = _dot_nt(kop_ref[hh, ks, :], qop_ref[hh])
            if masked:
                key = lax.broadcasted_iota(jnp.int32, (hk, t), 0) + slot * hk
                qry = lax.broadcasted_iota(jnp.int32, (hk, t), 1)
                st = jnp.where(key <= qry, st, NEG)
            s_ref[slot, hh] = st
            cm_ref[slot, hh] = jnp.max(st, axis=0, keepdims=True)

    def soft(slot):
        for hh in range(2):
            m_old = m_ref[hh]
            m_new = jnp.maximum(m_old, cm_ref[slot, hh])
            al_ref[slot, hh] = jnp.exp2(m_old - m_new)
            p_ref[slot, hh] = jnp.exp2(s_ref[slot, hh] - m_new).astype(BF16)
            m_ref[hh] = m_new

    def pv(slot, half):
        for hh in range(2):
            acc_ref[hh] = acc_ref[hh] * al_ref[slot, hh] + jnp.dot(
                vt_ref[hh, half], p_ref[slot, hh], preferred_element_type=F32)

    def step(i, prev, has_next):
        pv(0, 2 * prev)
        soft(0)
        scores(1, 2 * i + 1, False)
        if has_next:
            scores(0, 2 * i + 2, False)
        pv(1, 2 * prev + 1)
        soft(1)

    lane_q = lax.broadcasted_iota(jnp.int32, (t, LANES), 1)
    low_q = lane_q < HALF
    ones0 = jnp.where((lane_q >= HALF) & (lane_q < HALF + N_PIECES), 1.0, 0.0)
    ones1 = jnp.where(lane_q < N_PIECES, 1.0, 0.0)

    for qi in range(seq // t):
        build(2 * qi)
        build(2 * qi + 1)
        rows = slice(qi * t, (qi + 1) * t)
        q = q_ref[rows, :].astype(F32)
        qop_ref[0] = jnp.where(low_q, q, ones0).astype(BF16)
        qop_ref[1] = jnp.where(low_q, ones1, q).astype(BF16)
        m_ref[...] = jnp.full(m_ref.shape, NEG, F32)
        acc_ref[...] = jnp.zeros(acc_ref.shape, F32)

        scores(0, 2 * qi, True)
        scores(1, 2 * qi + 1, True)
        soft(0)
        if qi > 0:
            scores(0, 0, False)
        soft(1)

        loop_pairs = max(qi - 1, 0) // 2

        def pair(j, carry, qi=qi):
            first = 2 * j
            step(first, jnp.where(j == 0, qi, first - 1), True)
            step(first + 1, first, True)
            return carry

        if loop_pairs:
            lax.fori_loop(0, loop_pairs, pair, 0)
        for i in range(2 * loop_pairs, qi):
            step(i, i - 1 if i > 0 else qi, i + 1 < qi)
        last = qi - 1 if qi > 0 else qi
        pv(0, 2 * last)
        pv(1, 2 * last + 1)

        ot = jnp.concatenate(
            [acc_ref[hh, 0:HALF, :] / acc_ref[hh, HALF:HALF + 1, :] for hh in range(2)], axis=0)
        o_ref[rows, :] = ot.T.astype(BF16)


def _fox(proj, aug, batch, seq):
    t = FOX_T
    nq = seq // t
    pairs = B_HEADS // 2
    qcol = _col_block("qB", LANES)
    kcol = _col_block("kB", LANES)
    vcol = _col_block("vB", LANES)
    return pl.pallas_call(
        _fox_kernel,
        grid=(batch, pairs),
        in_specs=[
            pl.BlockSpec((seq, LANES), lambda b, p: (b, qcol + p)),
            pl.BlockSpec((seq, LANES), lambda b, p: (b, kcol + p)),
            pl.BlockSpec((seq, LANES), lambda b, p: (b, vcol + p)),
            pl.BlockSpec((seq, LANES), lambda b, p: (b, p)),
        ],
        out_specs=pl.BlockSpec((seq, LANES), lambda b, p: (b, p)),
        out_shape=jax.ShapeDtypeStruct((batch * seq, B_WIDTH), BF16),
        scratch_shapes=[
            pltpu.VMEM((2, seq, LANES), BF16),
            pltpu.VMEM((2, seq // FOX_H, VT_ROWS, FOX_H), BF16),
            pltpu.VMEM((2, t, LANES), BF16),
            pltpu.VMEM((2, 2, FOX_H, t), F32),
            pltpu.VMEM((2, 2, 1, t), F32),
            pltpu.VMEM((2, 2, FOX_H, t), BF16),
            pltpu.VMEM((2, 2, 1, t), F32),
            pltpu.VMEM((2, 1, t), F32),
            pltpu.VMEM((2, VT_ROWS, t), F32),
        ],
        compiler_params=pltpu.CompilerParams(
            dimension_semantics=("arbitrary", "arbitrary"),
            vmem_limit_bytes=VMEM_LIMIT),
        name="fox",
    )(proj, proj, proj, aug)


MERGE_TM = 256


Z_BLOCK = 256
Z_PARTS = A_WIDTH // Z_BLOCK
assert A_WIDTH == B_WIDTH and A_WIDTH % Z_BLOCK == 0


def _merge_kernel(x_ref, ya_ref, yb_ref, mk_ref, mv_ref, qc_ref, *refs):
    za_refs = refs[:Z_PARTS]
    zb_refs = refs[Z_PARTS:2 * Z_PARTS]
    zc_ref, g0_ref, g1_ref, g2_ref, wa_ref, wb_ref, wc_ref, wo_ref, o_ref = refs[2 * Z_PARTS:]

    def memory_attention():
        outs = []
        for h in range(C_HEADS):
            sl = slice(h * C_HEAD_DIM, (h + 1) * C_HEAD_DIM)
            s = _dot_nt(qc_ref[:, sl], mk_ref[0, :, sl])
            p = jnp.exp(s - jnp.max(s, axis=-1, keepdims=True))
            o = jnp.dot(p.astype(BF16), mv_ref[0, :, sl], preferred_element_type=F32)
            outs.append(o / jnp.sum(p, axis=-1, keepdims=True))
        return jnp.concatenate(outs, axis=1)

    def branch(y, z_refs, w_ref):
        z = jnp.concatenate([r[...] for r in z_refs], axis=1).astype(F32)
        h = (y * (z * _sigmoid(z))).astype(BF16)
        return jnp.dot(h, w_ref[...], preferred_element_type=F32)

    y = _sigmoid(g0_ref[...].astype(F32)) * branch(ya_ref[...].astype(F32), za_refs, wa_ref)
    y = y + _sigmoid(g1_ref[...].astype(F32)) * branch(yb_ref[...].astype(F32), zb_refs, wb_ref)
    y = y + _sigmoid(g2_ref[...].astype(F32)) * branch(memory_attention(), (zc_ref,), wc_ref)
    o_ref[...] = x_ref[...] + jnp.dot(y.astype(BF16), wo_ref[...], preferred_element_type=F32)


def _merge(x2, ya, yb, mk, mv, proj, wa, wb, wc, wo, seq):
    t, d = x2.shape
    tm = MERGE_TM
    tiles_per_seq = seq // tm
    gcol = _col_block("g", D_MODEL)
    row = lambda width, col: pl.BlockSpec((tm, width), lambda i: (i, col))
    full = lambda a: pl.BlockSpec(a.shape, lambda i: (0, 0), pipeline_mode=pl.Buffered(1))
    mem = lambda a: pl.BlockSpec((1,) + a.shape[1:], lambda i: (i // tiles_per_seq, 0, 0))
    z_specs = lambda name: [row(Z_BLOCK, _col_block(name, Z_BLOCK, part)) for part in range(Z_PARTS)]
    n_proj = 2 * Z_PARTS + 5
    return pl.pallas_call(
        _merge_kernel,
        grid=(t // tm,),
        in_specs=[
            row(d, 0), row(A_WIDTH, 0), row(B_WIDTH, 0), mem(mk), mem(mv),
            row(C_WIDTH, _col_block("qC", C_WIDTH)),
            *z_specs("zA"), *z_specs("zB"), row(C_WIDTH, _col_block("zC", C_WIDTH)),
            row(d, gcol), row(d, gcol + 1), row(d, gcol + 2),
            full(wa), full(wb), full(wc), full(wo),
        ],
        out_specs=pl.BlockSpec((tm, d), lambda i: (i, 0)),
        out_shape=jax.ShapeDtypeStruct((t, d), F32),
        compiler_params=pltpu.CompilerParams(
            dimension_semantics=("arbitrary",), vmem_limit_bytes=VMEM_LIMIT),
        name="merge",
    )(x2, ya, yb, mk, mv, *([proj] * n_proj), wa, wb, wc, wo)


def _group_mean_matrices():
    lane = np.arange(2 * LANES)
    chunk = lane // LANES
    mats = []
    for tile in NORM_TILES:
        for c in range(0, IN_CHUNKS, 2):
            dims = np.asarray([_NORM_KIND.get(_chunk_piece(tile * IN_TN + (c + k) * LANES), LANES)
                               for k in range(2)])[chunk]
            same = ((lane[:, None] // dims[:, None] == lane[None, :] // dims[None, :])
                    & (chunk[:, None] == chunk[None, :]))
            mats.append(same.astype(np.float32) / dims[None, :])
    return np.stack(mats)


def _column_scales(gains):
    scale = {"qA": HEAD_DIM ** -0.5 * LOG2E, "qB": HEAD_DIM ** -0.5 * LOG2E,
             "qC": C_HEAD_DIM ** -0.5}
    cols = []
    for tile in NORM_TILES:
        for c in range(IN_CHUNKS):
            piece = _chunk_piece(tile * IN_TN + c * LANES)
            if piece in _NORM_KIND:
                g = gains[piece].astype(F32) * scale.get(piece, 1.0)
                cols.append(jnp.tile(g, LANES // g.shape[0]))
            else:
                cols.append(jnp.zeros((LANES,), F32))
    return jnp.concatenate(cols).reshape(1, len(NORM_TILES) * IN_TN)


def _layer(x, mem, norm_gain, mem_norm_gain, w_rows, b_forget, q_gain_a, k_gain_a, sinks_a,
           q_gain_b, k_gain_b, q_gain_c, k_gain_c, w_mem_kv, w_branch_a, w_branch_b,
           w_branch_c, w_out):
    batch, seq, d = x.shape
    x2 = x.reshape(batch * seq, d)

    w_f = w_rows[F_START * K_CHUNKS:(F_START + F_SHIFT) * K_CHUNKS].reshape(F_SHIFT, d)
    w_f = jnp.pad(w_f, ((0, LANES - F_SHIFT), (0, 0))).astype(BF16)
    b_f = jnp.pad(b_forget.astype(F32), (0, LANES - F_SHIFT)).reshape(1, LANES)
    cscale = _column_scales({"qA": q_gain_a, "kA": k_gain_a, "qB": q_gain_b, "kB": k_gain_b,
                             "qC": q_gain_c})
    gmat = jnp.asarray(_group_mean_matrices(), BF16)
    place = jnp.asarray(_placement_matrix(), BF16)

    mk, mv = _mem_kv(mem, mem_norm_gain.reshape(1, d), w_mem_kv, k_gain_c.reshape(1, C_HEAD_DIM))
    hn, aug = _norm_x(x2, norm_gain.reshape(1, d), w_f, b_f, place, seq)
    proj = _in_proj(hn, w_rows, _w_tile0(w_rows), cscale, gmat)
    ya, merge_weights = _swa(sinks_a.astype(F32), jnp.asarray(_swa_key_bias(), F32), proj,
                             (w_branch_a, w_branch_b, w_branch_c, w_out), batch, seq)
    yb = _fox(proj, aug, batch, seq)
    out = _merge(x2, ya, yb, mk, mv, proj, *merge_weights, seq)
    return out.reshape(batch, seq, d)


def kernel(x, mem, norm_gain, mem_norm_gain, w_in, b_forget, q_gain_a, k_gain_a, sinks_a,
           q_gain_b, k_gain_b, q_gain_c, k_gain_c, w_mem_kv, w_branch_a, w_branch_b,
           w_branch_c, w_out):
    depth = norm_gain.shape[0]
    w_rows = jnp.swapaxes(w_in, 1, 2).reshape(depth, -1, LANES)
    for layer in range(depth):
        x = _layer(x, mem, norm_gain[layer], mem_norm_gain[layer], w_rows[layer], b_forget[layer],
                   q_gain_a[layer], k_gain_a[layer], sinks_a[layer], q_gain_b[layer],
                   k_gain_b[layer], q_gain_c[layer], k_gain_c[layer], w_mem_kv[layer],
                   w_branch_a[layer], w_branch_b[layer], w_branch_c[layer], w_out[layer])
    return x
```

```python
import functools

import jax
import jax.numpy as jnp
import numpy as np
from jax import lax
from jax.experimental import pallas as pl
from jax.experimental.pallas import tpu as pltpu

F32 = jnp.float32
BF16 = jnp.bfloat16

D_MODEL = 2048
HEAD_DIM = 64
A_Q_HEADS = 12
A_KV_HEADS = 4
A_GROUP = A_Q_HEADS // A_KV_HEADS
WINDOW = 128
B_HEADS = 12
C_HEADS = 4
C_HEAD_DIM = 128
A_WIDTH = A_Q_HEADS * HEAD_DIM
A_KV_WIDTH = A_KV_HEADS * HEAD_DIM
B_WIDTH = B_HEADS * HEAD_DIM
C_WIDTH = C_HEADS * C_HEAD_DIM
EPS = 1e-6
NEG = -1e30

LANES = 128
HALF = LANES // 2
VT_ROWS = HALF + 16
LOG2E = float(np.log2(np.e))
N_PIECES = 3
VMEM_LIMIT = 56 * 1024 * 1024

_SRC = {}
_off = 0
for _name, _w in (("qA", A_WIDTH), ("kA", A_KV_WIDTH), ("vA", A_KV_WIDTH), ("zA", A_WIDTH),
                  ("qB", B_WIDTH), ("kB", B_WIDTH), ("vB", B_WIDTH), ("zB", B_WIDTH),
                  ("fB", B_HEADS), ("qC", C_WIDTH), ("zC", C_WIDTH), ("g", 3 * D_MODEL)):
    _SRC[_name] = (_off, _w)
    _off += _w

F_START, F_SHIFT = _SRC["fB"]
_DST = {n: (o if o < F_START else o - F_SHIFT) for n, (o, _) in _SRC.items() if n != "fB"}
PROJ_WIDTH = _SRC["g"][0] + _SRC["g"][1] - F_SHIFT
assert all(o % LANES == 0 for o in _DST.values()) and F_START % LANES == 0


def _col_block(name, width, part=0):
    assert _DST[name] % width == 0
    return _DST[name] // width + part


def _rms(x, gain):
    ms = jnp.mean(x * x, axis=-1, keepdims=True)
    return x * lax.rsqrt(ms + EPS) * gain


def _sigmoid(t):
    return 0.5 * (jnp.tanh(0.5 * t) + 1.0)


def _dot_nt(a, b):
    return lax.dot_general(a, b, (((1,), (1,)), ((), ())), preferred_element_type=F32)


def _mem_kv_kernel(mem_ref, gain_ref, w_ref, kgain_ref, mk_ref, mv_ref):
    hn = _rms(mem_ref[0], gain_ref[...]).astype(BF16)
    kv = jnp.dot(hn, w_ref[...].astype(BF16), preferred_element_type=F32)
    for h in range(C_HEADS):
        sl = slice(h * C_HEAD_DIM, (h + 1) * C_HEAD_DIM)
        mk_ref[0, :, sl] = _rms(kv[:, sl], kgain_ref[...]).astype(BF16)
    mv_ref[0] = kv[:, C_WIDTH:].astype(BF16)


def _mem_kv(mem, gain, w, kgain):
    b, m, d = mem.shape
    return pl.pallas_call(
        _mem_kv_kernel,
        grid=(b,),
        in_specs=[
            pl.BlockSpec((1, m, d), lambda i: (i, 0, 0)),
            pl.BlockSpec((1, d), lambda i: (0, 0)),
            pl.BlockSpec((d, 2 * C_WIDTH), lambda i: (0, 0)),
            pl.BlockSpec((1, C_HEAD_DIM), lambda i: (0, 0)),
        ],
        out_specs=[
            pl.BlockSpec((1, m, C_WIDTH), lambda i: (i, 0, 0)),
            pl.BlockSpec((1, m, C_WIDTH), lambda i: (i, 0, 0)),
        ],
        out_shape=[jax.ShapeDtypeStruct((b, m, C_WIDTH), BF16)] * 2,
        compiler_params=pltpu.CompilerParams(
            dimension_semantics=("arbitrary",), vmem_limit_bytes=VMEM_LIMIT),
        name="mem_kv",
    )(mem, gain, w, kgain)


NX_TM = 1024


def _norm_x_kernel(tiles_per_seq, x_ref, gain_ref, wf_ref, bf_ref, place_ref, hn_ref, aug_ref,
                   carry_ref):
    i = pl.program_id(0)

    @pl.when(i % tiles_per_seq == 0)
    def _():
        carry_ref[...] = jnp.zeros_like(carry_ref)

    hn = _rms(x_ref[...], gain_ref[...]).astype(BF16)
    hn_ref[...] = hn
    f_logit = _dot_nt(hn, wf_ref[...]) + bf_ref[...]
    x = jnp.minimum(f_logit, 0.0) - jnp.log1p(jnp.exp(-jnp.abs(f_logit)))
    row = lax.broadcasted_iota(jnp.int32, x.shape, 0)
    shift = 1
    while shift < NX_TM:
        x = x + jnp.where(row >= shift, pltpu.roll(x, shift, 0), 0.0)
        shift *= 2
    c = x + carry_ref[...]
    carry_ref[...] = c[NX_TM - 1:NX_TM, :]
    rest = c * (-LOG2E)
    pieces = []
    for _ in range(N_PIECES):
        p = rest.astype(BF16)
        pieces.append(p)
        rest = rest - p.astype(F32)
    stacked = jnp.concatenate(pieces, axis=1)
    aug_ref[...] = jnp.dot(stacked, place_ref[...], preferred_element_type=F32).astype(BF16)


def _norm_x(x2, gain, wf, bfg, place, seq):
    t, d = x2.shape
    return pl.pallas_call(
        functools.partial(_norm_x_kernel, seq // NX_TM),
        grid=(t // NX_TM,),
        in_specs=[
            pl.BlockSpec((NX_TM, d), lambda i: (i, 0)),
            pl.BlockSpec((1, d), lambda i: (0, 0)),
            pl.BlockSpec((LANES, d), lambda i: (0, 0)),
            pl.BlockSpec((1, LANES), lambda i: (0, 0)),
            pl.BlockSpec((N_PIECES * LANES, B_WIDTH), lambda i: (0, 0)),
        ],
        out_specs=[
            pl.BlockSpec((NX_TM, d), lambda i: (i, 0)),
            pl.BlockSpec((NX_TM, B_WIDTH), lambda i: (i, 0)),
        ],
        out_shape=[jax.ShapeDtypeStruct((t, d), BF16),
                   jax.ShapeDtypeStruct((t, B_WIDTH), BF16)],
        scratch_shapes=[pltpu.VMEM((1, LANES), F32)],
        compiler_params=pltpu.CompilerParams(
            dimension_semantics=("arbitrary",), vmem_limit_bytes=VMEM_LIMIT),
        name="norm_x",
    )(x2, gain, wf, bfg, place)


IN_TM = 2048
IN_TN = 1024
IN_CHUNKS = IN_TN // LANES
K_CHUNKS = D_MODEL // LANES
N_IN_TILES = PROJ_WIDTH // IN_TN
N_ALIGNED_TILES = F_START // IN_TN
assert F_START % IN_TN == 0 and PROJ_WIDTH % IN_TN == 0 and F_SHIFT < LANES

_NORM_KIND = {"qA": HEAD_DIM, "kA": HEAD_DIM, "qB": HEAD_DIM, "kB": HEAD_DIM, "qC": C_HEAD_DIM}


def _chunk_piece(col):
    for name, off in _DST.items():
        if off <= col < off + _SRC[name][1]:
            return name
    raise ValueError(col)


def _norm_prefix(tile):
    kinds = [_chunk_piece(tile * IN_TN + c * LANES) in _NORM_KIND for c in range(IN_CHUNKS)]
    n = sum(kinds)
    assert kinds == [True] * n + [False] * (IN_CHUNKS - n)
    return n


NORM_PREFIX = tuple(_norm_prefix(j) for j in range(N_IN_TILES))
NORM_TILES = tuple(j for j in range(N_IN_TILES) if NORM_PREFIX[j])


def _norm_slot(j):
    slot = 0
    for tile in NORM_TILES[1:]:
        slot = slot + (j >= tile).astype(jnp.int32)
    return slot


IN_MCH = 512


def _gather_weight_rows(w_ref, dst_ref, src_row, dst_rows, n_rows):
    for c in range(K_CHUNKS):
        dst_ref[dst_rows, c * LANES:(c + 1) * LANES] = (
            w_ref[pl.ds(src_row + c, n_rows, stride=K_CHUNKS), :].astype(BF16))


def _w_tile0_kernel(w_ref, o_ref):
    _gather_weight_rows(w_ref, o_ref, 0, slice(None), IN_TN)


def _w_index(tile):
    col = tile * IN_TN + jnp.where(tile >= N_ALIGNED_TILES, F_SHIFT, 0)
    return (col * K_CHUNKS, 0)


def _w_tile0(w_rows):
    return pl.pallas_call(
        _w_tile0_kernel,
        grid=(1,),
        in_specs=[pl.BlockSpec((pl.Element(IN_TN * K_CHUNKS), pl.Element(LANES)),
                               lambda i: _w_index(i))],
        out_specs=pl.BlockSpec((IN_TN, D_MODEL), lambda i: (0, 0)),
        out_shape=jax.ShapeDtypeStruct((IN_TN, D_MODEL), BF16),
        compiler_params=pltpu.CompilerParams(
            dimension_semantics=("arbitrary",), vmem_limit_bytes=VMEM_LIMIT),
        name="w_tile0",
    )(w_rows)


def _in_proj_kernel(n_row_tiles, hn_ref, w_ref, w0_ref, cscale_ref, gmat_ref, o_ref,
                    wbf_a, wbf_b, acc_ref):
    j = pl.program_id(0)
    i = pl.program_id(1)
    share = IN_TN // n_row_tiles

    @pl.when((j == 0) & (i == 0))
    def _():
        wbf_a[...] = w0_ref[...]

    def epilogue(chunk, prefix):
        rows = slice(chunk * IN_MCH, (chunk + 1) * IN_MCH)
        acc = acc_ref.at[chunk % 2]
        for c in range(prefix // 2):
            sl = slice(c * 2 * LANES, (c + 1) * 2 * LANES)
            a = acc[:, sl]
            ms = jnp.dot((a * a).astype(BF16), gmat_ref[c], preferred_element_type=F32)
            o_ref[rows, sl] = (a * lax.rsqrt(ms + EPS) * cscale_ref[:, sl]).astype(BF16)
        if prefix < IN_CHUNKS:
            rest = slice(prefix * LANES, IN_TN)
            o_ref[rows, rest] = acc[:, rest].astype(BF16)

    def step(cur, nxt, prefix):
        dst = pl.ds(pl.multiple_of(i * share, share), share)
        n_chunks = hn_ref.shape[0] // IN_MCH
        for c in range(n_chunks + 1):
            if c < n_chunks:
                acc_ref[c % 2] = _dot_nt(hn_ref[c * IN_MCH:(c + 1) * IN_MCH, :], cur[...])
            if c == 0:
                _gather_weight_rows(w_ref, nxt, 0, dst, share)
            if c >= 1:
                epilogue(c - 1, prefix)

    def is_any(tiles):
        cond = j == tiles[0]
        for tile in tiles[1:]:
            cond = cond | (j == tile)
        return cond

    for parity, prefix in sorted({(t % 2, NORM_PREFIX[t]) for t in range(N_IN_TILES)}):
        tiles = [t for t in range(N_IN_TILES) if (t % 2, NORM_PREFIX[t]) == (parity, prefix)]
        cur, nxt = (wbf_a, wbf_b) if parity == 0 else (wbf_b, wbf_a)
        pl.when(is_any(tiles))(functools.partial(step, cur, nxt, prefix))


def _in_proj(hn, w_rows, w_first, cscale, gmat):
    t, d = hn.shape
    n_row_tiles = t // IN_TM
    share = IN_TN // n_row_tiles
    assert IN_TN % n_row_tiles == 0 and share % 16 == 0

    def w_share_index(j, i):
        row, col = _w_index(jnp.minimum(j + 1, N_IN_TILES - 1))
        return (pl.multiple_of(row + i * (share * K_CHUNKS), K_CHUNKS), col)

    return pl.pallas_call(
        functools.partial(_in_proj_kernel, n_row_tiles),
        grid=(N_IN_TILES, n_row_tiles),
        in_specs=[
            pl.BlockSpec((IN_TM, d), lambda j, i: (i, 0)),
            pl.BlockSpec((pl.Element(share * K_CHUNKS), pl.Element(LANES)), w_share_index),
            pl.BlockSpec((IN_TN, d), lambda j, i: (0, 0), pipeline_mode=pl.Buffered(1)),
            pl.BlockSpec((1, IN_TN), lambda j, i: (0, _norm_slot(j))),
            pl.BlockSpec((IN_CHUNKS // 2, 2 * LANES, 2 * LANES),
                         lambda j, i: (_norm_slot(j), 0, 0)),
        ],
        out_specs=pl.BlockSpec((IN_TM, IN_TN), lambda j, i: (i, j)),
        out_shape=jax.ShapeDtypeStruct((t, PROJ_WIDTH), BF16),
        scratch_shapes=[
            pltpu.VMEM((IN_TN, d), BF16),
            pltpu.VMEM((IN_TN, d), BF16),
            pltpu.VMEM((2, IN_MCH, IN_TN), F32),
        ],
        compiler_params=pltpu.CompilerParams(
            dimension_semantics=("arbitrary", "arbitrary"), vmem_limit_bytes=VMEM_LIMIT),
        name="in_proj",
    )(hn, w_rows, w_first, cscale, gmat)


def _aug_lane(head):
    return HALF if head % 2 == 0 else 0


def _placement_matrix():
    place = np.zeros((N_PIECES * LANES, B_WIDTH), np.float32)
    for h in range(B_HEADS):
        for p in range(N_PIECES):
            place[p * LANES + h, (h // 2) * LANES + _aug_lane(h) + p] = 1.0
    return place


SWA_QB = 4
SWA_WIN = (SWA_QB + 1) * WINDOW


def _alibi_slopes_log2():
    return [float(2.0 ** (-8.0 * (h + 1) / A_Q_HEADS)) * LOG2E for h in range(A_Q_HEADS)]


def _swa_key_bias():
    key = np.arange(2 * WINDOW)[:, None]
    qry = np.arange(WINDOW)[None, :]
    rel = qry + WINDOW - key
    visible = (rel >= 0) & (rel < WINDOW)
    slopes = np.asarray(_alibi_slopes_log2(), np.float32)[:, None, None]
    return np.where(visible[None], slopes * key[None].astype(np.float32), np.float32(NEG))


def _swa_kernel(n_cast, sinks_ref, bias_ref, q_ref, kp_ref, kc_ref, vp_ref, vc_ref, *refs):
    cast_in, o_ref, cast_out = refs[:n_cast], refs[n_cast], refs[n_cast + 1:2 * n_cast + 1]
    kop_ref, vt_ref, s_ref, p_ref, sh_ref, ot_ref = refs[2 * n_cast + 1:]
    for src, dst in zip(cast_in, cast_out):
        dst[...] = src[...].astype(BF16)
    step = pl.program_id(1)
    blk = WINDOW
    lane = lax.broadcasted_iota(jnp.int32, (SWA_WIN, LANES), 1)

    kwin = jnp.concatenate([kp_ref[...], kc_ref[...]], axis=0).astype(F32)
    vwin_t = jnp.concatenate([vp_ref[...], vc_ref[...]], axis=0).astype(F32).T
    ones_rows = jnp.where(
        lax.broadcasted_iota(jnp.int32, (VT_ROWS - HALF, SWA_WIN), 0) == 0, 1.0, 0.0)
    for g in range(A_KV_HEADS):
        slab = kwin[:, (g // 2) * LANES:(g // 2 + 1) * LANES]
        own = (lane < HALF) if g % 2 == 0 else (lane >= HALF)
        kz = jnp.where(own, slab, 0.0)
        kop_ref[g, 0] = kz.astype(BF16)
        kop_ref[g, 1] = pltpu.roll(kz, HALF, 1).astype(BF16)
        vt_ref[g] = jnp.concatenate(
            [vwin_t[g * HEAD_DIM:(g + 1) * HEAD_DIM], ones_rows], axis=0).astype(BF16)

    t_win = lax.broadcasted_iota(jnp.int32, (1, blk), 1).astype(F32) + float(blk)
    slopes = _alibi_slopes_log2()
    sinks = [sinks_ref[h] * LOG2E + slopes[h] * t_win for h in range(A_Q_HEADS)]
    key_row = lax.broadcasted_iota(jnp.int32, (2 * blk, blk), 0)

    items = [(i, g) for i in range(SWA_QB) for g in range(A_KV_HEADS)]
    keys_of = lambda i: slice(i * blk, (i + 2) * blk)

    def stage_scores(n):
        i, g = items[n]
        for k in range(A_GROUP):
            h = g * A_GROUP + k
            q_slab = q_ref[i * blk:(i + 1) * blk, (h // 2) * LANES:(h // 2 + 1) * LANES]
            s_ref[n % 2, k] = _dot_nt(kop_ref[g, (h + g) % 2, keys_of(i), :], q_slab)

    def stage_softmax(n):
        i, g = items[n]
        for k in range(A_GROUP):
            h = g * A_GROUP + k
            st = s_ref[n % 2, k] + bias_ref[h]
            if i == 0:
                st = jnp.where(((step * SWA_QB - 1) * blk + key_row) >= 0, st, NEG)
            m = jnp.maximum(jnp.max(st, axis=0, keepdims=True), sinks[h])
            p_ref[n % 2, k] = jnp.exp2(st - m).astype(BF16)
            sh_ref[n % 2, k] = jnp.exp2(sinks[h] - m)

    def stage_values(n):
        i, g = items[n]
        for k in range(A_GROUP):
            h = g * A_GROUP + k
            ot = jnp.dot(vt_ref[g, :, keys_of(i)], p_ref[n % 2, k],
                         preferred_element_type=F32)
            ot_ref[h * HEAD_DIM:(h + 1) * HEAD_DIM, :] = (
                ot[0:HEAD_DIM] / (ot[HEAD_DIM:HEAD_DIM + 1] + sh_ref[n % 2, k]))
        if g == A_KV_HEADS - 1:
            o_ref[i * blk:(i + 1) * blk, :] = ot_ref[...].T.astype(BF16)

    for n in range(len(items) + 2):
        if n < len(items):
            stage_scores(n)
        if 2 <= n:
            stage_values(n - 2)
        if 1 <= n <= len(items):
            stage_softmax(n - 1)


def _swa(sinks, bias, proj, cast_weights, batch, seq):
    blk = WINDOW
    nb = seq // blk
    steps = nb // SWA_QB
    n_steps = batch * steps
    qcol = _col_block("qA", A_WIDTH)
    kcol = _col_block("kA", A_KV_WIDTH)
    vcol = _col_block("vA", A_KV_WIDTH)
    cur = lambda b, s: b * steps + s
    prev = lambda b, s: b * nb + jnp.maximum(s * SWA_QB - 1, 0)
    slab_specs = []
    for w in cast_weights:
        assert w.shape[0] % (16 * n_steps) == 0
        slab_specs.append(pl.BlockSpec((w.shape[0] // n_steps, w.shape[1]),
                                       lambda b, s: (cur(b, s), 0)))
    outs = pl.pallas_call(
        functools.partial(_swa_kernel, len(cast_weights)),
        grid=(batch, steps),
        in_specs=[
            pl.BlockSpec(memory_space=pltpu.SMEM),
            pl.BlockSpec(bias.shape, lambda b, s: (0, 0, 0)),
            pl.BlockSpec((SWA_QB * blk, A_WIDTH), lambda b, s: (cur(b, s), qcol)),
            pl.BlockSpec((blk, A_KV_WIDTH), lambda b, s: (prev(b, s), kcol)),
            pl.BlockSpec((SWA_QB * blk, A_KV_WIDTH), lambda b, s: (cur(b, s), kcol)),
            pl.BlockSpec((blk, A_KV_WIDTH), lambda b, s: (prev(b, s), vcol)),
            pl.BlockSpec((SWA_QB * blk, A_KV_WIDTH), lambda b, s: (cur(b, s), vcol)),
            *slab_specs,
        ],
        out_specs=[pl.BlockSpec((SWA_QB * blk, A_WIDTH), lambda b, s: (cur(b, s), 0)),
                   *slab_specs],
        out_shape=[jax.ShapeDtypeStruct((batch * seq, A_WIDTH), BF16),
                   *[jax.ShapeDtypeStruct(w.shape, BF16) for w in cast_weights]],
        scratch_shapes=[
            pltpu.VMEM((A_KV_HEADS, 2, SWA_WIN, LANES), BF16),
            pltpu.VMEM((A_KV_HEADS, VT_ROWS, SWA_WIN), BF16),
            pltpu.VMEM((2, A_GROUP, 2 * blk, blk), F32),
            pltpu.VMEM((2, A_GROUP, 2 * blk, blk), BF16),
            pltpu.VMEM((2, A_GROUP, 1, blk), F32),
            pltpu.VMEM((A_WIDTH, blk), F32),
        ],
        compiler_params=pltpu.CompilerParams(
            dimension_semantics=("arbitrary", "arbitrary"), vmem_limit_bytes=VMEM_LIMIT),
        name="swa",
    )(sinks, bias, proj, proj, proj, proj, proj, *cast_weights)
    return outs[0], outs[1:]


FOX_T = 512


FOX_H = FOX_T // 2


def _fox_kernel(q_ref, k_ref, v_ref, aug_ref, o_ref,
                kop_ref, vt_ref, qop_ref, s_ref, cm_ref, p_ref, al_ref, m_ref, acc_ref):
    seq = k_ref.shape[0]
    t = FOX_T
    hk = FOX_H

    lane = lax.broadcasted_iota(jnp.int32, (hk, LANES), 1)
    low = lane < HALF

    def build(r):
        sl = slice(r * hk, (r + 1) * hk)
        kk = k_ref[sl, :].astype(F32)
        aa = aug_ref[sl, :].astype(F32)
        vv = v_ref[sl, :].astype(F32)
        kop_ref[0, sl, :] = jnp.where(low, kk, aa).astype(BF16)
        kop_ref[1, sl, :] = jnp.where(low, aa, kk).astype(BF16)
        vvt = vv.T
        ones_rows = jnp.where(
            lax.broadcasted_iota(jnp.int32, (VT_ROWS - HALF, hk), 0) == 0, 1.0, 0.0)
        for hh in range(2):
            vt_ref[hh, r] = jnp.concatenate(
                [vvt[hh * HALF:(hh + 1) * HALF], ones_rows], axis=0).astype(BF16)

    def scores(slot, half, masked):
        ks = pl.ds(pl.multiple_of(half * hk, hk), hk)
        for hh in range(2):
            st = _dot_nt(kop_ref[hh, ks, :], qop_ref[hh])
            if masked:
                key = lax.broadcasted_iota(jnp.int32, (hk, t), 0) + slot * hk
                qry = lax.broadcasted_iota(jnp.int32, (hk, t), 1)
                st = jnp.where(key <= qry, st, NEG)
            s_ref[slot, hh] = st
            cm_ref[slot, hh] = jnp.max(st, axis=0, keepdims=True)

    def soft(slot):
        for hh in range(2):
            m_old = m_ref[hh]
            m_new = jnp.maximum(m_old, cm_ref[slot, hh])
            al_ref[slot, hh] = jnp.exp2(m_old - m_new)
            p_ref[slot, hh] = jnp.exp2(s_ref[slot, hh] - m_new).astype(BF16)
            m_ref[hh] = m_new

    def pv(slot, half):
        for hh in range(2):
            acc_ref[hh] = acc_ref[hh] * al_ref[slot, hh] + jnp.dot(
                vt_ref[hh, half], p_ref[slot, hh], preferred_element_type=F32)

    def step(i, prev, has_next):
        pv(0, 2 * prev)
        soft(0)
        scores(1, 2 * i + 1, False)
        if has_next:
            scores(0, 2 * i + 2, False)
        pv(1, 2 * prev + 1)
        soft(1)

    lane_q = lax.broadcasted_iota(jnp.int32, (t, LANES), 1)
    low_q = lane_q < HALF
    ones0 = jnp.where((lane_q >= HALF) & (lane_q < HALF + N_PIECES), 1.0, 0.0)
    ones1 = jnp.where(lane_q < N_PIECES, 1.0, 0.0)

    for qi in range(seq // t):
        build(2 * qi)
        build(2 * qi + 1)
        rows = slice(qi * t, (qi + 1) * t)
        q = q_ref[rows, :].astype(F32)
        qop_ref[0] = jnp.where(low_q, q, ones0).astype(BF16)
        qop_ref[1] = jnp.where(low_q, ones1, q).astype(BF16)
        m_ref[...] = jnp.full(m_ref.shape, NEG, F32)
        acc_ref[...] = jnp.zeros(acc_ref.shape, F32)

        scores(0, 2 * qi, True)
        scores(1, 2 * qi + 1, True)
        soft(0)
        if qi > 0:
            scores(0, 0, False)
        soft(1)

        loop_pairs = max(qi - 1, 0) // 2

        def pair(j, carry, qi=qi):
            first = 2 * j
            step(first, jnp.where(j == 0, qi, first - 1), True)
            step(first + 1, first, True)
            return carry

        if loop_pairs:
            lax.fori_loop(0, loop_pairs, pair, 0)
        for i in range(2 * loop_pairs, qi):
            step(i, i - 1 if i > 0 else qi, i + 1 < qi)
        last = qi - 1 if qi > 0 else qi
        pv(0, 2 * last)
        pv(1, 2 * last + 1)

        ot = jnp.concatenate(
            [acc_ref[hh, 0:HALF, :] / acc_ref[hh, HALF:HALF + 1, :] for hh in range(2)], axis=0)
        o_ref[rows, :] = ot.T.astype(BF16)


def _fox(proj, aug, batch, seq):
    t = FOX_T
    nq = seq // t
    pairs = B_HEADS // 2
    qcol = _col_block("qB", LANES)
    kcol = _col_block("kB", LANES)
    vcol = _col_block("vB", LANES)
    return pl.pallas_call(
        _fox_kernel,
        grid=(batch, pairs),
        in_specs=[
            pl.BlockSpec((seq, LANES), lambda b, p: (b, qcol + p)),
            pl.BlockSpec((seq, LANES), lambda b, p: (b, kcol + p)),
            pl.BlockSpec((seq, LANES), lambda b, p: (b, vcol + p)),
            pl.BlockSpec((seq, LANES), lambda b, p: (b, p)),
        ],
        out_specs=pl.BlockSpec((seq, LANES), lambda b, p: (b, p)),
        out_shape=jax.ShapeDtypeStruct((batch * seq, B_WIDTH), BF16),
        scratch_shapes=[
            pltpu.VMEM((2, seq, LANES), BF16),
            pltpu.VMEM((2, seq // FOX_H, VT_ROWS, FOX_H), BF16),
            pltpu.VMEM((2, t, LANES), BF16),
            pltpu.VMEM((2, 2, FOX_H, t), F32),
            pltpu.VMEM((2, 2, 1, t), F32),
            pltpu.VMEM((2, 2, FOX_H, t), BF16),
            pltpu.VMEM((2, 2, 1, t), F32),
            pltpu.VMEM((2, 1, t), F32),
            pltpu.VMEM((2, VT_ROWS, t), F32),
        ],
        compiler_params=pltpu.CompilerParams(
            dimension_semantics=("arbitrary", "arbitrary"),
            vmem_limit_bytes=VMEM_LIMIT),
        name="fox",
    )(proj, proj, proj, aug)


MERGE_TM = 256


Z_BLOCK = 256
Z_PARTS = A_WIDTH // Z_BLOCK
assert A_WIDTH == B_WIDTH and A_WIDTH % Z_BLOCK == 0


def _merge_kernel(x_ref, ya_ref, yb_ref, mk_ref, mv_ref, qc_ref, *refs):
    za_refs = refs[:Z_PARTS]
    zb_refs = refs[Z_PARTS:2 * Z_PARTS]
    zc_ref, g0_ref, g1_ref, g2_ref, wa_ref, wb_ref, wc_ref, wo_ref, o_ref = refs[2 * Z_PARTS:]

    def memory_attention():
        outs = []
        for h in range(C_HEADS):
            sl = slice(h * C_HEAD_DIM, (h + 1) * C_HEAD_DIM)
            s = _dot_nt(qc_ref[:, sl], mk_ref[0, :, sl])
            p = jnp.exp(s - jnp.max(s, axis=-1, keepdims=True))
            o = jnp.dot(p.astype(BF16), mv_ref[0, :, sl], preferred_element_type=F32)
            outs.append(o / jnp.sum(p, axis=-1, keepdims=True))
        return jnp.concatenate(outs, axis=1)

    def branch(y, z_refs, w_ref):
        z = jnp.concatenate([r[...] for r in z_refs], axis=1).astype(F32)
        h = (y * (z * _sigmoid(z))).astype(BF16)
        return jnp.dot(h, w_ref[...], preferred_element_type=F32)

    y = _sigmoid(g0_ref[...].astype(F32)) * branch(ya_ref[...].astype(F32), za_refs, wa_ref)
    y = y + _sigmoid(g1_ref[...].astype(F32)) * branch(yb_ref[...].astype(F32), zb_refs, wb_ref)
    y = y + _sigmoid(g2_ref[...].astype(F32)) * branch(memory_attention(), (zc_ref,), wc_ref)
    o_ref[...] = x_ref[...] + jnp.dot(y.astype(BF16), wo_ref[...], preferred_element_type=F32)


def _merge(x2, ya, yb, mk, mv, proj, wa, wb, wc, wo, seq):
    t, d = x2.shape
    tm = MERGE_TM
    tiles_per_seq = seq // tm
    gcol = _col_block("g", D_MODEL)
    row = lambda width, col: pl.BlockSpec((tm, width), lambda i: (i, col))
    full = lambda a: pl.BlockSpec(a.shape, lambda i: (0, 0), pipeline_mode=pl.Buffered(1))
    mem = lambda a: pl.BlockSpec((1,) + a.shape[1:], lambda i: (i // tiles_per_seq, 0, 0))
    z_specs = lambda name: [row(Z_BLOCK, _col_block(name, Z_BLOCK, part)) for part in range(Z_PARTS)]
    n_proj = 2 * Z_PARTS + 5
    return pl.pallas_call(
        _merge_kernel,
        grid=(t // tm,),
        in_specs=[
            row(d, 0), row(A_WIDTH, 0), row(B_WIDTH, 0), mem(mk), mem(mv),
            row(C_WIDTH, _col_block("qC", C_WIDTH)),
            *z_specs("zA"), *z_specs("zB"), row(C_WIDTH, _col_block("zC", C_WIDTH)),
            row(d, gcol), row(d, gcol + 1), row(d, gcol + 2),
            full(wa), full(wb), full(wc), full(wo),
        ],
        out_specs=pl.BlockSpec((tm, d), lambda i: (i, 0)),
        out_shape=jax.ShapeDtypeStruct((t, d), F32),
        compiler_params=pltpu.CompilerParams(
            dimension_semantics=("arbitrary",), vmem_limit_bytes=VMEM_LIMIT),
        name="merge",
    )(x2, ya, yb, mk, mv, *([proj] * n_proj), wa, wb, wc, wo)


def _group_mean_matrices():
    lane = np.arange(2 * LANES)
    chunk = lane // LANES
    mats = []
    for tile in NORM_TILES:
        for c in range(0, IN_CHUNKS, 2):
            dims = np.asarray([_NORM_KIND.get(_chunk_piece(tile * IN_TN + (c + k) * LANES), LANES)
                               for k in range(2)])[chunk]
            same = ((lane[:, None] // dims[:, None] == lane[None, :] // dims[None, :])
                    & (chunk[:, None] == chunk[None, :]))
            mats.append(same.astype(np.float32) / dims[None, :])
    return np.stack(mats)


def _column_scales(gains):
    scale = {"qA": HEAD_DIM ** -0.5 * LOG2E, "qB": HEAD_DIM ** -0.5 * LOG2E,
             "qC": C_HEAD_DIM ** -0.5}
    cols = []
    for tile in NORM_TILES:
        for c in range(IN_CHUNKS):
            piece = _chunk_piece(tile * IN_TN + c * LANES)
            if piece in _NORM_KIND:
                g = gains[piece].astype(F32) * scale.get(piece, 1.0)
                cols.append(jnp.tile(g, LANES // g.shape[0]))
            else:
                cols.append(jnp.zeros((LANES,), F32))
    return jnp.concatenate(cols).reshape(1, len(NORM_TILES) * IN_TN)


def _layer(x, mem, norm_gain, mem_norm_gain, w_rows, b_forget, q_gain_a, k_gain_a, sinks_a,
           q_gain_b, k_gain_b, q_gain_c, k_gain_c, w_mem_kv, w_branch_a, w_branch_b,
           w_branch_c, w_out):
    batch, seq, d = x.shape
    x2 = x.reshape(batch * seq, d)

    w_f = w_rows[F_START * K_CHUNKS:(F_START + F_SHIFT) * K_CHUNKS].reshape(F_SHIFT, d)
    w_f = jnp.pad(w_f, ((0, LANES - F_SHIFT), (0, 0))).astype(BF16)
    b_f = jnp.pad(b_forget.astype(F32), (0, LANES - F_SHIFT)).reshape(1, LANES)
    cscale = _column_scales({"qA": q_gain_a, "kA": k_gain_a, "qB": q_gain_b, "kB": k_gain_b,
                             "qC": q_gain_c})
    gmat = jnp.asarray(_group_mean_matrices(), BF16)
    place = jnp.asarray(_placement_matrix(), BF16)

    mk, mv = _mem_kv(mem, mem_norm_gain.reshape(1, d), w_mem_kv, k_gain_c.reshape(1, C_HEAD_DIM))
    hn, aug = _norm_x(x2, norm_gain.reshape(1, d), w_f, b_f, place, seq)
    proj = _in_proj(hn, w_rows, _w_tile0(w_rows), cscale, gmat)
    ya, merge_weights = _swa(sinks_a.astype(F32), jnp.asarray(_swa_key_bias(), F32), proj,
                             (w_branch_a, w_branch_b, w_branch_c, w_out), batch, seq)
    yb = _fox(proj, aug, batch, seq)
    out = _merge(x2, ya, yb, mk, mv, proj, *merge_weights, seq)
    return out.reshape(batch, seq, d)


def kernel(x, mem, norm_gain, mem_norm_gain, w_in, b_forget, q_gain_a, k_gain_a, sinks_a,
           q_gain_b, k_gain_b, q_gain_c, k_gain_c, w_mem_kv, w_branch_a, w_branch_b,
           w_branch_c, w_out):
    depth = norm_gain.shape[0]
    w_rows = jnp.swapaxes(w_in, 1, 2).reshape(depth, -1, LANES)
    for layer in range(depth):
        x = _layer(x, mem, norm_gain[layer], mem_norm_gain[layer], w_rows[layer], b_forget[layer],
                   q_gain_a[layer], k_gain_a[layer], sinks_a[layer], q_gain_b[layer],
                   k_gain_b[layer], q_gain_c[layer], k_gain_c[layer], w_mem_kv[layer],
                   w_branch_a[layer], w_branch_b[layer], w_branch_c[layer], w_out[layer])
    return x
```

```python
import functools

import jax
import jax.numpy as jnp
import numpy as np
from jax import lax
from jax.experimental import pallas as pl
from jax.experimental.pallas import tpu as pltpu

F32 = jnp.float32
BF16 = jnp.bfloat16

D_MODEL = 2048
HEAD_DIM = 64
A_Q_HEADS = 12
A_KV_HEADS = 4
A_GROUP = A_Q_HEADS // A_KV_HEADS
WINDOW = 128
B_HEADS = 12
C_HEADS = 4
C_HEAD_DIM = 128
A_WIDTH = A_Q_HEADS * HEAD_DIM
A_KV_WIDTH = A_KV_HEADS * HEAD_DIM
B_WIDTH = B_HEADS * HEAD_DIM
C_WIDTH = C_HEADS * C_HEAD_DIM
EPS = 1e-6
NEG = -1e30

LANES = 128
HALF = LANES // 2
VT_ROWS = HALF + 16
LOG2E = float(np.log2(np.e))
N_PIECES = 3
VMEM_LIMIT = 56 * 1024 * 1024

_SRC = {}
_off = 0
for _name, _w in (("qA", A_WIDTH), ("kA", A_KV_WIDTH), ("vA", A_KV_WIDTH), ("zA", A_WIDTH),
                  ("qB", B_WIDTH), ("kB", B_WIDTH), ("vB", B_WIDTH), ("zB", B_WIDTH),
                  ("fB", B_HEADS), ("qC", C_WIDTH), ("zC", C_WIDTH), ("g", 3 * D_MODEL)):
    _SRC[_name] = (_off, _w)
    _off += _w

F_START, F_SHIFT = _SRC["fB"]
_DST = {n: (o if o < F_START else o - F_SHIFT) for n, (o, _) in _SRC.items() if n != "fB"}
PROJ_WIDTH = _SRC["g"][0] + _SRC["g"][1] - F_SHIFT
assert all(o % LANES == 0 for o in _DST.values()) and F_START % LANES == 0


def _col_block(name, width, part=0):
    assert _DST[name] % width == 0
    return _DST[name] // width + part


def _rms(x, gain):
    ms = jnp.mean(x * x, axis=-1, keepdims=True)
    return x * lax.rsqrt(ms + EPS) * gain


def _sigmoid(t):
    return 0.5 * (jnp.tanh(0.5 * t) + 1.0)


def _dot_nt(a, b):
    return lax.dot_general(a, b, (((1,), (1,)), ((), ())), preferred_element_type=F32)


def _mem_kv_kernel(mem_ref, gain_ref, w_ref, kgain_ref, wrows_ref, mk_ref, mv_ref, w0_ref):
    @pl.when(pl.program_id(0) == 0)
    def _():
        _gather_weight_rows(wrows_ref, w0_ref, 0, slice(None), IN_TN)

    hn = _rms(mem_ref[0], gain_ref[...]).astype(BF16)
    kv = jnp.dot(hn, w_ref[...].astype(BF16), preferred_element_type=F32)
    for h in range(C_HEADS):
        sl = slice(h * C_HEAD_DIM, (h + 1) * C_HEAD_DIM)
        mk_ref[0, :, sl] = _rms(kv[:, sl], kgain_ref[...]).astype(BF16)
    mv_ref[0] = kv[:, C_WIDTH:].astype(BF16)


def _mem_kv(mem, gain, w, kgain, w_rows):
    b, m, d = mem.shape
    return pl.pallas_call(
        _mem_kv_kernel,
        grid=(b,),
        in_specs=[
            pl.BlockSpec((1, m, d), lambda i: (i, 0, 0)),
            pl.BlockSpec((1, d), lambda i: (0, 0)),
            pl.BlockSpec((d, 2 * C_WIDTH), lambda i: (0, 0)),
            pl.BlockSpec((1, C_HEAD_DIM), lambda i: (0, 0)),
            pl.BlockSpec((pl.Element(IN_TN * K_CHUNKS), pl.Element(LANES)),
                         lambda i: _w_index(0 * i)),
        ],
        out_specs=[
            pl.BlockSpec((1, m, C_WIDTH), lambda i: (i, 0, 0)),
            pl.BlockSpec((1, m, C_WIDTH), lambda i: (i, 0, 0)),
            pl.BlockSpec((IN_TN, D_MODEL), lambda i: (0, 0)),
        ],
        out_shape=[jax.ShapeDtypeStruct((b, m, C_WIDTH), BF16)] * 2
        + [jax.ShapeDtypeStruct((IN_TN, D_MODEL), BF16)],
        compiler_params=pltpu.CompilerParams(
            dimension_semantics=("arbitrary",), vmem_limit_bytes=VMEM_LIMIT),
        name="mem_kv",
    )(mem, gain, w, kgain, w_rows)


NX_TM = 1024


def _norm_x_kernel(tiles_per_seq, x_ref, gain_ref, wf_ref, bf_ref, place_ref, hn_ref, aug_ref,
                   carry_ref):
    i = pl.program_id(0)

    @pl.when(i % tiles_per_seq == 0)
    def _():
        carry_ref[...] = jnp.zeros_like(carry_ref)

    hn = _rms(x_ref[...], gain_ref[...]).astype(BF16)
    hn_ref[...] = hn
    f_logit = _dot_nt(hn, wf_ref[...]) + bf_ref[...]
    x = jnp.minimum(f_logit, 0.0) - jnp.log1p(jnp.exp(-jnp.abs(f_logit)))
    row = lax.broadcasted_iota(jnp.int32, x.shape, 0)
    shift = 1
    while shift < NX_TM:
        x = x + jnp.where(row >= shift, pltpu.roll(x, shift, 0), 0.0)
        shift *= 2
    c = x + carry_ref[...]
    carry_ref[...] = c[NX_TM - 1:NX_TM, :]
    rest = c * (-LOG2E)
    pieces = []
    for _ in range(N_PIECES):
        p = rest.astype(BF16)
        pieces.append(p)
        rest = rest - p.astype(F32)
    stacked = jnp.concatenate(pieces, axis=1)
    aug_ref[...] = jnp.dot(stacked, place_ref[...], preferred_element_type=F32).astype(BF16)


def _norm_x(x2, gain, wf, bfg, place, seq):
    t, d = x2.shape
    return pl.pallas_call(
        functools.partial(_norm_x_kernel, seq // NX_TM),
        grid=(t // NX_TM,),
        in_specs=[
            pl.BlockSpec((NX_TM, d), lambda i: (i, 0)),
            pl.BlockSpec((1, d), lambda i: (0, 0)),
            pl.BlockSpec((LANES, d), lambda i: (0, 0)),
            pl.BlockSpec((1, LANES), lambda i: (0, 0)),
            pl.BlockSpec((N_PIECES * LANES, B_WIDTH), lambda i: (0, 0)),
        ],
        out_specs=[
            pl.BlockSpec((NX_TM, d), lambda i: (i, 0)),
            pl.BlockSpec((NX_TM, B_WIDTH), lambda i: (i, 0)),
        ],
        out_shape=[jax.ShapeDtypeStruct((t, d), BF16),
                   jax.ShapeDtypeStruct((t, B_WIDTH), BF16)],
        scratch_shapes=[pltpu.VMEM((1, LANES), F32)],
        compiler_params=pltpu.CompilerParams(
            dimension_semantics=("arbitrary",), vmem_limit_bytes=VMEM_LIMIT),
        name="norm_x",
    )(x2, gain, wf, bfg, place)


IN_TM = 1024
IN_TN = 1024
IN_CHUNKS = IN_TN // LANES
K_CHUNKS = D_MODEL // LANES
N_IN_TILES = PROJ_WIDTH // IN_TN
N_ALIGNED_TILES = F_START // IN_TN
assert F_START % IN_TN == 0 and PROJ_WIDTH % IN_TN == 0 and F_SHIFT < LANES

_NORM_KIND = {"qA": HEAD_DIM, "kA": HEAD_DIM, "qB": HEAD_DIM, "kB": HEAD_DIM, "qC": C_HEAD_DIM}


def _chunk_piece(col):
    for name, off in _DST.items():
        if off <= col < off + _SRC[name][1]:
            return name
    raise ValueError(col)


def _norm_prefix(tile):
    kinds = [_chunk_piece(tile * IN_TN + c * LANES) in _NORM_KIND for c in range(IN_CHUNKS)]
    n = sum(kinds)
    assert kinds == [True] * n + [False] * (IN_CHUNKS - n)
    return n


NORM_PREFIX = tuple(_norm_prefix(j) for j in range(N_IN_TILES))
NORM_TILES = tuple(j for j in range(N_IN_TILES) if NORM_PREFIX[j])


def _norm_slot(j):
    slot = 0
    for tile in NORM_TILES[1:]:
        slot = slot + (j >= tile).astype(jnp.int32)
    return slot


IN_MCH = 512


def _gather_weight_rows(w_ref, dst_ref, src_row, dst_rows, n_rows):
    for c in range(K_CHUNKS):
        dst_ref[dst_rows, c * LANES:(c + 1) * LANES] = (
            w_ref[pl.ds(src_row + c, n_rows, stride=K_CHUNKS), :].astype(BF16))


def _w_index(tile):
    col = tile * IN_TN + jnp.where(tile >= N_ALIGNED_TILES, F_SHIFT, 0)
    return (col * K_CHUNKS, 0)


def _in_proj_kernel(n_row_tiles, hn_ref, w_ref, w0_ref, cscale_ref, gmat_ref, o_ref,
                    wbf_a, wbf_b, acc_ref):
    j = pl.program_id(0)
    i = pl.program_id(1)
    share = IN_TN // n_row_tiles

    @pl.when((j == 0) & (i == 0))
    def _():
        wbf_a[...] = w0_ref[...]

    def epilogue(chunk, prefix):
        rows = slice(chunk * IN_MCH, (chunk + 1) * IN_MCH)
        acc = acc_ref.at[chunk % 2]
        for c in range(prefix // 2):
            sl = slice(c * 2 * LANES, (c + 1) * 2 * LANES)
            a = acc[:, sl]
            ms = jnp.dot((a * a).astype(BF16), gmat_ref[c], preferred_element_type=F32)
            o_ref[rows, sl] = (a * lax.rsqrt(ms + EPS) * cscale_ref[:, sl]).astype(BF16)
        if prefix < IN_CHUNKS:
            rest = slice(prefix * LANES, IN_TN)
            o_ref[rows, rest] = acc[:, rest].astype(BF16)

    def step(cur, nxt, prefix):
        src = pl.multiple_of(i * (share * K_CHUNKS), share * K_CHUNKS)
        dst = pl.ds(pl.multiple_of(i * share, share), share)
        n_chunks = hn_ref.shape[0] // IN_MCH
        for c in range(n_chunks + 1):
            if c < n_chunks:
                acc_ref[c % 2] = _dot_nt(hn_ref[c * IN_MCH:(c + 1) * IN_MCH, :], cur[...])
            if c == 0:
                _gather_weight_rows(w_ref, nxt, src, dst, share)
            if c >= 1:
                epilogue(c - 1, prefix)

    def is_any(tiles):
        cond = j == tiles[0]
        for tile in tiles[1:]:
            cond = cond | (j == tile)
        return cond

    for parity, prefix in sorted({(t % 2, NORM_PREFIX[t]) for t in range(N_IN_TILES)}):
        tiles = [t for t in range(N_IN_TILES) if (t % 2, NORM_PREFIX[t]) == (parity, prefix)]
        cur, nxt = (wbf_a, wbf_b) if parity == 0 else (wbf_b, wbf_a)
        pl.when(is_any(tiles))(functools.partial(step, cur, nxt, prefix))


def _in_proj(hn, w_rows, w_first, cscale, gmat):
    t, d = hn.shape
    n_row_tiles = t // IN_TM
    assert IN_TN % n_row_tiles == 0 and (IN_TN // n_row_tiles) % 16 == 0
    return pl.pallas_call(
        functools.partial(_in_proj_kernel, n_row_tiles),
        grid=(N_IN_TILES, n_row_tiles),
        in_specs=[
            pl.BlockSpec((IN_TM, d), lambda j, i: (i, 0)),
            pl.BlockSpec((pl.Element(IN_TN * K_CHUNKS), pl.Element(LANES)),
                         lambda j, i: _w_index(jnp.minimum(j + 1, N_IN_TILES - 1))),
            pl.BlockSpec((IN_TN, d), lambda j, i: (0, 0), pipeline_mode=pl.Buffered(1)),
            pl.BlockSpec((1, IN_TN), lambda j, i: (0, _norm_slot(j))),
            pl.BlockSpec((IN_CHUNKS // 2, 2 * LANES, 2 * LANES),
                         lambda j, i: (_norm_slot(j), 0, 0)),
        ],
        out_specs=pl.BlockSpec((IN_TM, IN_TN), lambda j, i: (i, j)),
        out_shape=jax.ShapeDtypeStruct((t, PROJ_WIDTH), BF16),
        scratch_shapes=[
            pltpu.VMEM((IN_TN, d), BF16),
            pltpu.VMEM((IN_TN, d), BF16),
            pltpu.VMEM((2, IN_MCH, IN_TN), F32),
        ],
        compiler_params=pltpu.CompilerParams(
            dimension_semantics=("arbitrary", "arbitrary"), vmem_limit_bytes=VMEM_LIMIT),
        name="in_proj",
    )(hn, w_rows, w_first, cscale, gmat)


def _aug_lane(head):
    return HALF if head % 2 == 0 else 0


def _placement_matrix():
    place = np.zeros((N_PIECES * LANES, B_WIDTH), np.float32)
    for h in range(B_HEADS):
        for p in range(N_PIECES):
            place[p * LANES + h, (h // 2) * LANES + _aug_lane(h) + p] = 1.0
    return place


SWA_QB = 4
SWA_WIN = (SWA_QB + 1) * WINDOW


def _alibi_slopes_log2():
    return [float(2.0 ** (-8.0 * (h + 1) / A_Q_HEADS)) * LOG2E for h in range(A_Q_HEADS)]


def _swa_key_bias():
    key = np.arange(2 * WINDOW)[:, None]
    qry = np.arange(WINDOW)[None, :]
    rel = qry + WINDOW - key
    visible = (rel >= 0) & (rel < WINDOW)
    slopes = np.asarray(_alibi_slopes_log2(), np.float32)[:, None, None]
    return np.where(visible[None], slopes * key[None].astype(np.float32), np.float32(NEG))


def _swa_kernel(n_cast, sinks_ref, bias_ref, q_ref, kp_ref, kc_ref, vp_ref, vc_ref, *refs):
    cast_in, o_ref, cast_out = refs[:n_cast], refs[n_cast], refs[n_cast + 1:2 * n_cast + 1]
    kop_ref, vt_ref, s_ref, p_ref, sh_ref, ot_ref = refs[2 * n_cast + 1:]
    for src, dst in zip(cast_in, cast_out):
        dst[...] = src[...].astype(BF16)
    step = pl.program_id(1)
    blk = WINDOW
    lane = lax.broadcasted_iota(jnp.int32, (SWA_WIN, LANES), 1)

    kwin = jnp.concatenate([kp_ref[...], kc_ref[...]], axis=0).astype(F32)
    vwin_t = jnp.concatenate([vp_ref[...], vc_ref[...]], axis=0).astype(F32).T
    ones_rows = jnp.where(
        lax.broadcasted_iota(jnp.int32, (VT_ROWS - HALF, SWA_WIN), 0) == 0, 1.0, 0.0)
    for g in range(A_KV_HEADS):
        slab = kwin[:, (g // 2) * LANES:(g // 2 + 1) * LANES]
        own = (lane < HALF) if g % 2 == 0 else (lane >= HALF)
        kz = jnp.where(own, slab, 0.0)
        kop_ref[g, 0] = kz.astype(BF16)
        kop_ref[g, 1] = pltpu.roll(kz, HALF, 1).astype(BF16)
        vt_ref[g] = jnp.concatenate(
            [vwin_t[g * HEAD_DIM:(g + 1) * HEAD_DIM], ones_rows], axis=0).astype(BF16)

    t_win = lax.broadcasted_iota(jnp.int32, (1, blk), 1).astype(F32) + float(blk)
    slopes = _alibi_slopes_log2()
    sinks = [sinks_ref[h] * LOG2E + slopes[h] * t_win for h in range(A_Q_HEADS)]
    key_row = lax.broadcasted_iota(jnp.int32, (2 * blk, blk), 0)

    items = [(i, g) for i in range(SWA_QB) for g in range(A_KV_HEADS)]
    keys_of = lambda i: slice(i * blk, (i + 2) * blk)

    def stage_scores(n):
        i, g = items[n]
        for k in range(A_GROUP):
            h = g * A_GROUP + k
            q_slab = q_ref[i * blk:(i + 1) * blk, (h // 2) * LANES:(h // 2 + 1) * LANES]
            s_ref[n % 2, k] = _dot_nt(kop_ref[g, (h + g) % 2, keys_of(i), :], q_slab)

    def stage_softmax(n):
        i, g = items[n]
        for k in range(A_GROUP):
            h = g * A_GROUP + k
            st = s_ref[n % 2, k] + bias_ref[h]
            if i == 0:
                st = jnp.where(((step * SWA_QB - 1) * blk + key_row) >= 0, st, NEG)
            m = jnp.maximum(jnp.max(st, axis=0, keepdims=True), sinks[h])
            p_ref[n % 2, k] = jnp.exp2(st - m).astype(BF16)
            sh_ref[n % 2, k] = jnp.exp2(sinks[h] - m)

    def stage_values(n):
        i, g = items[n]
        for k in range(A_GROUP):
            h = g * A_GROUP + k
            ot = jnp.dot(vt_ref[g, :, keys_of(i)], p_ref[n % 2, k],
                         preferred_element_type=F32)
            ot_ref[h * HEAD_DIM:(h + 1) * HEAD_DIM, :] = (
                ot[0:HEAD_DIM] / (ot[HEAD_DIM:HEAD_DIM + 1] + sh_ref[n % 2, k]))
        if g == A_KV_HEADS - 1:
            o_ref[i * blk:(i + 1) * blk, :] = ot_ref[...].T.astype(BF16)

    for n in range(len(items) + 2):
        if n < len(items):
            stage_scores(n)
        if 2 <= n:
            stage_values(n - 2)
        if 1 <= n <= len(items):
            stage_softmax(n - 1)


def _swa(sinks, bias, proj, cast_weights, batch, seq):
    blk = WINDOW
    nb = seq // blk
    steps = nb // SWA_QB
    n_steps = batch * steps
    qcol = _col_block("qA", A_WIDTH)
    kcol = _col_block("kA", A_KV_WIDTH)
    vcol = _col_block("vA", A_KV_WIDTH)
    cur = lambda b, s: b * steps + s
    prev = lambda b, s: b * nb + jnp.maximum(s * SWA_QB - 1, 0)
    slab_specs = []
    for w in cast_weights:
        assert w.shape[0] % (16 * n_steps) == 0
        slab_specs.append(pl.BlockSpec((w.shape[0] // n_steps, w.shape[1]),
                                       lambda b, s: (cur(b, s), 0)))
    outs = pl.pallas_call(
        functools.partial(_swa_kernel, len(cast_weights)),
        grid=(batch, steps),
        in_specs=[
            pl.BlockSpec(memory_space=pltpu.SMEM),
            pl.BlockSpec(bias.shape, lambda b, s: (0, 0, 0)),
            pl.BlockSpec((SWA_QB * blk, A_WIDTH), lambda b, s: (cur(b, s), qcol)),
            pl.BlockSpec((blk, A_KV_WIDTH), lambda b, s: (prev(b, s), kcol)),
            pl.BlockSpec((SWA_QB * blk, A_KV_WIDTH), lambda b, s: (cur(b, s), kcol)),
            pl.BlockSpec((blk, A_KV_WIDTH), lambda b, s: (prev(b, s), vcol)),
            pl.BlockSpec((SWA_QB * blk, A_KV_WIDTH), lambda b, s: (cur(b, s), vcol)),
            *slab_specs,
        ],
        out_specs=[pl.BlockSpec((SWA_QB * blk, A_WIDTH), lambda b, s: (cur(b, s), 0)),
                   *slab_specs],
        out_shape=[jax.ShapeDtypeStruct((batch * seq, A_WIDTH), BF16),
                   *[jax.ShapeDtypeStruct(w.shape, BF16) for w in cast_weights]],
        scratch_shapes=[
            pltpu.VMEM((A_KV_HEADS, 2, SWA_WIN, LANES), BF16),
            pltpu.VMEM((A_KV_HEADS, VT_ROWS, SWA_WIN), BF16),
            pltpu.VMEM((2, A_GROUP, 2 * blk, blk), F32),
            pltpu.VMEM((2, A_GROUP, 2 * blk, blk), BF16),
            pltpu.VMEM((2, A_GROUP, 1, blk), F32),
            pltpu.VMEM((A_WIDTH, blk), F32),
        ],
        compiler_params=pltpu.CompilerParams(
            dimension_semantics=("arbitrary", "arbitrary"), vmem_limit_bytes=VMEM_LIMIT),
        name="swa",
    )(sinks, bias, proj, proj, proj, proj, proj, *cast_weights)
    return outs[0], outs[1:]


FOX_T = 512


FOX_H = FOX_T // 2


def _fox_kernel(q_ref, k_ref, v_ref, aug_ref, o_ref,
                kop_ref, vt_ref, qop_ref, s_ref, cm_ref, p_ref, al_ref, m_ref, acc_ref):
    seq = k_ref.shape[0]
    t = FOX_T
    hk = FOX_H

    lane = lax.broadcasted_iota(jnp.int32, (hk, LANES), 1)
    low = lane < HALF

    def build(r):
        sl = slice(r * hk, (r + 1) * hk)
        kk = k_ref[sl, :].astype(F32)
        aa = aug_ref[sl, :].astype(F32)
        vv = v_ref[sl, :].astype(F32)
        kop_ref[0, sl, :] = jnp.where(low, kk, aa).astype(BF16)
        kop_ref[1, sl, :] = jnp.where(low, aa, kk).astype(BF16)
        vvt = vv.T
        ones_rows = jnp.where(
            lax.broadcasted_iota(jnp.int32, (VT_ROWS - HALF, hk), 0) == 0, 1.0, 0.0)
        for hh in range(2):
            vt_ref[hh, r] = jnp.concatenate(
                [vvt[hh * HALF:(hh + 1) * HALF], ones_rows], axis=0).astype(BF16)

    def scores(slot, half, masked):
        ks = pl.ds(pl.multiple_of(half * hk, hk), hk)
        for hh in range(2):
            st = _dot_nt(kop_ref[hh, ks, :], qop_ref[hh])
            if masked:
                key = lax.broadcasted_iota(jnp.int32, (hk, t), 0) + slot * hk
                qry = lax.broadcasted_iota(jnp.int32, (hk, t), 1)
                st = jnp.where(key <= qry, st, NEG)
            s_ref[slot, hh] = st
            cm_ref[slot, hh] = jnp.max(st, axis=0, keepdims=True)

    def soft(slot):
        for hh in range(2):
            m_old = m_ref[hh]
            m_new = jnp.maximum(m_old, cm_ref[slot, hh])
            al_ref[slot, hh] = jnp.exp2(m_old - m_new)
            p_ref[slot, hh] = jnp.exp2(s_ref[slot, hh] - m_new).astype(BF16)
            m_ref[hh] = m_new

    def pv(slot, half):
        for hh in range(2):
            acc_ref[hh] = acc_ref[hh] * al_ref[slot, hh] + jnp.dot(
                vt_ref[hh, half], p_ref[slot, hh], preferred_element_type=F32)

    def step(i, prev, has_next, in_loop=True):
        if in_loop:
            pv(0, 2 * prev)
            soft(0)
            scores(1, 2 * i + 1, False)
            if has_next:
                scores(0, 2 * i + 2, False)
            pv(1, 2 * prev + 1)
            soft(1)
        else:
            scores(1, 2 * i + 1, False)
            pv(0, 2 * prev)
            soft(0)
            pv(1, 2 * prev + 1)
            if has_next:
                scores(0, 2 * i + 2, False)
            soft(1)

    lane_q = lax.broadcasted_iota(jnp.int32, (t, LANES), 1)
    low_q = lane_q < HALF
    ones0 = jnp.where((lane_q >= HALF) & (lane_q < HALF + N_PIECES), 1.0, 0.0)
    ones1 = jnp.where(lane_q < N_PIECES, 1.0, 0.0)

    for qi in range(seq // t):
        build(2 * qi)
        build(2 * qi + 1)
        rows = slice(qi * t, (qi + 1) * t)
        q = q_ref[rows, :].astype(F32)
        qop_ref[0] = jnp.where(low_q, q, ones0).astype(BF16)
        qop_ref[1] = jnp.where(low_q, ones1, q).astype(BF16)
        m_ref[...] = jnp.full(m_ref.shape, NEG, F32)
        acc_ref[...] = jnp.zeros(acc_ref.shape, F32)

        scores(0, 2 * qi, True)
        scores(1, 2 * qi + 1, True)
        soft(0)
        if qi > 0:
            scores(0, 0, False)
        soft(1)

        loop_pairs = max(qi - 1, 0) // 2

        def pair(j, carry, qi=qi):
            first = 2 * j
            step(first, jnp.where(j == 0, qi, first - 1), True)
            step(first + 1, first, True)
            return carry

        if loop_pairs > 1:
            lax.fori_loop(0, loop_pairs, pair, 0)
        static_from = 2 * loop_pairs if loop_pairs > 1 else 0
        for i in range(static_from, qi):
            step(i, i - 1 if i > 0 else qi, i + 1 < qi, in_loop=False)
        last = qi - 1 if qi > 0 else qi
        pv(0, 2 * last)
        pv(1, 2 * last + 1)

        ot = jnp.concatenate(
            [acc_ref[hh, 0:HALF, :] / acc_ref[hh, HALF:HALF + 1, :] for hh in range(2)], axis=0)
        o_ref[rows, :] = ot.T.astype(BF16)


def _fox(proj, aug, batch, seq):
    t = FOX_T
    nq = seq // t
    pairs = B_HEADS // 2
    qcol = _col_block("qB", LANES)
    kcol = _col_block("kB", LANES)
    vcol = _col_block("vB", LANES)
    return pl.pallas_call(
        _fox_kernel,
        grid=(batch, pairs),
        in_specs=[
            pl.BlockSpec((seq, LANES), lambda b, p: (b, qcol + p)),
            pl.BlockSpec((seq, LANES), lambda b, p: (b, kcol + p)),
            pl.BlockSpec((seq, LANES), lambda b, p: (b, vcol + p)),
            pl.BlockSpec((seq, LANES), lambda b, p: (b, p)),
        ],
        out_specs=pl.BlockSpec((seq, LANES), lambda b, p: (b, p)),
        out_shape=jax.ShapeDtypeStruct((batch * seq, B_WIDTH), BF16),
        scratch_shapes=[
            pltpu.VMEM((2, seq, LANES), BF16),
            pltpu.VMEM((2, seq // FOX_H, VT_ROWS, FOX_H), BF16),
            pltpu.VMEM((2, t, LANES), BF16),
            pltpu.VMEM((2, 2, FOX_H, t), F32),
            pltpu.VMEM((2, 2, 1, t), F32),
            pltpu.VMEM((2, 2, FOX_H, t), BF16),
            pltpu.VMEM((2, 2, 1, t), F32),
            pltpu.VMEM((2, 1, t), F32),
            pltpu.VMEM((2, VT_ROWS, t), F32),
        ],
        compiler_params=pltpu.CompilerParams(
            dimension_semantics=("arbitrary", "arbitrary"),
            vmem_limit_bytes=VMEM_LIMIT),
        name="fox",
    )(proj, proj, proj, aug)


MERGE_TM = 256


Z_BLOCK = 256
Z_PARTS = A_WIDTH // Z_BLOCK
assert A_WIDTH == B_WIDTH and A_WIDTH % Z_BLOCK == 0


def _merge_kernel(x_ref, ya_ref, yb_ref, mk_ref, mv_ref, qc_ref, *refs):
    za_refs = refs[:Z_PARTS]
    zb_refs = refs[Z_PARTS:2 * Z_PARTS]
    zc_ref, g0_ref, g1_ref, g2_ref, wa_ref, wb_ref, wc_ref, wo_ref, o_ref = refs[2 * Z_PARTS:]

    def memory_attention():
        outs = []
        for h in range(C_HEADS):
            sl = slice(h * C_HEAD_DIM, (h + 1) * C_HEAD_DIM)
            s = _dot_nt(qc_ref[:, sl], mk_ref[0, :, sl])
            p = jnp.exp(s - jnp.max(s, axis=-1, keepdims=True))
            o = jnp.dot(p.astype(BF16), mv_ref[0, :, sl], preferred_element_type=F32)
            outs.append(o / jnp.sum(p, axis=-1, keepdims=True))
        return jnp.concatenate(outs, axis=1)

    def branch(y, z_refs, w_ref):
        z = jnp.concatenate([r[...] for r in z_refs], axis=1).astype(F32)
        h = (y * (z * _sigmoid(z))).astype(BF16)
        return jnp.dot(h, w_ref[...], preferred_element_type=F32)

    y = _sigmoid(g0_ref[...].astype(F32)) * branch(ya_ref[...].astype(F32), za_refs, wa_ref)
    y = y + _sigmoid(g1_ref[...].astype(F32)) * branch(yb_ref[...].astype(F32), zb_refs, wb_ref)
    y = y + _sigmoid(g2_ref[...].astype(F32)) * branch(memory_attention(), (zc_ref,), wc_ref)
    o_ref[...] = x_ref[...] + jnp.dot(y.astype(BF16), wo_ref[...], preferred_element_type=F32)


def _merge(x2, ya, yb, mk, mv, proj, wa, wb, wc, wo, seq):
    t, d = x2.shape
    tm = MERGE_TM
    tiles_per_seq = seq // tm
    gcol = _col_block("g", D_MODEL)
    row = lambda width, col: pl.BlockSpec((tm, width), lambda i: (i, col))
    full = lambda a: pl.BlockSpec(a.shape, lambda i: (0, 0), pipeline_mode=pl.Buffered(1))
    mem = lambda a: pl.BlockSpec((1,) + a.shape[1:], lambda i: (i // tiles_per_seq, 0, 0))
    z_specs = lambda name: [row(Z_BLOCK, _col_block(name, Z_BLOCK, part)) for part in range(Z_PARTS)]
    n_proj = 2 * Z_PARTS + 5
    return pl.pallas_call(
        _merge_kernel,
        grid=(t // tm,),
        in_specs=[
            row(d, 0), row(A_WIDTH, 0), row(B_WIDTH, 0), mem(mk), mem(mv),
            row(C_WIDTH, _col_block("qC", C_WIDTH)),
            *z_specs("zA"), *z_specs("zB"), row(C_WIDTH, _col_block("zC", C_WIDTH)),
            row(d, gcol), row(d, gcol + 1), row(d, gcol + 2),
            full(wa), full(wb), full(wc), full(wo),
        ],
        out_specs=pl.BlockSpec((tm, d), lambda i: (i, 0)),
        out_shape=jax.ShapeDtypeStruct((t, d), F32),
        compiler_params=pltpu.CompilerParams(
            dimension_semantics=("arbitrary",), vmem_limit_bytes=VMEM_LIMIT),
        name="merge",
    )(x2, ya, yb, mk, mv, *([proj] * n_proj), wa, wb, wc, wo)


def _group_mean_matrices():
    lane = np.arange(2 * LANES)
    chunk = lane // LANES
    mats = []
    for tile in NORM_TILES:
        for c in range(0, IN_CHUNKS, 2):
            dims = np.asarray([_NORM_KIND.get(_chunk_piece(tile * IN_TN + (c + k) * LANES), LANES)
                               for k in range(2)])[chunk]
            same = ((lane[:, None] // dims[:, None] == lane[None, :] // dims[None, :])
                    & (chunk[:, None] == chunk[None, :]))
            mats.append(same.astype(np.float32) / dims[None, :])
    return np.stack(mats)


def _column_scales(gains):
    scale = {"qA": HEAD_DIM ** -0.5 * LOG2E, "qB": HEAD_DIM ** -0.5 * LOG2E,
             "qC": C_HEAD_DIM ** -0.5}
    cols = []
    for tile in NORM_TILES:
        for c in range(IN_CHUNKS):
            piece = _chunk_piece(tile * IN_TN + c * LANES)
            if piece in _NORM_KIND:
                g = gains[piece].astype(F32) * scale.get(piece, 1.0)
                cols.append(jnp.tile(g, LANES // g.shape[0]))
            else:
                cols.append(jnp.zeros((LANES,), F32))
    return jnp.concatenate(cols).reshape(1, len(NORM_TILES) * IN_TN)


def _layer(x, mem, norm_gain, mem_norm_gain, w_rows, b_forget, q_gain_a, k_gain_a, sinks_a,
           q_gain_b, k_gain_b, q_gain_c, k_gain_c, w_mem_kv, w_branch_a, w_branch_b,
           w_branch_c, w_out):
    batch, seq, d = x.shape
    x2 = x.reshape(batch * seq, d)

    w_f = w_rows[F_START * K_CHUNKS:(F_START + F_SHIFT) * K_CHUNKS].reshape(F_SHIFT, d)
    w_f = jnp.pad(w_f, ((0, LANES - F_SHIFT), (0, 0))).astype(BF16)
    b_f = jnp.pad(b_forget.astype(F32), (0, LANES - F_SHIFT)).reshape(1, LANES)
    cscale = _column_scales({"qA": q_gain_a, "kA": k_gain_a, "qB": q_gain_b, "kB": k_gain_b,
                             "qC": q_gain_c})
    gmat = jnp.asarray(_group_mean_matrices(), BF16)
    place = jnp.asarray(_placement_matrix(), BF16)

    mk, mv, w_first = _mem_kv(mem, mem_norm_gain.reshape(1, d), w_mem_kv,
                              k_gain_c.reshape(1, C_HEAD_DIM), w_rows)
    hn, aug = _norm_x(x2, norm_gain.reshape(1, d), w_f, b_f, place, seq)
    proj = _in_proj(hn, w_rows, w_first, cscale, gmat)
    ya, merge_weights = _swa(sinks_a.astype(F32), jnp.asarray(_swa_key_bias(), F32), proj,
                             (w_branch_a, w_branch_b, w_branch_c, w_out), batch, seq)
    yb = _fox(proj, aug, batch, seq)
    out = _merge(x2, ya, yb, mk, mv, proj, *merge_weights, seq)
    return out.reshape(batch, seq, d)


def kernel(x, mem, norm_gain, mem_norm_gain, w_in, b_forget, q_gain_a, k_gain_a, sinks_a,
           q_gain_b, k_gain_b, q_gain_c, k_gain_c, w_mem_kv, w_branch_a, w_branch_b,
           w_branch_c, w_out):
    depth = norm_gain.shape[0]
    w_rows = jnp.swapaxes(w_in, 1, 2).reshape(depth, -1, LANES)
    for layer in range(depth):
        x = _layer(x, mem, norm_gain[layer], mem_norm_gain[layer], w_rows[layer], b_forget[layer],
                   q_gain_a[layer], k_gain_a[layer], sinks_a[layer], q_gain_b[layer],
                   k_gain_b[layer], q_gain_c[layer], k_gain_c[layer], w_mem_kv[layer],
                   w_branch_a[layer], w_branch_b[layer], w_branch_c[layer], w_out[layer])
    return x
```

```python
import functools

import jax
import jax.numpy as jnp
import numpy as np
from jax import lax
from jax.experimental import pallas as pl
from jax.experimental.pallas import tpu as pltpu

F32 = jnp.float32
BF16 = jnp.bfloat16

D_MODEL = 2048
HEAD_DIM = 64
A_Q_HEADS = 12
A_KV_HEADS = 4
A_GROUP = A_Q_HEADS // A_KV_HEADS
WINDOW = 128
B_HEADS = 12
C_HEADS = 4
C_HEAD_DIM = 128
A_WIDTH = A_Q_HEADS * HEAD_DIM
A_KV_WIDTH = A_KV_HEADS * HEAD_DIM
B_WIDTH = B_HEADS * HEAD_DIM
C_WIDTH = C_HEADS * C_HEAD_DIM
EPS = 1e-6
NEG = -1e30

LANES = 128
HALF = LANES // 2
VT_ROWS = HALF + 16
LOG2E = float(np.log2(np.e))
N_PIECES = 3
VMEM_LIMIT = 56 * 1024 * 1024

_SRC = {}
_off = 0
for _name, _w in (("qA", A_WIDTH), ("kA", A_KV_WIDTH), ("vA", A_KV_WIDTH), ("zA", A_WIDTH),
                  ("qB", B_WIDTH), ("kB", B_WIDTH), ("vB", B_WIDTH), ("zB", B_WIDTH),
                  ("fB", B_HEADS), ("qC", C_WIDTH), ("zC", C_WIDTH), ("g", 3 * D_MODEL)):
    _SRC[_name] = (_off, _w)
    _off += _w

F_START, F_SHIFT = _SRC["fB"]
_DST = {n: (o if o < F_START else o - F_SHIFT) for n, (o, _) in _SRC.items() if n != "fB"}
PROJ_WIDTH = _SRC["g"][0] + _SRC["g"][1] - F_SHIFT
assert all(o % LANES == 0 for o in _DST.values()) and F_START % LANES == 0


def _col_block(name, width, part=0):
    assert _DST[name] % width == 0
    return _DST[name] // width + part


def _rms(x, gain):
    ms = jnp.mean(x * x, axis=-1, keepdims=True)
    return x * lax.rsqrt(ms + EPS) * gain


def _sigmoid(t):
    return 0.5 * (jnp.tanh(0.5 * t) + 1.0)


def _dot_nt(a, b):
    return lax.dot_general(a, b, (((1,), (1,)), ((), ())), preferred_element_type=F32)


def _mem_kv_kernel(mem_ref, gain_ref, w_ref, kgain_ref, wrows_ref, mk_ref, mv_ref, w0_ref):
    @pl.when(pl.program_id(0) == 0)
    def _():
        _gather_weight_rows(wrows_ref, w0_ref, 0, slice(None), IN_TN)

    hn = _rms(mem_ref[0], gain_ref[...]).astype(BF16)
    kv = jnp.dot(hn, w_ref[...].astype(BF16), preferred_element_type=F32)
    for h in range(C_HEADS):
        sl = slice(h * C_HEAD_DIM, (h + 1) * C_HEAD_DIM)
        mk_ref[0, :, sl] = _rms(kv[:, sl], kgain_ref[...]).astype(BF16)
    mv_ref[0] = kv[:, C_WIDTH:].astype(BF16)


def _mem_kv(mem, gain, w, kgain, w_rows):
    b, m, d = mem.shape
    return pl.pallas_call(
        _mem_kv_kernel,
        grid=(b,),
        in_specs=[
            pl.BlockSpec((1, m, d), lambda i: (i, 0, 0)),
            pl.BlockSpec((1, d), lambda i: (0, 0)),
            pl.BlockSpec((d, 2 * C_WIDTH), lambda i: (0, 0)),
            pl.BlockSpec((1, C_HEAD_DIM), lambda i: (0, 0)),
            pl.BlockSpec((pl.Element(IN_TN * K_CHUNKS), pl.Element(LANES)),
                         lambda i: _w_index(0 * i)),
        ],
        out_specs=[
            pl.BlockSpec((1, m, C_WIDTH), lambda i: (i, 0, 0)),
            pl.BlockSpec((1, m, C_WIDTH), lambda i: (i, 0, 0)),
            pl.BlockSpec((IN_TN, D_MODEL), lambda i: (0, 0)),
        ],
        out_shape=[jax.ShapeDtypeStruct((b, m, C_WIDTH), BF16)] * 2
        + [jax.ShapeDtypeStruct((IN_TN, D_MODEL), BF16)],
        compiler_params=pltpu.CompilerParams(
            dimension_semantics=("arbitrary",), vmem_limit_bytes=VMEM_LIMIT),
        name="mem_kv",
    )(mem, gain, w, kgain, w_rows)


NX_TM = 1024


def _norm_x_kernel(tiles_per_seq, x_ref, gain_ref, wf_ref, bf_ref, place_ref, hn_ref, aug_ref,
                   carry_ref):
    i = pl.program_id(0)

    @pl.when(i % tiles_per_seq == 0)
    def _():
        carry_ref[...] = jnp.zeros_like(carry_ref)

    hn = _rms(x_ref[...], gain_ref[...]).astype(BF16)
    hn_ref[...] = hn
    f_logit = _dot_nt(hn, wf_ref[...]) + bf_ref[...]
    x = jnp.minimum(f_logit, 0.0) - jnp.log1p(jnp.exp(-jnp.abs(f_logit)))
    row = lax.broadcasted_iota(jnp.int32, x.shape, 0)
    shift = 1
    while shift < NX_TM:
        x = x + jnp.where(row >= shift, pltpu.roll(x, shift, 0), 0.0)
        shift *= 2
    c = x + carry_ref[...]
    carry_ref[...] = c[NX_TM - 1:NX_TM, :]
    rest = c * (-LOG2E)
    pieces = []
    for _ in range(N_PIECES):
        p = rest.astype(BF16)
        pieces.append(p)
        rest = rest - p.astype(F32)
    stacked = jnp.concatenate(pieces, axis=1)
    aug_ref[...] = jnp.dot(stacked, place_ref[...], preferred_element_type=F32).astype(BF16)


def _norm_x(x2, gain, wf, bfg, place, seq):
    t, d = x2.shape
    return pl.pallas_call(
        functools.partial(_norm_x_kernel, seq // NX_TM),
        grid=(t // NX_TM,),
        in_specs=[
            pl.BlockSpec((NX_TM, d), lambda i: (i, 0)),
            pl.BlockSpec((1, d), lambda i: (0, 0)),
            pl.BlockSpec((LANES, d), lambda i: (0, 0)),
            pl.BlockSpec((1, LANES), lambda i: (0, 0)),
            pl.BlockSpec((N_PIECES * LANES, B_WIDTH), lambda i: (0, 0)),
        ],
        out_specs=[
            pl.BlockSpec((NX_TM, d), lambda i: (i, 0)),
            pl.BlockSpec((NX_TM, B_WIDTH), lambda i: (i, 0)),
        ],
        out_shape=[jax.ShapeDtypeStruct((t, d), BF16),
                   jax.ShapeDtypeStruct((t, B_WIDTH), BF16)],
        scratch_shapes=[pltpu.VMEM((1, LANES), F32)],
        compiler_params=pltpu.CompilerParams(
            dimension_semantics=("arbitrary",), vmem_limit_bytes=VMEM_LIMIT),
        name="norm_x",
    )(x2, gain, wf, bfg, place)


IN_TM = 1024
IN_TN = 1024
IN_CHUNKS = IN_TN // LANES
K_CHUNKS = D_MODEL // LANES
N_IN_TILES = PROJ_WIDTH // IN_TN
N_ALIGNED_TILES = F_START // IN_TN
assert F_START % IN_TN == 0 and PROJ_WIDTH % IN_TN == 0 and F_SHIFT < LANES

_NORM_KIND = {"qA": HEAD_DIM, "kA": HEAD_DIM, "qB": HEAD_DIM, "kB": HEAD_DIM, "qC": C_HEAD_DIM}


def _chunk_piece(col):
    for name, off in _DST.items():
        if off <= col < off + _SRC[name][1]:
            return name
    raise ValueError(col)


def _norm_prefix(tile):
    kinds = [_chunk_piece(tile * IN_TN + c * LANES) in _NORM_KIND for c in range(IN_CHUNKS)]
    n = sum(kinds)
    assert kinds == [True] * n + [False] * (IN_CHUNKS - n)
    return n


NORM_PREFIX = tuple(_norm_prefix(j) for j in range(N_IN_TILES))
NORM_TILES = tuple(j for j in range(N_IN_TILES) if NORM_PREFIX[j])


def _norm_slot(j):
    slot = 0
    for tile in NORM_TILES[1:]:
        slot = slot + (j >= tile).astype(jnp.int32)
    return slot


IN_MCH = 512


def _gather_weight_rows(w_ref, dst_ref, src_row, dst_rows, n_rows):
    for c in range(K_CHUNKS):
        dst_ref[dst_rows, c * LANES:(c + 1) * LANES] = (
            w_ref[pl.ds(src_row + c, n_rows, stride=K_CHUNKS), :].astype(BF16))


def _w_index(tile):
    col = tile * IN_TN + jnp.where(tile >= N_ALIGNED_TILES, F_SHIFT, 0)
    return (col * K_CHUNKS, 0)


def _in_proj_kernel(n_row_tiles, hn_ref, w_ref, w0_ref, cscale_ref, gmat_ref, o_ref,
                    wbf_a, wbf_b, acc_ref):
    j = pl.program_id(0)
    i = pl.program_id(1)
    share = IN_TN // n_row_tiles

    @pl.when((j == 0) & (i == 0))
    def _():
        wbf_a[...] = w0_ref[...]

    def epilogue(chunk, prefix):
        rows = slice(chunk * IN_MCH, (chunk + 1) * IN_MCH)
        acc = acc_ref.at[chunk % 2]
        for c in range(prefix // 2):
            sl = slice(c * 2 * LANES, (c + 1) * 2 * LANES)
            a = acc[:, sl]
            ms = jnp.dot((a * a).astype(BF16), gmat_ref[c], preferred_element_type=F32)
            o_ref[rows, sl] = (a * lax.rsqrt(ms + EPS) * cscale_ref[:, sl]).astype(BF16)
        if prefix < IN_CHUNKS:
            rest = slice(prefix * LANES, IN_TN)
            o_ref[rows, rest] = acc[:, rest].astype(BF16)

    def step(cur, nxt, prefix):
        src = pl.multiple_of(i * (share * K_CHUNKS), share * K_CHUNKS)
        dst = pl.ds(pl.multiple_of(i * share, share), share)
        n_chunks = hn_ref.shape[0] // IN_MCH
        for c in range(n_chunks + 1):
            if c < n_chunks:
                acc_ref[c % 2] = _dot_nt(hn_ref[c * IN_MCH:(c + 1) * IN_MCH, :], cur[...])
            if c == 0:
                _gather_weight_rows(w_ref, nxt, src, dst, share)
            if c >= 1:
                epilogue(c - 1, prefix)

    def is_any(tiles):
        cond = j == tiles[0]
        for tile in tiles[1:]:
            cond = cond | (j == tile)
        return cond

    for parity, prefix in sorted({(t % 2, NORM_PREFIX[t]) for t in range(N_IN_TILES)}):
        tiles = [t for t in range(N_IN_TILES) if (t % 2, NORM_PREFIX[t]) == (parity, prefix)]
        cur, nxt = (wbf_a, wbf_b) if parity == 0 else (wbf_b, wbf_a)
        pl.when(is_any(tiles))(functools.partial(step, cur, nxt, prefix))


def _in_proj(hn, w_rows, w_first, cscale, gmat):
    t, d = hn.shape
    n_row_tiles = t // IN_TM
    assert IN_TN % n_row_tiles == 0 and (IN_TN // n_row_tiles) % 16 == 0
    return pl.pallas_call(
        functools.partial(_in_proj_kernel, n_row_tiles),
        grid=(N_IN_TILES, n_row_tiles),
        in_specs=[
            pl.BlockSpec((IN_TM, d), lambda j, i: (i, 0)),
            pl.BlockSpec((pl.Element(IN_TN * K_CHUNKS), pl.Element(LANES)),
                         lambda j, i: _w_index(jnp.minimum(j + 1, N_IN_TILES - 1))),
            pl.BlockSpec((IN_TN, d), lambda j, i: (0, 0), pipeline_mode=pl.Buffered(1)),
            pl.BlockSpec((1, IN_TN), lambda j, i: (0, _norm_slot(j))),
            pl.BlockSpec((IN_CHUNKS // 2, 2 * LANES, 2 * LANES),
                         lambda j, i: (_norm_slot(j), 0, 0)),
        ],
        out_specs=pl.BlockSpec((IN_TM, IN_TN), lambda j, i: (i, j)),
        out_shape=jax.ShapeDtypeStruct((t, PROJ_WIDTH), BF16),
        scratch_shapes=[
            pltpu.VMEM((IN_TN, d), BF16),
            pltpu.VMEM((IN_TN, d), BF16),
            pltpu.VMEM((2, IN_MCH, IN_TN), F32),
        ],
        compiler_params=pltpu.CompilerParams(
            dimension_semantics=("arbitrary", "arbitrary"), vmem_limit_bytes=VMEM_LIMIT),
        name="in_proj",
    )(hn, w_rows, w_first, cscale, gmat)


def _aug_lane(head):
    return HALF if head % 2 == 0 else 0


def _placement_matrix():
    place = np.zeros((N_PIECES * LANES, B_WIDTH), np.float32)
    for h in range(B_HEADS):
        for p in range(N_PIECES):
            place[p * LANES + h, (h // 2) * LANES + _aug_lane(h) + p] = 1.0
    return place


SWA_QB = 8
SWA_WIN = (SWA_QB + 1) * WINDOW


def _alibi_slopes_log2():
    return [float(2.0 ** (-8.0 * (h + 1) / A_Q_HEADS)) * LOG2E for h in range(A_Q_HEADS)]


def _swa_key_bias():
    key = np.arange(2 * WINDOW)[:, None]
    qry = np.arange(WINDOW)[None, :]
    rel = qry + WINDOW - key
    visible = (rel >= 0) & (rel < WINDOW)
    slopes = np.asarray(_alibi_slopes_log2(), np.float32)[:, None, None]
    return np.where(visible[None], slopes * key[None].astype(np.float32), np.float32(NEG))


def _swa_kernel(n_cast, sinks_ref, bias_ref, q_ref, kp_ref, kc_ref, vp_ref, vc_ref, *refs):
    cast_in, o_ref, cast_out = refs[:n_cast], refs[n_cast], refs[n_cast + 1:2 * n_cast + 1]
    kop_ref, vt_ref, s_ref, p_ref, sh_ref, ot_ref = refs[2 * n_cast + 1:]
    for src, dst in zip(cast_in, cast_out):
        dst[...] = src[...].astype(BF16)
    step = pl.program_id(1)
    blk = WINDOW
    lane = lax.broadcasted_iota(jnp.int32, (SWA_WIN, LANES), 1)

    kwin = jnp.concatenate([kp_ref[...], kc_ref[...]], axis=0).astype(F32)
    vwin_t = jnp.concatenate([vp_ref[...], vc_ref[...]], axis=0).astype(F32).T
    ones_rows = jnp.where(
        lax.broadcasted_iota(jnp.int32, (VT_ROWS - HALF, SWA_WIN), 0) == 0, 1.0, 0.0)
    for g in range(A_KV_HEADS):
        slab = kwin[:, (g // 2) * LANES:(g // 2 + 1) * LANES]
        own = (lane < HALF) if g % 2 == 0 else (lane >= HALF)
        kz = jnp.where(own, slab, 0.0)
        kop_ref[g, 0] = kz.astype(BF16)
        kop_ref[g, 1] = pltpu.roll(kz, HALF, 1).astype(BF16)
        vt_ref[g] = jnp.concatenate(
            [vwin_t[g * HEAD_DIM:(g + 1) * HEAD_DIM], ones_rows], axis=0).astype(BF16)

    t_win = lax.broadcasted_iota(jnp.int32, (1, blk), 1).astype(F32) + float(blk)
    slopes = _alibi_slopes_log2()
    sinks = [sinks_ref[h] * LOG2E + slopes[h] * t_win for h in range(A_Q_HEADS)]
    key_row = lax.broadcasted_iota(jnp.int32, (2 * blk, blk), 0)

    items = [(i, g) for i in range(SWA_QB) for g in range(A_KV_HEADS)]
    keys_of = lambda i: slice(i * blk, (i + 2) * blk)

    def stage_scores(n):
        i, g = items[n]
        for k in range(A_GROUP):
            h = g * A_GROUP + k
            q_slab = q_ref[i * blk:(i + 1) * blk, (h // 2) * LANES:(h // 2 + 1) * LANES]
            s_ref[n % 2, k] = _dot_nt(kop_ref[g, (h + g) % 2, keys_of(i), :], q_slab)

    def stage_softmax(n):
        i, g = items[n]
        for k in range(A_GROUP):
            h = g * A_GROUP + k
            st = s_ref[n % 2, k] + bias_ref[h]
            if i == 0:
                st = jnp.where(((step * SWA_QB - 1) * blk + key_row) >= 0, st, NEG)
            m = jnp.maximum(jnp.max(st, axis=0, keepdims=True), sinks[h])
            p_ref[n % 2, k] = jnp.exp2(st - m).astype(BF16)
            sh_ref[n % 2, k] = jnp.exp2(sinks[h] - m)

    def stage_values(n):
        i, g = items[n]
        for k in range(A_GROUP):
            h = g * A_GROUP + k
            ot = jnp.dot(vt_ref[g, :, keys_of(i)], p_ref[n % 2, k],
                         preferred_element_type=F32)
            ot_ref[h * HEAD_DIM:(h + 1) * HEAD_DIM, :] = (
                ot[0:HEAD_DIM] / (ot[HEAD_DIM:HEAD_DIM + 1] + sh_ref[n % 2, k]))
        if g == A_KV_HEADS - 1:
            o_ref[i * blk:(i + 1) * blk, :] = ot_ref[...].T.astype(BF16)

    for n in range(len(items) + 2):
        if n < len(items):
            stage_scores(n)
        if 2 <= n:
            stage_values(n - 2)
        if 1 <= n <= len(items):
            stage_softmax(n - 1)


def _swa(sinks, bias, proj, cast_weights, batch, seq):
    blk = WINDOW
    nb = seq // blk
    steps = nb // SWA_QB
    n_steps = batch * steps
    qcol = _col_block("qA", A_WIDTH)
    kcol = _col_block("kA", A_KV_WIDTH)
    vcol = _col_block("vA", A_KV_WIDTH)
    cur = lambda b, s: b * steps + s
    prev = lambda b, s: b * nb + jnp.maximum(s * SWA_QB - 1, 0)
    slab_specs = []
    for w in cast_weights:
        assert w.shape[0] % (16 * n_steps) == 0
        slab_specs.append(pl.BlockSpec((w.shape[0] // n_steps, w.shape[1]),
                                       lambda b, s: (cur(b, s), 0)))
    outs = pl.pallas_call(
        functools.partial(_swa_kernel, len(cast_weights)),
        grid=(batch, steps),
        in_specs=[
            pl.BlockSpec(memory_space=pltpu.SMEM),
            pl.BlockSpec(bias.shape, lambda b, s: (0, 0, 0)),
            pl.BlockSpec((SWA_QB * blk, A_WIDTH), lambda b, s: (cur(b, s), qcol)),
            pl.BlockSpec((blk, A_KV_WIDTH), lambda b, s: (prev(b, s), kcol)),
            pl.BlockSpec((SWA_QB * blk, A_KV_WIDTH), lambda b, s: (cur(b, s), kcol)),
            pl.BlockSpec((blk, A_KV_WIDTH), lambda b, s: (prev(b, s), vcol)),
            pl.BlockSpec((SWA_QB * blk, A_KV_WIDTH), lambda b, s: (cur(b, s), vcol)),
            *slab_specs,
        ],
        out_specs=[pl.BlockSpec((SWA_QB * blk, A_WIDTH), lambda b, s: (cur(b, s), 0)),
                   *slab_specs],
        out_shape=[jax.ShapeDtypeStruct((batch * seq, A_WIDTH), BF16),
                   *[jax.ShapeDtypeStruct(w.shape, BF16) for w in cast_weights]],
        scratch_shapes=[
            pltpu.VMEM((A_KV_HEADS, 2, SWA_WIN, LANES), BF16),
            pltpu.VMEM((A_KV_HEADS, VT_ROWS, SWA_WIN), BF16),
            pltpu.VMEM((2, A_GROUP, 2 * blk, blk), F32),
            pltpu.VMEM((2, A_GROUP, 2 * blk, blk), BF16),
            pltpu.VMEM((2, A_GROUP, 1, blk), F32),
            pltpu.VMEM((A_WIDTH, blk), F32),
        ],
        compiler_params=pltpu.CompilerParams(
            dimension_semantics=("arbitrary", "arbitrary"), vmem_limit_bytes=VMEM_LIMIT),
        name="swa",
    )(sinks, bias, proj, proj, proj, proj, proj, *cast_weights)
    return outs[0], outs[1:]


FOX_T = 512


FOX_H = FOX_T // 2


def _fox_kernel(q_ref, k_ref, v_ref, aug_ref, o_ref,
                kop_ref, vt_ref, qop_ref, s_ref, cm_ref, p_ref, al_ref, m_ref, acc_ref):
    seq = k_ref.shape[0]
    t = FOX_T
    hk = FOX_H

    lane = lax.broadcasted_iota(jnp.int32, (hk, LANES), 1)
    low = lane < HALF

    def build(r):
        sl = slice(r * hk, (r + 1) * hk)
        kk = k_ref[sl, :].astype(F32)
        aa = aug_ref[sl, :].astype(F32)
        vv = v_ref[sl, :].astype(F32)
        kop_ref[0, sl, :] = jnp.where(low, kk, aa).astype(BF16)
        kop_ref[1, sl, :] = jnp.where(low, aa, kk).astype(BF16)
        vvt = vv.T
        ones_rows = jnp.where(
            lax.broadcasted_iota(jnp.int32, (VT_ROWS - HALF, hk), 0) == 0, 1.0, 0.0)
        for hh in range(2):
            vt_ref[hh, r] = jnp.concatenate(
                [vvt[hh * HALF:(hh + 1) * HALF], ones_rows], axis=0).astype(BF16)

    def scores(slot, half, masked):
        ks = pl.ds(pl.multiple_of(half * hk, hk), hk)
        for hh in range(2):
            st = _dot_nt(kop_ref[hh, ks, :], qop_ref[hh])
            if masked:
                key = lax.broadcasted_iota(jnp.int32, (hk, t), 0) + slot * hk
                qry = lax.broadcasted_iota(jnp.int32, (hk, t), 1)
                st = jnp.where(key <= qry, st, NEG)
            s_ref[slot, hh] = st
            cm_ref[slot, hh] = jnp.max(st, axis=0, keepdims=True)

    def soft(slot):
        for hh in range(2):
            m_old = m_ref[hh]
            m_new = jnp.maximum(m_old, cm_ref[slot, hh])
            al_ref[slot, hh] = jnp.exp2(m_old - m_new)
            p_ref[slot, hh] = jnp.exp2(s_ref[slot, hh] - m_new).astype(BF16)
            m_ref[hh] = m_new

    def pv(slot, half):
        for hh in range(2):
            acc_ref[hh] = acc_ref[hh] * al_ref[slot, hh] + jnp.dot(
                vt_ref[hh, half], p_ref[slot, hh], preferred_element_type=F32)

    def step(i, prev, has_next, in_loop=True):
        if in_loop:
            pv(0, 2 * prev)
            soft(0)
            scores(1, 2 * i + 1, False)
            if has_next:
                scores(0, 2 * i + 2, False)
            pv(1, 2 * prev + 1)
            soft(1)
        else:
            scores(1, 2 * i + 1, False)
            pv(0, 2 * prev)
            soft(0)
            pv(1, 2 * prev + 1)
            if has_next:
                scores(0, 2 * i + 2, False)
            soft(1)

    lane_q = lax.broadcasted_iota(jnp.int32, (t, LANES), 1)
    low_q = lane_q < HALF
    ones0 = jnp.where((lane_q >= HALF) & (lane_q < HALF + N_PIECES), 1.0, 0.0)
    ones1 = jnp.where(lane_q < N_PIECES, 1.0, 0.0)

    for qi in range(seq // t):
        build(2 * qi)
        build(2 * qi + 1)
        rows = slice(qi * t, (qi + 1) * t)
        q = q_ref[rows, :].astype(F32)
        qop_ref[0] = jnp.where(low_q, q, ones0).astype(BF16)
        qop_ref[1] = jnp.where(low_q, ones1, q).astype(BF16)
        m_ref[...] = jnp.full(m_ref.shape, NEG, F32)
        acc_ref[...] = jnp.zeros(acc_ref.shape, F32)

        scores(0, 2 * qi, True)
        scores(1, 2 * qi + 1, True)
        soft(0)
        if qi > 0:
            scores(0, 0, False)
        soft(1)

        loop_pairs = max(qi - 1, 0) // 2

        def pair(j, carry, qi=qi):
            first = 2 * j
            step(first, jnp.where(j == 0, qi, first - 1), True)
            step(first + 1, first, True)
            return carry

        if loop_pairs > 1:
            lax.fori_loop(0, loop_pairs, pair, 0)
        static_from = 2 * loop_pairs if loop_pairs > 1 else 0
        for i in range(static_from, qi):
            step(i, i - 1 if i > 0 else qi, i + 1 < qi, in_loop=False)
        last = qi - 1 if qi > 0 else qi
        pv(0, 2 * last)
        pv(1, 2 * last + 1)

        ot = jnp.concatenate(
            [acc_ref[hh, 0:HALF, :] / acc_ref[hh, HALF:HALF + 1, :] for hh in range(2)], axis=0)
        o_ref[rows, :] = ot.T.astype(BF16)


def _fox(proj, aug, batch, seq):
    t = FOX_T
    nq = seq // t
    pairs = B_HEADS // 2
    qcol = _col_block("qB", LANES)
    kcol = _col_block("kB", LANES)
    vcol = _col_block("vB", LANES)
    return pl.pallas_call(
        _fox_kernel,
        grid=(batch, pairs),
        in_specs=[
            pl.BlockSpec((seq, LANES), lambda b, p: (b, qcol + p)),
            pl.BlockSpec((seq, LANES), lambda b, p: (b, kcol + p)),
            pl.BlockSpec((seq, LANES), lambda b, p: (b, vcol + p)),
            pl.BlockSpec((seq, LANES), lambda b, p: (b, p)),
        ],
        out_specs=pl.BlockSpec((seq, LANES), lambda b, p: (b, p)),
        out_shape=jax.ShapeDtypeStruct((batch * seq, B_WIDTH), BF16),
        scratch_shapes=[
            pltpu.VMEM((2, seq, LANES), BF16),
            pltpu.VMEM((2, seq // FOX_H, VT_ROWS, FOX_H), BF16),
            pltpu.VMEM((2, t, LANES), BF16),
            pltpu.VMEM((2, 2, FOX_H, t), F32),
            pltpu.VMEM((2, 2, 1, t), F32),
            pltpu.VMEM((2, 2, FOX_H, t), BF16),
            pltpu.VMEM((2, 2, 1, t), F32),
            pltpu.VMEM((2, 1, t), F32),
            pltpu.VMEM((2, VT_ROWS, t), F32),
        ],
        compiler_params=pltpu.CompilerParams(
            dimension_semantics=("arbitrary", "arbitrary"),
            vmem_limit_bytes=VMEM_LIMIT),
        name="fox",
    )(proj, proj, proj, aug)


MERGE_TM = 256


Z_BLOCK = 256
Z_PARTS = A_WIDTH // Z_BLOCK
assert A_WIDTH == B_WIDTH and A_WIDTH % Z_BLOCK == 0


def _merge_kernel(x_ref, ya_ref, yb_ref, mk_ref, mv_ref, qc_ref, *refs):
    za_refs = refs[:Z_PARTS]
    zb_refs = refs[Z_PARTS:2 * Z_PARTS]
    zc_ref, g0_ref, g1_ref, g2_ref, wa_ref, wb_ref, wc_ref, wo_ref, o_ref = refs[2 * Z_PARTS:]

    def memory_attention():
        outs = []
        for h in range(C_HEADS):
            sl = slice(h * C_HEAD_DIM, (h + 1) * C_HEAD_DIM)
            s = _dot_nt(qc_ref[:, sl], mk_ref[0, :, sl])
            p = jnp.exp(s - jnp.max(s, axis=-1, keepdims=True))
            o = jnp.dot(p.astype(BF16), mv_ref[0, :, sl], preferred_element_type=F32)
            outs.append(o / jnp.sum(p, axis=-1, keepdims=True))
        return jnp.concatenate(outs, axis=1)

    def branch(y, z_refs, w_ref):
        z = jnp.concatenate([r[...] for r in z_refs], axis=1).astype(F32)
        h = (y * (z * _sigmoid(z))).astype(BF16)
        return jnp.dot(h, w_ref[...], preferred_element_type=F32)

    y = _sigmoid(g0_ref[...].astype(F32)) * branch(ya_ref[...].astype(F32), za_refs, wa_ref)
    y = y + _sigmoid(g1_ref[...].astype(F32)) * branch(yb_ref[...].astype(F32), zb_refs, wb_ref)
    y = y + _sigmoid(g2_ref[...].astype(F32)) * branch(memory_attention(), (zc_ref,), wc_ref)
    o_ref[...] = x_ref[...] + jnp.dot(y.astype(BF16), wo_ref[...], preferred_element_type=F32)


def _merge(x2, ya, yb, mk, mv, proj, wa, wb, wc, wo, seq):
    t, d = x2.shape
    tm = MERGE_TM
    tiles_per_seq = seq // tm
    gcol = _col_block("g", D_MODEL)
    row = lambda width, col: pl.BlockSpec((tm, width), lambda i: (i, col))
    full = lambda a: pl.BlockSpec(a.shape, lambda i: (0, 0), pipeline_mode=pl.Buffered(1))
    mem = lambda a: pl.BlockSpec((1,) + a.shape[1:], lambda i: (i // tiles_per_seq, 0, 0))
    z_specs = lambda name: [row(Z_BLOCK, _col_block(name, Z_BLOCK, part)) for part in range(Z_PARTS)]
    n_proj = 2 * Z_PARTS + 5
    return pl.pallas_call(
        _merge_kernel,
        grid=(t // tm,),
        in_specs=[
            row(d, 0), row(A_WIDTH, 0), row(B_WIDTH, 0), mem(mk), mem(mv),
            row(C_WIDTH, _col_block("qC", C_WIDTH)),
            *z_specs("zA"), *z_specs("zB"), row(C_WIDTH, _col_block("zC", C_WIDTH)),
            row(d, gcol), row(d, gcol + 1), row(d, gcol + 2),
            full(wa), full(wb), full(wc), full(wo),
        ],
        out_specs=pl.BlockSpec((tm, d), lambda i: (i, 0)),
        out_shape=jax.ShapeDtypeStruct((t, d), F32),
        compiler_params=pltpu.CompilerParams(
            dimension_semantics=("arbitrary",), vmem_limit_bytes=VMEM_LIMIT),
        name="merge",
    )(x2, ya, yb, mk, mv, *([proj] * n_proj), wa, wb, wc, wo)


def _group_mean_matrices():
    lane = np.arange(2 * LANES)
    chunk = lane // LANES
    mats = []
    for tile in NORM_TILES:
        for c in range(0, IN_CHUNKS, 2):
            dims = np.asarray([_NORM_KIND.get(_chunk_piece(tile * IN_TN + (c + k) * LANES), LANES)
                               for k in range(2)])[chunk]
            same = ((lane[:, None] // dims[:, None] == lane[None, :] // dims[None, :])
                    & (chunk[:, None] == chunk[None, :]))
            mats.append(same.astype(np.float32) / dims[None, :])
    return np.stack(mats)


def _column_scales(gains):
    scale = {"qA": HEAD_DIM ** -0.5 * LOG2E, "qB": HEAD_DIM ** -0.5 * LOG2E,
             "qC": C_HEAD_DIM ** -0.5}
    cols = []
    for tile in NORM_TILES:
        for c in range(IN_CHUNKS):
            piece = _chunk_piece(tile * IN_TN + c * LANES)
            if piece in _NORM_KIND:
                g = gains[piece].astype(F32) * scale.get(piece, 1.0)
                cols.append(jnp.tile(g, LANES // g.shape[0]))
            else:
                cols.append(jnp.zeros((LANES,), F32))
    return jnp.concatenate(cols).reshape(1, len(NORM_TILES) * IN_TN)


def _layer(x, mem, norm_gain, mem_norm_gain, w_rows, b_forget, q_gain_a, k_gain_a, sinks_a,
           q_gain_b, k_gain_b, q_gain_c, k_gain_c, w_mem_kv, w_branch_a, w_branch_b,
           w_branch_c, w_out):
    batch, seq, d = x.shape
    x2 = x.reshape(batch * seq, d)

    w_f = w_rows[F_START * K_CHUNKS:(F_START + F_SHIFT) * K_CHUNKS].reshape(F_SHIFT, d)
    w_f = jnp.pad(w_f, ((0, LANES - F_SHIFT), (0, 0))).astype(BF16)
    b_f = jnp.pad(b_forget.astype(F32), (0, LANES - F_SHIFT)).reshape(1, LANES)
    cscale = _column_scales({"qA": q_gain_a, "kA": k_gain_a, "qB": q_gain_b, "kB": k_gain_b,
                             "qC": q_gain_c})
    gmat = jnp.asarray(_group_mean_matrices(), BF16)
    place = jnp.asarray(_placement_matrix(), BF16)

    mk, mv, w_first = _mem_kv(mem, mem_norm_gain.reshape(1, d), w_mem_kv,
                              k_gain_c.reshape(1, C_HEAD_DIM), w_rows)
    hn, aug = _norm_x(x2, norm_gain.reshape(1, d), w_f, b_f, place, seq)
    proj = _in_proj(hn, w_rows, w_first, cscale, gmat)
    ya, merge_weights = _swa(sinks_a.astype(F32), jnp.asarray(_swa_key_bias(), F32), proj,
                             (w_branch_a, w_branch_b, w_branch_c, w_out), batch, seq)
    yb = _fox(proj, aug, batch, seq)
    out = _merge(x2, ya, yb, mk, mv, proj, *merge_weights, seq)
    return out.reshape(batch, seq, d)


def kernel(x, mem, norm_gain, mem_norm_gain, w_in, b_forget, q_gain_a, k_gain_a, sinks_a,
           q_gain_b, k_gain_b, q_gain_c, k_gain_c, w_mem_kv, w_branch_a, w_branch_b,
           w_branch_c, w_out):
    depth = norm_gain.shape[0]
    w_rows = jnp.swapaxes(w_in, 1, 2).reshape(depth, -1, LANES)
    for layer in range(depth):
        x = _layer(x, mem, norm_gain[layer], mem_norm_gain[layer], w_rows[layer], b_forget[layer],
                   q_gain_a[layer], k_gain_a[layer], sinks_a[layer], q_gain_b[layer],
                   k_gain_b[layer], q_gain_c[layer], k_gain_c[layer], w_mem_kv[layer],
                   w_branch_a[layer], w_branch_b[layer], w_branch_c[layer], w_out[layer])
    return x
```

```python
import functools

import jax
import jax.numpy as jnp
import numpy as np
from jax import lax
from jax.experimental import pallas as pl
from jax.experimental.pallas import tpu as pltpu

F32 = jnp.float32
BF16 = jnp.bfloat16

D_MODEL = 2048
HEAD_DIM = 64
A_Q_HEADS = 12
A_KV_HEADS = 4
A_GROUP = A_Q_HEADS // A_KV_HEADS
WINDOW = 128
B_HEADS = 12
C_HEADS = 4
C_HEAD_DIM = 128
A_WIDTH = A_Q_HEADS * HEAD_DIM
A_KV_WIDTH = A_KV_HEADS * HEAD_DIM
B_WIDTH = B_HEADS * HEAD_DIM
C_WIDTH = C_HEADS * C_HEAD_DIM
EPS = 1e-6
NEG = -1e30

LANES = 128
HALF = LANES // 2
VT_ROWS = HALF + 16
LOG2E = float(np.log2(np.e))
N_PIECES = 3
VMEM_LIMIT = 56 * 1024 * 1024

_SRC = {}
_off = 0
for _name, _w in (("qA", A_WIDTH), ("kA", A_KV_WIDTH), ("vA", A_KV_WIDTH), ("zA", A_WIDTH),
                  ("qB", B_WIDTH), ("kB", B_WIDTH), ("vB", B_WIDTH), ("zB", B_WIDTH),
                  ("fB", B_HEADS), ("qC", C_WIDTH), ("zC", C_WIDTH), ("g", 3 * D_MODEL)):
    _SRC[_name] = (_off, _w)
    _off += _w

F_START, F_SHIFT = _SRC["fB"]
_DST = {n: (o if o < F_START else o - F_SHIFT) for n, (o, _) in _SRC.items() if n != "fB"}
PROJ_WIDTH = _SRC["g"][0] + _SRC["g"][1] - F_SHIFT
assert all(o % LANES == 0 for o in _DST.values()) and F_START % LANES == 0


def _col_block(name, width, part=0):
    assert _DST[name] % width == 0
    return _DST[name] // width + part


def _rms(x, gain):
    ms = jnp.mean(x * x, axis=-1, keepdims=True)
    return x * lax.rsqrt(ms + EPS) * gain


def _sigmoid(t):
    return 0.5 * (jnp.tanh(0.5 * t) + 1.0)


def _dot_nt(a, b):
    return lax.dot_general(a, b, (((1,), (1,)), ((), ())), preferred_element_type=F32)


def _mem_kv_kernel(mem_ref, gain_ref, w_ref, kgain_ref, wrows_ref, mk_ref, mv_ref, w0_ref):
    @pl.when(pl.program_id(0) == 0)
    def _():
        _gather_weight_rows(wrows_ref, w0_ref, 0, slice(None), IN_TN)

    hn = _rms(mem_ref[0], gain_ref[...]).astype(BF16)
    kv = jnp.dot(hn, w_ref[...].astype(BF16), preferred_element_type=F32)
    for h in range(C_HEADS):
        sl = slice(h * C_HEAD_DIM, (h + 1) * C_HEAD_DIM)
        mk_ref[0, :, sl] = _rms(kv[:, sl], kgain_ref[...]).astype(BF16)
    mv_ref[0] = kv[:, C_WIDTH:].astype(BF16)


def _mem_kv(mem, gain, w, kgain, w_rows):
    b, m, d = mem.shape
    return pl.pallas_call(
        _mem_kv_kernel,
        grid=(b,),
        in_specs=[
            pl.BlockSpec((1, m, d), lambda i: (i, 0, 0)),
            pl.BlockSpec((1, d), lambda i: (0, 0)),
            pl.BlockSpec((d, 2 * C_WIDTH), lambda i: (0, 0)),
            pl.BlockSpec((1, C_HEAD_DIM), lambda i: (0, 0)),
            pl.BlockSpec((pl.Element(IN_TN * K_CHUNKS), pl.Element(LANES)),
                         lambda i: _w_index(0 * i)),
        ],
        out_specs=[
            pl.BlockSpec((1, m, C_WIDTH), lambda i: (i, 0, 0)),
            pl.BlockSpec((1, m, C_WIDTH), lambda i: (i, 0, 0)),
            pl.BlockSpec((IN_TN, D_MODEL), lambda i: (0, 0)),
        ],
        out_shape=[jax.ShapeDtypeStruct((b, m, C_WIDTH), BF16)] * 2
        + [jax.ShapeDtypeStruct((IN_TN, D_MODEL), BF16)],
        compiler_params=pltpu.CompilerParams(
            dimension_semantics=("arbitrary",), vmem_limit_bytes=VMEM_LIMIT),
        name="mem_kv",
    )(mem, gain, w, kgain, w_rows)


NX_TM = 1024


def _norm_x_kernel(tiles_per_seq, x_ref, gain_ref, wf_ref, bf_ref, place_ref, hn_ref, aug_ref,
                   carry_ref):
    i = pl.program_id(0)

    @pl.when(i % tiles_per_seq == 0)
    def _():
        carry_ref[...] = jnp.zeros_like(carry_ref)

    hn = _rms(x_ref[...], gain_ref[...]).astype(BF16)
    hn_ref[...] = hn
    f_logit = _dot_nt(hn, wf_ref[...]) + bf_ref[...]
    x = jnp.minimum(f_logit, 0.0) - jnp.log1p(jnp.exp(-jnp.abs(f_logit)))
    row = lax.broadcasted_iota(jnp.int32, x.shape, 0)
    shift = 1
    while shift < NX_TM:
        x = x + jnp.where(row >= shift, pltpu.roll(x, shift, 0), 0.0)
        shift *= 2
    c = x + carry_ref[...]
    carry_ref[...] = c[NX_TM - 1:NX_TM, :]
    rest = c * (-LOG2E)
    pieces = []
    for _ in range(N_PIECES):
        p = rest.astype(BF16)
        pieces.append(p)
        rest = rest - p.astype(F32)
    stacked = jnp.concatenate(pieces, axis=1)
    aug_ref[...] = jnp.dot(stacked, place_ref[...], preferred_element_type=F32).astype(BF16)


def _norm_x(x2, gain, wf, bfg, place, seq):
    t, d = x2.shape
    return pl.pallas_call(
        functools.partial(_norm_x_kernel, seq // NX_TM),
        grid=(t // NX_TM,),
        in_specs=[
            pl.BlockSpec((NX_TM, d), lambda i: (i, 0)),
            pl.BlockSpec((1, d), lambda i: (0, 0)),
            pl.BlockSpec((LANES, d), lambda i: (0, 0)),
            pl.BlockSpec((1, LANES), lambda i: (0, 0)),
            pl.BlockSpec((N_PIECES * LANES, B_WIDTH), lambda i: (0, 0)),
        ],
        out_specs=[
            pl.BlockSpec((NX_TM, d), lambda i: (i, 0)),
            pl.BlockSpec((NX_TM, B_WIDTH), lambda i: (i, 0)),
        ],
        out_shape=[jax.ShapeDtypeStruct((t, d), BF16),
                   jax.ShapeDtypeStruct((t, B_WIDTH), BF16)],
        scratch_shapes=[pltpu.VMEM((1, LANES), F32)],
        compiler_params=pltpu.CompilerParams(
            dimension_semantics=("arbitrary",), vmem_limit_bytes=VMEM_LIMIT),
        name="norm_x",
    )(x2, gain, wf, bfg, place)


IN_TM = 1024
IN_TN = 1024
IN_CHUNKS = IN_TN // LANES
K_CHUNKS = D_MODEL // LANES
N_IN_TILES = PROJ_WIDTH // IN_TN
N_ALIGNED_TILES = F_START // IN_TN
assert F_START % IN_TN == 0 and PROJ_WIDTH % IN_TN == 0 and F_SHIFT < LANES

_NORM_KIND = {"qA": HEAD_DIM, "kA": HEAD_DIM, "qB": HEAD_DIM, "kB": HEAD_DIM, "qC": C_HEAD_DIM}


def _chunk_piece(col):
    for name, off in _DST.items():
        if off <= col < off + _SRC[name][1]:
            return name
    raise ValueError(col)


def _norm_prefix(tile):
    kinds = [_chunk_piece(tile * IN_TN + c * LANES) in _NORM_KIND for c in range(IN_CHUNKS)]
    n = sum(kinds)
    assert kinds == [True] * n + [False] * (IN_CHUNKS - n)
    return n


NORM_PREFIX = tuple(_norm_prefix(j) for j in range(N_IN_TILES))
NORM_TILES = tuple(j for j in range(N_IN_TILES) if NORM_PREFIX[j])


def _norm_slot(j):
    slot = 0
    for tile in NORM_TILES[1:]:
        slot = slot + (j >= tile).astype(jnp.int32)
    return slot


IN_MCH = 512


def _gather_weight_rows(w_ref, dst_ref, src_row, dst_rows, n_rows):
    for c in range(K_CHUNKS):
        dst_ref[dst_rows, c * LANES:(c + 1) * LANES] = (
            w_ref[pl.ds(src_row + c, n_rows, stride=K_CHUNKS), :].astype(BF16))


def _w_index(tile):
    col = tile * IN_TN + jnp.where(tile >= N_ALIGNED_TILES, F_SHIFT, 0)
    return (col * K_CHUNKS, 0)


def _in_proj_kernel(n_row_tiles, hn_ref, w_ref, w0_ref, cscale_ref, gmat_ref, o_ref,
                    wbf_a, wbf_b, acc_ref):
    j = pl.program_id(0)
    i = pl.program_id(1)
    share = IN_TN // n_row_tiles

    @pl.when((j == 0) & (i == 0))
    def _():
        wbf_a[...] = w0_ref[...]

    def epilogue(chunk, prefix):
        rows = slice(chunk * IN_MCH, (chunk + 1) * IN_MCH)
        acc = acc_ref.at[chunk % 2]
        for c in range(prefix // 2):
            sl = slice(c * 2 * LANES, (c + 1) * 2 * LANES)
            a = acc[:, sl]
            ms = jnp.dot((a * a).astype(BF16), gmat_ref[c], preferred_element_type=F32)
            o_ref[rows, sl] = (a * lax.rsqrt(ms + EPS) * cscale_ref[:, sl]).astype(BF16)
        if prefix < IN_CHUNKS:
            rest = slice(prefix * LANES, IN_TN)
            o_ref[rows, rest] = acc[:, rest].astype(BF16)

    def step(cur, nxt, prefix):
        src = pl.multiple_of(i * (share * K_CHUNKS), share * K_CHUNKS)
        dst = pl.ds(pl.multiple_of(i * share, share), share)
        n_chunks = hn_ref.shape[0] // IN_MCH
        for c in range(n_chunks + 1):
            if c < n_chunks:
                acc_ref[c % 2] = _dot_nt(hn_ref[c * IN_MCH:(c + 1) * IN_MCH, :], cur[...])
            if c == 0:
                _gather_weight_rows(w_ref, nxt, src, dst, share)
            if c >= 1:
                epilogue(c - 1, prefix)

    def is_any(tiles):
        cond = j == tiles[0]
        for tile in tiles[1:]:
            cond = cond | (j == tile)
        return cond

    for parity, prefix in sorted({(t % 2, NORM_PREFIX[t]) for t in range(N_IN_TILES)}):
        tiles = [t for t in range(N_IN_TILES) if (t % 2, NORM_PREFIX[t]) == (parity, prefix)]
        cur, nxt = (wbf_a, wbf_b) if parity == 0 else (wbf_b, wbf_a)
        pl.when(is_any(tiles))(functools.partial(step, cur, nxt, prefix))


def _in_proj(hn, w_rows, w_first, cscale, gmat):
    t, d = hn.shape
    n_row_tiles = t // IN_TM
    assert IN_TN % n_row_tiles == 0 and (IN_TN // n_row_tiles) % 16 == 0
    return pl.pallas_call(
        functools.partial(_in_proj_kernel, n_row_tiles),
        grid=(N_IN_TILES, n_row_tiles),
        in_specs=[
            pl.BlockSpec((IN_TM, d), lambda j, i: (i, 0)),
            pl.BlockSpec((pl.Element(IN_TN * K_CHUNKS), pl.Element(LANES)),
                         lambda j, i: _w_index(jnp.minimum(j + 1, N_IN_TILES - 1))),
            pl.BlockSpec((IN_TN, d), lambda j, i: (0, 0), pipeline_mode=pl.Buffered(1)),
            pl.BlockSpec((1, IN_TN), lambda j, i: (0, _norm_slot(j))),
            pl.BlockSpec((IN_CHUNKS // 2, 2 * LANES, 2 * LANES),
                         lambda j, i: (_norm_slot(j), 0, 0)),
        ],
        out_specs=pl.BlockSpec((IN_TM, IN_TN), lambda j, i: (i, j)),
        out_shape=jax.ShapeDtypeStruct((t, PROJ_WIDTH), BF16),
        scratch_shapes=[
            pltpu.VMEM((IN_TN, d), BF16),
            pltpu.VMEM((IN_TN, d), BF16),
            pltpu.VMEM((2, IN_MCH, IN_TN), F32),
        ],
        compiler_params=pltpu.CompilerParams(
            dimension_semantics=("arbitrary", "arbitrary"), vmem_limit_bytes=VMEM_LIMIT),
        name="in_proj",
    )(hn, w_rows, w_first, cscale, gmat)


def _aug_lane(head):
    return HALF if head % 2 == 0 else 0


def _placement_matrix():
    place = np.zeros((N_PIECES * LANES, B_WIDTH), np.float32)
    for h in range(B_HEADS):
        for p in range(N_PIECES):
            place[p * LANES + h, (h // 2) * LANES + _aug_lane(h) + p] = 1.0
    return place


SWA_QB = 16
SWA_WIN = (SWA_QB + 1) * WINDOW


def _alibi_slopes_log2():
    return [float(2.0 ** (-8.0 * (h + 1) / A_Q_HEADS)) * LOG2E for h in range(A_Q_HEADS)]


def _swa_key_bias():
    key = np.arange(2 * WINDOW)[:, None]
    qry = np.arange(WINDOW)[None, :]
    rel = qry + WINDOW - key
    visible = (rel >= 0) & (rel < WINDOW)
    slopes = np.asarray(_alibi_slopes_log2(), np.float32)[:, None, None]
    return np.where(visible[None], slopes * key[None].astype(np.float32), np.float32(NEG))


def _swa_kernel(n_cast, sinks_ref, bias_ref, q_ref, kp_ref, kc_ref, vp_ref, vc_ref, *refs):
    cast_in, o_ref, cast_out = refs[:n_cast], refs[n_cast], refs[n_cast + 1:2 * n_cast + 1]
    kop_ref, vt_ref, s_ref, p_ref, sh_ref, ot_ref = refs[2 * n_cast + 1:]
    for src, dst in zip(cast_in, cast_out):
        dst[...] = src[...].astype(BF16)
    step = pl.program_id(1)
    blk = WINDOW
    lane = lax.broadcasted_iota(jnp.int32, (SWA_WIN, LANES), 1)

    kwin = jnp.concatenate([kp_ref[...], kc_ref[...]], axis=0).astype(F32)
    vwin_t = jnp.concatenate([vp_ref[...], vc_ref[...]], axis=0).astype(F32).T
    ones_rows = jnp.where(
        lax.broadcasted_iota(jnp.int32, (VT_ROWS - HALF, SWA_WIN), 0) == 0, 1.0, 0.0)
    for g in range(A_KV_HEADS):
        slab = kwin[:, (g // 2) * LANES:(g // 2 + 1) * LANES]
        own = (lane < HALF) if g % 2 == 0 else (lane >= HALF)
        kz = jnp.where(own, slab, 0.0)
        kop_ref[g, 0] = kz.astype(BF16)
        kop_ref[g, 1] = pltpu.roll(kz, HALF, 1).astype(BF16)
        vt_ref[g] = jnp.concatenate(
            [vwin_t[g * HEAD_DIM:(g + 1) * HEAD_DIM], ones_rows], axis=0).astype(BF16)

    t_win = lax.broadcasted_iota(jnp.int32, (1, blk), 1).astype(F32) + float(blk)
    slopes = _alibi_slopes_log2()
    sinks = [sinks_ref[h] * LOG2E + slopes[h] * t_win for h in range(A_Q_HEADS)]
    key_row = lax.broadcasted_iota(jnp.int32, (2 * blk, blk), 0)

    items = [(i, g) for i in range(SWA_QB) for g in range(A_KV_HEADS)]
    keys_of = lambda i: slice(i * blk, (i + 2) * blk)

    def stage_scores(n):
        i, g = items[n]
        for k in range(A_GROUP):
            h = g * A_GROUP + k
            q_slab = q_ref[i * blk:(i + 1) * blk, (h // 2) * LANES:(h // 2 + 1) * LANES]
            s_ref[n % 2, k] = _dot_nt(kop_ref[g, (h + g) % 2, keys_of(i), :], q_slab)

    def stage_softmax(n):
        i, g = items[n]
        for k in range(A_GROUP):
            h = g * A_GROUP + k
            st = s_ref[n % 2, k] + bias_ref[h]
            if i == 0:
                st = jnp.where(((step * SWA_QB - 1) * blk + key_row) >= 0, st, NEG)
            m = jnp.maximum(jnp.max(st, axis=0, keepdims=True), sinks[h])
            p_ref[n % 2, k] = jnp.exp2(st - m).astype(BF16)
            sh_ref[n % 2, k] = jnp.exp2(sinks[h] - m)

    def stage_values(n):
        i, g = items[n]
        for k in range(A_GROUP):
            h = g * A_GROUP + k
            ot = jnp.dot(vt_ref[g, :, keys_of(i)], p_ref[n % 2, k],
                         preferred_element_type=F32)
            ot_ref[h * HEAD_DIM:(h + 1) * HEAD_DIM, :] = (
                ot[0:HEAD_DIM] / (ot[HEAD_DIM:HEAD_DIM + 1] + sh_ref[n % 2, k]))
        if g == A_KV_HEADS - 1:
            o_ref[i * blk:(i + 1) * blk, :] = ot_ref[...].T.astype(BF16)

    for n in range(len(items) + 2):
        if n < len(items):
            stage_scores(n)
        if 2 <= n:
            stage_values(n - 2)
        if 1 <= n <= len(items):
            stage_softmax(n - 1)


def _swa(sinks, bias, proj, cast_weights, batch, seq):
    blk = WINDOW
    nb = seq // blk
    steps = nb // SWA_QB
    n_steps = batch * steps
    qcol = _col_block("qA", A_WIDTH)
    kcol = _col_block("kA", A_KV_WIDTH)
    vcol = _col_block("vA", A_KV_WIDTH)
    cur = lambda b, s: b * steps + s
    prev = lambda b, s: b * nb + jnp.maximum(s * SWA_QB - 1, 0)
    slab_specs = []
    for w in cast_weights:
        assert w.shape[0] % (16 * n_steps) == 0
        slab_specs.append(pl.BlockSpec((w.shape[0] // n_steps, w.shape[1]),
                                       lambda b, s: (cur(b, s), 0)))
    outs = pl.pallas_call(
        functools.partial(_swa_kernel, len(cast_weights)),
        grid=(batch, steps),
        in_specs=[
            pl.BlockSpec(memory_space=pltpu.SMEM),
            pl.BlockSpec(bias.shape, lambda b, s: (0, 0, 0)),
            pl.BlockSpec((SWA_QB * blk, A_WIDTH), lambda b, s: (cur(b, s), qcol)),
            pl.BlockSpec((blk, A_KV_WIDTH), lambda b, s: (prev(b, s), kcol)),
            pl.BlockSpec((SWA_QB * blk, A_KV_WIDTH), lambda b, s: (cur(b, s), kcol)),
            pl.BlockSpec((blk, A_KV_WIDTH), lambda b, s: (prev(b, s), vcol)),
            pl.BlockSpec((SWA_QB * blk, A_KV_WIDTH), lambda b, s: (cur(b, s), vcol)),
            *slab_specs,
        ],
        out_specs=[pl.BlockSpec((SWA_QB * blk, A_WIDTH), lambda b, s: (cur(b, s), 0)),
                   *slab_specs],
        out_shape=[jax.ShapeDtypeStruct((batch * seq, A_WIDTH), BF16),
                   *[jax.ShapeDtypeStruct(w.shape, BF16) for w in cast_weights]],
        scratch_shapes=[
            pltpu.VMEM((A_KV_HEADS, 2, SWA_WIN, LANES), BF16),
            pltpu.VMEM((A_KV_HEADS, VT_ROWS, SWA_WIN), BF16),
            pltpu.VMEM((2, A_GROUP, 2 * blk, blk), F32),
            pltpu.VMEM((2, A_GROUP, 2 * blk, blk), BF16),
            pltpu.VMEM((2, A_GROUP, 1, blk), F32),
            pltpu.VMEM((A_WIDTH, blk), F32),
        ],
        compiler_params=pltpu.CompilerParams(
            dimension_semantics=("arbitrary", "arbitrary"), vmem_limit_bytes=VMEM_LIMIT),
        name="swa",
    )(sinks, bias, proj, proj, proj, proj, proj, *cast_weights)
    return outs[0], outs[1:]


FOX_T = 512


FOX_H = FOX_T // 2


def _fox_kernel(q_ref, k_ref, v_ref, aug_ref, o_ref,
                kop_ref, vt_ref, qop_ref, s_ref, cm_ref, p_ref, al_ref, m_ref, acc_ref):
    seq = k_ref.shape[0]
    t = FOX_T
    hk = FOX_H

    lane = lax.broadcasted_iota(jnp.int32, (hk, LANES), 1)
    low = lane < HALF

    def build(r):
        sl = slice(r * hk, (r + 1) * hk)
        kk = k_ref[sl, :].astype(F32)
        aa = aug_ref[sl, :].astype(F32)
        vv = v_ref[sl, :].astype(F32)
        kop_ref[0, sl, :] = jnp.where(low, kk, aa).astype(BF16)
        kop_ref[1, sl, :] = jnp.where(low, aa, kk).astype(BF16)
        vvt = vv.T
        ones_rows = jnp.where(
            lax.broadcasted_iota(jnp.int32, (VT_ROWS - HALF, hk), 0) == 0, 1.0, 0.0)
        for hh in range(2):
            vt_ref[hh, r] = jnp.concatenate(
                [vvt[hh * HALF:(hh + 1) * HALF], ones_rows], axis=0).astype(BF16)

    def scores(slot, half, masked):
        ks = pl.ds(pl.multiple_of(half * hk, hk), hk)
        for hh in range(2):
            st = _dot_nt(kop_ref[hh, ks, :], qop_ref[hh])
            if masked:
                key = lax.broadcasted_iota(jnp.int32, (hk, t), 0) + slot * hk
                qry = lax.broadcasted_iota(jnp.int32, (hk, t), 1)
                st = jnp.where(key <= qry, st, NEG)
            s_ref[slot, hh] = st
            cm_ref[slot, hh] = jnp.max(st, axis=0, keepdims=True)

    def soft(slot):
        for hh in range(2):
            m_old = m_ref[hh]
            m_new = jnp.maximum(m_old, cm_ref[slot, hh])
            al_ref[slot, hh] = jnp.exp2(m_old - m_new)
            p_ref[slot, hh] = jnp.exp2(s_ref[slot, hh] - m_new).astype(BF16)
            m_ref[hh] = m_new

    def pv(slot, half):
        for hh in range(2):
            acc_ref[hh] = acc_ref[hh] * al_ref[slot, hh] + jnp.dot(
                vt_ref[hh, half], p_ref[slot, hh], preferred_element_type=F32)

    def step(i, prev, has_next, in_loop=True):
        if in_loop:
            pv(0, 2 * prev)
            soft(0)
            scores(1, 2 * i + 1, False)
            if has_next:
                scores(0, 2 * i + 2, False)
            pv(1, 2 * prev + 1)
            soft(1)
        else:
            scores(1, 2 * i + 1, False)
            pv(0, 2 * prev)
            soft(0)
            pv(1, 2 * prev + 1)
            if has_next:
                scores(0, 2 * i + 2, False)
            soft(1)

    lane_q = lax.broadcasted_iota(jnp.int32, (t, LANES), 1)
    low_q = lane_q < HALF
    ones0 = jnp.where((lane_q >= HALF) & (lane_q < HALF + N_PIECES), 1.0, 0.0)
    ones1 = jnp.where(lane_q < N_PIECES, 1.0, 0.0)

    for qi in range(seq // t):
        build(2 * qi)
        build(2 * qi + 1)
        rows = slice(qi * t, (qi + 1) * t)
        q = q_ref[rows, :].astype(F32)
        qop_ref[0] = jnp.where(low_q, q, ones0).astype(BF16)
        qop_ref[1] = jnp.where(low_q, ones1, q).astype(BF16)
        m_ref[...] = jnp.full(m_ref.shape, NEG, F32)
        acc_ref[...] = jnp.zeros(acc_ref.shape, F32)

        scores(0, 2 * qi, True)
        scores(1, 2 * qi + 1, True)
        soft(0)
        if qi > 0:
            scores(0, 0, False)
        soft(1)

        loop_pairs = max(qi - 1, 0) // 2

        def pair(j, carry, qi=qi):
            first = 2 * j
            step(first, jnp.where(j == 0, qi, first - 1), True)
            step(first + 1, first, True)
            return carry

        if loop_pairs > 1:
            lax.fori_loop(0, loop_pairs, pair, 0)
        static_from = 2 * loop_pairs if loop_pairs > 1 else 0
        for i in range(static_from, qi):
            step(i, i - 1 if i > 0 else qi, i + 1 < qi, in_loop=False)
        last = qi - 1 if qi > 0 else qi
        pv(0, 2 * last)
        pv(1, 2 * last + 1)

        ot = jnp.concatenate(
            [acc_ref[hh, 0:HALF, :] / acc_ref[hh, HALF:HALF + 1, :] for hh in range(2)], axis=0)
        o_ref[rows, :] = ot.T.astype(BF16)


def _fox(proj, aug, batch, seq):
    t = FOX_T
    nq = seq // t
    pairs = B_HEADS // 2
    qcol = _col_block("qB", LANES)
    kcol = _col_block("kB", LANES)
    vcol = _col_block("vB", LANES)
    return pl.pallas_call(
        _fox_kernel,
        grid=(batch, pairs),
        in_specs=[
            pl.BlockSpec((seq, LANES), lambda b, p: (b, qcol + p)),
            pl.BlockSpec((seq, LANES), lambda b, p: (b, kcol + p)),
            pl.BlockSpec((seq, LANES), lambda b, p: (b, vcol + p)),
            pl.BlockSpec((seq, LANES), lambda b, p: (b, p)),
        ],
        out_specs=pl.BlockSpec((seq, LANES), lambda b, p: (b, p)),
        out_shape=jax.ShapeDtypeStruct((batch * seq, B_WIDTH), BF16),
        scratch_shapes=[
            pltpu.VMEM((2, seq, LANES), BF16),
            pltpu.VMEM((2, seq // FOX_H, VT_ROWS, FOX_H), BF16),
            pltpu.VMEM((2, t, LANES), BF16),
            pltpu.VMEM((2, 2, FOX_H, t), F32),
            pltpu.VMEM((2, 2, 1, t), F32),
            pltpu.VMEM((2, 2, FOX_H, t), BF16),
            pltpu.VMEM((2, 2, 1, t), F32),
            pltpu.VMEM((2, 1, t), F32),
            pltpu.VMEM((2, VT_ROWS, t), F32),
        ],
        compiler_params=pltpu.CompilerParams(
            dimension_semantics=("arbitrary", "arbitrary"),
            vmem_limit_bytes=VMEM_LIMIT),
        name="fox",
    )(proj, proj, proj, aug)


MERGE_TM = 256


Z_BLOCK = 256
Z_PARTS = A_WIDTH // Z_BLOCK
assert A_WIDTH == B_WIDTH and A_WIDTH % Z_BLOCK == 0


def _merge_kernel(x_ref, ya_ref, yb_ref, mk_ref, mv_ref, qc_ref, *refs):
    za_refs = refs[:Z_PARTS]
    zb_refs = refs[Z_PARTS:2 * Z_PARTS]
    zc_ref, g0_ref, g1_ref, g2_ref, wa_ref, wb_ref, wc_ref, wo_ref, o_ref = refs[2 * Z_PARTS:]

    def memory_attention():
        outs = []
        for h in range(C_HEADS):
            sl = slice(h * C_HEAD_DIM, (h + 1) * C_HEAD_DIM)
            s = _dot_nt(qc_ref[:, sl], mk_ref[0, :, sl])
            p = jnp.exp(s - jnp.max(s, axis=-1, keepdims=True))
            o = jnp.dot(p.astype(BF16), mv_ref[0, :, sl], preferred_element_type=F32)
            outs.append(o / jnp.sum(p, axis=-1, keepdims=True))
        return jnp.concatenate(outs, axis=1)

    def branch(y, z_refs, w_ref):
        z = jnp.concatenate([r[...] for r in z_refs], axis=1).astype(F32)
        h = (y * (z * _sigmoid(z))).astype(BF16)
        return jnp.dot(h, w_ref[...], preferred_element_type=F32)

    y = _sigmoid(g0_ref[...].astype(F32)) * branch(ya_ref[...].astype(F32), za_refs, wa_ref)
    y = y + _sigmoid(g1_ref[...].astype(F32)) * branch(yb_ref[...].astype(F32), zb_refs, wb_ref)
    y = y + _sigmoid(g2_ref[...].astype(F32)) * branch(memory_attention(), (zc_ref,), wc_ref)
    o_ref[...] = x_ref[...] + jnp.dot(y.astype(BF16), wo_ref[...], preferred_element_type=F32)


def _merge(x2, ya, yb, mk, mv, proj, wa, wb, wc, wo, seq):
    t, d = x2.shape
    tm = MERGE_TM
    tiles_per_seq = seq // tm
    gcol = _col_block("g", D_MODEL)
    row = lambda width, col: pl.BlockSpec((tm, width), lambda i: (i, col))
    full = lambda a: pl.BlockSpec(a.shape, lambda i: (0, 0), pipeline_mode=pl.Buffered(1))
    mem = lambda a: pl.BlockSpec((1,) + a.shape[1:], lambda i: (i // tiles_per_seq, 0, 0))
    z_specs = lambda name: [row(Z_BLOCK, _col_block(name, Z_BLOCK, part)) for part in range(Z_PARTS)]
    n_proj = 2 * Z_PARTS + 5
    return pl.pallas_call(
        _merge_kernel,
        grid=(t // tm,),
        in_specs=[
            row(d, 0), row(A_WIDTH, 0), row(B_WIDTH, 0), mem(mk), mem(mv),
            row(C_WIDTH, _col_block("qC", C_WIDTH)),
            *z_specs("zA"), *z_specs("zB"), row(C_WIDTH, _col_block("zC", C_WIDTH)),
            row(d, gcol), row(d, gcol + 1), row(d, gcol + 2),
            full(wa), full(wb), full(wc), full(wo),
        ],
        out_specs=pl.BlockSpec((tm, d), lambda i: (i, 0)),
        out_shape=jax.ShapeDtypeStruct((t, d), F32),
        compiler_params=pltpu.CompilerParams(
            dimension_semantics=("arbitrary",), vmem_limit_bytes=VMEM_LIMIT),
        name="merge",
    )(x2, ya, yb, mk, mv, *([proj] * n_proj), wa, wb, wc, wo)


def _group_mean_matrices():
    lane = np.arange(2 * LANES)
    chunk = lane // LANES
    mats = []
    for tile in NORM_TILES:
        for c in range(0, IN_CHUNKS, 2):
            dims = np.asarray([_NORM_KIND.get(_chunk_piece(tile * IN_TN + (c + k) * LANES), LANES)
                               for k in range(2)])[chunk]
            same = ((lane[:, None] // dims[:, None] == lane[None, :] // dims[None, :])
                    & (chunk[:, None] == chunk[None, :]))
            mats.append(same.astype(np.float32) / dims[None, :])
    return np.stack(mats)


def _column_scales(gains):
    scale = {"qA": HEAD_DIM ** -0.5 * LOG2E, "qB": HEAD_DIM ** -0.5 * LOG2E,
             "qC": C_HEAD_DIM ** -0.5}
    cols = []
    for tile in NORM_TILES:
        for c in range(IN_CHUNKS):
            piece = _chunk_piece(tile * IN_TN + c * LANES)
            if piece in _NORM_KIND:
                g = gains[piece].astype(F32) * scale.get(piece, 1.0)
                cols.append(jnp.tile(g, LANES // g.shape[0]))
            else:
                cols.append(jnp.zeros((LANES,), F32))
    return jnp.concatenate(cols).reshape(1, len(NORM_TILES) * IN_TN)


def _layer(x, mem, norm_gain, mem_norm_gain, w_rows, b_forget, q_gain_a, k_gain_a, sinks_a,
           q_gain_b, k_gain_b, q_gain_c, k_gain_c, w_mem_kv, w_branch_a, w_branch_b,
           w_branch_c, w_out):
    batch, seq, d = x.shape
    x2 = x.reshape(batch * seq, d)

    w_f = w_rows[F_START * K_CHUNKS:(F_START + F_SHIFT) * K_CHUNKS].reshape(F_SHIFT, d)
    w_f = jnp.pad(w_f, ((0, LANES - F_SHIFT), (0, 0))).astype(BF16)
    b_f = jnp.pad(b_forget.astype(F32), (0, LANES - F_SHIFT)).reshape(1, LANES)
    cscale = _column_scales({"qA": q_gain_a, "kA": k_gain_a, "qB": q_gain_b, "kB": k_gain_b,
                             "qC": q_gain_c})
    gmat = jnp.asarray(_group_mean_matrices(), BF16)
    place = jnp.asarray(_placement_matrix(), BF16)

    mk, mv, w_first = _mem_kv(mem, mem_norm_gain.reshape(1, d), w_mem_kv,
                              k_gain_c.reshape(1, C_HEAD_DIM), w_rows)
    hn, aug = _norm_x(x2, norm_gain.reshape(1, d), w_f, b_f, place, seq)
    proj = _in_proj(hn, w_rows, w_first, cscale, gmat)
    ya, merge_weights = _swa(sinks_a.astype(F32), jnp.asarray(_swa_key_bias(), F32), proj,
                             (w_branch_a, w_branch_b, w_branch_c, w_out), batch, seq)
    yb = _fox(proj, aug, batch, seq)
    out = _merge(x2, ya, yb, mk, mv, proj, *merge_weights, seq)
    return out.reshape(batch, seq, d)


def kernel(x, mem, norm_gain, mem_norm_gain, w_in, b_forget, q_gain_a, k_gain_a, sinks_a,
           q_gain_b, k_gain_b, q_gain_c, k_gain_c, w_mem_kv, w_branch_a, w_branch_b,
           w_branch_c, w_out):
    depth = norm_gain.shape[0]
    w_rows = jnp.swapaxes(w_in, 1, 2).reshape(depth, -1, LANES)
    for layer in range(depth):
        x = _layer(x, mem, norm_gain[layer], mem_norm_gain[layer], w_rows[layer], b_forget[layer],
                   q_gain_a[layer], k_gain_a[layer], sinks_a[layer], q_gain_b[layer],
                   k_gain_b[layer], q_gain_c[layer], k_gain_c[layer], w_mem_kv[layer],
                   w_branch_a[layer], w_branch_b[layer], w_branch_c[layer], w_out[layer])
    return x
```

```python
import functools

import jax
import jax.numpy as jnp
import numpy as np
from jax import lax
from jax.experimental import pallas as pl
from jax.experimental.pallas import tpu as pltpu

F32 = jnp.float32
BF16 = jnp.bfloat16

D_MODEL = 2048
HEAD_DIM = 64
A_Q_HEADS = 12
A_KV_HEADS = 4
A_GROUP = A_Q_HEADS // A_KV_HEADS
WINDOW = 128
B_HEADS = 12
C_HEADS = 4
C_HEAD_DIM = 128
A_WIDTH = A_Q_HEADS * HEAD_DIM
A_KV_WIDTH = A_KV_HEADS * HEAD_DIM
B_WIDTH = B_HEADS * HEAD_DIM
C_WIDTH = C_HEADS * C_HEAD_DIM
EPS = 1e-6
NEG = -1e30

LANES = 128
HALF = LANES // 2
VT_ROWS = HALF + 16
LOG2E = float(np.log2(np.e))
N_PIECES = 3
VMEM_LIMIT = 56 * 1024 * 1024

_SRC = {}
_off = 0
for _name, _w in (("qA", A_WIDTH), ("kA", A_KV_WIDTH), ("vA", A_KV_WIDTH), ("zA", A_WIDTH),
                  ("qB", B_WIDTH), ("kB", B_WIDTH), ("vB", B_WIDTH), ("zB", B_WIDTH),
                  ("fB", B_HEADS), ("qC", C_WIDTH), ("zC", C_WIDTH), ("g", 3 * D_MODEL)):
    _SRC[_name] = (_off, _w)
    _off += _w

F_START, F_SHIFT = _SRC["fB"]
_DST = {n: (o if o < F_START else o - F_SHIFT) for n, (o, _) in _SRC.items() if n != "fB"}
PROJ_WIDTH = _SRC["g"][0] + _SRC["g"][1] - F_SHIFT
assert all(o % LANES == 0 for o in _DST.values()) and F_START % LANES == 0


def _col_block(name, width, part=0):
    assert _DST[name] % width == 0
    return _DST[name] // width + part


def _rms(x, gain):
    ms = jnp.mean(x * x, axis=-1, keepdims=True)
    return x * lax.rsqrt(ms + EPS) * gain


def _sigmoid(t):
    return 0.5 * (jnp.tanh(0.5 * t) + 1.0)


def _dot_nt(a, b):
    return lax.dot_general(a, b, (((1,), (1,)), ((), ())), preferred_element_type=F32)


def _mem_kv_kernel(mem_ref, gain_ref, w_ref, kgain_ref, wrows_ref, mk_ref, mv_ref, w0_ref):
    @pl.when(pl.program_id(0) == 0)
    def _():
        _gather_weight_rows(wrows_ref, w0_ref, 0, slice(None), IN_TN)

    hn = _rms(mem_ref[0], gain_ref[...]).astype(BF16)
    kv = jnp.dot(hn, w_ref[...].astype(BF16), preferred_element_type=F32)
    for h in range(C_HEADS):
        sl = slice(h * C_HEAD_DIM, (h + 1) * C_HEAD_DIM)
        mk_ref[0, :, sl] = _rms(kv[:, sl], kgain_ref[...]).astype(BF16)
    mv_ref[0] = kv[:, C_WIDTH:].astype(BF16)


def _mem_kv(mem, gain, w, kgain, w_rows):
    b, m, d = mem.shape
    return pl.pallas_call(
        _mem_kv_kernel,
        grid=(b,),
        in_specs=[
            pl.BlockSpec((1, m, d), lambda i: (i, 0, 0)),
            pl.BlockSpec((1, d), lambda i: (0, 0)),
            pl.BlockSpec((d, 2 * C_WIDTH), lambda i: (0, 0)),
            pl.BlockSpec((1, C_HEAD_DIM), lambda i: (0, 0)),
            pl.BlockSpec((pl.Element(IN_TN * K_CHUNKS), pl.Element(LANES)),
                         lambda i: _w_index(0 * i)),
        ],
        out_specs=[
            pl.BlockSpec((1, m, C_WIDTH), lambda i: (i, 0, 0)),
            pl.BlockSpec((1, m, C_WIDTH), lambda i: (i, 0, 0)),
            pl.BlockSpec((IN_TN, D_MODEL), lambda i: (0, 0)),
        ],
        out_shape=[jax.ShapeDtypeStruct((b, m, C_WIDTH), BF16)] * 2
        + [jax.ShapeDtypeStruct((IN_TN, D_MODEL), BF16)],
        compiler_params=pltpu.CompilerParams(
            dimension_semantics=("arbitrary",), vmem_limit_bytes=VMEM_LIMIT),
        name="mem_kv",
    )(mem, gain, w, kgain, w_rows)


NX_TM = 1024


def _norm_x_kernel(tiles_per_seq, x_ref, gain_ref, wf_ref, bf_ref, place_ref, hn_ref, aug_ref,
                   carry_ref):
    i = pl.program_id(0)

    @pl.when(i % tiles_per_seq == 0)
    def _():
        carry_ref[...] = jnp.zeros_like(carry_ref)

    hn = _rms(x_ref[...], gain_ref[...]).astype(BF16)
    hn_ref[...] = hn
    f_logit = _dot_nt(hn, wf_ref[...]) + bf_ref[...]
    x = jnp.minimum(f_logit, 0.0) - jnp.log1p(jnp.exp(-jnp.abs(f_logit)))
    row = lax.broadcasted_iota(jnp.int32, x.shape, 0)
    shift = 1
    while shift < NX_TM:
        x = x + jnp.where(row >= shift, pltpu.roll(x, shift, 0), 0.0)
        shift *= 2
    c = x + carry_ref[...]
    carry_ref[...] = c[NX_TM - 1:NX_TM, :]
    rest = c * (-LOG2E)
    pieces = []
    for _ in range(N_PIECES):
        p = rest.astype(BF16)
        pieces.append(p)
        rest = rest - p.astype(F32)
    stacked = jnp.concatenate(pieces, axis=1)
    aug_ref[...] = jnp.dot(stacked, place_ref[...], preferred_element_type=F32).astype(BF16)


def _norm_x(x2, gain, wf, bfg, place, seq):
    t, d = x2.shape
    return pl.pallas_call(
        functools.partial(_norm_x_kernel, seq // NX_TM),
        grid=(t // NX_TM,),
        in_specs=[
            pl.BlockSpec((NX_TM, d), lambda i: (i, 0)),
            pl.BlockSpec((1, d), lambda i: (0, 0)),
            pl.BlockSpec((LANES, d), lambda i: (0, 0)),
            pl.BlockSpec((1, LANES), lambda i: (0, 0)),
            pl.BlockSpec((N_PIECES * LANES, B_WIDTH), lambda i: (0, 0)),
        ],
        out_specs=[
            pl.BlockSpec((NX_TM, d), lambda i: (i, 0)),
            pl.BlockSpec((NX_TM, B_WIDTH), lambda i: (i, 0)),
        ],
        out_shape=[jax.ShapeDtypeStruct((t, d), BF16),
                   jax.ShapeDtypeStruct((t, B_WIDTH), BF16)],
        scratch_shapes=[pltpu.VMEM((1, LANES), F32)],
        compiler_params=pltpu.CompilerParams(
            dimension_semantics=("arbitrary",), vmem_limit_bytes=VMEM_LIMIT),
        name="norm_x",
    )(x2, gain, wf, bfg, place)


IN_TM = 1024
IN_TN = 1024
IN_CHUNKS = IN_TN // LANES
K_CHUNKS = D_MODEL // LANES
N_IN_TILES = PROJ_WIDTH // IN_TN
N_ALIGNED_TILES = F_START // IN_TN
assert F_START % IN_TN == 0 and PROJ_WIDTH % IN_TN == 0 and F_SHIFT < LANES

_NORM_KIND = {"qA": HEAD_DIM, "kA": HEAD_DIM, "qB": HEAD_DIM, "kB": HEAD_DIM, "qC": C_HEAD_DIM}


def _chunk_piece(col):
    for name, off in _DST.items():
        if off <= col < off + _SRC[name][1]:
            return name
    raise ValueError(col)


def _norm_prefix(tile):
    kinds = [_chunk_piece(tile * IN_TN + c * LANES) in _NORM_KIND for c in range(IN_CHUNKS)]
    n = sum(kinds)
    assert kinds == [True] * n + [False] * (IN_CHUNKS - n)
    return n


NORM_PREFIX = tuple(_norm_prefix(j) for j in range(N_IN_TILES))
NORM_TILES = tuple(j for j in range(N_IN_TILES) if NORM_PREFIX[j])
GATE_TILE = tuple(all(_chunk_piece(j * IN_TN + c * LANES) == "g" for c in range(IN_CHUNKS))
                  for j in range(N_IN_TILES))
assert _DST["g"] % IN_TN == 0 and not any(g and p for g, p in zip(GATE_TILE, NORM_PREFIX))


def _norm_slot(j):
    slot = 0
    for tile in NORM_TILES[1:]:
        slot = slot + (j >= tile).astype(jnp.int32)
    return slot


IN_MCH = 512


def _gather_weight_rows(w_ref, dst_ref, src_row, dst_rows, n_rows):
    for c in range(K_CHUNKS):
        dst_ref[dst_rows, c * LANES:(c + 1) * LANES] = (
            w_ref[pl.ds(src_row + c, n_rows, stride=K_CHUNKS), :].astype(BF16))


def _w_index(tile):
    col = tile * IN_TN + jnp.where(tile >= N_ALIGNED_TILES, F_SHIFT, 0)
    return (col * K_CHUNKS, 0)


def _in_proj_kernel(n_row_tiles, hn_ref, w_ref, w0_ref, cscale_ref, gmat_ref, o_ref,
                    wbf_a, wbf_b, acc_ref):
    j = pl.program_id(0)
    i = pl.program_id(1)
    share = IN_TN // n_row_tiles

    @pl.when((j == 0) & (i == 0))
    def _():
        wbf_a[...] = w0_ref[...]

    def epilogue(chunk, prefix, gate):
        rows = slice(chunk * IN_MCH, (chunk + 1) * IN_MCH)
        acc = acc_ref.at[chunk % 2]
        for c in range(prefix // 2):
            sl = slice(c * 2 * LANES, (c + 1) * 2 * LANES)
            a = acc[:, sl]
            ms = jnp.dot((a * a).astype(BF16), gmat_ref[c], preferred_element_type=F32)
            o_ref[rows, sl] = (a * lax.rsqrt(ms + EPS) * cscale_ref[:, sl]).astype(BF16)
        if prefix < IN_CHUNKS:
            rest = slice(prefix * LANES, IN_TN)
            a = acc[:, rest]
            o_ref[rows, rest] = (_sigmoid(a) if gate else a).astype(BF16)

    def step(cur, nxt, prefix, gate):
        src = pl.multiple_of(i * (share * K_CHUNKS), share * K_CHUNKS)
        dst = pl.ds(pl.multiple_of(i * share, share), share)
        n_chunks = hn_ref.shape[0] // IN_MCH
        for c in range(n_chunks + 1):
            if c < n_chunks:
                acc_ref[c % 2] = _dot_nt(hn_ref[c * IN_MCH:(c + 1) * IN_MCH, :], cur[...])
            if c == 0:
                _gather_weight_rows(w_ref, nxt, src, dst, share)
            if c >= 1:
                epilogue(c - 1, prefix, gate)

    def is_any(tiles):
        cond = j == tiles[0]
        for tile in tiles[1:]:
            cond = cond | (j == tile)
        return cond

    kind = lambda t: (t % 2, NORM_PREFIX[t], GATE_TILE[t])
    for parity, prefix, gate in sorted({kind(t) for t in range(N_IN_TILES)}):
        tiles = [t for t in range(N_IN_TILES) if kind(t) == (parity, prefix, gate)]
        cur, nxt = (wbf_a, wbf_b) if parity == 0 else (wbf_b, wbf_a)
        pl.when(is_any(tiles))(functools.partial(step, cur, nxt, prefix, gate))


def _in_proj(hn, w_rows, w_first, cscale, gmat):
    t, d = hn.shape
    n_row_tiles = t // IN_TM
    assert IN_TN % n_row_tiles == 0 and (IN_TN // n_row_tiles) % 16 == 0
    return pl.pallas_call(
        functools.partial(_in_proj_kernel, n_row_tiles),
        grid=(N_IN_TILES, n_row_tiles),
        in_specs=[
            pl.BlockSpec((IN_TM, d), lambda j, i: (i, 0)),
            pl.BlockSpec((pl.Element(IN_TN * K_CHUNKS), pl.Element(LANES)),
                         lambda j, i: _w_index(jnp.minimum(j + 1, N_IN_TILES - 1))),
            pl.BlockSpec((IN_TN, d), lambda j, i: (0, 0), pipeline_mode=pl.Buffered(1)),
            pl.BlockSpec((1, IN_TN), lambda j, i: (0, _norm_slot(j))),
            pl.BlockSpec((IN_CHUNKS // 2, 2 * LANES, 2 * LANES),
                         lambda j, i: (_norm_slot(j), 0, 0)),
        ],
        out_specs=pl.BlockSpec((IN_TM, IN_TN), lambda j, i: (i, j)),
        out_shape=jax.ShapeDtypeStruct((t, PROJ_WIDTH), BF16),
        scratch_shapes=[
            pltpu.VMEM((IN_TN, d), BF16),
            pltpu.VMEM((IN_TN, d), BF16),
            pltpu.VMEM((2, IN_MCH, IN_TN), F32),
        ],
        compiler_params=pltpu.CompilerParams(
            dimension_semantics=("arbitrary", "arbitrary"), vmem_limit_bytes=VMEM_LIMIT),
        name="in_proj",
    )(hn, w_rows, w_first, cscale, gmat)


def _aug_lane(head):
    return HALF if head % 2 == 0 else 0


def _placement_matrix():
    place = np.zeros((N_PIECES * LANES, B_WIDTH), np.float32)
    for h in range(B_HEADS):
        for p in range(N_PIECES):
            place[p * LANES + h, (h // 2) * LANES + _aug_lane(h) + p] = 1.0
    return place


SWA_QB = 8
SWA_WIN = (SWA_QB + 1) * WINDOW


def _alibi_slopes_log2():
    return [float(2.0 ** (-8.0 * (h + 1) / A_Q_HEADS)) * LOG2E for h in range(A_Q_HEADS)]


def _swa_key_bias():
    key = np.arange(2 * WINDOW)[:, None]
    qry = np.arange(WINDOW)[None, :]
    rel = qry + WINDOW - key
    visible = (rel >= 0) & (rel < WINDOW)
    slopes = np.asarray(_alibi_slopes_log2(), np.float32)[:, None, None]
    return np.where(visible[None], slopes * key[None].astype(np.float32), np.float32(NEG))


def _swa_kernel(n_cast, sinks_ref, bias_ref, q_ref, kp_ref, kc_ref, vp_ref, vc_ref, *refs):
    cast_in, o_ref, cast_out = refs[:n_cast], refs[n_cast], refs[n_cast + 1:2 * n_cast + 1]
    kop_ref, vt_ref, s_ref, p_ref, sh_ref, ot_ref = refs[2 * n_cast + 1:]
    for src, dst in zip(cast_in, cast_out):
        dst[...] = src[...].astype(BF16)
    step = pl.program_id(1)
    blk = WINDOW
    lane = lax.broadcasted_iota(jnp.int32, (SWA_WIN, LANES), 1)

    kwin = jnp.concatenate([kp_ref[...], kc_ref[...]], axis=0).astype(F32)
    vwin_t = jnp.concatenate([vp_ref[...], vc_ref[...]], axis=0).astype(F32).T
    ones_rows = jnp.where(
        lax.broadcasted_iota(jnp.int32, (VT_ROWS - HALF, SWA_WIN), 0) == 0, 1.0, 0.0)
    for g in range(A_KV_HEADS):
        slab = kwin[:, (g // 2) * LANES:(g // 2 + 1) * LANES]
        own = (lane < HALF) if g % 2 == 0 else (lane >= HALF)
        kz = jnp.where(own, slab, 0.0)
        kop_ref[g, 0] = kz.astype(BF16)
        kop_ref[g, 1] = pltpu.roll(kz, HALF, 1).astype(BF16)
        vt_ref[g] = jnp.concatenate(
            [vwin_t[g * HEAD_DIM:(g + 1) * HEAD_DIM], ones_rows], axis=0).astype(BF16)

    t_win = lax.broadcasted_iota(jnp.int32, (1, blk), 1).astype(F32) + float(blk)
    slopes = _alibi_slopes_log2()
    sinks = [sinks_ref[h] * LOG2E + slopes[h] * t_win for h in range(A_Q_HEADS)]
    key_row = lax.broadcasted_iota(jnp.int32, (2 * blk, blk), 0)

    items = [(i, g) for i in range(SWA_QB) for g in range(A_KV_HEADS)]
    keys_of = lambda i: slice(i * blk, (i + 2) * blk)

    def stage_scores(n):
        i, g = items[n]
        for k in range(A_GROUP):
            h = g * A_GROUP + k
            q_slab = q_ref[i * blk:(i + 1) * blk, (h // 2) * LANES:(h // 2 + 1) * LANES]
            s_ref[n % 2, k] = _dot_nt(kop_ref[g, (h + g) % 2, keys_of(i), :], q_slab)

    def stage_softmax(n):
        i, g = items[n]
        for k in range(A_GROUP):
            h = g * A_GROUP + k
            st = s_ref[n % 2, k] + bias_ref[h]
            if i == 0:
                st = jnp.where(((step * SWA_QB - 1) * blk + key_row) >= 0, st, NEG)
            m = jnp.maximum(jnp.max(st, axis=0, keepdims=True), sinks[h])
            p_ref[n % 2, k] = jnp.exp2(st - m).astype(BF16)
            sh_ref[n % 2, k] = jnp.exp2(sinks[h] - m)

    def stage_values(n):
        i, g = items[n]
        for k in range(A_GROUP):
            h = g * A_GROUP + k
            ot = jnp.dot(vt_ref[g, :, keys_of(i)], p_ref[n % 2, k],
                         preferred_element_type=F32)
            ot_ref[h * HEAD_DIM:(h + 1) * HEAD_DIM, :] = (
                ot[0:HEAD_DIM] / (ot[HEAD_DIM:HEAD_DIM + 1] + sh_ref[n % 2, k]))
        if g == A_KV_HEADS - 1:
            o_ref[i * blk:(i + 1) * blk, :] = ot_ref[...].T.astype(BF16)

    for n in range(len(items) + 2):
        if n < len(items):
            stage_scores(n)
        if 2 <= n:
            stage_values(n - 2)
        if 1 <= n <= len(items):
            stage_softmax(n - 1)


def _swa(sinks, bias, proj, cast_weights, batch, seq):
    blk = WINDOW
    nb = seq // blk
    steps = nb // SWA_QB
    n_steps = batch * steps
    qcol = _col_block("qA", A_WIDTH)
    kcol = _col_block("kA", A_KV_WIDTH)
    vcol = _col_block("vA", A_KV_WIDTH)
    cur = lambda b, s: b * steps + s
    prev = lambda b, s: b * nb + jnp.maximum(s * SWA_QB - 1, 0)
    slab_specs = []
    for w in cast_weights:
        assert w.shape[0] % (16 * n_steps) == 0
        slab_specs.append(pl.BlockSpec((w.shape[0] // n_steps, w.shape[1]),
                                       lambda b, s: (cur(b, s), 0)))
    outs = pl.pallas_call(
        functools.partial(_swa_kernel, len(cast_weights)),
        grid=(batch, steps),
        in_specs=[
            pl.BlockSpec(memory_space=pltpu.SMEM),
            pl.BlockSpec(bias.shape, lambda b, s: (0, 0, 0)),
            pl.BlockSpec((SWA_QB * blk, A_WIDTH), lambda b, s: (cur(b, s), qcol)),
            pl.BlockSpec((blk, A_KV_WIDTH), lambda b, s: (prev(b, s), kcol)),
            pl.BlockSpec((SWA_QB * blk, A_KV_WIDTH), lambda b, s: (cur(b, s), kcol)),
            pl.BlockSpec((blk, A_KV_WIDTH), lambda b, s: (prev(b, s), vcol)),
            pl.BlockSpec((SWA_QB * blk, A_KV_WIDTH), lambda b, s: (cur(b, s), vcol)),
            *slab_specs,
        ],
        out_specs=[pl.BlockSpec((SWA_QB * blk, A_WIDTH), lambda b, s: (cur(b, s), 0)),
                   *slab_specs],
        out_shape=[jax.ShapeDtypeStruct((batch * seq, A_WIDTH), BF16),
                   *[jax.ShapeDtypeStruct(w.shape, BF16) for w in cast_weights]],
        scratch_shapes=[
            pltpu.VMEM((A_KV_HEADS, 2, SWA_WIN, LANES), BF16),
            pltpu.VMEM((A_KV_HEADS, VT_ROWS, SWA_WIN), BF16),
            pltpu.VMEM((2, A_GROUP, 2 * blk, blk), F32),
            pltpu.VMEM((2, A_GROUP, 2 * blk, blk), BF16),
            pltpu.VMEM((2, A_GROUP, 1, blk), F32),
            pltpu.VMEM((A_WIDTH, blk), F32),
        ],
        compiler_params=pltpu.CompilerParams(
            dimension_semantics=("arbitrary", "arbitrary"), vmem_limit_bytes=VMEM_LIMIT),
        name="swa",
    )(sinks, bias, proj, proj, proj, proj, proj, *cast_weights)
    return outs[0], outs[1:]


FOX_T = 512


FOX_H = FOX_T // 2


def _fox_kernel(q_ref, k_ref, v_ref, aug_ref, o_ref,
                kop_ref, vt_ref, qop_ref, s_ref, cm_ref, p_ref, al_ref, m_ref, acc_ref):
    seq = k_ref.shape[0]
    t = FOX_T
    hk = FOX_H

    lane = lax.broadcasted_iota(jnp.int32, (hk, LANES), 1)
    low = lane < HALF

    def build(r):
        sl = slice(r * hk, (r + 1) * hk)
        kk = k_ref[sl, :].astype(F32)
        aa = aug_ref[sl, :].astype(F32)
        vv = v_ref[sl, :].astype(F32)
        kop_ref[0, sl, :] = jnp.where(low, kk, aa).astype(BF16)
        kop_ref[1, sl, :] = jnp.where(low, aa, kk).astype(BF16)
        vvt = vv.T
        ones_rows = jnp.where(
            lax.broadcasted_iota(jnp.int32, (VT_ROWS - HALF, hk), 0) == 0, 1.0, 0.0)
        for hh in range(2):
            vt_ref[hh, r] = jnp.concatenate(
                [vvt[hh * HALF:(hh + 1) * HALF], ones_rows], axis=0).astype(BF16)

    def scores(slot, half, masked):
        ks = pl.ds(pl.multiple_of(half * hk, hk), hk)
        for hh in range(2):
            st = _dot_nt(kop_ref[hh, ks, :], qop_ref[hh])
            if masked:
                key = lax.broadcasted_iota(jnp.int32, (hk, t), 0) + slot * hk
                qry = lax.broadcasted_iota(jnp.int32, (hk, t), 1)
                st = jnp.where(key <= qry, st, NEG)
            s_ref[slot, hh] = st
            cm_ref[slot, hh] = jnp.max(st, axis=0, keepdims=True)

    def soft(slot):
        for hh in range(2):
            m_old = m_ref[hh]
            m_new = jnp.maximum(m_old, cm_ref[slot, hh])
            al_ref[slot, hh] = jnp.exp2(m_old - m_new)
            p_ref[slot, hh] = jnp.exp2(s_ref[slot, hh] - m_new).astype(BF16)
            m_ref[hh] = m_new

    def pv(slot, half):
        for hh in range(2):
            acc_ref[hh] = acc_ref[hh] * al_ref[slot, hh] + jnp.dot(
                vt_ref[hh, half], p_ref[slot, hh], preferred_element_type=F32)

    def step(i, prev, has_next, in_loop=True):
        if in_loop:
            pv(0, 2 * prev)
            soft(0)
            scores(1, 2 * i + 1, False)
            if has_next:
                scores(0, 2 * i + 2, False)
            pv(1, 2 * prev + 1)
            soft(1)
        else:
            scores(1, 2 * i + 1, False)
            pv(0, 2 * prev)
            soft(0)
            pv(1, 2 * prev + 1)
            if has_next:
                scores(0, 2 * i + 2, False)
            soft(1)

    lane_q = lax.broadcasted_iota(jnp.int32, (t, LANES), 1)
    low_q = lane_q < HALF
    ones0 = jnp.where((lane_q >= HALF) & (lane_q < HALF + N_PIECES), 1.0, 0.0)
    ones1 = jnp.where(lane_q < N_PIECES, 1.0, 0.0)

    for qi in range(seq // t):
        build(2 * qi)
        build(2 * qi + 1)
        rows = slice(qi * t, (qi + 1) * t)
        q = q_ref[rows, :].astype(F32)
        qop_ref[0] = jnp.where(low_q, q, ones0).astype(BF16)
        qop_ref[1] = jnp.where(low_q, ones1, q).astype(BF16)
        m_ref[...] = jnp.full(m_ref.shape, NEG, F32)
        acc_ref[...] = jnp.zeros(acc_ref.shape, F32)

        scores(0, 2 * qi, True)
        scores(1, 2 * qi + 1, True)
        soft(0)
        if qi > 0:
            scores(0, 0, False)
        soft(1)

        loop_pairs = max(qi - 1, 0) // 2

        def pair(j, carry, qi=qi):
            first = 2 * j
            step(first, jnp.where(j == 0, qi, first - 1), True)
            step(first + 1, first, True)
            return carry

        if loop_pairs > 1:
            lax.fori_loop(0, loop_pairs, pair, 0)
        static_from = 2 * loop_pairs if loop_pairs > 1 else 0
        for i in range(static_from, qi):
            step(i, i - 1 if i > 0 else qi, i + 1 < qi, in_loop=False)
        last = qi - 1 if qi > 0 else qi
        pv(0, 2 * last)
        pv(1, 2 * last + 1)

        ot = jnp.concatenate(
            [acc_ref[hh, 0:HALF, :] / acc_ref[hh, HALF:HALF + 1, :] for hh in range(2)], axis=0)
        o_ref[rows, :] = ot.T.astype(BF16)


def _fox(proj, aug, batch, seq):
    t = FOX_T
    nq = seq // t
    pairs = B_HEADS // 2
    qcol = _col_block("qB", LANES)
    kcol = _col_block("kB", LANES)
    vcol = _col_block("vB", LANES)
    return pl.pallas_call(
        _fox_kernel,
        grid=(batch, pairs),
        in_specs=[
            pl.BlockSpec((seq, LANES), lambda b, p: (b, qcol + p)),
            pl.BlockSpec((seq, LANES), lambda b, p: (b, kcol + p)),
            pl.BlockSpec((seq, LANES), lambda b, p: (b, vcol + p)),
            pl.BlockSpec((seq, LANES), lambda b, p: (b, p)),
        ],
        out_specs=pl.BlockSpec((seq, LANES), lambda b, p: (b, p)),
        out_shape=jax.ShapeDtypeStruct((batch * seq, B_WIDTH), BF16),
        scratch_shapes=[
            pltpu.VMEM((2, seq, LANES), BF16),
            pltpu.VMEM((2, seq // FOX_H, VT_ROWS, FOX_H), BF16),
            pltpu.VMEM((2, t, LANES), BF16),
            pltpu.VMEM((2, 2, FOX_H, t), F32),
            pltpu.VMEM((2, 2, 1, t), F32),
            pltpu.VMEM((2, 2, FOX_H, t), BF16),
            pltpu.VMEM((2, 2, 1, t), F32),
            pltpu.VMEM((2, 1, t), F32),
            pltpu.VMEM((2, VT_ROWS, t), F32),
        ],
        compiler_params=pltpu.CompilerParams(
            dimension_semantics=("arbitrary", "arbitrary"),
            vmem_limit_bytes=VMEM_LIMIT),
        name="fox",
    )(proj, proj, proj, aug)


MERGE_TM = 256


Z_BLOCK = 256
Z_PARTS = A_WIDTH // Z_BLOCK
assert A_WIDTH == B_WIDTH and A_WIDTH % Z_BLOCK == 0


def _merge_kernel(x_ref, ya_ref, yb_ref, mk_ref, mv_ref, qc_ref, *refs):
    za_refs = refs[:Z_PARTS]
    zb_refs = refs[Z_PARTS:2 * Z_PARTS]
    zc_ref, g0_ref, g1_ref, g2_ref, wa_ref, wb_ref, wc_ref, wo_ref, o_ref = refs[2 * Z_PARTS:]

    def memory_attention():
        outs = []
        for h in range(C_HEADS):
            sl = slice(h * C_HEAD_DIM, (h + 1) * C_HEAD_DIM)
            s = _dot_nt(qc_ref[:, sl], mk_ref[0, :, sl])
            p = jnp.exp(s - jnp.max(s, axis=-1, keepdims=True))
            o = jnp.dot(p.astype(BF16), mv_ref[0, :, sl], preferred_element_type=F32)
            outs.append(o / jnp.sum(p, axis=-1, keepdims=True))
        return jnp.concatenate(outs, axis=1)

    def branch(y, z_refs, w_ref):
        z = jnp.concatenate([r[...] for r in z_refs], axis=1).astype(F32)
        h = (y * (z * _sigmoid(z))).astype(BF16)
        return jnp.dot(h, w_ref[...], preferred_element_type=F32)

    y = g0_ref[...].astype(F32) * branch(ya_ref[...].astype(F32), za_refs, wa_ref)
    y = y + g1_ref[...].astype(F32) * branch(yb_ref[...].astype(F32), zb_refs, wb_ref)
    y = y + g2_ref[...].astype(F32) * branch(memory_attention(), (zc_ref,), wc_ref)
    o_ref[...] = x_ref[...] + jnp.dot(y.astype(BF16), wo_ref[...], preferred_element_type=F32)


def _merge(x2, ya, yb, mk, mv, proj, wa, wb, wc, wo, seq):
    t, d = x2.shape
    tm = MERGE_TM
    tiles_per_seq = seq // tm
    gcol = _col_block("g", D_MODEL)
    row = lambda width, col: pl.BlockSpec((tm, width), lambda i: (i, col))
    full = lambda a: pl.BlockSpec(a.shape, lambda i: (0, 0), pipeline_mode=pl.Buffered(1))
    mem = lambda a: pl.BlockSpec((1,) + a.shape[1:], lambda i: (i // tiles_per_seq, 0, 0))
    z_specs = lambda name: [row(Z_BLOCK, _col_block(name, Z_BLOCK, part)) for part in range(Z_PARTS)]
    n_proj = 2 * Z_PARTS + 5
    return pl.pallas_call(
        _merge_kernel,
        grid=(t // tm,),
        in_specs=[
            row(d, 0), row(A_WIDTH, 0), row(B_WIDTH, 0), mem(mk), mem(mv),
            row(C_WIDTH, _col_block("qC", C_WIDTH)),
            *z_specs("zA"), *z_specs("zB"), row(C_WIDTH, _col_block("zC", C_WIDTH)),
            row(d, gcol), row(d, gcol + 1), row(d, gcol + 2),
            full(wa), full(wb), full(wc), full(wo),
        ],
        out_specs=pl.BlockSpec((tm, d), lambda i: (i, 0)),
        out_shape=jax.ShapeDtypeStruct((t, d), F32),
        compiler_params=pltpu.CompilerParams(
            dimension_semantics=("arbitrary",), vmem_limit_bytes=VMEM_LIMIT),
        name="merge",
    )(x2, ya, yb, mk, mv, *([proj] * n_proj), wa, wb, wc, wo)


def _group_mean_matrices():
    lane = np.arange(2 * LANES)
    chunk = lane // LANES
    mats = []
    for tile in NORM_TILES:
        for c in range(0, IN_CHUNKS, 2):
            dims = np.asarray([_NORM_KIND.get(_chunk_piece(tile * IN_TN + (c + k) * LANES), LANES)
                               for k in range(2)])[chunk]
            same = ((lane[:, None] // dims[:, None] == lane[None, :] // dims[None, :])
                    & (chunk[:, None] == chunk[None, :]))
            mats.append(same.astype(np.float32) / dims[None, :])
    return np.stack(mats)


def _column_scales(gains):
    scale = {"qA": HEAD_DIM ** -0.5 * LOG2E, "qB": HEAD_DIM ** -0.5 * LOG2E,
             "qC": C_HEAD_DIM ** -0.5}
    cols = []
    for tile in NORM_TILES:
        for c in range(IN_CHUNKS):
            piece = _chunk_piece(tile * IN_TN + c * LANES)
            if piece in _NORM_KIND:
                g = gains[piece].astype(F32) * scale.get(piece, 1.0)
                cols.append(jnp.tile(g, LANES // g.shape[0]))
            else:
                cols.append(jnp.zeros((LANES,), F32))
    return jnp.concatenate(cols).reshape(1, len(NORM_TILES) * IN_TN)


def _layer(x, mem, norm_gain, mem_norm_gain, w_rows, b_forget, q_gain_a, k_gain_a, sinks_a,
           q_gain_b, k_gain_b, q_gain_c, k_gain_c, w_mem_kv, w_branch_a, w_branch_b,
           w_branch_c, w_out):
    batch, seq, d = x.shape
    x2 = x.reshape(batch * seq, d)

    w_f = w_rows[F_START * K_CHUNKS:(F_START + F_SHIFT) * K_CHUNKS].reshape(F_SHIFT, d)
    w_f = jnp.pad(w_f, ((0, LANES - F_SHIFT), (0, 0))).astype(BF16)
    b_f = jnp.pad(b_forget.astype(F32), (0, LANES - F_SHIFT)).reshape(1, LANES)
    cscale = _column_scales({"qA": q_gain_a, "kA": k_gain_a, "qB": q_gain_b, "kB": k_gain_b,
                             "qC": q_gain_c})
    gmat = jnp.asarray(_group_mean_matrices(), BF16)
    place = jnp.asarray(_placement_matrix(), BF16)

    mk, mv, w_first = _mem_kv(mem, mem_norm_gain.reshape(1, d), w_mem_kv,
                              k_gain_c.reshape(1, C_HEAD_DIM), w_rows)
    hn, aug = _norm_x(x2, norm_gain.reshape(1, d), w_f, b_f, place, seq)
    proj = _in_proj(hn, w_rows, w_first, cscale, gmat)
    ya, merge_weights = _swa(sinks_a.astype(F32), jnp.asarray(_swa_key_bias(), F32), proj,
                             (w_branch_a, w_branch_b, w_branch_c, w_out), batch, seq)
    yb = _fox(proj, aug, batch, seq)
    out = _merge(x2, ya, yb, mk, mv, proj, *merge_weights, seq)
    return out.reshape(batch, seq, d)


def kernel(x, mem, norm_gain, mem_norm_gain, w_in, b_forget, q_gain_a, k_gain_a, sinks_a,
           q_gain_b, k_gain_b, q_gain_c, k_gain_c, w_mem_kv, w_branch_a, w_branch_b,
           w_branch_c, w_out):
    depth = norm_gain.shape[0]
    w_rows = jnp.swapaxes(w_in, 1, 2).reshape(depth, -1, LANES)
    for layer in range(depth):
        x = _layer(x, mem, norm_gain[layer], mem_norm_gain[layer], w_rows[layer], b_forget[layer],
                   q_gain_a[layer], k_gain_a[layer], sinks_a[layer], q_gain_b[layer],
                   k_gain_b[layer], q_gain_c[layer], k_gain_c[layer], w_mem_kv[layer],
                   w_branch_a[layer], w_branch_b[layer], w_branch_c[layer], w_out[layer])
    return x
```

```python
import functools

import jax
import jax.numpy as jnp
import numpy as np
from jax import lax
from jax.experimental import pallas as pl
from jax.experimental.pallas import tpu as pltpu

F32 = jnp.float32
BF16 = jnp.bfloat16

D_MODEL = 2048
HEAD_DIM = 64
A_Q_HEADS = 12
A_KV_HEADS = 4
A_GROUP = A_Q_HEADS // A_KV_HEADS
WINDOW = 128
B_HEADS = 12
C_HEADS = 4
C_HEAD_DIM = 128
A_WIDTH = A_Q_HEADS * HEAD_DIM
A_KV_WIDTH = A_KV_HEADS * HEAD_DIM
B_WIDTH = B_HEADS * HEAD_DIM
C_WIDTH = C_HEADS * C_HEAD_DIM
EPS = 1e-6
NEG = -1e30

LANES = 128
HALF = LANES // 2
VT_ROWS = HALF + 16
LOG2E = float(np.log2(np.e))
N_PIECES = 3
VMEM_LIMIT = 56 * 1024 * 1024

_SRC = {}
_off = 0
for _name, _w in (("qA", A_WIDTH), ("kA", A_KV_WIDTH), ("vA", A_KV_WIDTH), ("zA", A_WIDTH),
                  ("qB", B_WIDTH), ("kB", B_WIDTH), ("vB", B_WIDTH), ("zB", B_WIDTH),
                  ("fB", B_HEADS), ("qC", C_WIDTH), ("zC", C_WIDTH), ("g", 3 * D_MODEL)):
    _SRC[_name] = (_off, _w)
    _off += _w

F_START, F_SHIFT = _SRC["fB"]
_DST = {n: (o if o < F_START else o - F_SHIFT) for n, (o, _) in _SRC.items() if n != "fB"}
PROJ_WIDTH = _SRC["g"][0] + _SRC["g"][1] - F_SHIFT
assert all(o % LANES == 0 for o in _DST.values()) and F_START % LANES == 0


def _col_block(name, width, part=0):
    assert _DST[name] % width == 0
    return _DST[name] // width + part


def _rms(x, gain):
    ms = jnp.mean(x * x, axis=-1, keepdims=True)
    return x * lax.rsqrt(ms + EPS) * gain


def _sigmoid(t):
    return 0.5 * (jnp.tanh(0.5 * t) + 1.0)


def _dot_nt(a, b):
    return lax.dot_general(a, b, (((1,), (1,)), ((), ())), preferred_element_type=F32)


def _mem_kv_kernel(mem_ref, gain_ref, w_ref, kgain_ref, wrows_ref, mk_ref, mv_ref, w0_ref):
    @pl.when(pl.program_id(0) == 0)
    def _():
        _gather_weight_rows(wrows_ref, w0_ref, 0, slice(None), IN_TN)

    hn = _rms(mem_ref[0], gain_ref[...]).astype(BF16)
    kv = jnp.dot(hn, w_ref[...].astype(BF16), preferred_element_type=F32)
    for h in range(C_HEADS):
        sl = slice(h * C_HEAD_DIM, (h + 1) * C_HEAD_DIM)
        mk_ref[0, :, sl] = _rms(kv[:, sl], kgain_ref[...]).astype(BF16)
    mv_ref[0] = kv[:, C_WIDTH:].astype(BF16)


def _mem_kv(mem, gain, w, kgain, w_rows):
    b, m, d = mem.shape
    return pl.pallas_call(
        _mem_kv_kernel,
        grid=(b,),
        in_specs=[
            pl.BlockSpec((1, m, d), lambda i: (i, 0, 0)),
            pl.BlockSpec((1, d), lambda i: (0, 0)),
            pl.BlockSpec((d, 2 * C_WIDTH), lambda i: (0, 0)),
            pl.BlockSpec((1, C_HEAD_DIM), lambda i: (0, 0)),
            pl.BlockSpec((pl.Element(IN_TN * K_CHUNKS), pl.Element(LANES)),
                         lambda i: _w_index(0 * i)),
        ],
        out_specs=[
            pl.BlockSpec((1, m, C_WIDTH), lambda i: (i, 0, 0)),
            pl.BlockSpec((1, m, C_WIDTH), lambda i: (i, 0, 0)),
            pl.BlockSpec((IN_TN, D_MODEL), lambda i: (0, 0)),
        ],
        out_shape=[jax.ShapeDtypeStruct((b, m, C_WIDTH), BF16)] * 2
        + [jax.ShapeDtypeStruct((IN_TN, D_MODEL), BF16)],
        compiler_params=pltpu.CompilerParams(
            dimension_semantics=("arbitrary",), vmem_limit_bytes=VMEM_LIMIT),
        name="mem_kv",
    )(mem, gain, w, kgain, w_rows)


NX_TM = 1024


def _norm_x_kernel(tiles_per_seq, x_ref, gain_ref, wf_ref, bf_ref, place_ref, hn_ref, aug_ref,
                   carry_ref):
    i = pl.program_id(0)

    @pl.when(i % tiles_per_seq == 0)
    def _():
        carry_ref[...] = jnp.zeros_like(carry_ref)

    hn = _rms(x_ref[...], gain_ref[...]).astype(BF16)
    hn_ref[...] = hn
    f_logit = _dot_nt(hn, wf_ref[...]) + bf_ref[...]
    x = jnp.minimum(f_logit, 0.0) - jnp.log1p(jnp.exp(-jnp.abs(f_logit)))
    row = lax.broadcasted_iota(jnp.int32, x.shape, 0)
    shift = 1
    while shift < NX_TM:
        x = x + jnp.where(row >= shift, pltpu.roll(x, shift, 0), 0.0)
        shift *= 2
    c = x + carry_ref[...]
    carry_ref[...] = c[NX_TM - 1:NX_TM, :]
    rest = c * (-LOG2E)
    pieces = []
    for _ in range(N_PIECES):
        p = rest.astype(BF16)
        pieces.append(p)
        rest = rest - p.astype(F32)
    stacked = jnp.concatenate(pieces, axis=1)
    aug_ref[...] = jnp.dot(stacked, place_ref[...], preferred_element_type=F32).astype(BF16)


def _norm_x(x2, gain, wf, bfg, place, seq):
    t, d = x2.shape
    return pl.pallas_call(
        functools.partial(_norm_x_kernel, seq // NX_TM),
        grid=(t // NX_TM,),
        in_specs=[
            pl.BlockSpec((NX_TM, d), lambda i: (i, 0)),
            pl.BlockSpec((1, d), lambda i: (0, 0)),
            pl.BlockSpec((LANES, d), lambda i: (0, 0)),
            pl.BlockSpec((1, LANES), lambda i: (0, 0)),
            pl.BlockSpec((N_PIECES * LANES, B_WIDTH), lambda i: (0, 0)),
        ],
        out_specs=[
            pl.BlockSpec((NX_TM, d), lambda i: (i, 0)),
            pl.BlockSpec((NX_TM, B_WIDTH), lambda i: (i, 0)),
        ],
        out_shape=[jax.ShapeDtypeStruct((t, d), BF16),
                   jax.ShapeDtypeStruct((t, B_WIDTH), BF16)],
        scratch_shapes=[pltpu.VMEM((1, LANES), F32)],
        compiler_params=pltpu.CompilerParams(
            dimension_semantics=("arbitrary",), vmem_limit_bytes=VMEM_LIMIT),
        name="norm_x",
    )(x2, gain, wf, bfg, place)


IN_TM = 1024
IN_TN = 1024
IN_CHUNKS = IN_TN // LANES
K_CHUNKS = D_MODEL // LANES
N_IN_TILES = PROJ_WIDTH // IN_TN
N_ALIGNED_TILES = F_START // IN_TN
assert F_START % IN_TN == 0 and PROJ_WIDTH % IN_TN == 0 and F_SHIFT < LANES

_NORM_KIND = {"qA": HEAD_DIM, "kA": HEAD_DIM, "qB": HEAD_DIM, "kB": HEAD_DIM, "qC": C_HEAD_DIM}


def _chunk_piece(col):
    for name, off in _DST.items():
        if off <= col < off + _SRC[name][1]:
            return name
    raise ValueError(col)


def _norm_prefix(tile):
    kinds = [_chunk_piece(tile * IN_TN + c * LANES) in _NORM_KIND for c in range(IN_CHUNKS)]
    n = sum(kinds)
    assert kinds == [True] * n + [False] * (IN_CHUNKS - n)
    return n


NORM_PREFIX = tuple(_norm_prefix(j) for j in range(N_IN_TILES))
NORM_TILES = tuple(j for j in range(N_IN_TILES) if NORM_PREFIX[j])


def _norm_slot(j):
    slot = 0
    for tile in NORM_TILES[1:]:
        slot = slot + (j >= tile).astype(jnp.int32)
    return slot


IN_MCH = 256


def _gather_weight_rows(w_ref, dst_ref, src_row, dst_rows, n_rows):
    for c in range(K_CHUNKS):
        dst_ref[dst_rows, c * LANES:(c + 1) * LANES] = (
            w_ref[pl.ds(src_row + c, n_rows, stride=K_CHUNKS), :].astype(BF16))


def _w_index(tile):
    col = tile * IN_TN + jnp.where(tile >= N_ALIGNED_TILES, F_SHIFT, 0)
    return (col * K_CHUNKS, 0)


def _in_proj_kernel(n_row_tiles, hn_ref, w_ref, w0_ref, cscale_ref, gmat_ref, o_ref,
                    wbf_a, wbf_b, acc_ref):
    j = pl.program_id(0)
    i = pl.program_id(1)
    share = IN_TN // n_row_tiles

    @pl.when((j == 0) & (i == 0))
    def _():
        wbf_a[...] = w0_ref[...]

    def epilogue(chunk, prefix):
        rows = slice(chunk * IN_MCH, (chunk + 1) * IN_MCH)
        acc = acc_ref.at[chunk % 2]
        for c in range(prefix // 2):
            sl = slice(c * 2 * LANES, (c + 1) * 2 * LANES)
            a = acc[:, sl]
            ms = jnp.dot((a * a).astype(BF16), gmat_ref[c], preferred_element_type=F32)
            o_ref[rows, sl] = (a * lax.rsqrt(ms + EPS) * cscale_ref[:, sl]).astype(BF16)
        if prefix < IN_CHUNKS:
            rest = slice(prefix * LANES, IN_TN)
            o_ref[rows, rest] = acc[:, rest].astype(BF16)

    def step(cur, nxt, prefix):
        src = pl.multiple_of(i * (share * K_CHUNKS), share * K_CHUNKS)
        dst = pl.ds(pl.multiple_of(i * share, share), share)
        n_chunks = hn_ref.shape[0] // IN_MCH
        for c in range(n_chunks + 1):
            if c < n_chunks:
                acc_ref[c % 2] = _dot_nt(hn_ref[c * IN_MCH:(c + 1) * IN_MCH, :], cur[...])
            if c == 0:
                _gather_weight_rows(w_ref, nxt, src, dst, share)
            if c >= 1:
                epilogue(c - 1, prefix)

    def is_any(tiles):
        cond = j == tiles[0]
        for tile in tiles[1:]:
            cond = cond | (j == tile)
        return cond

    for parity, prefix in sorted({(t % 2, NORM_PREFIX[t]) for t in range(N_IN_TILES)}):
        tiles = [t for t in range(N_IN_TILES) if (t % 2, NORM_PREFIX[t]) == (parity, prefix)]
        cur, nxt = (wbf_a, wbf_b) if parity == 0 else (wbf_b, wbf_a)
        pl.when(is_any(tiles))(functools.partial(step, cur, nxt, prefix))


def _in_proj(hn, w_rows, w_first, cscale, gmat):
    t, d = hn.shape
    n_row_tiles = t // IN_TM
    assert IN_TN % n_row_tiles == 0 and (IN_TN // n_row_tiles) % 16 == 0
    return pl.pallas_call(
        functools.partial(_in_proj_kernel, n_row_tiles),
        grid=(N_IN_TILES, n_row_tiles),
        in_specs=[
            pl.BlockSpec((IN_TM, d), lambda j, i: (i, 0)),
            pl.BlockSpec((pl.Element(IN_TN * K_CHUNKS), pl.Element(LANES)),
                         lambda j, i: _w_index(jnp.minimum(j + 1, N_IN_TILES - 1))),
            pl.BlockSpec((IN_TN, d), lambda j, i: (0, 0), pipeline_mode=pl.Buffered(1)),
            pl.BlockSpec((1, IN_TN), lambda j, i: (0, _norm_slot(j))),
            pl.BlockSpec((IN_CHUNKS // 2, 2 * LANES, 2 * LANES),
                         lambda j, i: (_norm_slot(j), 0, 0)),
        ],
        out_specs=pl.BlockSpec((IN_TM, IN_TN), lambda j, i: (i, j)),
        out_shape=jax.ShapeDtypeStruct((t, PROJ_WIDTH), BF16),
        scratch_shapes=[
            pltpu.VMEM((IN_TN, d), BF16),
            pltpu.VMEM((IN_TN, d), BF16),
            pltpu.VMEM((2, IN_MCH, IN_TN), F32),
        ],
        compiler_params=pltpu.CompilerParams(
            dimension_semantics=("arbitrary", "arbitrary"), vmem_limit_bytes=VMEM_LIMIT),
        name="in_proj",
    )(hn, w_rows, w_first, cscale, gmat)


def _aug_lane(head):
    return HALF if head % 2 == 0 else 0


def _placement_matrix():
    place = np.zeros((N_PIECES * LANES, B_WIDTH), np.float32)
    for h in range(B_HEADS):
        for p in range(N_PIECES):
            place[p * LANES + h, (h // 2) * LANES + _aug_lane(h) + p] = 1.0
    return place


SWA_QB = 8
SWA_WIN = (SWA_QB + 1) * WINDOW


def _alibi_slopes_log2():
    return [float(2.0 ** (-8.0 * (h + 1) / A_Q_HEADS)) * LOG2E for h in range(A_Q_HEADS)]


def _swa_key_bias():
    key = np.arange(2 * WINDOW)[:, None]
    qry = np.arange(WINDOW)[None, :]
    rel = qry + WINDOW - key
    visible = (rel >= 0) & (rel < WINDOW)
    slopes = np.asarray(_alibi_slopes_log2(), np.float32)[:, None, None]
    return np.where(visible[None], slopes * key[None].astype(np.float32), np.float32(NEG))


def _swa_kernel(n_cast, sinks_ref, bias_ref, q_ref, kp_ref, kc_ref, vp_ref, vc_ref, *refs):
    cast_in, o_ref, cast_out = refs[:n_cast], refs[n_cast], refs[n_cast + 1:2 * n_cast + 1]
    kop_ref, vt_ref, s_ref, p_ref, sh_ref, ot_ref = refs[2 * n_cast + 1:]
    for src, dst in zip(cast_in, cast_out):
        dst[...] = src[...].astype(BF16)
    step = pl.program_id(1)
    blk = WINDOW
    lane = lax.broadcasted_iota(jnp.int32, (SWA_WIN, LANES), 1)

    kwin = jnp.concatenate([kp_ref[...], kc_ref[...]], axis=0).astype(F32)
    vwin_t = jnp.concatenate([vp_ref[...], vc_ref[...]], axis=0).astype(F32).T
    ones_rows = jnp.where(
        lax.broadcasted_iota(jnp.int32, (VT_ROWS - HALF, SWA_WIN), 0) == 0, 1.0, 0.0)
    for g in range(A_KV_HEADS):
        slab = kwin[:, (g // 2) * LANES:(g // 2 + 1) * LANES]
        own = (lane < HALF) if g % 2 == 0 else (lane >= HALF)
        kz = jnp.where(own, slab, 0.0)
        kop_ref[g, 0] = kz.astype(BF16)
        kop_ref[g, 1] = pltpu.roll(kz, HALF, 1).astype(BF16)
        vt_ref[g] = jnp.concatenate(
            [vwin_t[g * HEAD_DIM:(g + 1) * HEAD_DIM], ones_rows], axis=0).astype(BF16)

    t_win = lax.broadcasted_iota(jnp.int32, (1, blk), 1).astype(F32) + float(blk)
    slopes = _alibi_slopes_log2()
    sinks = [sinks_ref[h] * LOG2E + slopes[h] * t_win for h in range(A_Q_HEADS)]
    key_row = lax.broadcasted_iota(jnp.int32, (2 * blk, blk), 0)

    items = [(i, g) for i in range(SWA_QB) for g in range(A_KV_HEADS)]
    keys_of = lambda i: slice(i * blk, (i + 2) * blk)

    def stage_scores(n):
        i, g = items[n]
        for k in range(A_GROUP):
            h = g * A_GROUP + k
            q_slab = q_ref[i * blk:(i + 1) * blk, (h // 2) * LANES:(h // 2 + 1) * LANES]
            s_ref[n % 2, k] = _dot_nt(kop_ref[g, (h + g) % 2, keys_of(i), :], q_slab)

    def stage_softmax(n):
        i, g = items[n]
        for k in range(A_GROUP):
            h = g * A_GROUP + k
            st = s_ref[n % 2, k] + bias_ref[h]
            if i == 0:
                st = jnp.where(((step * SWA_QB - 1) * blk + key_row) >= 0, st, NEG)
            m = jnp.maximum(jnp.max(st, axis=0, keepdims=True), sinks[h])
            p_ref[n % 2, k] = jnp.exp2(st - m).astype(BF16)
            sh_ref[n % 2, k] = jnp.exp2(sinks[h] - m)

    def stage_values(n):
        i, g = items[n]
        for k in range(A_GROUP):
            h = g * A_GROUP + k
            ot = jnp.dot(vt_ref[g, :, keys_of(i)], p_ref[n % 2, k],
                         preferred_element_type=F32)
            ot_ref[h * HEAD_DIM:(h + 1) * HEAD_DIM, :] = (
                ot[0:HEAD_DIM] / (ot[HEAD_DIM:HEAD_DIM + 1] + sh_ref[n % 2, k]))
        if g == A_KV_HEADS - 1:
            o_ref[i * blk:(i + 1) * blk, :] = ot_ref[...].T.astype(BF16)

    for n in range(len(items) + 2):
        if n < len(items):
            stage_scores(n)
        if 2 <= n:
            stage_values(n - 2)
        if 1 <= n <= len(items):
            stage_softmax(n - 1)


def _swa(sinks, bias, proj, cast_weights, batch, seq):
    blk = WINDOW
    nb = seq // blk
    steps = nb // SWA_QB
    n_steps = batch * steps
    qcol = _col_block("qA", A_WIDTH)
    kcol = _col_block("kA", A_KV_WIDTH)
    vcol = _col_block("vA", A_KV_WIDTH)
    cur = lambda b, s: b * steps + s
    prev = lambda b, s: b * nb + jnp.maximum(s * SWA_QB - 1, 0)
    slab_specs = []
    for w in cast_weights:
        assert w.shape[0] % (16 * n_steps) == 0
        slab_specs.append(pl.BlockSpec((w.shape[0] // n_steps, w.shape[1]),
                                       lambda b, s: (cur(b, s), 0)))
    outs = pl.pallas_call(
        functools.partial(_swa_kernel, len(cast_weights)),
        grid=(batch, steps),
        in_specs=[
            pl.BlockSpec(memory_space=pltpu.SMEM),
            pl.BlockSpec(bias.shape, lambda b, s: (0, 0, 0)),
            pl.BlockSpec((SWA_QB * blk, A_WIDTH), lambda b, s: (cur(b, s), qcol)),
            pl.BlockSpec((blk, A_KV_WIDTH), lambda b, s: (prev(b, s), kcol)),
            pl.BlockSpec((SWA_QB * blk, A_KV_WIDTH), lambda b, s: (cur(b, s), kcol)),
            pl.BlockSpec((blk, A_KV_WIDTH), lambda b, s: (prev(b, s), vcol)),
            pl.BlockSpec((SWA_QB * blk, A_KV_WIDTH), lambda b, s: (cur(b, s), vcol)),
            *slab_specs,
        ],
        out_specs=[pl.BlockSpec((SWA_QB * blk, A_WIDTH), lambda b, s: (cur(b, s), 0)),
                   *slab_specs],
        out_shape=[jax.ShapeDtypeStruct((batch * seq, A_WIDTH), BF16),
                   *[jax.ShapeDtypeStruct(w.shape, BF16) for w in cast_weights]],
        scratch_shapes=[
            pltpu.VMEM((A_KV_HEADS, 2, SWA_WIN, LANES), BF16),
            pltpu.VMEM((A_KV_HEADS, VT_ROWS, SWA_WIN), BF16),
            pltpu.VMEM((2, A_GROUP, 2 * blk, blk), F32),
            pltpu.VMEM((2, A_GROUP, 2 * blk, blk), BF16),
            pltpu.VMEM((2, A_GROUP, 1, blk), F32),
            pltpu.VMEM((A_WIDTH, blk), F32),
        ],
        compiler_params=pltpu.CompilerParams(
            dimension_semantics=("arbitrary", "arbitrary"), vmem_limit_bytes=VMEM_LIMIT),
        name="swa",
    )(sinks, bias, proj, proj, proj, proj, proj, *cast_weights)
    return outs[0], outs[1:]


FOX_T = 512


FOX_H = FOX_T // 2


def _fox_kernel(q_ref, k_ref, v_ref, aug_ref, o_ref,
                kop_ref, vt_ref, qop_ref, s_ref, cm_ref, p_ref, al_ref, m_ref, acc_ref):
    seq = k_ref.shape[0]
    t = FOX_T
    hk = FOX_H

    lane = lax.broadcasted_iota(jnp.int32, (hk, LANES), 1)
    low = lane < HALF

    def build(r):
        sl = slice(r * hk, (r + 1) * hk)
        kk = k_ref[sl, :].astype(F32)
        aa = aug_ref[sl, :].astype(F32)
        vv = v_ref[sl, :].astype(F32)
        kop_ref[0, sl, :] = jnp.where(low, kk, aa).astype(BF16)
        kop_ref[1, sl, :] = jnp.where(low, aa, kk).astype(BF16)
        vvt = vv.T
        ones_rows = jnp.where(
            lax.broadcasted_iota(jnp.int32, (VT_ROWS - HALF, hk), 0) == 0, 1.0, 0.0)
        for hh in range(2):
            vt_ref[hh, r] = jnp.concatenate(
                [vvt[hh * HALF:(hh + 1) * HALF], ones_rows], axis=0).astype(BF16)

    def scores(slot, half, masked):
        ks = pl.ds(pl.multiple_of(half * hk, hk), hk)
        for hh in range(2):
            st = _dot_nt(kop_ref[hh, ks, :], qop_ref[hh])
            if masked:
                key = lax.broadcasted_iota(jnp.int32, (hk, t), 0) + slot * hk
                qry = lax.broadcasted_iota(jnp.int32, (hk, t), 1)
                st = jnp.where(key <= qry, st, NEG)
            s_ref[slot, hh] = st
            cm_ref[slot, hh] = jnp.max(st, axis=0, keepdims=True)

    def soft(slot):
        for hh in range(2):
            m_old = m_ref[hh]
            m_new = jnp.maximum(m_old, cm_ref[slot, hh])
            al_ref[slot, hh] = jnp.exp2(m_old - m_new)
            p_ref[slot, hh] = jnp.exp2(s_ref[slot, hh] - m_new).astype(BF16)
            m_ref[hh] = m_new

    def pv(slot, half):
        for hh in range(2):
            acc_ref[hh] = acc_ref[hh] * al_ref[slot, hh] + jnp.dot(
                vt_ref[hh, half], p_ref[slot, hh], preferred_element_type=F32)

    def step(i, prev, has_next, in_loop=True):
        if in_loop:
            pv(0, 2 * prev)
            soft(0)
            scores(1, 2 * i + 1, False)
            if has_next:
                scores(0, 2 * i + 2, False)
            pv(1, 2 * prev + 1)
            soft(1)
        else:
            scores(1, 2 * i + 1, False)
            pv(0, 2 * prev)
            soft(0)
            pv(1, 2 * prev + 1)
            if has_next:
                scores(0, 2 * i + 2, False)
            soft(1)

    lane_q = lax.broadcasted_iota(jnp.int32, (t, LANES), 1)
    low_q = lane_q < HALF
    ones0 = jnp.where((lane_q >= HALF) & (lane_q < HALF + N_PIECES), 1.0, 0.0)
    ones1 = jnp.where(lane_q < N_PIECES, 1.0, 0.0)

    for qi in range(seq // t):
        build(2 * qi)
        build(2 * qi + 1)
        rows = slice(qi * t, (qi + 1) * t)
        q = q_ref[rows, :].astype(F32)
        qop_ref[0] = jnp.where(low_q, q, ones0).astype(BF16)
        qop_ref[1] = jnp.where(low_q, ones1, q).astype(BF16)
        m_ref[...] = jnp.full(m_ref.shape, NEG, F32)
        acc_ref[...] = jnp.zeros(acc_ref.shape, F32)

        scores(0, 2 * qi, True)
        scores(1, 2 * qi + 1, True)
        soft(0)
        if qi > 0:
            scores(0, 0, False)
        soft(1)

        loop_pairs = max(qi - 1, 0) // 2

        def pair(j, carry, qi=qi):
            first = 2 * j
            step(first, jnp.where(j == 0, qi, first - 1), True)
            step(first + 1, first, True)
            return carry

        if loop_pairs > 1:
            lax.fori_loop(0, loop_pairs, pair, 0)
        static_from = 2 * loop_pairs if loop_pairs > 1 else 0
        for i in range(static_from, qi):
            step(i, i - 1 if i > 0 else qi, i + 1 < qi, in_loop=False)
        last = qi - 1 if qi > 0 else qi
        pv(0, 2 * last)
        pv(1, 2 * last + 1)

        ot = jnp.concatenate(
            [acc_ref[hh, 0:HALF, :] / acc_ref[hh, HALF:HALF + 1, :] for hh in range(2)], axis=0)
        o_ref[rows, :] = ot.T.astype(BF16)


def _fox(proj, aug, batch, seq):
    t = FOX_T
    nq = seq // t
    pairs = B_HEADS // 2
    qcol = _col_block("qB", LANES)
    kcol = _col_block("kB", LANES)
    vcol = _col_block("vB", LANES)
    return pl.pallas_call(
        _fox_kernel,
        grid=(batch, pairs),
        in_specs=[
            pl.BlockSpec((seq, LANES), lambda b, p: (b, qcol + p)),
            pl.BlockSpec((seq, LANES), lambda b, p: (b, kcol + p)),
            pl.BlockSpec((seq, LANES), lambda b, p: (b, vcol + p)),
            pl.BlockSpec((seq, LANES), lambda b, p: (b, p)),
        ],
        out_specs=pl.BlockSpec((seq, LANES), lambda b, p: (b, p)),
        out_shape=jax.ShapeDtypeStruct((batch * seq, B_WIDTH), BF16),
        scratch_shapes=[
            pltpu.VMEM((2, seq, LANES), BF16),
            pltpu.VMEM((2, seq // FOX_H, VT_ROWS, FOX_H), BF16),
            pltpu.VMEM((2, t, LANES), BF16),
            pltpu.VMEM((2, 2, FOX_H, t), F32),
            pltpu.VMEM((2, 2, 1, t), F32),
            pltpu.VMEM((2, 2, FOX_H, t), BF16),
            pltpu.VMEM((2, 2, 1, t), F32),
            pltpu.VMEM((2, 1, t), F32),
            pltpu.VMEM((2, VT_ROWS, t), F32),
        ],
        compiler_params=pltpu.CompilerParams(
            dimension_semantics=("arbitrary", "arbitrary"),
            vmem_limit_bytes=VMEM_LIMIT),
        name="fox",
    )(proj, proj, proj, aug)


MERGE_TM = 256


Z_BLOCK = 256
Z_PARTS = A_WIDTH // Z_BLOCK
assert A_WIDTH == B_WIDTH and A_WIDTH % Z_BLOCK == 0


def _merge_kernel(x_ref, ya_ref, yb_ref, mk_ref, mv_ref, qc_ref, *refs):
    za_refs = refs[:Z_PARTS]
    zb_refs = refs[Z_PARTS:2 * Z_PARTS]
    zc_ref, g0_ref, g1_ref, g2_ref, wa_ref, wb_ref, wc_ref, wo_ref, o_ref = refs[2 * Z_PARTS:]

    def memory_attention():
        outs = []
        for h in range(C_HEADS):
            sl = slice(h * C_HEAD_DIM, (h + 1) * C_HEAD_DIM)
            s = _dot_nt(qc_ref[:, sl], mk_ref[0, :, sl])
            p = jnp.exp(s - jnp.max(s, axis=-1, keepdims=True))
            o = jnp.dot(p.astype(BF16), mv_ref[0, :, sl], preferred_element_type=F32)
            outs.append(o / jnp.sum(p, axis=-1, keepdims=True))
        return jnp.concatenate(outs, axis=1)

    def branch(y, z_refs, w_ref):
        z = jnp.concatenate([r[...] for r in z_refs], axis=1).astype(F32)
        h = (y * (z * _sigmoid(z))).astype(BF16)
        return jnp.dot(h, w_ref[...], preferred_element_type=F32)

    y = _sigmoid(g0_ref[...].astype(F32)) * branch(ya_ref[...].astype(F32), za_refs, wa_ref)
    y = y + _sigmoid(g1_ref[...].astype(F32)) * branch(yb_ref[...].astype(F32), zb_refs, wb_ref)
    y = y + _sigmoid(g2_ref[...].astype(F32)) * branch(memory_attention(), (zc_ref,), wc_ref)
    o_ref[...] = x_ref[...] + jnp.dot(y.astype(BF16), wo_ref[...], preferred_element_type=F32)


def _merge(x2, ya, yb, mk, mv, proj, wa, wb, wc, wo, seq):
    t, d = x2.shape
    tm = MERGE_TM
    tiles_per_seq = seq // tm
    gcol = _col_block("g", D_MODEL)
    row = lambda width, col: pl.BlockSpec((tm, width), lambda i: (i, col))
    full = lambda a: pl.BlockSpec(a.shape, lambda i: (0, 0), pipeline_mode=pl.Buffered(1))
    mem = lambda a: pl.BlockSpec((1,) + a.shape[1:], lambda i: (i // tiles_per_seq, 0, 0))
    z_specs = lambda name: [row(Z_BLOCK, _col_block(name, Z_BLOCK, part)) for part in range(Z_PARTS)]
    n_proj = 2 * Z_PARTS + 5
    return pl.pallas_call(
        _merge_kernel,
        grid=(t // tm,),
        in_specs=[
            row(d, 0), row(A_WIDTH, 0), row(B_WIDTH, 0), mem(mk), mem(mv),
            row(C_WIDTH, _col_block("qC", C_WIDTH)),
            *z_specs("zA"), *z_specs("zB"), row(C_WIDTH, _col_block("zC", C_WIDTH)),
            row(d, gcol), row(d, gcol + 1), row(d, gcol + 2),
            full(wa), full(wb), full(wc), full(wo),
        ],
        out_specs=pl.BlockSpec((tm, d), lambda i: (i, 0)),
        out_shape=jax.ShapeDtypeStruct((t, d), F32),
        compiler_params=pltpu.CompilerParams(
            dimension_semantics=("arbitrary",), vmem_limit_bytes=VMEM_LIMIT),
        name="merge",
    )(x2, ya, yb, mk, mv, *([proj] * n_proj), wa, wb, wc, wo)


def _group_mean_matrices():
    lane = np.arange(2 * LANES)
    chunk = lane // LANES
    mats = []
    for tile in NORM_TILES:
        for c in range(0, IN_CHUNKS, 2):
            dims = np.asarray([_NORM_KIND.get(_chunk_piece(tile * IN_TN + (c + k) * LANES), LANES)
                               for k in range(2)])[chunk]
            same = ((lane[:, None] // dims[:, None] == lane[None, :] // dims[None, :])
                    & (chunk[:, None] == chunk[None, :]))
            mats.append(same.astype(np.float32) / dims[None, :])
    return np.stack(mats)


def _column_scales(gains):
    scale = {"qA": HEAD_DIM ** -0.5 * LOG2E, "qB": HEAD_DIM ** -0.5 * LOG2E,
             "qC": C_HEAD_DIM ** -0.5}
    cols = []
    for tile in NORM_TILES:
        for c in range(IN_CHUNKS):
            piece = _chunk_piece(tile * IN_TN + c * LANES)
            if piece in _NORM_KIND:
                g = gains[piece].astype(F32) * scale.get(piece, 1.0)
                cols.append(jnp.tile(g, LANES // g.shape[0]))
            else:
                cols.append(jnp.zeros((LANES,), F32))
    return jnp.concatenate(cols).reshape(1, len(NORM_TILES) * IN_TN)


def _layer(x, mem, norm_gain, mem_norm_gain, w_rows, b_forget, q_gain_a, k_gain_a, sinks_a,
           q_gain_b, k_gain_b, q_gain_c, k_gain_c, w_mem_kv, w_branch_a, w_branch_b,
           w_branch_c, w_out):
    batch, seq, d = x.shape
    x2 = x.reshape(batch * seq, d)

    w_f = w_rows[F_START * K_CHUNKS:(F_START + F_SHIFT) * K_CHUNKS].reshape(F_SHIFT, d)
    w_f = jnp.pad(w_f, ((0, LANES - F_SHIFT), (0, 0))).astype(BF16)
    b_f = jnp.pad(b_forget.astype(F32), (0, LANES - F_SHIFT)).reshape(1, LANES)
    cscale = _column_scales({"qA": q_gain_a, "kA": k_gain_a, "qB": q_gain_b, "kB": k_gain_b,
                             "qC": q_gain_c})
    gmat = jnp.asarray(_group_mean_matrices(), BF16)
    place = jnp.asarray(_placement_matrix(), BF16)

    mk, mv, w_first = _mem_kv(mem, mem_norm_gain.reshape(1, d), w_mem_kv,
                              k_gain_c.reshape(1, C_HEAD_DIM), w_rows)
    hn, aug = _norm_x(x2, norm_gain.reshape(1, d), w_f, b_f, place, seq)
    proj = _in_proj(hn, w_rows, w_first, cscale, gmat)
    ya, merge_weights = _swa(sinks_a.astype(F32), jnp.asarray(_swa_key_bias(), F32), proj,
                             (w_branch_a, w_branch_b, w_branch_c, w_out), batch, seq)
    yb = _fox(proj, aug, batch, seq)
    out = _merge(x2, ya, yb, mk, mv, proj, *merge_weights, seq)
    return out.reshape(batch, seq, d)


def kernel(x, mem, norm_gain, mem_norm_gain, w_in, b_forget, q_gain_a, k_gain_a, sinks_a,
           q_gain_b, k_gain_b, q_gain_c, k_gain_c, w_mem_kv, w_branch_a, w_branch_b,
           w_branch_c, w_out):
    depth = norm_gain.shape[0]
    w_rows = jnp.swapaxes(w_in, 1, 2).reshape(depth, -1, LANES)
    for layer in range(depth):
        x = _layer(x, mem, norm_gain[layer], mem_norm_gain[layer], w_rows[layer], b_forget[layer],
                   q_gain_a[layer], k_gain_a[layer], sinks_a[layer], q_gain_b[layer],
                   k_gain_b[layer], q_gain_c[layer], k_gain_c[layer], w_mem_kv[layer],
                   w_branch_a[layer], w_branch_b[layer], w_branch_c[layer], w_out[layer])
    return x
```

```python
import functools

import jax
import jax.numpy as jnp
import numpy as np
from jax import lax
from jax.experimental import pallas as pl
from jax.experimental.pallas import tpu as pltpu

F32 = jnp.float32
BF16 = jnp.bfloat16

D_MODEL = 2048
HEAD_DIM = 64
A_Q_HEADS = 12
A_KV_HEADS = 4
A_GROUP = A_Q_HEADS // A_KV_HEADS
WINDOW = 128
B_HEADS = 12
C_HEADS = 4
C_HEAD_DIM = 128
A_WIDTH = A_Q_HEADS * HEAD_DIM
A_KV_WIDTH = A_KV_HEADS * HEAD_DIM
B_WIDTH = B_HEADS * HEAD_DIM
C_WIDTH = C_HEADS * C_HEAD_DIM
EPS = 1e-6
NEG = -1e30

LANES = 128
HALF = LANES // 2
VT_ROWS = HALF + 16
LOG2E = float(np.log2(np.e))
N_PIECES = 3
VMEM_LIMIT = 56 * 1024 * 1024

_SRC = {}
_off = 0
for _name, _w in (("qA", A_WIDTH), ("kA", A_KV_WIDTH), ("vA", A_KV_WIDTH), ("zA", A_WIDTH),
                  ("qB", B_WIDTH), ("kB", B_WIDTH), ("vB", B_WIDTH), ("zB", B_WIDTH),
                  ("fB", B_HEADS), ("qC", C_WIDTH), ("zC", C_WIDTH), ("g", 3 * D_MODEL)):
    _SRC[_name] = (_off, _w)
    _off += _w

F_START, F_SHIFT = _SRC["fB"]
_DST = {n: (o if o < F_START else o - F_SHIFT) for n, (o, _) in _SRC.items() if n != "fB"}
PROJ_WIDTH = _SRC["g"][0] + _SRC["g"][1] - F_SHIFT
assert all(o % LANES == 0 for o in _DST.values()) and F_START % LANES == 0


def _col_block(name, width, part=0):
    assert _DST[name] % width == 0
    return _DST[name] // width + part


def _rms(x, gain):
    ms = jnp.mean(x * x, axis=-1, keepdims=True)
    return x * lax.rsqrt(ms + EPS) * gain


def _sigmoid(t):
    return 0.5 * (jnp.tanh(0.5 * t) + 1.0)


def _dot_nt(a, b):
    return lax.dot_general(a, b, (((1,), (1,)), ((), ())), preferred_element_type=F32)


def _mem_kv_kernel(mem_ref, gain_ref, w_ref, kgain_ref, wrows_ref, mk_ref, mv_ref, w0_ref):
    @pl.when(pl.program_id(0) == 0)
    def _():
        _gather_weight_rows(wrows_ref, w0_ref, 0, slice(None), IN_TN)

    hn = _rms(mem_ref[0], gain_ref[...]).astype(BF16)
    kv = jnp.dot(hn, w_ref[...].astype(BF16), preferred_element_type=F32)
    for h in range(C_HEADS):
        sl = slice(h * C_HEAD_DIM, (h + 1) * C_HEAD_DIM)
        mk_ref[0, :, sl] = _rms(kv[:, sl], kgain_ref[...]).astype(BF16)
    mv_ref[0] = kv[:, C_WIDTH:].astype(BF16)


def _mem_kv(mem, gain, w, kgain, w_rows):
    b, m, d = mem.shape
    return pl.pallas_call(
        _mem_kv_kernel,
        grid=(b,),
        in_specs=[
            pl.BlockSpec((1, m, d), lambda i: (i, 0, 0)),
            pl.BlockSpec((1, d), lambda i: (0, 0)),
            pl.BlockSpec((d, 2 * C_WIDTH), lambda i: (0, 0)),
            pl.BlockSpec((1, C_HEAD_DIM), lambda i: (0, 0)),
            pl.BlockSpec((pl.Element(IN_TN * K_CHUNKS), pl.Element(LANES)),
                         lambda i: _w_index(0 * i)),
        ],
        out_specs=[
            pl.BlockSpec((1, m, C_WIDTH), lambda i: (i, 0, 0)),
            pl.BlockSpec((1, m, C_WIDTH), lambda i: (i, 0, 0)),
            pl.BlockSpec((IN_TN, D_MODEL), lambda i: (0, 0)),
        ],
        out_shape=[jax.ShapeDtypeStruct((b, m, C_WIDTH), BF16)] * 2
        + [jax.ShapeDtypeStruct((IN_TN, D_MODEL), BF16)],
        compiler_params=pltpu.CompilerParams(
            dimension_semantics=("arbitrary",), vmem_limit_bytes=VMEM_LIMIT),
        name="mem_kv",
    )(mem, gain, w, kgain, w_rows)


NX_TM = 1024


def _norm_x_kernel(tiles_per_seq, x_ref, gain_ref, wf_ref, bf_ref, place_ref, hn_ref, aug_ref,
                   carry_ref):
    i = pl.program_id(0)

    @pl.when(i % tiles_per_seq == 0)
    def _():
        carry_ref[...] = jnp.zeros_like(carry_ref)

    hn = _rms(x_ref[...], gain_ref[...]).astype(BF16)
    hn_ref[...] = hn
    f_logit = _dot_nt(hn, wf_ref[...]) + bf_ref[...]
    x = jnp.minimum(f_logit, 0.0) - jnp.log1p(jnp.exp(-jnp.abs(f_logit)))
    row = lax.broadcasted_iota(jnp.int32, x.shape, 0)
    shift = 1
    while shift < NX_TM:
        x = x + jnp.where(row >= shift, pltpu.roll(x, shift, 0), 0.0)
        shift *= 2
    c = x + carry_ref[...]
    carry_ref[...] = c[NX_TM - 1:NX_TM, :]
    rest = c * (-LOG2E)
    pieces = []
    for _ in range(N_PIECES):
        p = rest.astype(BF16)
        pieces.append(p)
        rest = rest - p.astype(F32)
    stacked = jnp.concatenate(pieces, axis=1)
    aug_ref[...] = jnp.dot(stacked, place_ref[...], preferred_element_type=F32).astype(BF16)


def _norm_x(x2, gain, wf, bfg, place, seq):
    t, d = x2.shape
    return pl.pallas_call(
        functools.partial(_norm_x_kernel, seq // NX_TM),
        grid=(t // NX_TM,),
        in_specs=[
            pl.BlockSpec((NX_TM, d), lambda i: (i, 0)),
            pl.BlockSpec((1, d), lambda i: (0, 0)),
            pl.BlockSpec((LANES, d), lambda i: (0, 0)),
            pl.BlockSpec((1, LANES), lambda i: (0, 0)),
            pl.BlockSpec((N_PIECES * LANES, B_WIDTH), lambda i: (0, 0)),
        ],
        out_specs=[
            pl.BlockSpec((NX_TM, d), lambda i: (i, 0)),
            pl.BlockSpec((NX_TM, B_WIDTH), lambda i: (i, 0)),
        ],
        out_shape=[jax.ShapeDtypeStruct((t, d), BF16),
                   jax.ShapeDtypeStruct((t, B_WIDTH), BF16)],
        scratch_shapes=[pltpu.VMEM((1, LANES), F32)],
        compiler_params=pltpu.CompilerParams(
            dimension_semantics=("arbitrary",), vmem_limit_bytes=VMEM_LIMIT),
        name="norm_x",
    )(x2, gain, wf, bfg, place)


IN_TM = 1024
IN_TN = 1024
IN_CHUNKS = IN_TN // LANES
K_CHUNKS = D_MODEL // LANES
N_IN_TILES = PROJ_WIDTH // IN_TN
N_ALIGNED_TILES = F_START // IN_TN
assert F_START % IN_TN == 0 and PROJ_WIDTH % IN_TN == 0 and F_SHIFT < LANES

_NORM_KIND = {"qA": HEAD_DIM, "kA": HEAD_DIM, "qB": HEAD_DIM, "kB": HEAD_DIM, "qC": C_HEAD_DIM}


def _chunk_piece(col):
    for name, off in _DST.items():
        if off <= col < off + _SRC[name][1]:
            return name
    raise ValueError(col)


def _norm_prefix(tile):
    kinds = [_chunk_piece(tile * IN_TN + c * LANES) in _NORM_KIND for c in range(IN_CHUNKS)]
    n = sum(kinds)
    assert kinds == [True] * n + [False] * (IN_CHUNKS - n)
    return n


NORM_PREFIX = tuple(_norm_prefix(j) for j in range(N_IN_TILES))
NORM_TILES = tuple(j for j in range(N_IN_TILES) if NORM_PREFIX[j])


def _norm_slot(j):
    slot = 0
    for tile in NORM_TILES[1:]:
        slot = slot + (j >= tile).astype(jnp.int32)
    return slot


IN_MCH = 1024


def _gather_weight_rows(w_ref, dst_ref, src_row, dst_rows, n_rows):
    for c in range(K_CHUNKS):
        dst_ref[dst_rows, c * LANES:(c + 1) * LANES] = (
            w_ref[pl.ds(src_row + c, n_rows, stride=K_CHUNKS), :].astype(BF16))


def _w_index(tile):
    col = tile * IN_TN + jnp.where(tile >= N_ALIGNED_TILES, F_SHIFT, 0)
    return (col * K_CHUNKS, 0)


def _in_proj_kernel(n_row_tiles, hn_ref, w_ref, w0_ref, cscale_ref, gmat_ref, o_ref,
                    wbf_a, wbf_b, acc_ref):
    j = pl.program_id(0)
    i = pl.program_id(1)
    share = IN_TN // n_row_tiles

    @pl.when((j == 0) & (i == 0))
    def _():
        wbf_a[...] = w0_ref[...]

    def epilogue(chunk, prefix):
        rows = slice(chunk * IN_MCH, (chunk + 1) * IN_MCH)
        acc = acc_ref.at[chunk % 2]
        for c in range(prefix // 2):
            sl = slice(c * 2 * LANES, (c + 1) * 2 * LANES)
            a = acc[:, sl]
            ms = jnp.dot((a * a).astype(BF16), gmat_ref[c], preferred_element_type=F32)
            o_ref[rows, sl] = (a * lax.rsqrt(ms + EPS) * cscale_ref[:, sl]).astype(BF16)
        if prefix < IN_CHUNKS:
            rest = slice(prefix * LANES, IN_TN)
            o_ref[rows, rest] = acc[:, rest].astype(BF16)

    def step(cur, nxt, prefix):
        src = pl.multiple_of(i * (share * K_CHUNKS), share * K_CHUNKS)
        dst = pl.ds(pl.multiple_of(i * share, share), share)
        n_chunks = hn_ref.shape[0] // IN_MCH
        for c in range(n_chunks + 1):
            if c < n_chunks:
                acc_ref[c % 2] = _dot_nt(hn_ref[c * IN_MCH:(c + 1) * IN_MCH, :], cur[...])
            if c == 0:
                _gather_weight_rows(w_ref, nxt, src, dst, share)
            if c >= 1:
                epilogue(c - 1, prefix)

    def is_any(tiles):
        cond = j == tiles[0]
        for tile in tiles[1:]:
            cond = cond | (j == tile)
        return cond

    for parity, prefix in sorted({(t % 2, NORM_PREFIX[t]) for t in range(N_IN_TILES)}):
        tiles = [t for t in range(N_IN_TILES) if (t % 2, NORM_PREFIX[t]) == (parity, prefix)]
        cur, nxt = (wbf_a, wbf_b) if parity == 0 else (wbf_b, wbf_a)
        pl.when(is_any(tiles))(functools.partial(step, cur, nxt, prefix))


def _in_proj(hn, w_rows, w_first, cscale, gmat):
    t, d = hn.shape
    n_row_tiles = t // IN_TM
    assert IN_TN % n_row_tiles == 0 and (IN_TN // n_row_tiles) % 16 == 0
    return pl.pallas_call(
        functools.partial(_in_proj_kernel, n_row_tiles),
        grid=(N_IN_TILES, n_row_tiles),
        in_specs=[
            pl.BlockSpec((IN_TM, d), lambda j, i: (i, 0)),
            pl.BlockSpec((pl.Element(IN_TN * K_CHUNKS), pl.Element(LANES)),
                         lambda j, i: _w_index(jnp.minimum(j + 1, N_IN_TILES - 1))),
            pl.BlockSpec((IN_TN, d), lambda j, i: (0, 0), pipeline_mode=pl.Buffered(1)),
            pl.BlockSpec((1, IN_TN), lambda j, i: (0, _norm_slot(j))),
            pl.BlockSpec((IN_CHUNKS // 2, 2 * LANES, 2 * LANES),
                         lambda j, i: (_norm_slot(j), 0, 0)),
        ],
        out_specs=pl.BlockSpec((IN_TM, IN_TN), lambda j, i: (i, j)),
        out_shape=jax.ShapeDtypeStruct((t, PROJ_WIDTH), BF16),
        scratch_shapes=[
            pltpu.VMEM((IN_TN, d), BF16),
            pltpu.VMEM((IN_TN, d), BF16),
            pltpu.VMEM((2, IN_MCH, IN_TN), F32),
        ],
        compiler_params=pltpu.CompilerParams(
            dimension_semantics=("arbitrary", "arbitrary"), vmem_limit_bytes=VMEM_LIMIT),
        name="in_proj",
    )(hn, w_rows, w_first, cscale, gmat)


def _aug_lane(head):
    return HALF if head % 2 == 0 else 0


def _placement_matrix():
    place = np.zeros((N_PIECES * LANES, B_WIDTH), np.float32)
    for h in range(B_HEADS):
        for p in range(N_PIECES):
            place[p * LANES + h, (h // 2) * LANES + _aug_lane(h) + p] = 1.0
    return place


SWA_QB = 8
SWA_WIN = (SWA_QB + 1) * WINDOW


def _alibi_slopes_log2():
    return [float(2.0 ** (-8.0 * (h + 1) / A_Q_HEADS)) * LOG2E for h in range(A_Q_HEADS)]


def _swa_key_bias():
    key = np.arange(2 * WINDOW)[:, None]
    qry = np.arange(WINDOW)[None, :]
    rel = qry + WINDOW - key
    visible = (rel >= 0) & (rel < WINDOW)
    slopes = np.asarray(_alibi_slopes_log2(), np.float32)[:, None, None]
    return np.where(visible[None], slopes * key[None].astype(np.float32), np.float32(NEG))


def _swa_kernel(n_cast, sinks_ref, bias_ref, q_ref, kp_ref, kc_ref, vp_ref, vc_ref, *refs):
    cast_in, o_ref, cast_out = refs[:n_cast], refs[n_cast], refs[n_cast + 1:2 * n_cast + 1]
    kop_ref, vt_ref, s_ref, p_ref, sh_ref, ot_ref = refs[2 * n_cast + 1:]
    for src, dst in zip(cast_in, cast_out):
        dst[...] = src[...].astype(BF16)
    step = pl.program_id(1)
    blk = WINDOW
    lane = lax.broadcasted_iota(jnp.int32, (SWA_WIN, LANES), 1)

    kwin = jnp.concatenate([kp_ref[...], kc_ref[...]], axis=0).astype(F32)
    vwin_t = jnp.concatenate([vp_ref[...], vc_ref[...]], axis=0).astype(F32).T
    ones_rows = jnp.where(
        lax.broadcasted_iota(jnp.int32, (VT_ROWS - HALF, SWA_WIN), 0) == 0, 1.0, 0.0)
    for g in range(A_KV_HEADS):
        slab = kwin[:, (g // 2) * LANES:(g // 2 + 1) * LANES]
        own = (lane < HALF) if g % 2 == 0 else (lane >= HALF)
        kz = jnp.where(own, slab, 0.0)
        kop_ref[g, 0] = kz.astype(BF16)
        kop_ref[g, 1] = pltpu.roll(kz, HALF, 1).astype(BF16)
        vt_ref[g] = jnp.concatenate(
            [vwin_t[g * HEAD_DIM:(g + 1) * HEAD_DIM], ones_rows], axis=0).astype(BF16)

    t_win = lax.broadcasted_iota(jnp.int32, (1, blk), 1).astype(F32) + float(blk)
    slopes = _alibi_slopes_log2()
    sinks = [sinks_ref[h] * LOG2E + slopes[h] * t_win for h in range(A_Q_HEADS)]
    key_row = lax.broadcasted_iota(jnp.int32, (2 * blk, blk), 0)

    items = [(i, g) for i in range(SWA_QB) for g in range(A_KV_HEADS)]
    keys_of = lambda i: slice(i * blk, (i + 2) * blk)

    def stage_scores(n):
        i, g = items[n]
        for k in range(A_GROUP):
            h = g * A_GROUP + k
            q_slab = q_ref[i * blk:(i + 1) * blk, (h // 2) * LANES:(h // 2 + 1) * LANES]
            s_ref[n % 2, k] = _dot_nt(kop_ref[g, (h + g) % 2, keys_of(i), :], q_slab)

    def stage_softmax(n):
        i, g = items[n]
        for k in range(A_GROUP):
            h = g * A_GROUP + k
            st = s_ref[n % 2, k] + bias_ref[h]
            if i == 0:
                st = jnp.where(((step * SWA_QB - 1) * blk + key_row) >= 0, st, NEG)
            m = jnp.maximum(jnp.max(st, axis=0, keepdims=True), sinks[h])
            p_ref[n % 2, k] = jnp.exp2(st - m).astype(BF16)
            sh_ref[n % 2, k] = jnp.exp2(sinks[h] - m)

    def stage_values(n):
        i, g = items[n]
        for k in range(A_GROUP):
            h = g * A_GROUP + k
            ot = jnp.dot(vt_ref[g, :, keys_of(i)], p_ref[n % 2, k],
                         preferred_element_type=F32)
            ot_ref[h * HEAD_DIM:(h + 1) * HEAD_DIM, :] = (
                ot[0:HEAD_DIM] / (ot[HEAD_DIM:HEAD_DIM + 1] + sh_ref[n % 2, k]))
        if g == A_KV_HEADS - 1:
            o_ref[i * blk:(i + 1) * blk, :] = ot_ref[...].T.astype(BF16)

    for n in range(len(items) + 2):
        if n < len(items):
            stage_scores(n)
        if 2 <= n:
            stage_values(n - 2)
        if 1 <= n <= len(items):
            stage_softmax(n - 1)


def _swa(sinks, bias, proj, cast_weights, batch, seq):
    blk = WINDOW
    nb = seq // blk
    steps = nb // SWA_QB
    n_steps = batch * steps
    qcol = _col_block("qA", A_WIDTH)
    kcol = _col_block("kA", A_KV_WIDTH)
    vcol = _col_block("vA", A_KV_WIDTH)
    cur = lambda b, s: b * steps + s
    prev = lambda b, s: b * nb + jnp.maximum(s * SWA_QB - 1, 0)
    slab_specs = []
    for w in cast_weights:
        assert w.shape[0] % (16 * n_steps) == 0
        slab_specs.append(pl.BlockSpec((w.shape[0] // n_steps, w.shape[1]),
                                       lambda b, s: (cur(b, s), 0)))
    outs = pl.pallas_call(
        functools.partial(_swa_kernel, len(cast_weights)),
        grid=(batch, steps),
        in_specs=[
            pl.BlockSpec(memory_space=pltpu.SMEM),
            pl.BlockSpec(bias.shape, lambda b, s: (0, 0, 0)),
            pl.BlockSpec((SWA_QB * blk, A_WIDTH), lambda b, s: (cur(b, s), qcol)),
            pl.BlockSpec((blk, A_KV_WIDTH), lambda b, s: (prev(b, s), kcol)),
            pl.BlockSpec((SWA_QB * blk, A_KV_WIDTH), lambda b, s: (cur(b, s), kcol)),
            pl.BlockSpec((blk, A_KV_WIDTH), lambda b, s: (prev(b, s), vcol)),
            pl.BlockSpec((SWA_QB * blk, A_KV_WIDTH), lambda b, s: (cur(b, s), vcol)),
            *slab_specs,
        ],
        out_specs=[pl.BlockSpec((SWA_QB * blk, A_WIDTH), lambda b, s: (cur(b, s), 0)),
                   *slab_specs],
        out_shape=[jax.ShapeDtypeStruct((batch * seq, A_WIDTH), BF16),
                   *[jax.ShapeDtypeStruct(w.shape, BF16) for w in cast_weights]],
        scratch_shapes=[
            pltpu.VMEM((A_KV_HEADS, 2, SWA_WIN, LANES), BF16),
            pltpu.VMEM((A_KV_HEADS, VT_ROWS, SWA_WIN), BF16),
            pltpu.VMEM((2, A_GROUP, 2 * blk, blk), F32),
            pltpu.VMEM((2, A_GROUP, 2 * blk, blk), BF16),
            pltpu.VMEM((2, A_GROUP, 1, blk), F32),
            pltpu.VMEM((A_WIDTH, blk), F32),
        ],
        compiler_params=pltpu.CompilerParams(
            dimension_semantics=("arbitrary", "arbitrary"), vmem_limit_bytes=VMEM_LIMIT),
        name="swa",
    )(sinks, bias, proj, proj, proj, proj, proj, *cast_weights)
    return outs[0], outs[1:]


FOX_T = 512


FOX_H = FOX_T // 2


def _fox_kernel(q_ref, k_ref, v_ref, aug_ref, o_ref,
                kop_ref, vt_ref, qop_ref, s_ref, cm_ref, p_ref, al_ref, m_ref, acc_ref):
    seq = k_ref.shape[0]
    t = FOX_T
    hk = FOX_H

    lane = lax.broadcasted_iota(jnp.int32, (hk, LANES), 1)
    low = lane < HALF

    def build(r):
        sl = slice(r * hk, (r + 1) * hk)
        kk = k_ref[sl, :].astype(F32)
        aa = aug_ref[sl, :].astype(F32)
        vv = v_ref[sl, :].astype(F32)
        kop_ref[0, sl, :] = jnp.where(low, kk, aa).astype(BF16)
        kop_ref[1, sl, :] = jnp.where(low, aa, kk).astype(BF16)
        vvt = vv.T
        ones_rows = jnp.where(
            lax.broadcasted_iota(jnp.int32, (VT_ROWS - HALF, hk), 0) == 0, 1.0, 0.0)
        for hh in range(2):
            vt_ref[hh, r] = jnp.concatenate(
                [vvt[hh * HALF:(hh + 1) * HALF], ones_rows], axis=0).astype(BF16)

    def scores(slot, half, masked):
        ks = pl.ds(pl.multiple_of(half * hk, hk), hk)
        for hh in range(2):
            st = _dot_nt(kop_ref[hh, ks, :], qop_ref[hh])
            if masked:
                key = lax.broadcasted_iota(jnp.int32, (hk, t), 0) + slot * hk
                qry = lax.broadcasted_iota(jnp.int32, (hk, t), 1)
                st = jnp.where(key <= qry, st, NEG)
            s_ref[slot, hh] = st
            cm_ref[slot, hh] = jnp.max(st, axis=0, keepdims=True)

    def soft(slot):
        for hh in range(2):
            m_old = m_ref[hh]
            m_new = jnp.maximum(m_old, cm_ref[slot, hh])
            al_ref[slot, hh] = jnp.exp2(m_old - m_new)
            p_ref[slot, hh] = jnp.exp2(s_ref[slot, hh] - m_new).astype(BF16)
            m_ref[hh] = m_new

    def pv(slot, half):
        for hh in range(2):
            acc_ref[hh] = acc_ref[hh] * al_ref[slot, hh] + jnp.dot(
                vt_ref[hh, half], p_ref[slot, hh], preferred_element_type=F32)

    def step(i, prev, has_next, in_loop=True):
        if in_loop:
            pv(0, 2 * prev)
            soft(0)
            scores(1, 2 * i + 1, False)
            if has_next:
                scores(0, 2 * i + 2, False)
            pv(1, 2 * prev + 1)
            soft(1)
        else:
            scores(1, 2 * i + 1, False)
            pv(0, 2 * prev)
            soft(0)
            pv(1, 2 * prev + 1)
            if has_next:
                scores(0, 2 * i + 2, False)
            soft(1)

    lane_q = lax.broadcasted_iota(jnp.int32, (t, LANES), 1)
    low_q = lane_q < HALF
    ones0 = jnp.where((lane_q >= HALF) & (lane_q < HALF + N_PIECES), 1.0, 0.0)
    ones1 = jnp.where(lane_q < N_PIECES, 1.0, 0.0)

    for qi in range(seq // t):
        build(2 * qi)
        build(2 * qi + 1)
        rows = slice(qi * t, (qi + 1) * t)
        q = q_ref[rows, :].astype(F32)
        qop_ref[0] = jnp.where(low_q, q, ones0).astype(BF16)
        qop_ref[1] = jnp.where(low_q, ones1, q).astype(BF16)
        m_ref[...] = jnp.full(m_ref.shape, NEG, F32)
        acc_ref[...] = jnp.zeros(acc_ref.shape, F32)

        scores(0, 2 * qi, True)
        scores(1, 2 * qi + 1, True)
        soft(0)
        if qi > 0:
            scores(0, 0, False)
        soft(1)

        loop_pairs = max(qi - 1, 0) // 2

        def pair(j, carry, qi=qi):
            first = 2 * j
            step(first, jnp.where(j == 0, qi, first - 1), True)
            step(first + 1, first, True)
            return carry

        if loop_pairs > 1:
            lax.fori_loop(0, loop_pairs, pair, 0)
        static_from = 2 * loop_pairs if loop_pairs > 1 else 0
        for i in range(static_from, qi):
            step(i, i - 1 if i > 0 else qi, i + 1 < qi, in_loop=False)
        last = qi - 1 if qi > 0 else qi
        pv(0, 2 * last)
        pv(1, 2 * last + 1)

        ot = jnp.concatenate(
            [acc_ref[hh, 0:HALF, :] / acc_ref[hh, HALF:HALF + 1, :] for hh in range(2)], axis=0)
        o_ref[rows, :] = ot.T.astype(BF16)


def _fox(proj, aug, batch, seq):
    t = FOX_T
    nq = seq // t
    pairs = B_HEADS // 2
    qcol = _col_block("qB", LANES)
    kcol = _col_block("kB", LANES)
    vcol = _col_block("vB", LANES)
    return pl.pallas_call(
        _fox_kernel,
        grid=(batch, pairs),
        in_specs=[
            pl.BlockSpec((seq, LANES), lambda b, p: (b, qcol + p)),
            pl.BlockSpec((seq, LANES), lambda b, p: (b, kcol + p)),
            pl.BlockSpec((seq, LANES), lambda b, p: (b, vcol + p)),
            pl.BlockSpec((seq, LANES), lambda b, p: (b, p)),
        ],
        out_specs=pl.BlockSpec((seq, LANES), lambda b, p: (b, p)),
        out_shape=jax.ShapeDtypeStruct((batch * seq, B_WIDTH), BF16),
        scratch_shapes=[
            pltpu.VMEM((2, seq, LANES), BF16),
            pltpu.VMEM((2, seq // FOX_H, VT_ROWS, FOX_H), BF16),
            pltpu.VMEM((2, t, LANES), BF16),
            pltpu.VMEM((2, 2, FOX_H, t), F32),
            pltpu.VMEM((2, 2, 1, t), F32),
            pltpu.VMEM((2, 2, FOX_H, t), BF16),
            pltpu.VMEM((2, 2, 1, t), F32),
            pltpu.VMEM((2, 1, t), F32),
            pltpu.VMEM((2, VT_ROWS, t), F32),
        ],
        compiler_params=pltpu.CompilerParams(
            dimension_semantics=("arbitrary", "arbitrary"),
            vmem_limit_bytes=VMEM_LIMIT),
        name="fox",
    )(proj, proj, proj, aug)


MERGE_TM = 256


Z_BLOCK = 256
Z_PARTS = A_WIDTH // Z_BLOCK
assert A_WIDTH == B_WIDTH and A_WIDTH % Z_BLOCK == 0


def _merge_kernel(x_ref, ya_ref, yb_ref, mk_ref, mv_ref, qc_ref, *refs):
    za_refs = refs[:Z_PARTS]
    zb_refs = refs[Z_PARTS:2 * Z_PARTS]
    zc_ref, g0_ref, g1_ref, g2_ref, wa_ref, wb_ref, wc_ref, wo_ref, o_ref = refs[2 * Z_PARTS:]

    def memory_attention():
        outs = []
        for h in range(C_HEADS):
            sl = slice(h * C_HEAD_DIM, (h + 1) * C_HEAD_DIM)
            s = _dot_nt(qc_ref[:, sl], mk_ref[0, :, sl])
            p = jnp.exp(s - jnp.max(s, axis=-1, keepdims=True))
            o = jnp.dot(p.astype(BF16), mv_ref[0, :, sl], preferred_element_type=F32)
            outs.append(o / jnp.sum(p, axis=-1, keepdims=True))
        return jnp.concatenate(outs, axis=1)

    def branch(y, z_refs, w_ref):
        z = jnp.concatenate([r[...] for r in z_refs], axis=1).astype(F32)
        h = (y * (z * _sigmoid(z))).astype(BF16)
        return jnp.dot(h, w_ref[...], preferred_element_type=F32)

    y = _sigmoid(g0_ref[...].astype(F32)) * branch(ya_ref[...].astype(F32), za_refs, wa_ref)
    y = y + _sigmoid(g1_ref[...].astype(F32)) * branch(yb_ref[...].astype(F32), zb_refs, wb_ref)
    y = y + _sigmoid(g2_ref[...].astype(F32)) * branch(memory_attention(), (zc_ref,), wc_ref)
    o_ref[...] = x_ref[...] + jnp.dot(y.astype(BF16), wo_ref[...], preferred_element_type=F32)


def _merge(x2, ya, yb, mk, mv, proj, wa, wb, wc, wo, seq):
    t, d = x2.shape
    tm = MERGE_TM
    tiles_per_seq = seq // tm
    gcol = _col_block("g", D_MODEL)
    row = lambda width, col: pl.BlockSpec((tm, width), lambda i: (i, col))
    full = lambda a: pl.BlockSpec(a.shape, lambda i: (0, 0), pipeline_mode=pl.Buffered(1))
    mem = lambda a: pl.BlockSpec((1,) + a.shape[1:], lambda i: (i // tiles_per_seq, 0, 0))
    z_specs = lambda name: [row(Z_BLOCK, _col_block(name, Z_BLOCK, part)) for part in range(Z_PARTS)]
    n_proj = 2 * Z_PARTS + 5
    return pl.pallas_call(
        _merge_kernel,
        grid=(t // tm,),
        in_specs=[
            row(d, 0), row(A_WIDTH, 0), row(B_WIDTH, 0), mem(mk), mem(mv),
            row(C_WIDTH, _col_block("qC", C_WIDTH)),
            *z_specs("zA"), *z_specs("zB"), row(C_WIDTH, _col_block("zC", C_WIDTH)),
            row(d, gcol), row(d, gcol + 1), row(d, gcol + 2),
            full(wa), full(wb), full(wc), full(wo),
        ],
        out_specs=pl.BlockSpec((tm, d), lambda i: (i, 0)),
        out_shape=jax.ShapeDtypeStruct((t, d), F32),
        compiler_params=pltpu.CompilerParams(
            dimension_semantics=("arbitrary",), vmem_limit_bytes=VMEM_LIMIT),
        name="merge",
    )(x2, ya, yb, mk, mv, *([proj] * n_proj), wa, wb, wc, wo)


def _group_mean_matrices():
    lane = np.arange(2 * LANES)
    chunk = lane // LANES
    mats = []
    for tile in NORM_TILES:
        for c in range(0, IN_CHUNKS, 2):
            dims = np.asarray([_NORM_KIND.get(_chunk_piece(tile * IN_TN + (c + k) * LANES), LANES)
                               for k in range(2)])[chunk]
            same = ((lane[:, None] // dims[:, None] == lane[None, :] // dims[None, :])
                    & (chunk[:, None] == chunk[None, :]))
            mats.append(same.astype(np.float32) / dims[None, :])
    return np.stack(mats)


def _column_scales(gains):
    scale = {"qA": HEAD_DIM ** -0.5 * LOG2E, "qB": HEAD_DIM ** -0.5 * LOG2E,
             "qC": C_HEAD_DIM ** -0.5}
    cols = []
    for tile in NORM_TILES:
        for c in range(IN_CHUNKS):
            piece = _chunk_piece(tile * IN_TN + c * LANES)
            if piece in _NORM_KIND:
                g = gains[piece].astype(F32) * scale.get(piece, 1.0)
                cols.append(jnp.tile(g, LANES // g.shape[0]))
            else:
                cols.append(jnp.zeros((LANES,), F32))
    return jnp.concatenate(cols).reshape(1, len(NORM_TILES) * IN_TN)


def _layer(x, mem, norm_gain, mem_norm_gain, w_rows, b_forget, q_gain_a, k_gain_a, sinks_a,
           q_gain_b, k_gain_b, q_gain_c, k_gain_c, w_mem_kv, w_branch_a, w_branch_b,
           w_branch_c, w_out):
    batch, seq, d = x.shape
    x2 = x.reshape(batch * seq, d)

    w_f = w_rows[F_START * K_CHUNKS:(F_START + F_SHIFT) * K_CHUNKS].reshape(F_SHIFT, d)
    w_f = jnp.pad(w_f, ((0, LANES - F_SHIFT), (0, 0))).astype(BF16)
    b_f = jnp.pad(b_forget.astype(F32), (0, LANES - F_SHIFT)).reshape(1, LANES)
    cscale = _column_scales({"qA": q_gain_a, "kA": k_gain_a, "qB": q_gain_b, "kB": k_gain_b,
                             "qC": q_gain_c})
    gmat = jnp.asarray(_group_mean_matrices(), BF16)
    place = jnp.asarray(_placement_matrix(), BF16)

    mk, mv, w_first = _mem_kv(mem, mem_norm_gain.reshape(1, d), w_mem_kv,
                              k_gain_c.reshape(1, C_HEAD_DIM), w_rows)
    hn, aug = _norm_x(x2, norm_gain.reshape(1, d), w_f, b_f, place, seq)
    proj = _in_proj(hn, w_rows, w_first, cscale, gmat)
    ya, merge_weights = _swa(sinks_a.astype(F32), jnp.asarray(_swa_key_bias(), F32), proj,
                             (w_branch_a, w_branch_b, w_branch_c, w_out), batch, seq)
    yb = _fox(proj, aug, batch, seq)
    out = _merge(x2, ya, yb, mk, mv, proj, *merge_weights, seq)
    return out.reshape(batch, seq, d)


def kernel(x, mem, norm_gain, mem_norm_gain, w_in, b_forget, q_gain_a, k_gain_a, sinks_a,
           q_gain_b, k_gain_b, q_gain_c, k_gain_c, w_mem_kv, w_branch_a, w_branch_b,
           w_branch_c, w_out):
    depth = norm_gain.shape[0]
    w_rows = jnp.swapaxes(w_in, 1, 2).reshape(depth, -1, LANES)
    for layer in range(depth):
        x = _layer(x, mem, norm_gain[layer], mem_norm_gain[layer], w_rows[layer], b_forget[layer],
                   q_gain_a[layer], k_gain_a[layer], sinks_a[layer], q_gain_b[layer],
                   k_gain_b[layer], q_gain_c[layer], k_gain_c[layer], w_mem_kv[layer],
                   w_branch_a[layer], w_branch_b[layer], w_branch_c[layer], w_out[layer])
    return x
```

```python
import functools

import jax
import jax.numpy as jnp
import numpy as np
from jax import lax
from jax.experimental import pallas as pl
from jax.experimental.pallas import tpu as pltpu

F32 = jnp.float32
BF16 = jnp.bfloat16

D_MODEL = 2048
HEAD_DIM = 64
A_Q_HEADS = 12
A_KV_HEADS = 4
A_GROUP = A_Q_HEADS // A_KV_HEADS
WINDOW = 128
B_HEADS = 12
C_HEADS = 4
C_HEAD_DIM = 128
A_WIDTH = A_Q_HEADS * HEAD_DIM
A_KV_WIDTH = A_KV_HEADS * HEAD_DIM
B_WIDTH = B_HEADS * HEAD_DIM
C_WIDTH = C_HEADS * C_HEAD_DIM
EPS = 1e-6
NEG = -1e30

LANES = 128
HALF = LANES // 2
VT_ROWS = HALF + 16
LOG2E = float(np.log2(np.e))
N_PIECES = 3
VMEM_LIMIT = 62 * 1024 * 1024

_SRC = {}
_off = 0
for _name, _w in (("qA", A_WIDTH), ("kA", A_KV_WIDTH), ("vA", A_KV_WIDTH), ("zA", A_WIDTH),
                  ("qB", B_WIDTH), ("kB", B_WIDTH), ("vB", B_WIDTH), ("zB", B_WIDTH),
                  ("fB", B_HEADS), ("qC", C_WIDTH), ("zC", C_WIDTH), ("g", 3 * D_MODEL)):
    _SRC[_name] = (_off, _w)
    _off += _w

F_START, F_SHIFT = _SRC["fB"]
_DST = {n: (o if o < F_START else o - F_SHIFT) for n, (o, _) in _SRC.items() if n != "fB"}
PROJ_WIDTH = _SRC["g"][0] + _SRC["g"][1] - F_SHIFT
assert all(o % LANES == 0 for o in _DST.values()) and F_START % LANES == 0


def _col_block(name, width, part=0):
    assert _DST[name] % width == 0
    return _DST[name] // width + part


def _rms(x, gain):
    ms = jnp.mean(x * x, axis=-1, keepdims=True)
    return x * lax.rsqrt(ms + EPS) * gain


def _sigmoid(t):
    return 0.5 * (jnp.tanh(0.5 * t) + 1.0)


def _dot_nt(a, b):
    return lax.dot_general(a, b, (((1,), (1,)), ((), ())), preferred_element_type=F32)


def _mem_kv_kernel(mem_ref, gain_ref, w_ref, kgain_ref, wrows_ref, mk_ref, mv_ref, w0_ref):
    @pl.when(pl.program_id(0) == 0)
    def _():
        _gather_weight_rows(wrows_ref, w0_ref, 0, slice(None), IN_TN)

    hn = _rms(mem_ref[0], gain_ref[...]).astype(BF16)
    kv = jnp.dot(hn, w_ref[...].astype(BF16), preferred_element_type=F32)
    for h in range(C_HEADS):
        sl = slice(h * C_HEAD_DIM, (h + 1) * C_HEAD_DIM)
        mk_ref[0, :, sl] = _rms(kv[:, sl], kgain_ref[...]).astype(BF16)
    mv_ref[0] = kv[:, C_WIDTH:].astype(BF16)


def _mem_kv(mem, gain, w, kgain, w_rows):
    b, m, d = mem.shape
    return pl.pallas_call(
        _mem_kv_kernel,
        grid=(b,),
        in_specs=[
            pl.BlockSpec((1, m, d), lambda i: (i, 0, 0)),
            pl.BlockSpec((1, d), lambda i: (0, 0)),
            pl.BlockSpec((d, 2 * C_WIDTH), lambda i: (0, 0)),
            pl.BlockSpec((1, C_HEAD_DIM), lambda i: (0, 0)),
            pl.BlockSpec((pl.Element(IN_TN * K_CHUNKS), pl.Element(LANES)),
                         lambda i: _w_index(0 * i)),
        ],
        out_specs=[
            pl.BlockSpec((1, m, C_WIDTH), lambda i: (i, 0, 0)),
            pl.BlockSpec((1, m, C_WIDTH), lambda i: (i, 0, 0)),
            pl.BlockSpec((IN_TN, D_MODEL), lambda i: (0, 0)),
        ],
        out_shape=[jax.ShapeDtypeStruct((b, m, C_WIDTH), BF16)] * 2
        + [jax.ShapeDtypeStruct((IN_TN, D_MODEL), BF16)],
        compiler_params=pltpu.CompilerParams(
            dimension_semantics=("arbitrary",), vmem_limit_bytes=VMEM_LIMIT),
        name="mem_kv",
    )(mem, gain, w, kgain, w_rows)


NX_TM = 1024


def _norm_x_kernel(tiles_per_seq, x_ref, gain_ref, wf_ref, bf_ref, place_ref, hn_ref, aug_ref,
                   carry_ref):
    i = pl.program_id(0)

    @pl.when(i % tiles_per_seq == 0)
    def _():
        carry_ref[...] = jnp.zeros_like(carry_ref)

    hn = _rms(x_ref[...], gain_ref[...]).astype(BF16)
    hn_ref[...] = hn
    f_logit = _dot_nt(hn, wf_ref[...]) + bf_ref[...]
    x = jnp.minimum(f_logit, 0.0) - jnp.log1p(jnp.exp(-jnp.abs(f_logit)))
    row = lax.broadcasted_iota(jnp.int32, x.shape, 0)
    shift = 1
    while shift < NX_TM:
        x = x + jnp.where(row >= shift, pltpu.roll(x, shift, 0), 0.0)
        shift *= 2
    c = x + carry_ref[...]
    carry_ref[...] = c[NX_TM - 1:NX_TM, :]
    rest = c * (-LOG2E)
    pieces = []
    for _ in range(N_PIECES):
        p = rest.astype(BF16)
        pieces.append(p)
        rest = rest - p.astype(F32)
    stacked = jnp.concatenate(pieces, axis=1)
    aug_ref[...] = jnp.dot(stacked, place_ref[...], preferred_element_type=F32).astype(BF16)


def _norm_x(x2, gain, wf, bfg, place, seq):
    t, d = x2.shape
    return pl.pallas_call(
        functools.partial(_norm_x_kernel, seq // NX_TM),
        grid=(t // NX_TM,),
        in_specs=[
            pl.BlockSpec((NX_TM, d), lambda i: (i, 0)),
            pl.BlockSpec((1, d), lambda i: (0, 0)),
            pl.BlockSpec((LANES, d), lambda i: (0, 0)),
            pl.BlockSpec((1, LANES), lambda i: (0, 0)),
            pl.BlockSpec((N_PIECES * LANES, B_WIDTH), lambda i: (0, 0)),
        ],
        out_specs=[
            pl.BlockSpec((NX_TM, d), lambda i: (i, 0)),
            pl.BlockSpec((NX_TM, B_WIDTH), lambda i: (i, 0)),
        ],
        out_shape=[jax.ShapeDtypeStruct((t, d), BF16),
                   jax.ShapeDtypeStruct((t, B_WIDTH), BF16)],
        scratch_shapes=[pltpu.VMEM((1, LANES), F32)],
        compiler_params=pltpu.CompilerParams(
            dimension_semantics=("arbitrary",), vmem_limit_bytes=VMEM_LIMIT),
        name="norm_x",
    )(x2, gain, wf, bfg, place)


IN_TM = 1024
IN_TN = 1024
IN_CHUNKS = IN_TN // LANES
K_CHUNKS = D_MODEL // LANES
N_IN_TILES = PROJ_WIDTH // IN_TN
N_ALIGNED_TILES = F_START // IN_TN
assert F_START % IN_TN == 0 and PROJ_WIDTH % IN_TN == 0 and F_SHIFT < LANES

_NORM_KIND = {"qA": HEAD_DIM, "kA": HEAD_DIM, "qB": HEAD_DIM, "kB": HEAD_DIM, "qC": C_HEAD_DIM}


def _chunk_piece(col):
    for name, off in _DST.items():
        if off <= col < off + _SRC[name][1]:
            return name
    raise ValueError(col)


def _norm_prefix(tile):
    kinds = [_chunk_piece(tile * IN_TN + c * LANES) in _NORM_KIND for c in range(IN_CHUNKS)]
    n = sum(kinds)
    assert kinds == [True] * n + [False] * (IN_CHUNKS - n)
    return n


NORM_PREFIX = tuple(_norm_prefix(j) for j in range(N_IN_TILES))
NORM_TILES = tuple(j for j in range(N_IN_TILES) if NORM_PREFIX[j])


def _norm_slot(j):
    slot = 0
    for tile in NORM_TILES[1:]:
        slot = slot + (j >= tile).astype(jnp.int32)
    return slot


IN_MCH = 1024


def _gather_weight_rows(w_ref, dst_ref, src_row, dst_rows, n_rows):
    for c in range(K_CHUNKS):
        dst_ref[dst_rows, c * LANES:(c + 1) * LANES] = (
            w_ref[pl.ds(src_row + c, n_rows, stride=K_CHUNKS), :].astype(BF16))


def _w_index(tile):
    col = tile * IN_TN + jnp.where(tile >= N_ALIGNED_TILES, F_SHIFT, 0)
    return (col * K_CHUNKS, 0)


def _in_proj_kernel(n_row_tiles, hn_ref, w_ref, w0_ref, cscale_ref, gmat_ref, o_ref,
                    wbf_a, wbf_b, acc_ref):
    j = pl.program_id(0)
    i = pl.program_id(1)
    share = IN_TN // n_row_tiles

    @pl.when((j == 0) & (i == 0))
    def _():
        wbf_a[...] = w0_ref[...]

    def epilogue(chunk, prefix):
        rows = slice(chunk * IN_MCH, (chunk + 1) * IN_MCH)
        acc = acc_ref.at[chunk % 2]
        for c in range(prefix // 2):
            sl = slice(c * 2 * LANES, (c + 1) * 2 * LANES)
            a = acc[:, sl]
            ms = jnp.dot((a * a).astype(BF16), gmat_ref[c], preferred_element_type=F32)
            o_ref[rows, sl] = (a * lax.rsqrt(ms + EPS) * cscale_ref[:, sl]).astype(BF16)
        if prefix < IN_CHUNKS:
            rest = slice(prefix * LANES, IN_TN)
            o_ref[rows, rest] = acc[:, rest].astype(BF16)

    def step(cur, nxt, prefix):
        src = pl.multiple_of(i * (share * K_CHUNKS), share * K_CHUNKS)
        dst = pl.ds(pl.multiple_of(i * share, share), share)
        n_chunks = hn_ref.shape[0] // IN_MCH
        for c in range(n_chunks + 1):
            if c < n_chunks:
                acc_ref[c % 2] = _dot_nt(hn_ref[c * IN_MCH:(c + 1) * IN_MCH, :], cur[...])
            if c == 0:
                _gather_weight_rows(w_ref, nxt, src, dst, share)
            if c >= 1:
                epilogue(c - 1, prefix)

    def is_any(tiles):
        cond = j == tiles[0]
        for tile in tiles[1:]:
            cond = cond | (j == tile)
        return cond

    for parity, prefix in sorted({(t % 2, NORM_PREFIX[t]) for t in range(N_IN_TILES)}):
        tiles = [t for t in range(N_IN_TILES) if (t % 2, NORM_PREFIX[t]) == (parity, prefix)]
        cur, nxt = (wbf_a, wbf_b) if parity == 0 else (wbf_b, wbf_a)
        pl.when(is_any(tiles))(functools.partial(step, cur, nxt, prefix))


def _in_proj(hn, w_rows, w_first, cscale, gmat):
    t, d = hn.shape
    n_row_tiles = t // IN_TM
    assert IN_TN % n_row_tiles == 0 and (IN_TN // n_row_tiles) % 16 == 0
    return pl.pallas_call(
        functools.partial(_in_proj_kernel, n_row_tiles),
        grid=(N_IN_TILES, n_row_tiles),
        in_specs=[
            pl.BlockSpec((IN_TM, d), lambda j, i: (i, 0)),
            pl.BlockSpec((pl.Element(IN_TN * K_CHUNKS), pl.Element(LANES)),
                         lambda j, i: _w_index(jnp.minimum(j + 1, N_IN_TILES - 1))),
            pl.BlockSpec((IN_TN, d), lambda j, i: (0, 0), pipeline_mode=pl.Buffered(1)),
            pl.BlockSpec((1, IN_TN), lambda j, i: (0, _norm_slot(j))),
            pl.BlockSpec((IN_CHUNKS // 2, 2 * LANES, 2 * LANES),
                         lambda j, i: (_norm_slot(j), 0, 0)),
        ],
        out_specs=pl.BlockSpec((IN_TM, IN_TN), lambda j, i: (i, j)),
        out_shape=jax.ShapeDtypeStruct((t, PROJ_WIDTH), BF16),
        scratch_shapes=[
            pltpu.VMEM((IN_TN, d), BF16),
            pltpu.VMEM((IN_TN, d), BF16),
            pltpu.VMEM((2, IN_MCH, IN_TN), F32),
        ],
        compiler_params=pltpu.CompilerParams(
            dimension_semantics=("arbitrary", "arbitrary"), vmem_limit_bytes=VMEM_LIMIT),
        name="in_proj",
    )(hn, w_rows, w_first, cscale, gmat)


def _aug_lane(head):
    return HALF if head % 2 == 0 else 0


def _placement_matrix():
    place = np.zeros((N_PIECES * LANES, B_WIDTH), np.float32)
    for h in range(B_HEADS):
        for p in range(N_PIECES):
            place[p * LANES + h, (h // 2) * LANES + _aug_lane(h) + p] = 1.0
    return place


SWA_QB = 8
SWA_WIN = (SWA_QB + 1) * WINDOW


def _alibi_slopes_log2():
    return [float(2.0 ** (-8.0 * (h + 1) / A_Q_HEADS)) * LOG2E for h in range(A_Q_HEADS)]


def _swa_key_bias():
    key = np.arange(2 * WINDOW)[:, None]
    qry = np.arange(WINDOW)[None, :]
    rel = qry + WINDOW - key
    visible = (rel >= 0) & (rel < WINDOW)
    slopes = np.asarray(_alibi_slopes_log2(), np.float32)[:, None, None]
    return np.where(visible[None], slopes * key[None].astype(np.float32), np.float32(NEG))


def _swa_kernel(n_cast, sinks_ref, bias_ref, q_ref, kp_ref, kc_ref, vp_ref, vc_ref, *refs):
    cast_in, o_ref, cast_out = refs[:n_cast], refs[n_cast], refs[n_cast + 1:2 * n_cast + 1]
    kop_ref, vt_ref, s_ref, p_ref, sh_ref, ot_ref = refs[2 * n_cast + 1:]
    for src, dst in zip(cast_in, cast_out):
        dst[...] = src[...].astype(BF16)
    step = pl.program_id(1)
    blk = WINDOW
    lane = lax.broadcasted_iota(jnp.int32, (SWA_WIN, LANES), 1)

    kwin = jnp.concatenate([kp_ref[...], kc_ref[...]], axis=0).astype(F32)
    vwin_t = jnp.concatenate([vp_ref[...], vc_ref[...]], axis=0).astype(F32).T
    ones_rows = jnp.where(
        lax.broadcasted_iota(jnp.int32, (VT_ROWS - HALF, SWA_WIN), 0) == 0, 1.0, 0.0)
    for g in range(A_KV_HEADS):
        slab = kwin[:, (g // 2) * LANES:(g // 2 + 1) * LANES]
        own = (lane < HALF) if g % 2 == 0 else (lane >= HALF)
        kz = jnp.where(own, slab, 0.0)
        kop_ref[g, 0] = kz.astype(BF16)
        kop_ref[g, 1] = pltpu.roll(kz, HALF, 1).astype(BF16)
        vt_ref[g] = jnp.concatenate(
            [vwin_t[g * HEAD_DIM:(g + 1) * HEAD_DIM], ones_rows], axis=0).astype(BF16)

    t_win = lax.broadcasted_iota(jnp.int32, (1, blk), 1).astype(F32) + float(blk)
    slopes = _alibi_slopes_log2()
    sinks = [sinks_ref[h] * LOG2E + slopes[h] * t_win for h in range(A_Q_HEADS)]
    key_row = lax.broadcasted_iota(jnp.int32, (2 * blk, blk), 0)

    items = [(i, g) for i in range(SWA_QB) for g in range(A_KV_HEADS)]
    keys_of = lambda i: slice(i * blk, (i + 2) * blk)

    def stage_scores(n):
        i, g = items[n]
        for k in range(A_GROUP):
            h = g * A_GROUP + k
            q_slab = q_ref[i * blk:(i + 1) * blk, (h // 2) * LANES:(h // 2 + 1) * LANES]
            s_ref[n % 2, k] = _dot_nt(kop_ref[g, (h + g) % 2, keys_of(i), :], q_slab)

    def stage_softmax(n):
        i, g = items[n]
        for k in range(A_GROUP):
            h = g * A_GROUP + k
            st = s_ref[n % 2, k] + bias_ref[h]
            if i == 0:
                st = jnp.where(((step * SWA_QB - 1) * blk + key_row) >= 0, st, NEG)
            m = jnp.maximum(jnp.max(st, axis=0, keepdims=True), sinks[h])
            p_ref[n % 2, k] = jnp.exp2(st - m).astype(BF16)
            sh_ref[n % 2, k] = jnp.exp2(sinks[h] - m)

    def stage_values(n):
        i, g = items[n]
        for k in range(A_GROUP):
            h = g * A_GROUP + k
            ot = jnp.dot(vt_ref[g, :, keys_of(i)], p_ref[n % 2, k],
                         preferred_element_type=F32)
            ot_ref[h * HEAD_DIM:(h + 1) * HEAD_DIM, :] = (
                ot[0:HEAD_DIM] / (ot[HEAD_DIM:HEAD_DIM + 1] + sh_ref[n % 2, k]))
        if g == A_KV_HEADS - 1:
            o_ref[i * blk:(i + 1) * blk, :] = ot_ref[...].T.astype(BF16)

    for n in range(len(items) + 2):
        if n < len(items):
            stage_scores(n)
        if 2 <= n:
            stage_values(n - 2)
        if 1 <= n <= len(items):
            stage_softmax(n - 1)


def _swa(sinks, bias, proj, cast_weights, batch, seq):
    blk = WINDOW
    nb = seq // blk
    steps = nb // SWA_QB
    n_steps = batch * steps
    qcol = _col_block("qA", A_WIDTH)
    kcol = _col_block("kA", A_KV_WIDTH)
    vcol = _col_block("vA", A_KV_WIDTH)
    cur = lambda b, s: b * steps + s
    prev = lambda b, s: b * nb + jnp.maximum(s * SWA_QB - 1, 0)
    slab_specs = []
    for w in cast_weights:
        assert w.shape[0] % (16 * n_steps) == 0
        slab_specs.append(pl.BlockSpec((w.shape[0] // n_steps, w.shape[1]),
                                       lambda b, s: (cur(b, s), 0)))
    outs = pl.pallas_call(
        functools.partial(_swa_kernel, len(cast_weights)),
        grid=(batch, steps),
        in_specs=[
            pl.BlockSpec(memory_space=pltpu.SMEM),
            pl.BlockSpec(bias.shape, lambda b, s: (0, 0, 0)),
            pl.BlockSpec((SWA_QB * blk, A_WIDTH), lambda b, s: (cur(b, s), qcol)),
            pl.BlockSpec((blk, A_KV_WIDTH), lambda b, s: (prev(b, s), kcol)),
            pl.BlockSpec((SWA_QB * blk, A_KV_WIDTH), lambda b, s: (cur(b, s), kcol)),
            pl.BlockSpec((blk, A_KV_WIDTH), lambda b, s: (prev(b, s), vcol)),
            pl.BlockSpec((SWA_QB * blk, A_KV_WIDTH), lambda b, s: (cur(b, s), vcol)),
            *slab_specs,
        ],
        out_specs=[pl.BlockSpec((SWA_QB * blk, A_WIDTH), lambda b, s: (cur(b, s), 0)),
                   *slab_specs],
        out_shape=[jax.ShapeDtypeStruct((batch * seq, A_WIDTH), BF16),
                   *[jax.ShapeDtypeStruct(w.shape, BF16) for w in cast_weights]],
        scratch_shapes=[
            pltpu.VMEM((A_KV_HEADS, 2, SWA_WIN, LANES), BF16),
            pltpu.VMEM((A_KV_HEADS, VT_ROWS, SWA_WIN), BF16),
            pltpu.VMEM((2, A_GROUP, 2 * blk, blk), F32),
            pltpu.VMEM((2, A_GROUP, 2 * blk, blk), BF16),
            pltpu.VMEM((2, A_GROUP, 1, blk), F32),
            pltpu.VMEM((A_WIDTH, blk), F32),
        ],
        compiler_params=pltpu.CompilerParams(
            dimension_semantics=("arbitrary", "arbitrary"), vmem_limit_bytes=VMEM_LIMIT),
        name="swa",
    )(sinks, bias, proj, proj, proj, proj, proj, *cast_weights)
    return outs[0], outs[1:]


FOX_T = 512


FOX_H = FOX_T // 2


def _fox_kernel(q_ref, k_ref, v_ref, aug_ref, o_ref,
                kop_ref, vt_ref, qop_ref, s_ref, cm_ref, p_ref, al_ref, m_ref, acc_ref):
    seq = k_ref.shape[0]
    t = FOX_T
    hk = FOX_H

    lane = lax.broadcasted_iota(jnp.int32, (hk, LANES), 1)
    low = lane < HALF

    def build(r):
        sl = slice(r * hk, (r + 1) * hk)
        kk = k_ref[sl, :].astype(F32)
        aa = aug_ref[sl, :].astype(F32)
        vv = v_ref[sl, :].astype(F32)
        kop_ref[0, sl, :] = jnp.where(low, kk, aa).astype(BF16)
        kop_ref[1, sl, :] = jnp.where(low, aa, kk).astype(BF16)
        vvt = vv.T
        ones_rows = jnp.where(
            lax.broadcasted_iota(jnp.int32, (VT_ROWS - HALF, hk), 0) == 0, 1.0, 0.0)
        for hh in range(2):
            vt_ref[hh, r] = jnp.concatenate(
                [vvt[hh * HALF:(hh + 1) * HALF], ones_rows], axis=0).astype(BF16)

    def scores(slot, half, masked):
        ks = pl.ds(pl.multiple_of(half * hk, hk), hk)
        for hh in range(2):
            st = _dot_nt(kop_ref[hh, ks, :], qop_ref[hh])
            if masked:
                key = lax.broadcasted_iota(jnp.int32, (hk, t), 0) + slot * hk
                qry = lax.broadcasted_iota(jnp.int32, (hk, t), 1)
                st = jnp.where(key <= qry, st, NEG)
            s_ref[slot, hh] = st
            cm_ref[slot, hh] = jnp.max(st, axis=0, keepdims=True)

    def soft(slot):
        for hh in range(2):
            m_old = m_ref[hh]
            m_new = jnp.maximum(m_old, cm_ref[slot, hh])
            al_ref[slot, hh] = jnp.exp2(m_old - m_new)
            p_ref[slot, hh] = jnp.exp2(s_ref[slot, hh] - m_new).astype(BF16)
            m_ref[hh] = m_new

    def pv(slot, half):
        for hh in range(2):
            acc_ref[hh] = acc_ref[hh] * al_ref[slot, hh] + jnp.dot(
                vt_ref[hh, half], p_ref[slot, hh], preferred_element_type=F32)

    def step(i, prev, has_next, in_loop=True):
        if in_loop:
            pv(0, 2 * prev)
            soft(0)
            scores(1, 2 * i + 1, False)
            if has_next:
                scores(0, 2 * i + 2, False)
            pv(1, 2 * prev + 1)
            soft(1)
        else:
            scores(1, 2 * i + 1, False)
            pv(0, 2 * prev)
            soft(0)
            pv(1, 2 * prev + 1)
            if has_next:
                scores(0, 2 * i + 2, False)
            soft(1)

    lane_q = lax.broadcasted_iota(jnp.int32, (t, LANES), 1)
    low_q = lane_q < HALF
    ones0 = jnp.where((lane_q >= HALF) & (lane_q < HALF + N_PIECES), 1.0, 0.0)
    ones1 = jnp.where(lane_q < N_PIECES, 1.0, 0.0)

    for qi in range(seq // t):
        build(2 * qi)
        build(2 * qi + 1)
        rows = slice(qi * t, (qi + 1) * t)
        q = q_ref[rows, :].astype(F32)
        qop_ref[0] = jnp.where(low_q, q, ones0).astype(BF16)
        qop_ref[1] = jnp.where(low_q, ones1, q).astype(BF16)
        m_ref[...] = jnp.full(m_ref.shape, NEG, F32)
        acc_ref[...] = jnp.zeros(acc_ref.shape, F32)

        scores(0, 2 * qi, True)
        scores(1, 2 * qi + 1, True)
        soft(0)
        if qi > 0:
            scores(0, 0, False)
        soft(1)

        loop_pairs = max(qi - 1, 0) // 2

        def pair(j, carry, qi=qi):
            first = 2 * j
            step(first, jnp.where(j == 0, qi, first - 1), True)
            step(first + 1, first, True)
            return carry

        if loop_pairs > 1:
            lax.fori_loop(0, loop_pairs, pair, 0)
        static_from = 2 * loop_pairs if loop_pairs > 1 else 0
        for i in range(static_from, qi):
            step(i, i - 1 if i > 0 else qi, i + 1 < qi, in_loop=False)
        last = qi - 1 if qi > 0 else qi
        pv(0, 2 * last)
        pv(1, 2 * last + 1)

        ot = jnp.concatenate(
            [acc_ref[hh, 0:HALF, :] / acc_ref[hh, HALF:HALF + 1, :] for hh in range(2)], axis=0)
        o_ref[rows, :] = ot.T.astype(BF16)


def _fox(proj, aug, batch, seq):
    t = FOX_T
    nq = seq // t
    pairs = B_HEADS // 2
    qcol = _col_block("qB", LANES)
    kcol = _col_block("kB", LANES)
    vcol = _col_block("vB", LANES)
    return pl.pallas_call(
        _fox_kernel,
        grid=(batch, pairs),
        in_specs=[
            pl.BlockSpec((seq, LANES), lambda b, p: (b, qcol + p)),
            pl.BlockSpec((seq, LANES), lambda b, p: (b, kcol + p)),
            pl.BlockSpec((seq, LANES), lambda b, p: (b, vcol + p)),
            pl.BlockSpec((seq, LANES), lambda b, p: (b, p)),
        ],
        out_specs=pl.BlockSpec((seq, LANES), lambda b, p: (b, p)),
        out_shape=jax.ShapeDtypeStruct((batch * seq, B_WIDTH), BF16),
        scratch_shapes=[
            pltpu.VMEM((2, seq, LANES), BF16),
            pltpu.VMEM((2, seq // FOX_H, VT_ROWS, FOX_H), BF16),
            pltpu.VMEM((2, t, LANES), BF16),
            pltpu.VMEM((2, 2, FOX_H, t), F32),
            pltpu.VMEM((2, 2, 1, t), F32),
            pltpu.VMEM((2, 2, FOX_H, t), BF16),
            pltpu.VMEM((2, 2, 1, t), F32),
            pltpu.VMEM((2, 1, t), F32),
            pltpu.VMEM((2, VT_ROWS, t), F32),
        ],
        compiler_params=pltpu.CompilerParams(
            dimension_semantics=("arbitrary", "arbitrary"),
            vmem_limit_bytes=VMEM_LIMIT),
        name="fox",
    )(proj, proj, proj, aug)


MERGE_TM = 512


Z_BLOCK = 256
Z_PARTS = A_WIDTH // Z_BLOCK
assert A_WIDTH == B_WIDTH and A_WIDTH % Z_BLOCK == 0


def _merge_kernel(x_ref, ya_ref, yb_ref, mk_ref, mv_ref, qc_ref, *refs):
    za_refs = refs[:Z_PARTS]
    zb_refs = refs[Z_PARTS:2 * Z_PARTS]
    zc_ref, g0_ref, g1_ref, g2_ref, wa_ref, wb_ref, wc_ref, wo_ref, o_ref = refs[2 * Z_PARTS:]

    def memory_attention():
        outs = []
        for h in range(C_HEADS):
            sl = slice(h * C_HEAD_DIM, (h + 1) * C_HEAD_DIM)
            s = _dot_nt(qc_ref[:, sl], mk_ref[0, :, sl])
            p = jnp.exp(s - jnp.max(s, axis=-1, keepdims=True))
            o = jnp.dot(p.astype(BF16), mv_ref[0, :, sl], preferred_element_type=F32)
            outs.append(o / jnp.sum(p, axis=-1, keepdims=True))
        return jnp.concatenate(outs, axis=1)

    def branch(y, z_refs, w_ref):
        z = jnp.concatenate([r[...] for r in z_refs], axis=1).astype(F32)
        h = (y * (z * _sigmoid(z))).astype(BF16)
        return jnp.dot(h, w_ref[...], preferred_element_type=F32)

    y = _sigmoid(g0_ref[...].astype(F32)) * branch(ya_ref[...].astype(F32), za_refs, wa_ref)
    y = y + _sigmoid(g1_ref[...].astype(F32)) * branch(yb_ref[...].astype(F32), zb_refs, wb_ref)
    y = y + _sigmoid(g2_ref[...].astype(F32)) * branch(memory_attention(), (zc_ref,), wc_ref)
    o_ref[...] = x_ref[...] + jnp.dot(y.astype(BF16), wo_ref[...], preferred_element_type=F32)


def _merge(x2, ya, yb, mk, mv, proj, wa, wb, wc, wo, seq):
    t, d = x2.shape
    tm = MERGE_TM
    tiles_per_seq = seq // tm
    gcol = _col_block("g", D_MODEL)
    row = lambda width, col: pl.BlockSpec((tm, width), lambda i: (i, col))
    full = lambda a: pl.BlockSpec(a.shape, lambda i: (0, 0), pipeline_mode=pl.Buffered(1))
    mem = lambda a: pl.BlockSpec((1,) + a.shape[1:], lambda i: (i // tiles_per_seq, 0, 0))
    z_specs = lambda name: [row(Z_BLOCK, _col_block(name, Z_BLOCK, part)) for part in range(Z_PARTS)]
    n_proj = 2 * Z_PARTS + 5
    return pl.pallas_call(
        _merge_kernel,
        grid=(t // tm,),
        in_specs=[
            row(d, 0), row(A_WIDTH, 0), row(B_WIDTH, 0), mem(mk), mem(mv),
            row(C_WIDTH, _col_block("qC", C_WIDTH)),
            *z_specs("zA"), *z_specs("zB"), row(C_WIDTH, _col_block("zC", C_WIDTH)),
            row(d, gcol), row(d, gcol + 1), row(d, gcol + 2),
            full(wa), full(wb), full(wc), full(wo),
        ],
        out_specs=pl.BlockSpec((tm, d), lambda i: (i, 0)),
        out_shape=jax.ShapeDtypeStruct((t, d), F32),
        compiler_params=pltpu.CompilerParams(
            dimension_semantics=("arbitrary",), vmem_limit_bytes=VMEM_LIMIT),
        name="merge",
    )(x2, ya, yb, mk, mv, *([proj] * n_proj), wa, wb, wc, wo)


def _group_mean_matrices():
    lane = np.arange(2 * LANES)
    chunk = lane // LANES
    mats = []
    for tile in NORM_TILES:
        for c in range(0, IN_CHUNKS, 2):
            dims = np.asarray([_NORM_KIND.get(_chunk_piece(tile * IN_TN + (c + k) * LANES), LANES)
                               for k in range(2)])[chunk]
            same = ((lane[:, None] // dims[:, None] == lane[None, :] // dims[None, :])
                    & (chunk[:, None] == chunk[None, :]))
            mats.append(same.astype(np.float32) / dims[None, :])
    return np.stack(mats)


def _column_scales(gains):
    scale = {"qA": HEAD_DIM ** -0.5 * LOG2E, "qB": HEAD_DIM ** -0.5 * LOG2E,
             "qC": C_HEAD_DIM ** -0.5}
    cols = []
    for tile in NORM_TILES:
        for c in range(IN_CHUNKS):
            piece = _chunk_piece(tile * IN_TN + c * LANES)
            if piece in _NORM_KIND:
                g = gains[piece].astype(F32) * scale.get(piece, 1.0)
                cols.append(jnp.tile(g, LANES // g.shape[0]))
            else:
                cols.append(jnp.zeros((LANES,), F32))
    return jnp.concatenate(cols).reshape(1, len(NORM_TILES) * IN_TN)


def _layer(x, mem, norm_gain, mem_norm_gain, w_rows, b_forget, q_gain_a, k_gain_a, sinks_a,
           q_gain_b, k_gain_b, q_gain_c, k_gain_c, w_mem_kv, w_branch_a, w_branch_b,
           w_branch_c, w_out):
    batch, seq, d = x.shape
    x2 = x.reshape(batch * seq, d)

    w_f = w_rows[F_START * K_CHUNKS:(F_START + F_SHIFT) * K_CHUNKS].reshape(F_SHIFT, d)
    w_f = jnp.pad(w_f, ((0, LANES - F_SHIFT), (0, 0))).astype(BF16)
    b_f = jnp.pad(b_forget.astype(F32), (0, LANES - F_SHIFT)).reshape(1, LANES)
    cscale = _column_scales({"qA": q_gain_a, "kA": k_gain_a, "qB": q_gain_b, "kB": k_gain_b,
                             "qC": q_gain_c})
    gmat = jnp.asarray(_group_mean_matrices(), BF16)
    place = jnp.asarray(_placement_matrix(), BF16)

    mk, mv, w_first = _mem_kv(mem, mem_norm_gain.reshape(1, d), w_mem_kv,
                              k_gain_c.reshape(1, C_HEAD_DIM), w_rows)
    hn, aug = _norm_x(x2, norm_gain.reshape(1, d), w_f, b_f, place, seq)
    proj = _in_proj(hn, w_rows, w_first, cscale, gmat)
    ya, merge_weights = _swa(sinks_a.astype(F32), jnp.asarray(_swa_key_bias(), F32), proj,
                             (w_branch_a, w_branch_b, w_branch_c, w_out), batch, seq)
    yb = _fox(proj, aug, batch, seq)
    out = _merge(x2, ya, yb, mk, mv, proj, *merge_weights, seq)
    return out.reshape(batch, seq, d)


def kernel(x, mem, norm_gain, mem_norm_gain, w_in, b_forget, q_gain_a, k_gain_a, sinks_a,
           q_gain_b, k_gain_b, q_gain_c, k_gain_c, w_mem_kv, w_branch_a, w_branch_b,
           w_branch_c, w_out):
    depth = norm_gain.shape[0]
    w_rows = jnp.swapaxes(w_in, 1, 2).reshape(depth, -1, LANES)
    for layer in range(depth):
        x = _layer(x, mem, norm_gain[layer], mem_norm_gain[layer], w_rows[layer], b_forget[layer],
                   q_gain_a[layer], k_gain_a[layer], sinks_a[layer], q_gain_b[layer],
                   k_gain_b[layer], q_gain_c[layer], k_gain_c[layer], w_mem_kv[layer],
                   w_branch_a[layer], w_branch_b[layer], w_branch_c[layer], w_out[layer])
    return x
```
